```python
import math
import jax, jax.numpy as jnp
from jax import lax
import numpy as np

D_MODEL = 1024
BATCH = 8
SEQ = 8192
DEPTH = 1

D_CONV = 512
CONV_WIDTH = 31
N_Q_HEADS = 8
N_KV_HEADS = 2
HEAD_DIM = 64
Q_PER_KV = N_Q_HEADS // N_KV_HEADS
WINDOW = 128
N_BUCKETS = 32
MAX_DISTANCE = 128
N_GROUPS = 4
EXPERTS_PER_GROUP = 8
N_EXPERTS = N_GROUPS * EXPERTS_PER_GROUP
TOP_K_IN_GROUP = 2
D_EXPERT = 256
MOE_BLOCK = 512
N_BRANCHES = 2
D_Q = N_Q_HEADS * HEAD_DIM
D_KV = N_KV_HEADS * HEAD_DIM
D_IN = 2 * D_CONV + D_Q + 2 * D_KV + N_BRANCHES * D_MODEL
EPS = 1e-6
NEG_INF = -1e30

kernel_name = "hybrid_conv_swa_hmoe_block"


def rms_norm(t, g):
    tf = t.astype(jnp.float32)
    y = tf * lax.rsqrt(jnp.mean(tf * tf, axis=-1, keepdims=True) + EPS)
    return (y * g.astype(jnp.float32)).astype(t.dtype)


def layer_norm(t, g, b):
    tf = t.astype(jnp.float32)
    mu = jnp.mean(tf, axis=-1, keepdims=True)
    var = jnp.mean(jnp.square(tf - mu), axis=-1, keepdims=True)
    y = (tf - mu) * lax.rsqrt(var + EPS) * g.astype(jnp.float32) + b.astype(jnp.float32)
    return y.astype(t.dtype)


def t5_causal_bucket(dist):
    max_exact = N_BUCKETS // 2
    d = jnp.maximum(dist, 1).astype(jnp.float32)
    large = max_exact + (jnp.log(d / max_exact) / math.log(MAX_DISTANCE / max_exact)
                         * (N_BUCKETS - max_exact)).astype(jnp.int32)
    large = jnp.minimum(large, N_BUCKETS - 1)
    return jnp.where(dist < max_exact, dist, large)


def conformer_conv(a, b, dw_kernel, dw_bias, ln_g, ln_b, w_pw_out):
    u = a * jax.nn.sigmoid(b)
    u = lax.conv_general_dilated(
        u, dw_kernel[:, None, :].astype(u.dtype), window_strides=(1,),
        padding=[(CONV_WIDTH - 1, 0)], dimension_numbers=('NWC', 'WIO', 'NWC'),
        feature_group_count=D_CONV) + dw_bias
    u = jax.nn.silu(layer_norm(u, ln_g, ln_b))
    return u @ w_pw_out


def sliding_window_attention(q, k, v, q_norm_g, k_norm_g, rel_bias_table, sinks):
    B, T = q.shape[0], q.shape[1]
    nb = T // WINDOW
    q = rms_norm(q, q_norm_g)
    k = rms_norm(k, k_norm_g)
    qb = q.reshape(B, nb, WINDOW, N_KV_HEADS, Q_PER_KV, HEAD_DIM)

    def band(t):
        tb = t.reshape(B, nb, WINDOW, N_KV_HEADS, HEAD_DIM)
        prev = jnp.pad(tb, ((0, 0), (1, 0), (0, 0), (0, 0), (0, 0)))[:, :-1]
        return jnp.concatenate([prev, tb], axis=2)

    kb, vb = band(k), band(v)
    s = jnp.einsum('bnqhgd,bnshd->bnhgqs', qb, kb,
                   preferred_element_type=jnp.float32) * (HEAD_DIM ** -0.5)
    qi = jnp.arange(WINDOW)[:, None]
    kj = jnp.arange(2 * WINDOW)[None, :]
    dist = qi + WINDOW - kj
    in_window = (dist >= 0) & (dist < WINDOW)
    bias = rel_bias_table[t5_causal_bucket(jnp.clip(dist, 0, MAX_DISTANCE))]
    bias = bias.astype(jnp.float32).transpose(2, 0, 1).reshape(
        N_KV_HEADS, Q_PER_KV, WINDOW, 2 * WINDOW)
    key_pos = jnp.arange(nb)[:, None] * WINDOW + kj - WINDOW
    mask = in_window[None] & (key_pos >= 0)[:, None, :]
    logits = jnp.where(mask[None, :, None, None], s + bias, NEG_INF)
    sink = sinks.astype(jnp.float32).reshape(N_KV_HEADS, Q_PER_KV)[:, :, None, None]
    m = jnp.maximum(jnp.max(logits, axis=-1, keepdims=True), sink)
    p = jnp.exp(logits - m)
    probs = p / (jnp.sum(p, axis=-1, keepdims=True) + jnp.exp(sink - m))
    o = jnp.einsum('bnhgqs,bnshd->bnqhgd', probs.astype(v.dtype), vb)
    return o.reshape(B, T, D_Q)


def hierarchical_moe(h, w_router_group, b_router_group, w_router_expert, b_router_expert,
                     w_exp_gate, w_exp_up, w_exp_down):
    B, T, D = h.shape
    N = B * T
    hf = h.reshape(N, D)
    gl = (hf @ w_router_group).astype(jnp.float32) + b_router_group.astype(jnp.float32)
    pg = jax.nn.softmax(gl, axis=-1)
    _, g_idx = lax.top_k(gl, 1)
    p_top = jnp.take_along_axis(pg, g_idx, axis=-1)
    el = (hf @ w_router_expert).astype(jnp.float32) + b_router_expert.astype(jnp.float32)
    el = el.reshape(N, N_GROUPS, EXPERTS_PER_GROUP)
    el_sel = jnp.take_along_axis(el, g_idx[:, :, None], axis=1)[:, 0]
    pe = jax.nn.softmax(el_sel, axis=-1)
    vals, e_in = lax.top_k(pe, TOP_K_IN_GROUP)
    w = vals / jnp.sum(vals, axis=-1, keepdims=True) * p_top
    expert_id = g_idx * EXPERTS_PER_GROUP + e_in

    M = N * TOP_K_IN_GROUP
    e_flat = expert_id.reshape(M)
    w_flat = w.reshape(M)
    tok = jnp.repeat(jnp.arange(N, dtype=jnp.int32), TOP_K_IN_GROUP)
    order = jnp.argsort(e_flat)
    e_sorted = e_flat[order]
    counts = jnp.zeros((N_EXPERTS,), jnp.int32).at[e_flat].add(1)
    start = jnp.cumsum(counts) - counts
    pcounts = (counts + MOE_BLOCK - 1) // MOE_BLOCK * MOE_BLOCK
    pend = jnp.cumsum(pcounts)
    pstart = pend - pcounts
    dest = pstart[e_sorted] + (jnp.arange(M, dtype=jnp.int32) - start[e_sorted])
    n_blocks = -(-M // MOE_BLOCK) + N_EXPERTS
    n_slots = n_blocks * MOE_BLOCK
    slot_tok = jnp.full((n_slots,), N, jnp.int32).at[dest].set(tok[order])
    slot_w = jnp.zeros((n_slots,), jnp.float32).at[dest].set(w_flat[order])
    block_e = jnp.minimum(
        jnp.searchsorted(pend, jnp.arange(n_blocks, dtype=jnp.int32) * MOE_BLOCK, side='right'),
        N_EXPERTS - 1)
    hpad = jnp.concatenate([hf, jnp.zeros((1, D), hf.dtype)], axis=0)

    def expert_block(args):
        e, toks, ws = args
        xb = hpad[toks]
        hid = jax.nn.silu(xb @ w_exp_gate[e]) * (xb @ w_exp_up[e])
        return (hid @ w_exp_down[e]) * ws[:, None].astype(xb.dtype)

    yb = lax.map(expert_block, (block_e, slot_tok.reshape(n_blocks, MOE_BLOCK),
                                slot_w.reshape(n_blocks, MOE_BLOCK)))
    out = jnp.zeros((N + 1, D), yb.dtype).at[slot_tok].add(yb.reshape(n_slots, D))[:N]
    return out.reshape(B, T, D)


def setup_inputs(seed: int = 0) -> dict:
    key = jax.random.key(seed)
    ks = jax.random.split(key, 32)
    f32 = jnp.float32
    nrm = lambda k, shape, s: jax.random.normal(k, shape, f32) * s
    L, D = DEPTH, D_MODEL
    return {
        "x": nrm(ks[0], (BATCH, SEQ, D), 1.0),
        "c": nrm(ks[1], (BATCH, D), 1.0),
        "w_ada": nrm(ks[2], (L, D, 6 * D), 0.5 * D ** -0.5),
        "b_ada": nrm(ks[3], (L, 6 * D), 0.02),
        "norm_mix_g": 1.0 + nrm(ks[4], (L, D), 0.02),
        "w_in": nrm(ks[5], (L, D, D_IN), D ** -0.5),
        "dw_kernel": nrm(ks[6], (L, CONV_WIDTH, D_CONV), CONV_WIDTH ** -0.5),
        "dw_bias": nrm(ks[7], (L, D_CONV), 0.02),
        "conv_ln_g": 1.0 + nrm(ks[8], (L, D_CONV), 0.02),
        "conv_ln_b": nrm(ks[9], (L, D_CONV), 0.02),
        "w_conv_out": nrm(ks[10], (L, D_CONV, D), D_CONV ** -0.5),
        "q_norm_g": 1.0 + nrm(ks[11], (L, HEAD_DIM), 0.02),
        "k_norm_g": 1.0 + nrm(ks[12], (L, HEAD_DIM), 0.02),
        "sinks": nrm(ks[13], (L, N_Q_HEADS), 0.5),
        "w_attn_out": nrm(ks[14], (L, D_Q, D), D_Q ** -0.5),
        "w_out": nrm(ks[15], (L, D, D), D ** -0.5),
        "rel_bias_table": nrm(ks[16], (N_BUCKETS, N_Q_HEADS), 0.5),
        "norm_ffn_g": 1.0 + nrm(ks[17], (L, D), 0.02),
        "w_router_group": nrm(ks[18], (L, D, N_GROUPS), D ** -0.5),
        "b_router_group": nrm(ks[19], (L, N_GROUPS), 0.01),
        "w_router_expert": nrm(ks[20], (L, D, N_EXPERTS), D ** -0.5),
        "b_router_expert": nrm(ks[21], (L, N_EXPERTS), 0.01),
        "w_exp_gate": nrm(ks[22], (L, N_EXPERTS, D, D_EXPERT), D ** -0.5),
        "w_exp_up": nrm(ks[23], (L, N_EXPERTS, D, D_EXPERT), D ** -0.5),
        "w_exp_down": nrm(ks[24], (L, N_EXPERTS, D_EXPERT, D), D_EXPERT ** -0.5),
    }


def reference(x, c, w_ada, b_ada, norm_mix_g, w_in, dw_kernel, dw_bias, conv_ln_g, conv_ln_b,
              w_conv_out, q_norm_g, k_norm_g, sinks, w_attn_out, w_out, rel_bias_table,
              norm_ffn_g, w_router_group, b_router_group, w_router_expert, b_router_expert,
              w_exp_gate, w_exp_up, w_exp_down):
    B, T, _ = x.shape
    splits = [D_CONV, 2 * D_CONV, 2 * D_CONV + D_Q, 2 * D_CONV + D_Q + D_KV,
              2 * D_CONV + D_Q + 2 * D_KV]
    for l in range(DEPTH):
        mod = jax.nn.silu(c) @ w_ada[l] + b_ada[l]
        sh1, sc1, g1, sh2, sc2, g2 = jnp.split(mod[:, None, :], 6, axis=-1)

        h = rms_norm(x, norm_mix_g[l]) * (1.0 + sc1) + sh1
        proj = h @ w_in[l]
        a_c, b_c, q, k, v, gate_logits = jnp.split(proj, splits, axis=-1)
        y_conv = conformer_conv(a_c, b_c, dw_kernel[l], dw_bias[l], conv_ln_g[l],
                                conv_ln_b[l], w_conv_out[l])
        y_attn = sliding_window_attention(
            q.reshape(B, T, N_Q_HEADS, HEAD_DIM), k.reshape(B, T, N_KV_HEADS, HEAD_DIM),
            v.reshape(B, T, N_KV_HEADS, HEAD_DIM), q_norm_g[l], k_norm_g[l],
            rel_bias_table, sinks[l]) @ w_attn_out[l]
        g_conv, g_attn = jnp.split(jax.nn.sigmoid(gate_logits), N_BRANCHES, axis=-1)
        x = x + g1 * ((g_conv * y_conv + g_attn * y_attn) @ w_out[l])

        h2 = rms_norm(x, norm_ffn_g[l]) * (1.0 + sc2) + sh2
        x = x + g2 * hierarchical_moe(h2, w_router_group[l], b_router_group[l],
                                      w_router_expert[l], b_router_expert[l],
                                      w_exp_gate[l], w_exp_up[l], w_exp_down[l])
    return x
```

```python
import functools
import math

import jax
import jax.numpy as jnp
import numpy as np
from jax import lax
from jax.experimental import pallas as pl
from jax.experimental.pallas import tpu as pltpu

D_MODEL = 1024
D_CONV = 512
CONV_WIDTH = 31
N_Q_HEADS = 8
N_KV_HEADS = 2
HEAD_DIM = 64
Q_PER_KV = N_Q_HEADS // N_KV_HEADS
WINDOW = 128
N_BUCKETS = 32
MAX_DISTANCE = 128
N_GROUPS = 4
EXPERTS_PER_GROUP = 8
N_EXPERTS = N_GROUPS * EXPERTS_PER_GROUP
TOP_K = 2
D_EXPERT = 256
D_Q = N_Q_HEADS * HEAD_DIM
D_KV = N_KV_HEADS * HEAD_DIM
EPS = 1e-6
NEG_INF = -1e30

LANES = 128
SEQ_TILE = 512
MOE_BLOCK = 512
SCATTER_TILE = 1024
COMBINE_TILE = 256
ROUTER_COLS = LANES
EXPERT_COL0 = 8
VMEM_LIMIT = 56 * 1024 * 1024

f32 = jnp.float32
bf16 = jnp.bfloat16
i32 = jnp.int32


def _dot(a, b):
    return jnp.dot(a, b, preferred_element_type=f32)


def _split(a):
    hi = a.astype(bf16)
    lo = (a - hi.astype(f32)).astype(bf16)
    return hi, lo


def _dot3(a, b):
    ah, al = _split(a)
    bh, bl = _split(b)
    return _dot(ah, bh) + _dot(al, bh) + _dot(ah, bl)


def _mod_kernel(c_ref, w_ref, b_ref, o_ref):
    c = c_ref[...]
    s = c * jax.nn.sigmoid(c)
    o_ref[...] = _dot3(s, w_ref[...]) + b_ref[...]


def _modulation(c, w_ada, b_ada):
    B, D = c.shape
    n_out = w_ada.shape[1]
    return pl.pallas_call(
        _mod_kernel,
        grid=(n_out // D,),
        in_specs=[pl.BlockSpec((B, D), lambda j: (0, 0)),
                  pl.BlockSpec((D, D), lambda j: (0, j)),
                  pl.BlockSpec((1, D), lambda j: (0, j))],
        out_specs=pl.BlockSpec((B, D), lambda j: (0, j)),
        out_shape=jax.ShapeDtypeStruct((B, n_out), f32),
        name="modulation",
    )(c, w_ada, b_ada.reshape(1, n_out))


def _band_buckets():
    qi = np.arange(WINDOW)[:, None]
    kj = np.arange(2 * WINDOW)[None, :]
    dist = np.clip(qi + WINDOW - kj, 0, MAX_DISTANCE)
    max_exact = N_BUCKETS // 2
    d = np.maximum(dist, 1).astype(np.float32)
    large = max_exact + (np.log(d / np.float32(max_exact)) / np.float32(math.log(MAX_DISTANCE / max_exact))
                         * np.float32(N_BUCKETS - max_exact)).astype(np.int32)
    large = np.minimum(large, N_BUCKETS - 1)
    return np.where(dist < max_exact, dist, large).astype(np.int32)


def _bias_kernel(tab_ref, bucket_ref, o_ref):
    bk = bucket_ref[...]
    for h in range(N_Q_HEADS):
        acc = jnp.zeros(bk.shape, f32)
        for b in range(N_BUCKETS):
            acc = jnp.where(bk == b, tab_ref[b, h], acc)
        o_ref[h] = acc


def _bias_band(rel_bias_table):
    return pl.pallas_call(
        _bias_kernel,
        in_specs=[pl.BlockSpec(memory_space=pltpu.SMEM),
                  pl.BlockSpec(memory_space=pltpu.VMEM)],
        out_specs=pl.BlockSpec(memory_space=pltpu.VMEM),
        out_shape=jax.ShapeDtypeStruct((N_Q_HEADS, WINDOW, 2 * WINDOW), f32),
        name="bias_band",
    )(rel_bias_table, jnp.asarray(_band_buckets()))


def _mixer_kernel(x_ref, mod_ref, gmix_ref, gffn_ref, win_ref, dw_ref, dwb_ref, lng_ref, lnb_ref,
                  wco_ref, wao_ref, wout_ref, qg_ref, kg_ref, bq_ref, bk_ref, bias_ref, sink_ref,
                  wrh_ref, wrl_ref, br_ref, tri_ref,
                  x1_ref, h2_ref, route_ref, wtok_ref, cnt_ref,
                  uext, k2, v2, osc):
    TM = x_ref.shape[0]
    HALO = 32
    b = pl.program_id(0)
    t = pl.program_id(1)
    first = t == 0

    @pl.when(first)
    def _():
        uext[0:HALO, :] = jnp.zeros((HALO, D_CONV), f32)
        k2[:, 0:WINDOW, :] = jnp.zeros((N_KV_HEADS, WINDOW, LANES), bf16)
        v2[:, 0:WINDOW, :] = jnp.zeros((N_KV_HEADS, WINDOW, LANES), bf16)

    @pl.when(first & (b == 0))
    def _():
        cnt_ref[...] = jnp.zeros(cnt_ref.shape, f32)

    x = x_ref[...]
    mod = mod_ref[...]
    sh1, sc1, g1 = mod[0:1, :], mod[1:2, :], mod[2:3, :]
    sh2, sc2, g2 = mod[3:4, :], mod[4:5, :], mod[5:6, :]
    del g2

    ms = jnp.mean(x * x, axis=-1, keepdims=True)
    h = (x * lax.rsqrt(ms + EPS)) * gmix_ref[...]
    h = h * (1.0 + sc1) + sh1
    hb = h.astype(bf16)

    ab = _dot(hb, win_ref[:, 0:2 * D_CONV])
    u = ab[:, 0:D_CONV] * jax.nn.sigmoid(ab[:, D_CONV:2 * D_CONV])
    uext[HALO:HALO + TM, :] = u
    acc = jnp.zeros((TM, D_CONV), f32)
    for r in range(8):
        a_list = [a for a in range(5) if 0 <= 8 * a + r - 2 <= CONV_WIDTH - 1]
        ur = uext[r:r + TM + 8 * max(a_list), :]
        for a in a_list:
            j = 8 * a + r - 2
            acc = acc + dw_ref[j:j + 1, :] * ur[8 * a:8 * a + TM, :]
    conv = acc + dwb_ref[...]
    uext[0:HALO, :] = uext[TM:TM + HALO, :]
    mu = jnp.mean(conv, axis=-1, keepdims=True)
    dc = conv - mu
    var = jnp.mean(dc * dc, axis=-1, keepdims=True)
    yn = dc * lax.rsqrt(var + EPS) * lng_ref[...] + lnb_ref[...]
    act = yn * jax.nn.sigmoid(yn)
    y_conv = _dot(act.astype(bf16), wco_ref[...])

    c0 = 2 * D_CONV
    qkv = _dot(hb, win_ref[:, c0:c0 + D_Q + 2 * D_KV])
    q = qkv[:, 0:D_Q]
    k = qkv[:, D_Q:D_Q + D_KV]
    v = qkv[:, D_Q + D_KV:D_Q + 2 * D_KV]

    def head_norm(z, blk_ref, g):
        hi, lo = _split(z * z)
        msq = _dot(hi, blk_ref[...]) + _dot(lo, blk_ref[...])
        return z * lax.rsqrt(msq + EPS) * g

    qn = head_norm(q, bq_ref, qg_ref[...]).astype(bf16)
    kn = head_norm(k, bk_ref, kg_ref[...])
    lo_half = lax.broadcasted_iota(i32, (TM, LANES), 1) < HEAD_DIM
    kr = pltpu.roll(kn, HEAD_DIM, 1)
    vr = pltpu.roll(v, HEAD_DIM, 1)
    k2[0, WINDOW:WINDOW + TM, :] = jnp.where(lo_half, kn, kr).astype(bf16)
    k2[1, WINDOW:WINDOW + TM, :] = jnp.where(lo_half, kr, kn).astype(bf16)
    v2[0, WINDOW:WINDOW + TM, :] = jnp.where(lo_half, v, vr).astype(bf16)
    v2[1, WINDOW:WINDOW + TM, :] = jnp.where(lo_half, vr, v).astype(bf16)

    QROWS = Q_PER_KV * WINDOW
    row = lax.broadcasted_iota(i32, (QROWS, 2 * WINDOW), 0) & (WINDOW - 1)
    col = lax.broadcasted_iota(i32, (QROWS, 2 * WINDOW), 1)
    dist = row + WINDOW - col
    in_window = (dist >= 0) & (dist < WINDOW)
    first_mask = in_window & (col >= jnp.where(first, WINDOW, 0))
    hrow = lax.broadcasted_iota(i32, (QROWS, 1), 0) // WINDOW
    lo128 = lax.broadcasted_iota(i32, (WINDOW, LANES), 1) < HEAD_DIM
    zero_q = jnp.zeros((WINDOW, LANES), bf16)
    for g in range(N_KV_HEADS):
        sink = jnp.where(hrow == 0, sink_ref[4 * g],
                         jnp.where(hrow == 1, sink_ref[4 * g + 1],
                                   jnp.where(hrow == 2, sink_ref[4 * g + 2], sink_ref[4 * g + 3])))
        bias_g = bias_ref[g]
        for j in range(TM // WINDOW):
            rs = slice(j * WINDOW, (j + 1) * WINDOW)
            qa = qn[rs, 2 * LANES * g:2 * LANES * g + LANES]
            qb = qn[rs, 2 * LANES * g + LANES:2 * LANES * (g + 1)]
            qs = jnp.concatenate([jnp.where(lo128, qa, zero_q), jnp.where(lo128, zero_q, qa),
                                  jnp.where(lo128, qb, zero_q), jnp.where(lo128, zero_q, qb)], axis=0)
            kk = k2[g, j * WINDOW:(j + 2) * WINDOW, :]
            s = lax.dot_general(qs, kk, (((1,), (1,)), ((), ())), preferred_element_type=f32)
            msk = first_mask if j == 0 else in_window
            logits = jnp.where(msk, s + bias_g, NEG_INF)
            m = jnp.maximum(jnp.max(logits, axis=-1, keepdims=True), sink)
            p = jnp.exp(logits - m)
            den = jnp.sum(p, axis=-1, keepdims=True) + jnp.exp(sink - m)
            o2 = _dot(p.astype(bf16), v2[g, j * WINDOW:(j + 2) * WINDOW, :]) * (1.0 / den)
            osc[rs, 2 * LANES * g:2 * LANES * g + LANES] = jnp.where(
                lo128, o2[0:WINDOW], o2[WINDOW:2 * WINDOW]).astype(bf16)
            osc[rs, 2 * LANES * g + LANES:2 * LANES * (g + 1)] = jnp.where(
                lo128, o2[2 * WINDOW:3 * WINDOW], o2[3 * WINDOW:4 * WINDOW]).astype(bf16)
    k2[:, 0:WINDOW, :] = k2[:, TM:TM + WINDOW, :]
    v2[:, 0:WINDOW, :] = v2[:, TM:TM + WINDOW, :]
    y_attn = _dot(osc[...], wao_ref[...])

    c1 = c0 + D_Q + 2 * D_KV
    g_conv = jax.nn.sigmoid(_dot(hb, win_ref[:, c1:c1 + D_MODEL]))
    g_attn = jax.nn.sigmoid(_dot(hb, win_ref[:, c1 + D_MODEL:c1 + 2 * D_MODEL]))
    merged = (g_conv * y_conv + g_attn * y_attn).astype(bf16)
    x1 = x + g1 * _dot(merged, wout_ref[...])
    x1_ref[...] = x1

    ms2 = jnp.mean(x1 * x1, axis=-1, keepdims=True)
    h2 = (x1 * lax.rsqrt(ms2 + EPS)) * gffn_ref[...]
    h2 = h2 * (1.0 + sc2) + sh2
    h2_ref[...] = h2
    hh, hl = _split(h2)
    lg = _dot(hh, wrh_ref[...]) + _dot(hl, wrh_ref[...]) + _dot(hh, wrl_ref[...]) + br_ref[...]
    lt = lg.T
    gl = lt[0:N_GROUPS, :]
    grow = lax.broadcasted_iota(i32, (N_GROUPS, TM), 0)
    gmax = jnp.max(gl, axis=0, keepdims=True)
    gi = jnp.min(jnp.where(gl == gmax, grow, N_GROUPS), axis=0, keepdims=True)
    p_top = 1.0 / jnp.sum(jnp.exp(gl - gmax), axis=0, keepdims=True)
    sel = lt[EXPERT_COL0:EXPERT_COL0 + EXPERTS_PER_GROUP, :]
    for gg in range(1, N_GROUPS):
        lo_r = EXPERT_COL0 + gg * EXPERTS_PER_GROUP
        sel = jnp.where(gi == gg, lt[lo_r:lo_r + EXPERTS_PER_GROUP, :], sel)
    erow = lax.broadcasted_iota(i32, (EXPERTS_PER_GROUP, TM), 0)
    m1 = jnp.max(sel, axis=0, keepdims=True)
    i1 = jnp.min(jnp.where(sel == m1, erow, EXPERTS_PER_GROUP), axis=0, keepdims=True)
    rest = jnp.where(erow == i1, -jnp.inf, sel)
    m2 = jnp.max(rest, axis=0, keepdims=True)
    i2 = jnp.min(jnp.where(rest == m2, erow, EXPERTS_PER_GROUP), axis=0, keepdims=True)
    z = jnp.sum(jnp.exp(sel - m1), axis=0, keepdims=True)
    v1 = 1.0 / z
    v2nd = jnp.exp(m2 - m1) / z
    w1 = v1 / (v1 + v2nd) * p_top
    w2 = v2nd / (v1 + v2nd) * p_top
    e1 = gi * EXPERTS_PER_GROUP + i1
    e2 = gi * EXPERTS_PER_GROUP + i2

    xrow = lax.broadcasted_iota(i32, (N_EXPERTS, TM), 0)
    oh1 = xrow == e1
    oh2 = xrow == e2
    both = jnp.where(oh1 | oh2, 1.0, 0.0)
    prefix = _dot(both.astype(bf16), tri_ref[...]) + cnt_ref[:, 0:1]
    r1 = jnp.sum(jnp.where(oh1, prefix, 0.0), axis=0, keepdims=True)
    r2 = jnp.sum(jnp.where(oh2, prefix, 0.0), axis=0, keepdims=True)
    cnt_ref[...] = cnt_ref[...] + jnp.sum(both, axis=1, keepdims=True)

    route_ref[...] = jnp.concatenate(
        [e1, e2, r1.astype(i32), r2.astype(i32), jnp.zeros((4, TM), i32)], axis=0)
    wpad = jnp.concatenate([w1, w2, jnp.zeros((LANES - 2, TM), f32)], axis=0)
    wtok_ref[...] = wpad.T


def _mixer(x, mod6, p):
    B, T, D = x.shape
    TM = min(SEQ_TILE, T)
    nt = T // TM
    N = B * T
    const = lambda shape: pl.BlockSpec(shape, lambda b, t: (0,) * len(shape))
    in_specs = [
        pl.BlockSpec((None, TM, D), lambda b, t: (b, t, 0)),
        pl.BlockSpec((None, 6, D), lambda b, t: (b, 0, 0)),
        const((1, D)), const((1, D)),
        const(p["w_in"].shape),
        const((CONV_WIDTH, D_CONV)), const((1, D_CONV)), const((1, D_CONV)), const((1, D_CONV)),
        const((D_CONV, D)), const((D_Q, D)), const((D, D)),
        const((1, D_Q)), const((1, D_KV)),
        const((D_Q, D_Q)), const((D_KV, D_KV)),
        const((N_KV_HEADS, Q_PER_KV * WINDOW, 2 * WINDOW)),
        pl.BlockSpec(memory_space=pltpu.SMEM),
        const((D, ROUTER_COLS)), const((D, ROUTER_COLS)), const((1, ROUTER_COLS)),
        const((TM, TM)),
    ]
    out_specs = [
        pl.BlockSpec((None, TM, D), lambda b, t: (b, t, 0)),
        pl.BlockSpec((None, TM, D), lambda b, t: (b, t, 0)),
        pl.BlockSpec((8, TM), lambda b, t: (0, b * nt + t)),
        pl.BlockSpec((TM, LANES), lambda b, t: (b * nt + t, 0)),
        pl.BlockSpec((N_EXPERTS, LANES), lambda b, t: (0, 0)),
    ]
    out_shape = [
        jax.ShapeDtypeStruct((B, T, D), f32),
        jax.ShapeDtypeStruct((B, T, D), f32),
        jax.ShapeDtypeStruct((8, N), i32),
        jax.ShapeDtypeStruct((N, LANES), f32),
        jax.ShapeDtypeStruct((N_EXPERTS, LANES), f32),
    ]
    scratch = [
        pltpu.VMEM((TM + 32, D_CONV), f32),
        pltpu.VMEM((N_KV_HEADS, TM + WINDOW, LANES), bf16),
        pltpu.VMEM((N_KV_HEADS, TM + WINDOW, LANES), bf16),
        pltpu.VMEM((TM, D_Q), bf16),
    ]
    tri = jnp.asarray(np.triu(np.ones((TM, TM), np.float32), 1), bf16)
    return pl.pallas_call(
        _mixer_kernel,
        grid=(B, nt),
        in_specs=in_specs, out_specs=out_specs, out_shape=out_shape, scratch_shapes=scratch,
        compiler_params=pltpu.CompilerParams(
            dimension_semantics=("arbitrary", "arbitrary"), vmem_limit_bytes=VMEM_LIMIT),
        name="mixer_router",
    )(x, mod6, p["gmix"], p["gffn"], p["w_in"], p["dw"], p["dwb"], p["lng"], p["lnb"],
      p["wco"], p["wao"], p["wout"], p["qg"], p["kg"], p["bq"], p["bk"], p["bias"], p["sinks"],
      p["wrh"], p["wrl"], p["br"], tri)


def _scatter_kernel(pstart_ref, route_ref, h2_ref, xs_in_ref, xs_ref, dest_ref, sem):
    del xs_in_ref
    TS = route_ref.shape[1]
    base = pl.program_id(0) * TS

    def row_copy(tok, slot):
        return pltpu.make_async_copy(h2_ref.at[pl.ds(tok, 1)], xs_ref.at[pl.ds(slot, 1)], sem)

    def issue(i, carry):
        d1 = pstart_ref[route_ref[0, i]] + route_ref[2, i]
        d2 = pstart_ref[route_ref[1, i]] + route_ref[3, i]
        row_copy(base + i, d1).start()
        row_copy(base + i, d2).start()
        dest_ref[0, i] = d1
        dest_ref[1, i] = d2
        return carry

    lax.fori_loop(0, TS, issue, 0)

    def drain(i, carry):
        row_copy(0, 0).wait()
        row_copy(0, 0).wait()
        return carry

    lax.fori_loop(0, TS, drain, 0)


def _scatter(pstart, route, h2, n_slots):
    N, D = h2.shape
    TS = min(SCATTER_TILE, N)
    xs0 = jnp.zeros((n_slots, D), f32)
    grid_spec = pltpu.PrefetchScalarGridSpec(
        num_scalar_prefetch=1,
        grid=(N // TS,),
        in_specs=[pl.BlockSpec((8, TS), lambda i, ps: (0, i), memory_space=pltpu.SMEM),
                  pl.BlockSpec(memory_space=pl.ANY),
                  pl.BlockSpec(memory_space=pl.ANY)],
        out_specs=[pl.BlockSpec(memory_space=pl.ANY),
                   pl.BlockSpec((2, TS), lambda i, ps: (0, i), memory_space=pltpu.SMEM)],
        scratch_shapes=[pltpu.SemaphoreType.DMA],
    )
    return pl.pallas_call(
        _scatter_kernel,
        grid_spec=grid_spec,
        out_shape=[jax.ShapeDtypeStruct((n_slots, D), f32),
                   jax.ShapeDtypeStruct((2, N), i32)],
        input_output_aliases={3: 0},
        compiler_params=pltpu.CompilerParams(dimension_semantics=("arbitrary",)),
        name="dispatch_scatter",
    )(pstart, route, h2, xs0)


def _expert_kernel(be_ref, nu_ref, xs_ref, wgu_ref, wd_ref, y_ref):
    @pl.when(pl.program_id(0) < nu_ref[0])
    def _():
        xb = xs_ref[...].astype(bf16)
        gu = _dot(xb, wgu_ref[...])
        gate = gu[:, 0:D_EXPERT]
        hid = (gate * jax.nn.sigmoid(gate)) * gu[:, D_EXPERT:2 * D_EXPERT]
        y_ref[...] = _dot(hid.astype(bf16), wd_ref[...])


def _experts(block_e, n_used, xs, wgu, wd):
    n_slots, D = xs.shape
    n_blocks = n_slots // MOE_BLOCK
    last = lambda i, nu: jnp.minimum(i, nu[0] - 1)
    grid_spec = pltpu.PrefetchScalarGridSpec(
        num_scalar_prefetch=2,
        grid=(n_blocks,),
        in_specs=[pl.BlockSpec((MOE_BLOCK, D), lambda i, be, nu: (last(i, nu), 0)),
                  pl.BlockSpec((None, D, 2 * D_EXPERT), lambda i, be, nu: (be[last(i, nu)], 0, 0)),
                  pl.BlockSpec((None, D_EXPERT, D), lambda i, be, nu: (be[last(i, nu)], 0, 0))],
        out_specs=pl.BlockSpec((MOE_BLOCK, D), lambda i, be, nu: (last(i, nu), 0)),
    )
    return pl.pallas_call(
        _expert_kernel,
        grid_spec=grid_spec,
        out_shape=jax.ShapeDtypeStruct((n_slots, D), f32),
        compiler_params=pltpu.CompilerParams(dimension_semantics=("arbitrary",),
                                             vmem_limit_bytes=VMEM_LIMIT),
        name="experts",
    )(block_e, n_used, xs, wgu, wd)


def _combine_kernel(dest_ref, y_ref, x1_ref, wtok_ref, mod_ref, o_ref, ybuf, sem):
    TS = x1_ref.shape[0]

    def row_copy(slot, k, i):
        return pltpu.make_async_copy(y_ref.at[pl.ds(slot, 1)], ybuf.at[k, pl.ds(i, 1)], sem)

    def issue(i, carry):
        row_copy(dest_ref[0, i], 0, i).start()
        row_copy(dest_ref[1, i], 1, i).start()
        return carry

    lax.fori_loop(0, TS, issue, 0)

    def drain(i, carry):
        row_copy(0, 0, 0).wait()
        row_copy(0, 1, 0).wait()
        return carry

    lax.fori_loop(0, TS, drain, 0)
    w = wtok_ref[...]
    g2 = mod_ref[5:6, :]
    moe = w[:, 0:1] * ybuf[0] + w[:, 1:2] * ybuf[1]
    o_ref[...] = x1_ref[...] + g2 * moe


def _combine(dest, y, x1, wtok, mod6, T):
    N, D = x1.shape
    TS = min(COMBINE_TILE, T)
    per_seq = T // TS
    return pl.pallas_call(
        _combine_kernel,
        grid=(N // TS,),
        in_specs=[pl.BlockSpec((2, TS), lambda i: (0, i), memory_space=pltpu.SMEM),
                  pl.BlockSpec(memory_space=pl.ANY),
                  pl.BlockSpec((TS, D), lambda i: (i, 0)),
                  pl.BlockSpec((TS, LANES), lambda i: (i, 0)),
                  pl.BlockSpec((None, 6, D), lambda i: (i // per_seq, 0, 0))],
        out_specs=pl.BlockSpec((TS, D), lambda i: (i, 0)),
        out_shape=jax.ShapeDtypeStruct((N, D), f32),
        scratch_shapes=[pltpu.VMEM((2, TS, D), f32), pltpu.SemaphoreType.DMA],
        compiler_params=pltpu.CompilerParams(dimension_semantics=("arbitrary",)),
        name="combine",
    )(dest, y, x1, wtok, mod6)


def _block_diag_mean(n, blk):
    m = np.kron(np.eye(n // blk, dtype=np.float32), np.full((blk, blk), 1.0 / blk, np.float32))
    return jnp.asarray(m, bf16)


def _layer(x, mod6, bias, l, w):
    B, T, D = x.shape
    N = B * T
    w_rg, w_re = w["w_router_group"][l], w["w_router_expert"][l]
    wr = jnp.zeros((D, ROUTER_COLS), f32)
    wr = wr.at[:, 0:N_GROUPS].set(w_rg).at[:, EXPERT_COL0:EXPERT_COL0 + N_EXPERTS].set(w_re)
    br = jnp.zeros((1, ROUTER_COLS), f32)
    br = br.at[0, 0:N_GROUPS].set(w["b_router_group"][l])
    br = br.at[0, EXPERT_COL0:EXPERT_COL0 + N_EXPERTS].set(w["b_router_expert"][l])
    wrh = wr.astype(bf16)
    wrl = (wr - wrh.astype(f32)).astype(bf16)
    p = dict(
        gmix=w["norm_mix_g"][l].reshape(1, D), gffn=w["norm_ffn_g"][l].reshape(1, D),
        w_in=w["w_in"][l].astype(bf16),
        dw=w["dw_kernel"][l], dwb=w["dw_bias"][l].reshape(1, D_CONV),
        lng=w["conv_ln_g"][l].reshape(1, D_CONV), lnb=w["conv_ln_b"][l].reshape(1, D_CONV),
        wco=w["w_conv_out"][l].astype(bf16), wao=w["w_attn_out"][l].astype(bf16),
        wout=w["w_out"][l].astype(bf16),
        qg=(jnp.tile(w["q_norm_g"][l], N_Q_HEADS) * (HEAD_DIM ** -0.5)).reshape(1, D_Q),
        kg=jnp.tile(w["k_norm_g"][l], N_KV_HEADS).reshape(1, D_KV),
        bq=_block_diag_mean(D_Q, HEAD_DIM), bk=_block_diag_mean(D_KV, HEAD_DIM),
        bias=bias, sinks=w["sinks"][l], wrh=wrh, wrl=wrl, br=br,
    )
    x1, h2, route, wtok, cnt = _mixer(x, mod6, p)

    counts = cnt[:, 0].astype(i32)
    pcounts = (counts + MOE_BLOCK - 1) // MOE_BLOCK * MOE_BLOCK
    pend = jnp.cumsum(pcounts)
    pstart = pend - pcounts
    n_blocks = -(-(N * TOP_K) // MOE_BLOCK) + N_EXPERTS
    block_e = jnp.minimum(
        jnp.searchsorted(pend, jnp.arange(n_blocks, dtype=i32) * MOE_BLOCK, side="right"),
        N_EXPERTS - 1).astype(i32)
    n_used = (pend[-1:] // MOE_BLOCK).astype(i32)

    xs, dest = _scatter(pstart, route, h2.reshape(N, D), n_blocks * MOE_BLOCK)
    wgu = jnp.concatenate([w["w_exp_gate"][l], w["w_exp_up"][l]], axis=-1).astype(bf16)
    wd = w["w_exp_down"][l].astype(bf16)
    y = _experts(block_e, n_used, xs, wgu, wd)
    out = _combine(dest, y, x1.reshape(N, D), wtok, mod6, T)
    return out.reshape(B, T, D)


def kernel(x, c, w_ada, b_ada, norm_mix_g, w_in, dw_kernel, dw_bias, conv_ln_g, conv_ln_b,
           w_conv_out, q_norm_g, k_norm_g, sinks, w_attn_out, w_out, rel_bias_table, norm_ffn_g,
           w_router_group, b_router_group, w_router_expert, b_router_expert,
           w_exp_gate, w_exp_up, w_exp_down):
    w = dict(norm_mix_g=norm_mix_g, w_in=w_in, dw_kernel=dw_kernel, dw_bias=dw_bias,
             conv_ln_g=conv_ln_g, conv_ln_b=conv_ln_b, w_conv_out=w_conv_out, q_norm_g=q_norm_g,
             k_norm_g=k_norm_g, sinks=sinks, w_attn_out=w_attn_out, w_out=w_out,
             norm_ffn_g=norm_ffn_g, w_router_group=w_router_group, b_router_group=b_router_group,
             w_router_expert=w_router_expert, b_router_expert=b_router_expert,
             w_exp_gate=w_exp_gate, w_exp_up=w_exp_up, w_exp_down=w_exp_down)
    B = x.shape[0]
    bias = _bias_band(rel_bias_table).reshape(N_KV_HEADS, Q_PER_KV * WINDOW, 2 * WINDOW)
    for l in range(w_ada.shape[0]):
        mod6 = _modulation(c, w_ada[l], b_ada[l]).reshape(B, 6, D_MODEL)
        x = _layer(x, mod6, bias, l, w)
    return x
```

```python
import functools
import math

import jax
import jax.numpy as jnp
import numpy as np
from jax import lax
from jax.experimental import pallas as pl
from jax.experimental.pallas import tpu as pltpu

D_MODEL = 1024
D_CONV = 512
CONV_WIDTH = 31
N_Q_HEADS = 8
N_KV_HEADS = 2
HEAD_DIM = 64
Q_PER_KV = N_Q_HEADS // N_KV_HEADS
WINDOW = 128
N_BUCKETS = 32
MAX_DISTANCE = 128
N_GROUPS = 4
EXPERTS_PER_GROUP = 8
N_EXPERTS = N_GROUPS * EXPERTS_PER_GROUP
TOP_K = 2
D_EXPERT = 256
D_Q = N_Q_HEADS * HEAD_DIM
D_KV = N_KV_HEADS * HEAD_DIM
EPS = 1e-6
NEG_INF = -1e30

LANES = 128
SEQ_TILE = 512
MOE_BLOCK = 512
SCATTER_TILE = 1024
COMBINE_TILE = 256
ROUTER_COLS = LANES
EXPERT_COL0 = 8
VMEM_LIMIT = 56 * 1024 * 1024

f32 = jnp.float32
bf16 = jnp.bfloat16
i32 = jnp.int32


def _dot(a, b):
    return jnp.dot(a, b, preferred_element_type=f32)


def _split(a):
    hi = a.astype(bf16)
    lo = (a - hi.astype(f32)).astype(bf16)
    return hi, lo


def _dot3(a, b):
    ah, al = _split(a)
    bh, bl = _split(b)
    return _dot(ah, bh) + _dot(al, bh) + _dot(ah, bl)


def _mod_kernel(c_ref, w_ref, b_ref, o_ref):
    c = c_ref[...]
    s = c * jax.nn.sigmoid(c)
    o_ref[...] = _dot3(s, w_ref[...]) + b_ref[...]


def _modulation(c, w_ada, b_ada):
    B, D = c.shape
    n_out = w_ada.shape[1]
    return pl.pallas_call(
        _mod_kernel,
        grid=(n_out // D,),
        in_specs=[pl.BlockSpec((B, D), lambda j: (0, 0)),
                  pl.BlockSpec((D, D), lambda j: (0, j)),
                  pl.BlockSpec((1, D), lambda j: (0, j))],
        out_specs=pl.BlockSpec((B, D), lambda j: (0, j)),
        out_shape=jax.ShapeDtypeStruct((B, n_out), f32),
        name="modulation",
    )(c, w_ada, b_ada.reshape(1, n_out))


def _band_buckets():
    qi = np.arange(WINDOW)[:, None]
    kj = np.arange(2 * WINDOW)[None, :]
    dist = np.clip(qi + WINDOW - kj, 0, MAX_DISTANCE)
    max_exact = N_BUCKETS // 2
    d = np.maximum(dist, 1).astype(np.float32)
    large = max_exact + (np.log(d / np.float32(max_exact)) / np.float32(math.log(MAX_DISTANCE / max_exact))
                         * np.float32(N_BUCKETS - max_exact)).astype(np.int32)
    large = np.minimum(large, N_BUCKETS - 1)
    return np.where(dist < max_exact, dist, large).astype(np.int32)


def _bias_kernel(tab_ref, bucket_ref, o_ref):
    bk = bucket_ref[...]
    for h in range(N_Q_HEADS):
        acc = jnp.zeros(bk.shape, f32)
        for b in range(N_BUCKETS):
            acc = jnp.where(bk == b, tab_ref[b, h], acc)
        o_ref[h] = acc


def _bias_band(rel_bias_table):
    return pl.pallas_call(
        _bias_kernel,
        in_specs=[pl.BlockSpec(memory_space=pltpu.SMEM),
                  pl.BlockSpec(memory_space=pltpu.VMEM)],
        out_specs=pl.BlockSpec(memory_space=pltpu.VMEM),
        out_shape=jax.ShapeDtypeStruct((N_Q_HEADS, WINDOW, 2 * WINDOW), f32),
        name="bias_band",
    )(rel_bias_table, jnp.asarray(_band_buckets()))


def _mixer_kernel(x_ref, mod_ref, gmix_ref, gffn_ref, win_ref, dw_ref, dwb_ref, lng_ref, lnb_ref,
                  wco_ref, wao_ref, wout_ref, qg_ref, kg_ref, bq_ref, bk_ref, bias_ref, sink_ref,
                  wrh_ref, wrl_ref, br_ref, tri_ref,
                  x1_ref, h2_ref, route_ref, wtok_ref, cnt_ref,
                  uext, k2, v2, osc):
    TM = x_ref.shape[0]
    HALO = 32
    b = pl.program_id(0)
    t = pl.program_id(1)
    first = t == 0

    @pl.when(first)
    def _():
        uext[0:HALO, :] = jnp.zeros((HALO, D_CONV), f32)
        k2[:, 0:WINDOW, :] = jnp.zeros((N_KV_HEADS, WINDOW, LANES), bf16)
        v2[:, 0:WINDOW, :] = jnp.zeros((N_KV_HEADS, WINDOW, LANES), bf16)

    @pl.when(first & (b == 0))
    def _():
        cnt_ref[...] = jnp.zeros(cnt_ref.shape, f32)

    x = x_ref[...]
    mod = mod_ref[...]
    sh1, sc1, g1 = mod[0:1, :], mod[1:2, :], mod[2:3, :]
    sh2, sc2, g2 = mod[3:4, :], mod[4:5, :], mod[5:6, :]
    del g2

    ms = jnp.mean(x * x, axis=-1, keepdims=True)
    h = (x * lax.rsqrt(ms + EPS)) * gmix_ref[...]
    h = h * (1.0 + sc1) + sh1
    hb = h.astype(bf16)

    ab = _dot(hb, win_ref[:, 0:2 * D_CONV])
    u = ab[:, 0:D_CONV] * jax.nn.sigmoid(ab[:, D_CONV:2 * D_CONV])
    uext[HALO:HALO + TM, :] = u
    acc = jnp.zeros((TM, D_CONV), f32)
    for r in range(8):
        a_list = [a for a in range(5) if 0 <= 8 * a + r - 2 <= CONV_WIDTH - 1]
        ur = uext[r:r + TM + 8 * max(a_list), :]
        for a in a_list:
            j = 8 * a + r - 2
            acc = acc + dw_ref[j:j + 1, :] * ur[8 * a:8 * a + TM, :]
    conv = acc + dwb_ref[...]
    uext[0:HALO, :] = uext[TM:TM + HALO, :]
    mu = jnp.mean(conv, axis=-1, keepdims=True)
    dc = conv - mu
    var = jnp.mean(dc * dc, axis=-1, keepdims=True)
    yn = dc * lax.rsqrt(var + EPS) * lng_ref[...] + lnb_ref[...]
    act = yn * jax.nn.sigmoid(yn)
    y_conv = _dot(act.astype(bf16), wco_ref[...])

    c0 = 2 * D_CONV
    qkv = _dot(hb, win_ref[:, c0:c0 + D_Q + 2 * D_KV])
    q = qkv[:, 0:D_Q]
    k = qkv[:, D_Q:D_Q + D_KV]
    v = qkv[:, D_Q + D_KV:D_Q + 2 * D_KV]

    def head_norm(z, blk_ref, g):
        hi, lo = _split(z * z)
        msq = _dot(hi, blk_ref[...]) + _dot(lo, blk_ref[...])
        return z * lax.rsqrt(msq + EPS) * g

    qn = head_norm(q, bq_ref, qg_ref[...]).astype(bf16)
    kn = head_norm(k, bk_ref, kg_ref[...])
    lo_half = lax.broadcasted_iota(i32, (TM, LANES), 1) < HEAD_DIM
    kr = pltpu.roll(kn, HEAD_DIM, 1)
    vr = pltpu.roll(v, HEAD_DIM, 1)
    k2[0, WINDOW:WINDOW + TM, :] = jnp.where(lo_half, kn, kr).astype(bf16)
    k2[1, WINDOW:WINDOW + TM, :] = jnp.where(lo_half, kr, kn).astype(bf16)
    v2[0, WINDOW:WINDOW + TM, :] = jnp.where(lo_half, v, vr).astype(bf16)
    v2[1, WINDOW:WINDOW + TM, :] = jnp.where(lo_half, vr, v).astype(bf16)

    QROWS = Q_PER_KV * WINDOW
    row = lax.broadcasted_iota(i32, (QROWS, 2 * WINDOW), 0) & (WINDOW - 1)
    col = lax.broadcasted_iota(i32, (QROWS, 2 * WINDOW), 1)
    dist = row + WINDOW - col
    in_window = (dist >= 0) & (dist < WINDOW)
    first_mask = in_window & (col >= jnp.where(first, WINDOW, 0))
    hrow = lax.broadcasted_iota(i32, (QROWS, 1), 0) // WINDOW
    lo128 = lax.broadcasted_iota(i32, (WINDOW, LANES), 1) < HEAD_DIM
    zero_q = jnp.zeros((WINDOW, LANES), bf16)
    for g in range(N_KV_HEADS):
        sink = jnp.where(hrow == 0, sink_ref[4 * g],
                         jnp.where(hrow == 1, sink_ref[4 * g + 1],
                                   jnp.where(hrow == 2, sink_ref[4 * g + 2], sink_ref[4 * g + 3])))
        bias_g = bias_ref[g]
        for j in range(TM // WINDOW):
            rs = slice(j * WINDOW, (j + 1) * WINDOW)
            qa = qn[rs, 2 * LANES * g:2 * LANES * g + LANES]
            qb = qn[rs, 2 * LANES * g + LANES:2 * LANES * (g + 1)]
            qs = jnp.concatenate([jnp.where(lo128, qa, zero_q), jnp.where(lo128, zero_q, qa),
                                  jnp.where(lo128, qb, zero_q), jnp.where(lo128, zero_q, qb)], axis=0)
            kk = k2[g, j * WINDOW:(j + 2) * WINDOW, :]
            s = lax.dot_general(qs, kk, (((1,), (1,)), ((), ())), preferred_element_type=f32)
            msk = first_mask if j == 0 else in_window
            logits = jnp.where(msk, s + bias_g, NEG_INF)
            m = jnp.maximum(jnp.max(logits, axis=-1, keepdims=True), sink)
            p = jnp.exp(logits - m)
            den = jnp.sum(p, axis=-1, keepdims=True) + jnp.exp(sink - m)
            o2 = _dot(p.astype(bf16), v2[g, j * WINDOW:(j + 2) * WINDOW, :]) * (1.0 / den)
            osc[rs, 2 * LANES * g:2 * LANES * g + LANES] = jnp.where(
                lo128, o2[0:WINDOW], o2[WINDOW:2 * WINDOW]).astype(bf16)
            osc[rs, 2 * LANES * g + LANES:2 * LANES * (g + 1)] = jnp.where(
                lo128, o2[2 * WINDOW:3 * WINDOW], o2[3 * WINDOW:4 * WINDOW]).astype(bf16)
    k2[:, 0:WINDOW, :] = k2[:, TM:TM + WINDOW, :]
    v2[:, 0:WINDOW, :] = v2[:, TM:TM + WINDOW, :]
    y_attn = _dot(osc[...], wao_ref[...])

    c1 = c0 + D_Q + 2 * D_KV
    g_conv = jax.nn.sigmoid(_dot(hb, win_ref[:, c1:c1 + D_MODEL]))
    g_attn = jax.nn.sigmoid(_dot(hb, win_ref[:, c1 + D_MODEL:c1 + 2 * D_MODEL]))
    merged = (g_conv * y_conv + g_attn * y_attn).astype(bf16)
    x1 = x + g1 * _dot(merged, wout_ref[...])
    x1_ref[...] = x1

    ms2 = jnp.mean(x1 * x1, axis=-1, keepdims=True)
    h2 = (x1 * lax.rsqrt(ms2 + EPS)) * gffn_ref[...]
    h2 = h2 * (1.0 + sc2) + sh2
    h2_ref[...] = h2
    hh, hl = _split(h2)
    lg = _dot(hh, wrh_ref[...]) + _dot(hl, wrh_ref[...]) + _dot(hh, wrl_ref[...]) + br_ref[...]
    lt = lg.T
    gl = lt[0:N_GROUPS, :]
    grow = lax.broadcasted_iota(i32, (N_GROUPS, TM), 0)
    gmax = jnp.max(gl, axis=0, keepdims=True)
    gi = jnp.min(jnp.where(gl == gmax, grow, N_GROUPS), axis=0, keepdims=True)
    p_top = 1.0 / jnp.sum(jnp.exp(gl - gmax), axis=0, keepdims=True)
    sel = lt[EXPERT_COL0:EXPERT_COL0 + EXPERTS_PER_GROUP, :]
    for gg in range(1, N_GROUPS):
        lo_r = EXPERT_COL0 + gg * EXPERTS_PER_GROUP
        sel = jnp.where(gi == gg, lt[lo_r:lo_r + EXPERTS_PER_GROUP, :], sel)
    erow = lax.broadcasted_iota(i32, (EXPERTS_PER_GROUP, TM), 0)
    m1 = jnp.max(sel, axis=0, keepdims=True)
    i1 = jnp.min(jnp.where(sel == m1, erow, EXPERTS_PER_GROUP), axis=0, keepdims=True)
    rest = jnp.where(erow == i1, -jnp.inf, sel)
    m2 = jnp.max(rest, axis=0, keepdims=True)
    i2 = jnp.min(jnp.where(rest == m2, erow, EXPERTS_PER_GROUP), axis=0, keepdims=True)
    z = jnp.sum(jnp.exp(sel - m1), axis=0, keepdims=True)
    v1 = 1.0 / z
    v2nd = jnp.exp(m2 - m1) / z
    w1 = v1 / (v1 + v2nd) * p_top
    w2 = v2nd / (v1 + v2nd) * p_top
    e1 = gi * EXPERTS_PER_GROUP + i1
    e2 = gi * EXPERTS_PER_GROUP + i2

    xrow = lax.broadcasted_iota(i32, (N_EXPERTS, TM), 0)
    oh1 = xrow == e1
    oh2 = xrow == e2
    both = jnp.where(oh1 | oh2, 1.0, 0.0)
    prefix = _dot(both.astype(bf16), tri_ref[...]) + cnt_ref[:, 0:1]
    r1 = jnp.sum(jnp.where(oh1, prefix, 0.0), axis=0, keepdims=True)
    r2 = jnp.sum(jnp.where(oh2, prefix, 0.0), axis=0, keepdims=True)
    cnt_ref[...] = cnt_ref[...] + jnp.sum(both, axis=1, keepdims=True)

    route_ref[...] = jnp.concatenate(
        [e1, e2, r1.astype(i32), r2.astype(i32), jnp.zeros((4, TM), i32)], axis=0)
    wpad = jnp.concatenate([w1, w2, jnp.zeros((LANES - 2, TM), f32)], axis=0)
    wtok_ref[...] = wpad.T


def _mixer(x, mod6, p):
    B, T, D = x.shape
    TM = min(SEQ_TILE, T)
    nt = T // TM
    N = B * T
    const = lambda shape: pl.BlockSpec(shape, lambda b, t: (0,) * len(shape))
    in_specs = [
        pl.BlockSpec((None, TM, D), lambda b, t: (b, t, 0)),
        pl.BlockSpec((None, 6, D), lambda b, t: (b, 0, 0)),
        const((1, D)), const((1, D)),
        const(p["w_in"].shape),
        const((CONV_WIDTH, D_CONV)), const((1, D_CONV)), const((1, D_CONV)), const((1, D_CONV)),
        const((D_CONV, D)), const((D_Q, D)), const((D, D)),
        const((1, D_Q)), const((1, D_KV)),
        const((D_Q, D_Q)), const((D_KV, D_KV)),
        const((N_KV_HEADS, Q_PER_KV * WINDOW, 2 * WINDOW)),
        pl.BlockSpec(memory_space=pltpu.SMEM),
        const((D, ROUTER_COLS)), const((D, ROUTER_COLS)), const((1, ROUTER_COLS)),
        const((TM, TM)),
    ]
    out_specs = [
        pl.BlockSpec((None, TM, D), lambda b, t: (b, t, 0)),
        pl.BlockSpec((None, TM, D), lambda b, t: (b, t, 0)),
        pl.BlockSpec((8, TM), lambda b, t: (0, b * nt + t)),
        pl.BlockSpec((TM, LANES), lambda b, t: (b * nt + t, 0)),
        pl.BlockSpec((N_EXPERTS, LANES), lambda b, t: (0, 0)),
    ]
    out_shape = [
        jax.ShapeDtypeStruct((B, T, D), f32),
        jax.ShapeDtypeStruct((B, T, D), f32),
        jax.ShapeDtypeStruct((8, N), i32),
        jax.ShapeDtypeStruct((N, LANES), f32),
        jax.ShapeDtypeStruct((N_EXPERTS, LANES), f32),
    ]
    scratch = [
        pltpu.VMEM((TM + 32, D_CONV), f32),
        pltpu.VMEM((N_KV_HEADS, TM + WINDOW, LANES), bf16),
        pltpu.VMEM((N_KV_HEADS, TM + WINDOW, LANES), bf16),
        pltpu.VMEM((TM, D_Q), bf16),
    ]
    tri = jnp.asarray(np.triu(np.ones((TM, TM), np.float32), 1), bf16)
    return pl.pallas_call(
        _mixer_kernel,
        grid=(B, nt),
        in_specs=in_specs, out_specs=out_specs, out_shape=out_shape, scratch_shapes=scratch,
        compiler_params=pltpu.CompilerParams(
            dimension_semantics=("arbitrary", "arbitrary"), vmem_limit_bytes=VMEM_LIMIT),
        name="mixer_router",
    )(x, mod6, p["gmix"], p["gffn"], p["w_in"], p["dw"], p["dwb"], p["lng"], p["lnb"],
      p["wco"], p["wao"], p["wout"], p["qg"], p["kg"], p["bq"], p["bk"], p["bias"], p["sinks"],
      p["wrh"], p["wrl"], p["br"], tri)


def _scatter_kernel(pstart_ref, route_ref, h2_ref, xs_ref, dest_ref, sem):
    TS = h2_ref.shape[0]

    def row_copy(i, slot):
        return pltpu.make_async_copy(h2_ref.at[pl.ds(i, 1)], xs_ref.at[pl.ds(slot, 1)], sem)

    def issue(i, carry):
        d1 = pstart_ref[route_ref[0, i]] + route_ref[2, i]
        d2 = pstart_ref[route_ref[1, i]] + route_ref[3, i]
        row_copy(i, d1).start()
        row_copy(i, d2).start()
        dest_ref[0, i] = d1
        dest_ref[1, i] = d2
        return carry

    lax.fori_loop(0, TS, issue, 0)

    def drain(i, carry):
        row_copy(0, 0).wait()
        row_copy(0, 0).wait()
        return carry

    lax.fori_loop(0, TS, drain, 0)


def _scatter(pstart, route, h2, n_slots):
    N, D = h2.shape
    TS = min(SCATTER_TILE, N)
    grid_spec = pltpu.PrefetchScalarGridSpec(
        num_scalar_prefetch=1,
        grid=(N // TS,),
        in_specs=[pl.BlockSpec((8, TS), lambda i, ps: (0, i), memory_space=pltpu.SMEM),
                  pl.BlockSpec((TS, D), lambda i, ps: (i, 0))],
        out_specs=[pl.BlockSpec(memory_space=pl.ANY),
                   pl.BlockSpec((2, TS), lambda i, ps: (0, i), memory_space=pltpu.SMEM)],
        scratch_shapes=[pltpu.SemaphoreType.DMA],
    )
    return pl.pallas_call(
        _scatter_kernel,
        grid_spec=grid_spec,
        out_shape=[jax.ShapeDtypeStruct((n_slots, D), f32),
                   jax.ShapeDtypeStruct((2, N), i32)],
        compiler_params=pltpu.CompilerParams(dimension_semantics=("arbitrary",)),
        name="dispatch_scatter",
    )(pstart, route, h2)


def _expert_kernel(be_ref, nv_ref, nu_ref, xs_ref, wgu_ref, wd_ref, y_ref):
    i = pl.program_id(0)

    @pl.when(i < nu_ref[0])
    def _():
        live = lax.broadcasted_iota(i32, (MOE_BLOCK, 1), 0) < nv_ref[i]
        xb = jnp.where(live, xs_ref[...], 0.0).astype(bf16)
        gu = _dot(xb, wgu_ref[...])
        gate = gu[:, 0:D_EXPERT]
        hid = (gate * jax.nn.sigmoid(gate)) * gu[:, D_EXPERT:2 * D_EXPERT]
        y_ref[...] = _dot(hid.astype(bf16), wd_ref[...])


def _experts(block_e, n_valid, n_used, xs, wgu, wd):
    n_slots, D = xs.shape
    n_blocks = n_slots // MOE_BLOCK
    last = lambda i, nu: jnp.minimum(i, nu[0] - 1)
    grid_spec = pltpu.PrefetchScalarGridSpec(
        num_scalar_prefetch=3,
        grid=(n_blocks,),
        in_specs=[pl.BlockSpec((MOE_BLOCK, D), lambda i, be, nv, nu: (last(i, nu), 0)),
                  pl.BlockSpec((None, D, 2 * D_EXPERT), lambda i, be, nv, nu: (be[last(i, nu)], 0, 0)),
                  pl.BlockSpec((None, D_EXPERT, D), lambda i, be, nv, nu: (be[last(i, nu)], 0, 0))],
        out_specs=pl.BlockSpec((MOE_BLOCK, D), lambda i, be, nv, nu: (last(i, nu), 0)),
    )
    return pl.pallas_call(
        _expert_kernel,
        grid_spec=grid_spec,
        out_shape=jax.ShapeDtypeStruct((n_slots, D), f32),
        compiler_params=pltpu.CompilerParams(dimension_semantics=("arbitrary",),
                                             vmem_limit_bytes=VMEM_LIMIT),
        name="experts",
    )(block_e, n_valid, n_used, xs, wgu, wd)


def _combine_kernel(dest_ref, y_ref, x1_ref, wtok_ref, mod_ref, o_ref, ybuf, sem):
    TS = x1_ref.shape[0]

    def row_copy(slot, k, i):
        return pltpu.make_async_copy(y_ref.at[pl.ds(slot, 1)], ybuf.at[k, pl.ds(i, 1)], sem)

    def issue(i, carry):
        row_copy(dest_ref[0, i], 0, i).start()
        row_copy(dest_ref[1, i], 1, i).start()
        return carry

    lax.fori_loop(0, TS, issue, 0)

    def drain(i, carry):
        row_copy(0, 0, 0).wait()
        row_copy(0, 1, 0).wait()
        return carry

    lax.fori_loop(0, TS, drain, 0)
    w = wtok_ref[...]
    g2 = mod_ref[5:6, :]
    moe = w[:, 0:1] * ybuf[0] + w[:, 1:2] * ybuf[1]
    o_ref[...] = x1_ref[...] + g2 * moe


def _combine(dest, y, x1, wtok, mod6, T):
    N, D = x1.shape
    TS = min(COMBINE_TILE, T)
    per_seq = T // TS
    return pl.pallas_call(
        _combine_kernel,
        grid=(N // TS,),
        in_specs=[pl.BlockSpec((2, TS), lambda i: (0, i), memory_space=pltpu.SMEM),
                  pl.BlockSpec(memory_space=pl.ANY),
                  pl.BlockSpec((TS, D), lambda i: (i, 0)),
                  pl.BlockSpec((TS, LANES), lambda i: (i, 0)),
                  pl.BlockSpec((None, 6, D), lambda i: (i // per_seq, 0, 0))],
        out_specs=pl.BlockSpec((TS, D), lambda i: (i, 0)),
        out_shape=jax.ShapeDtypeStruct((N, D), f32),
        scratch_shapes=[pltpu.VMEM((2, TS, D), f32), pltpu.SemaphoreType.DMA],
        compiler_params=pltpu.CompilerParams(dimension_semantics=("arbitrary",)),
        name="combine",
    )(dest, y, x1, wtok, mod6)


def _block_diag_mean(n, blk):
    m = np.kron(np.eye(n // blk, dtype=np.float32), np.full((blk, blk), 1.0 / blk, np.float32))
    return jnp.asarray(m, bf16)


def _layer(x, mod6, bias, l, w):
    B, T, D = x.shape
    N = B * T
    w_rg, w_re = w["w_router_group"][l], w["w_router_expert"][l]
    wr = jnp.zeros((D, ROUTER_COLS), f32)
    wr = wr.at[:, 0:N_GROUPS].set(w_rg).at[:, EXPERT_COL0:EXPERT_COL0 + N_EXPERTS].set(w_re)
    br = jnp.zeros((1, ROUTER_COLS), f32)
    br = br.at[0, 0:N_GROUPS].set(w["b_router_group"][l])
    br = br.at[0, EXPERT_COL0:EXPERT_COL0 + N_EXPERTS].set(w["b_router_expert"][l])
    wrh = wr.astype(bf16)
    wrl = (wr - wrh.astype(f32)).astype(bf16)
    p = dict(
        gmix=w["norm_mix_g"][l].reshape(1, D), gffn=w["norm_ffn_g"][l].reshape(1, D),
        w_in=w["w_in"][l].astype(bf16),
        dw=w["dw_kernel"][l], dwb=w["dw_bias"][l].reshape(1, D_CONV),
        lng=w["conv_ln_g"][l].reshape(1, D_CONV), lnb=w["conv_ln_b"][l].reshape(1, D_CONV),
        wco=w["w_conv_out"][l].astype(bf16), wao=w["w_attn_out"][l].astype(bf16),
        wout=w["w_out"][l].astype(bf16),
        qg=(jnp.tile(w["q_norm_g"][l], N_Q_HEADS) * (HEAD_DIM ** -0.5)).reshape(1, D_Q),
        kg=jnp.tile(w["k_norm_g"][l], N_KV_HEADS).reshape(1, D_KV),
        bq=_block_diag_mean(D_Q, HEAD_DIM), bk=_block_diag_mean(D_KV, HEAD_DIM),
        bias=bias, sinks=w["sinks"][l], wrh=wrh, wrl=wrl, br=br,
    )
    x1, h2, route, wtok, cnt = _mixer(x, mod6, p)

    counts = cnt[:, 0].astype(i32)
    pcounts = (counts + MOE_BLOCK - 1) // MOE_BLOCK * MOE_BLOCK
    pend = jnp.cumsum(pcounts)
    pstart = pend - pcounts
    n_blocks = -(-(N * TOP_K) // MOE_BLOCK) + N_EXPERTS
    blk0 = jnp.arange(n_blocks, dtype=i32) * MOE_BLOCK
    block_e = jnp.minimum(jnp.sum((pend[None, :] <= blk0[:, None]).astype(i32), axis=1), N_EXPERTS - 1)
    n_valid = jnp.clip((pstart + counts)[block_e] - blk0, 0, MOE_BLOCK).astype(i32)
    n_used = (pend[-1:] // MOE_BLOCK).astype(i32)

    xs, dest = _scatter(pstart, route, h2.reshape(N, D), n_blocks * MOE_BLOCK)
    wgu = jnp.concatenate([w["w_exp_gate"][l], w["w_exp_up"][l]], axis=-1).astype(bf16)
    wd = w["w_exp_down"][l].astype(bf16)
    y = _experts(block_e, n_valid, n_used, xs, wgu, wd)
    out = _combine(dest, y, x1.reshape(N, D), wtok, mod6, T)
    return out.reshape(B, T, D)


def kernel(x, c, w_ada, b_ada, norm_mix_g, w_in, dw_kernel, dw_bias, conv_ln_g, conv_ln_b,
           w_conv_out, q_norm_g, k_norm_g, sinks, w_attn_out, w_out, rel_bias_table, norm_ffn_g,
           w_router_group, b_router_group, w_router_expert, b_router_expert,
           w_exp_gate, w_exp_up, w_exp_down):
    w = dict(norm_mix_g=norm_mix_g, w_in=w_in, dw_kernel=dw_kernel, dw_bias=dw_bias,
             conv_ln_g=conv_ln_g, conv_ln_b=conv_ln_b, w_conv_out=w_conv_out, q_norm_g=q_norm_g,
             k_norm_g=k_norm_g, sinks=sinks, w_attn_out=w_attn_out, w_out=w_out,
             norm_ffn_g=norm_ffn_g, w_router_group=w_router_group, b_router_group=b_router_group,
             w_router_expert=w_router_expert, b_router_expert=b_router_expert,
             w_exp_gate=w_exp_gate, w_exp_up=w_exp_up, w_exp_down=w_exp_down)
    B = x.shape[0]
    bias = _bias_band(rel_bias_table).reshape(N_KV_HEADS, Q_PER_KV * WINDOW, 2 * WINDOW)
    for l in range(w_ada.shape[0]):
        mod6 = _modulation(c, w_ada[l], b_ada[l]).reshape(B, 6, D_MODEL)
        x = _layer(x, mod6, bias, l, w)
    return x
```

```python
import functools
import math

import jax
import jax.numpy as jnp
import numpy as np
from jax import lax
from jax.experimental import pallas as pl
from jax.experimental.pallas import tpu as pltpu
from jax.experimental.pallas import tpu_sc as plsc

D_MODEL = 1024
D_CONV = 512
CONV_WIDTH = 31
N_Q_HEADS = 8
N_KV_HEADS = 2
HEAD_DIM = 64
Q_PER_KV = N_Q_HEADS // N_KV_HEADS
WINDOW = 128
N_BUCKETS = 32
MAX_DISTANCE = 128
N_GROUPS = 4
EXPERTS_PER_GROUP = 8
N_EXPERTS = N_GROUPS * EXPERTS_PER_GROUP
TOP_K = 2
D_EXPERT = 256
D_Q = N_Q_HEADS * HEAD_DIM
D_KV = N_KV_HEADS * HEAD_DIM
EPS = 1e-6
NEG_INF = -1e30

LANES = 128
SEQ_TILE = 512
MOE_BLOCK = 512
DEST_TILE = 8192
COMBINE_TILE = 512
SC_CORES = 2
SC_SUBCORES = 16
SC_WORKERS = SC_CORES * SC_SUBCORES
SC_CHUNK = 32
ROUTER_COLS = LANES
EXPERT_COL0 = 8
VMEM_LIMIT = 56 * 1024 * 1024

f32 = jnp.float32
bf16 = jnp.bfloat16
i32 = jnp.int32


def _dot(a, b):
    return jnp.dot(a, b, preferred_element_type=f32)


def _split(a):
    hi = a.astype(bf16)
    lo = (a - hi.astype(f32)).astype(bf16)
    return hi, lo


def _dot3(a, b):
    ah, al = _split(a)
    bh, bl = _split(b)
    return _dot(ah, bh) + _dot(al, bh) + _dot(ah, bl)


def _mod_kernel(c_ref, w_ref, b_ref, o_ref):
    c = c_ref[...]
    s = c * jax.nn.sigmoid(c)
    o_ref[...] = _dot3(s, w_ref[...]) + b_ref[...]


def _modulation(c, w_ada, b_ada):
    B, D = c.shape
    n_out = w_ada.shape[1]
    return pl.pallas_call(
        _mod_kernel,
        grid=(n_out // D,),
        in_specs=[pl.BlockSpec((B, D), lambda j: (0, 0)),
                  pl.BlockSpec((D, D), lambda j: (0, j)),
                  pl.BlockSpec((1, D), lambda j: (0, j))],
        out_specs=pl.BlockSpec((B, D), lambda j: (0, j)),
        out_shape=jax.ShapeDtypeStruct((B, n_out), f32),
        name="modulation",
    )(c, w_ada, b_ada.reshape(1, n_out))


def _band_buckets():
    qi = np.arange(WINDOW)[:, None]
    kj = np.arange(2 * WINDOW)[None, :]
    dist = np.clip(qi + WINDOW - kj, 0, MAX_DISTANCE)
    max_exact = N_BUCKETS // 2
    d = np.maximum(dist, 1).astype(np.float32)
    large = max_exact + (np.log(d / np.float32(max_exact)) / np.float32(math.log(MAX_DISTANCE / max_exact))
                         * np.float32(N_BUCKETS - max_exact)).astype(np.int32)
    large = np.minimum(large, N_BUCKETS - 1)
    return np.where(dist < max_exact, dist, large).astype(np.int32)


def _bias_kernel(tab_ref, bucket_ref, o_ref):
    bk = bucket_ref[...]
    for h in range(N_Q_HEADS):
        acc = jnp.zeros(bk.shape, f32)
        for b in range(N_BUCKETS):
            acc = jnp.where(bk == b, tab_ref[b, h], acc)
        o_ref[h] = acc


def _bias_band(rel_bias_table):
    return pl.pallas_call(
        _bias_kernel,
        in_specs=[pl.BlockSpec(memory_space=pltpu.SMEM),
                  pl.BlockSpec(memory_space=pltpu.VMEM)],
        out_specs=pl.BlockSpec(memory_space=pltpu.VMEM),
        out_shape=jax.ShapeDtypeStruct((N_Q_HEADS, WINDOW, 2 * WINDOW), f32),
        name="bias_band",
    )(rel_bias_table, jnp.asarray(_band_buckets()))


def _mixer_kernel(x_ref, mod_ref, gmix_ref, gffn_ref, win_ref, dw_ref, dwb_ref, lng_ref, lnb_ref,
                  wco_ref, wao_ref, wout_ref, qg_ref, kg_ref, bq_ref, bk_ref, bias_ref, sink_ref,
                  wrh_ref, wrl_ref, br_ref, tri_ref,
                  x1_ref, h2_ref, route_ref, wtok_ref, cnt_ref,
                  uext, k2, v2, osc):
    TM = x_ref.shape[0]
    HALO = 32
    b = pl.program_id(0)
    t = pl.program_id(1)
    first = t == 0

    @pl.when(first)
    def _():
        uext[0:HALO, :] = jnp.zeros((HALO, D_CONV), f32)
        k2[:, 0:WINDOW, :] = jnp.zeros((N_KV_HEADS, WINDOW, LANES), bf16)
        v2[:, 0:WINDOW, :] = jnp.zeros((N_KV_HEADS, WINDOW, LANES), bf16)

    @pl.when(first & (b == 0))
    def _():
        cnt_ref[...] = jnp.zeros(cnt_ref.shape, f32)

    x = x_ref[...]
    mod = mod_ref[...]
    sh1, sc1, g1 = mod[0:1, :], mod[1:2, :], mod[2:3, :]
    sh2, sc2, g2 = mod[3:4, :], mod[4:5, :], mod[5:6, :]
    del g2

    ms = jnp.mean(x * x, axis=-1, keepdims=True)
    h = (x * lax.rsqrt(ms + EPS)) * gmix_ref[...]
    h = h * (1.0 + sc1) + sh1
    hb = h.astype(bf16)

    ab = _dot(hb, win_ref[:, 0:2 * D_CONV])
    u = ab[:, 0:D_CONV] * jax.nn.sigmoid(ab[:, D_CONV:2 * D_CONV])
    uext[HALO:HALO + TM, :] = u
    acc = jnp.zeros((TM, D_CONV), f32)
    for r in range(8):
        a_list = [a for a in range(5) if 0 <= 8 * a + r - 2 <= CONV_WIDTH - 1]
        ur = uext[r:r + TM + 8 * max(a_list), :]
        for a in a_list:
            j = 8 * a + r - 2
            acc = acc + dw_ref[j:j + 1, :] * ur[8 * a:8 * a + TM, :]
    conv = acc + dwb_ref[...]
    uext[0:HALO, :] = uext[TM:TM + HALO, :]
    mu = jnp.mean(conv, axis=-1, keepdims=True)
    dc = conv - mu
    var = jnp.mean(dc * dc, axis=-1, keepdims=True)
    yn = dc * lax.rsqrt(var + EPS) * lng_ref[...] + lnb_ref[...]
    act = yn * jax.nn.sigmoid(yn)
    y_conv = _dot(act.astype(bf16), wco_ref[...])

    c0 = 2 * D_CONV
    qkv = _dot(hb, win_ref[:, c0:c0 + D_Q + 2 * D_KV])
    q = qkv[:, 0:D_Q]
    k = qkv[:, D_Q:D_Q + D_KV]
    v = qkv[:, D_Q + D_KV:D_Q + 2 * D_KV]

    def head_norm(z, blk_ref, g):
        hi, lo = _split(z * z)
        msq = _dot(hi, blk_ref[...]) + _dot(lo, blk_ref[...])
        return z * lax.rsqrt(msq + EPS) * g

    qn = head_norm(q, bq_ref, qg_ref[...]).astype(bf16)
    kn = head_norm(k, bk_ref, kg_ref[...])
    lo_half = lax.broadcasted_iota(i32, (TM, LANES), 1) < HEAD_DIM
    kr = pltpu.roll(kn, HEAD_DIM, 1)
    vr = pltpu.roll(v, HEAD_DIM, 1)
    k2[0, WINDOW:WINDOW + TM, :] = jnp.where(lo_half, kn, kr).astype(bf16)
    k2[1, WINDOW:WINDOW + TM, :] = jnp.where(lo_half, kr, kn).astype(bf16)
    v2[0, WINDOW:WINDOW + TM, :] = jnp.where(lo_half, v, vr).astype(bf16)
    v2[1, WINDOW:WINDOW + TM, :] = jnp.where(lo_half, vr, v).astype(bf16)

    QROWS = Q_PER_KV * WINDOW
    row = lax.broadcasted_iota(i32, (QROWS, 2 * WINDOW), 0) & (WINDOW - 1)
    col = lax.broadcasted_iota(i32, (QROWS, 2 * WINDOW), 1)
    dist = row + WINDOW - col
    in_window = (dist >= 0) & (dist < WINDOW)
    first_mask = in_window & (col >= jnp.where(first, WINDOW, 0))
    hrow = lax.broadcasted_iota(i32, (QROWS, 1), 0) // WINDOW
    lo128 = lax.broadcasted_iota(i32, (WINDOW, LANES), 1) < HEAD_DIM
    zero_q = jnp.zeros((WINDOW, LANES), bf16)
    for g in range(N_KV_HEADS):
        sink = jnp.where(hrow == 0, sink_ref[4 * g],
                         jnp.where(hrow == 1, sink_ref[4 * g + 1],
                                   jnp.where(hrow == 2, sink_ref[4 * g + 2], sink_ref[4 * g + 3])))
        bias_g = bias_ref[g]
        for j in range(TM // WINDOW):
            rs = slice(j * WINDOW, (j + 1) * WINDOW)
            qa = qn[rs, 2 * LANES * g:2 * LANES * g + LANES]
            qb = qn[rs, 2 * LANES * g + LANES:2 * LANES * (g + 1)]
            qs = jnp.concatenate([jnp.where(lo128, qa, zero_q), jnp.where(lo128, zero_q, qa),
                                  jnp.where(lo128, qb, zero_q), jnp.where(lo128, zero_q, qb)], axis=0)
            kk = k2[g, j * WINDOW:(j + 2) * WINDOW, :]
            s = lax.dot_general(qs, kk, (((1,), (1,)), ((), ())), preferred_element_type=f32)
            msk = first_mask if j == 0 else in_window
            logits = jnp.where(msk, s + bias_g, NEG_INF)
            m = jnp.maximum(jnp.max(logits, axis=-1, keepdims=True), sink)
            p = jnp.exp(logits - m)
            den = jnp.sum(p, axis=-1, keepdims=True) + jnp.exp(sink - m)
            o2 = _dot(p.astype(bf16), v2[g, j * WINDOW:(j + 2) * WINDOW, :]) * (1.0 / den)
            osc[rs, 2 * LANES * g:2 * LANES * g + LANES] = jnp.where(
                lo128, o2[0:WINDOW], o2[WINDOW:2 * WINDOW]).astype(bf16)
            osc[rs, 2 * LANES * g + LANES:2 * LANES * (g + 1)] = jnp.where(
                lo128, o2[2 * WINDOW:3 * WINDOW], o2[3 * WINDOW:4 * WINDOW]).astype(bf16)
    k2[:, 0:WINDOW, :] = k2[:, TM:TM + WINDOW, :]
    v2[:, 0:WINDOW, :] = v2[:, TM:TM + WINDOW, :]
    y_attn = _dot(osc[...], wao_ref[...])

    c1 = c0 + D_Q + 2 * D_KV
    g_conv = jax.nn.sigmoid(_dot(hb, win_ref[:, c1:c1 + D_MODEL]))
    g_attn = jax.nn.sigmoid(_dot(hb, win_ref[:, c1 + D_MODEL:c1 + 2 * D_MODEL]))
    merged = (g_conv * y_conv + g_attn * y_attn).astype(bf16)
    x1 = x + g1 * _dot(merged, wout_ref[...])
    x1_ref[...] = x1

    ms2 = jnp.mean(x1 * x1, axis=-1, keepdims=True)
    h2 = (x1 * lax.rsqrt(ms2 + EPS)) * gffn_ref[...]
    h2 = h2 * (1.0 + sc2) + sh2
    h2_ref[...] = h2
    hh, hl = _split(h2)
    lg = _dot(hh, wrh_ref[...]) + _dot(hl, wrh_ref[...]) + _dot(hh, wrl_ref[...]) + br_ref[...]
    lt = lg.T
    gl = lt[0:N_GROUPS, :]
    grow = lax.broadcasted_iota(i32, (N_GROUPS, TM), 0)
    gmax = jnp.max(gl, axis=0, keepdims=True)
    gi = jnp.min(jnp.where(gl == gmax, grow, N_GROUPS), axis=0, keepdims=True)
    p_top = 1.0 / jnp.sum(jnp.exp(gl - gmax), axis=0, keepdims=True)
    sel = lt[EXPERT_COL0:EXPERT_COL0 + EXPERTS_PER_GROUP, :]
    for gg in range(1, N_GROUPS):
        lo_r = EXPERT_COL0 + gg * EXPERTS_PER_GROUP
        sel = jnp.where(gi == gg, lt[lo_r:lo_r + EXPERTS_PER_GROUP, :], sel)
    erow = lax.broadcasted_iota(i32, (EXPERTS_PER_GROUP, TM), 0)
    m1 = jnp.max(sel, axis=0, keepdims=True)
    i1 = jnp.min(jnp.where(sel == m1, erow, EXPERTS_PER_GROUP), axis=0, keepdims=True)
    rest = jnp.where(erow == i1, -jnp.inf, sel)
    m2 = jnp.max(rest, axis=0, keepdims=True)
    i2 = jnp.min(jnp.where(rest == m2, erow, EXPERTS_PER_GROUP), axis=0, keepdims=True)
    z = jnp.sum(jnp.exp(sel - m1), axis=0, keepdims=True)
    v1 = 1.0 / z
    v2nd = jnp.exp(m2 - m1) / z
    w1 = v1 / (v1 + v2nd) * p_top
    w2 = v2nd / (v1 + v2nd) * p_top
    e1 = gi * EXPERTS_PER_GROUP + i1
    e2 = gi * EXPERTS_PER_GROUP + i2

    xrow = lax.broadcasted_iota(i32, (N_EXPERTS, TM), 0)
    oh1 = xrow == e1
    oh2 = xrow == e2
    both = jnp.where(oh1 | oh2, 1.0, 0.0)
    prefix = _dot(both.astype(bf16), tri_ref[...]) + cnt_ref[:, 0:1]
    r1 = jnp.sum(jnp.where(oh1, prefix, 0.0), axis=0, keepdims=True)
    r2 = jnp.sum(jnp.where(oh2, prefix, 0.0), axis=0, keepdims=True)
    cnt_ref[...] = cnt_ref[...] + jnp.sum(both, axis=1, keepdims=True)

    route_ref[...] = jnp.concatenate(
        [e1, e2, r1.astype(i32), r2.astype(i32), jnp.zeros((4, TM), i32)], axis=0)
    wpad = jnp.concatenate([w1, w2, jnp.zeros((LANES - 2, TM), f32)], axis=0)
    wtok_ref[...] = wpad.T


def _mixer(x, mod6, p):
    B, T, D = x.shape
    TM = min(SEQ_TILE, T)
    nt = T // TM
    N = B * T
    const = lambda shape: pl.BlockSpec(shape, lambda b, t: (0,) * len(shape))
    in_specs = [
        pl.BlockSpec((None, TM, D), lambda b, t: (b, t, 0)),
        pl.BlockSpec((None, 6, D), lambda b, t: (b, 0, 0)),
        const((1, D)), const((1, D)),
        const(p["w_in"].shape),
        const((CONV_WIDTH, D_CONV)), const((1, D_CONV)), const((1, D_CONV)), const((1, D_CONV)),
        const((D_CONV, D)), const((D_Q, D)), const((D, D)),
        const((1, D_Q)), const((1, D_KV)),
        const((D_Q, D_Q)), const((D_KV, D_KV)),
        const((N_KV_HEADS, Q_PER_KV * WINDOW, 2 * WINDOW)),
        pl.BlockSpec(memory_space=pltpu.SMEM),
        const((D, ROUTER_COLS)), const((D, ROUTER_COLS)), const((1, ROUTER_COLS)),
        const((TM, TM)),
    ]
    out_specs = [
        pl.BlockSpec((None, TM, D), lambda b, t: (b, t, 0)),
        pl.BlockSpec((None, TM, D), lambda b, t: (b, t, 0)),
        pl.BlockSpec((8, TM), lambda b, t: (0, b * nt + t)),
        pl.BlockSpec((TM, LANES), lambda b, t: (b * nt + t, 0)),
        pl.BlockSpec((N_EXPERTS, LANES), lambda b, t: (0, 0)),
    ]
    out_shape = [
        jax.ShapeDtypeStruct((B, T, D), f32),
        jax.ShapeDtypeStruct((B, T, D), f32),
        jax.ShapeDtypeStruct((8, N), i32),
        jax.ShapeDtypeStruct((N, LANES), f32),
        jax.ShapeDtypeStruct((N_EXPERTS, LANES), f32),
    ]
    scratch = [
        pltpu.VMEM((TM + 32, D_CONV), f32),
        pltpu.VMEM((N_KV_HEADS, TM + WINDOW, LANES), bf16),
        pltpu.VMEM((N_KV_HEADS, TM + WINDOW, LANES), bf16),
        pltpu.VMEM((TM, D_Q), bf16),
    ]
    tri = jnp.asarray(np.triu(np.ones((TM, TM), np.float32), 1), bf16)
    return pl.pallas_call(
        _mixer_kernel,
        grid=(B, nt),
        in_specs=in_specs, out_specs=out_specs, out_shape=out_shape, scratch_shapes=scratch,
        compiler_params=pltpu.CompilerParams(
            dimension_semantics=("arbitrary", "arbitrary"), vmem_limit_bytes=VMEM_LIMIT),
        name="mixer_router",
    )(x, mod6, p["gmix"], p["gffn"], p["w_in"], p["dw"], p["dwb"], p["lng"], p["lnb"],
      p["wco"], p["wao"], p["wout"], p["qg"], p["kg"], p["bq"], p["bk"], p["bias"], p["sinks"],
      p["wrh"], p["wrl"], p["br"], tri)


def _dest_kernel(pstart_ref, route_ref, dest_ref):
    e = route_ref[0:2, :]
    base = jnp.zeros(e.shape, i32)
    for x in range(N_EXPERTS):
        base = jnp.where(e == x, pstart_ref[x], base)
    dest_ref[...] = base + route_ref[2:4, :]


def _dest_slots(pstart, route):
    N = route.shape[1]
    TS = min(DEST_TILE, N)
    return pl.pallas_call(
        _dest_kernel,
        grid=(N // TS,),
        in_specs=[pl.BlockSpec(memory_space=pltpu.SMEM),
                  pl.BlockSpec((8, TS), lambda i: (0, i))],
        out_specs=pl.BlockSpec((2, TS), lambda i: (0, i)),
        out_shape=jax.ShapeDtypeStruct((2, N), i32),
        name="dest_slots",
    )(pstart, route)


def _sc_mesh():
    return plsc.VectorSubcoreMesh(core_axis_name="c", subcore_axis_name="s")


def _sc_scatter(h2, dest, n_slots):
    N, D = h2.shape
    per_w = N // SC_WORKERS
    C = SC_CHUNK
    n_chunks = per_w // C
    dest4 = dest.reshape(TOP_K, SC_WORKERS, n_chunks, C)

    @functools.partial(
        pl.kernel, mesh=_sc_mesh(),
        out_type=jax.ShapeDtypeStruct((n_slots, D), f32),
        scratch_types=[pltpu.VMEM((n_chunks, C), i32), pltpu.VMEM((n_chunks, C), i32),
                       pltpu.VMEM((2, C, D), f32),
                       pltpu.SemaphoreType.DMA((2,)), pltpu.SemaphoreType.DMA((2,))],
        name="sc_dispatch_scatter",
    )
    def run(h2_hbm, dest_hbm, xs_hbm, idx0, idx1, buf, sem_in, sem_out):
        wid = lax.axis_index("s") * SC_CORES + lax.axis_index("c")
        base = wid * per_w
        pltpu.sync_copy(dest_hbm.at[0, wid], idx0)
        pltpu.sync_copy(dest_hbm.at[1, wid], idx1)

        def load(j, b):
            return pltpu.make_async_copy(h2_hbm.at[pl.ds(base + j * C, C)], buf.at[b], sem_in.at[b])

        def put(j, b, idx):
            return pltpu.make_async_copy(buf.at[b], xs_hbm.at[idx.at[j]], sem_out.at[b])

        load(0, 0).start()

        @pl.loop(0, n_chunks, step=2)
        def _(j0):
            for b in range(2):
                j = j0 + b
                load(j, b).wait()

                @pl.when(j >= 1)
                def _():
                    put(j - 1, 1 - b, idx0).wait()
                    put(j - 1, 1 - b, idx1).wait()

                @pl.when(j + 1 < n_chunks)
                def _():
                    load(j + 1, 1 - b).start()

                put(j, b, idx0).start()
                put(j, b, idx1).start()

        put(n_chunks - 1, 1, idx0).wait()
        put(n_chunks - 1, 1, idx1).wait()

    return run(h2, dest4)


def _sc_gather(y, dest, N):
    D = y.shape[1]
    per_w = N // SC_WORKERS
    C = SC_CHUNK
    n_chunks = per_w // C
    dest4 = dest.reshape(TOP_K, SC_WORKERS, n_chunks, C)

    @functools.partial(
        pl.kernel, mesh=_sc_mesh(),
        out_type=jax.ShapeDtypeStruct((TOP_K, N, D), f32),
        scratch_types=[pltpu.VMEM((n_chunks, C), i32), pltpu.VMEM((n_chunks, C), i32),
                       pltpu.VMEM((2, C, D), f32),
                       pltpu.SemaphoreType.DMA((2,)), pltpu.SemaphoreType.DMA((2,))],
        name="sc_combine_gather",
    )
    def run(y_hbm, dest_hbm, yg_hbm, idx0, idx1, buf, sem_in, sem_out):
        wid = lax.axis_index("s") * SC_CORES + lax.axis_index("c")
        base = wid * per_w
        pltpu.sync_copy(dest_hbm.at[0, wid], idx0)
        pltpu.sync_copy(dest_hbm.at[1, wid], idx1)
        idx = (idx0, idx1)

        def get(j, k):
            return pltpu.make_async_copy(y_hbm.at[idx[k].at[j]], buf.at[k], sem_in.at[k])

        def put(j, k):
            return pltpu.make_async_copy(buf.at[k], yg_hbm.at[k, pl.ds(base + j * C, C)], sem_out.at[k])

        get(0, 0).start()
        get(0, 1).start()

        @pl.loop(0, n_chunks)
        def _(j):
            for k in range(TOP_K):
                get(j, k).wait()
                put(j, k).start()
            for k in range(TOP_K):
                put(j, k).wait()

                @pl.when(j + 1 < n_chunks)
                def _():
                    get(j + 1, k).start()

    return run(y, dest4)


def _expert_kernel(be_ref, nv_ref, nu_ref, xs_ref, wgu_ref, wd_ref, y_ref):
    i = pl.program_id(0)

    @pl.when(i < nu_ref[0])
    def _():
        live = lax.broadcasted_iota(i32, (MOE_BLOCK, 1), 0) < nv_ref[i]
        xb = jnp.where(live, xs_ref[...], 0.0).astype(bf16)
        gu = _dot(xb, wgu_ref[...])
        gate = gu[:, 0:D_EXPERT]
        hid = (gate * jax.nn.sigmoid(gate)) * gu[:, D_EXPERT:2 * D_EXPERT]
        y_ref[...] = _dot(hid.astype(bf16), wd_ref[...])


def _experts(block_e, n_valid, n_used, xs, wgu, wd):
    n_slots, D = xs.shape
    n_blocks = n_slots // MOE_BLOCK
    last = lambda i, nu: jnp.minimum(i, nu[0] - 1)
    grid_spec = pltpu.PrefetchScalarGridSpec(
        num_scalar_prefetch=3,
        grid=(n_blocks,),
        in_specs=[pl.BlockSpec((MOE_BLOCK, D), lambda i, be, nv, nu: (last(i, nu), 0)),
                  pl.BlockSpec((None, D, 2 * D_EXPERT), lambda i, be, nv, nu: (be[last(i, nu)], 0, 0)),
                  pl.BlockSpec((None, D_EXPERT, D), lambda i, be, nv, nu: (be[last(i, nu)], 0, 0))],
        out_specs=pl.BlockSpec((MOE_BLOCK, D), lambda i, be, nv, nu: (last(i, nu), 0)),
    )
    return pl.pallas_call(
        _expert_kernel,
        grid_spec=grid_spec,
        out_shape=jax.ShapeDtypeStruct((n_slots, D), f32),
        compiler_params=pltpu.CompilerParams(dimension_semantics=("arbitrary",),
                                             vmem_limit_bytes=VMEM_LIMIT),
        name="experts",
    )(block_e, n_valid, n_used, xs, wgu, wd)


def _combine_kernel(yg_ref, x1_ref, wtok_ref, mod_ref, o_ref):
    w = wtok_ref[...]
    g2 = mod_ref[5:6, :]
    moe = w[:, 0:1] * yg_ref[0] + w[:, 1:2] * yg_ref[1]
    o_ref[...] = x1_ref[...] + g2 * moe


def _combine(yg, x1, wtok, mod6, T):
    N, D = x1.shape
    TS = min(COMBINE_TILE, T)
    per_seq = T // TS
    return pl.pallas_call(
        _combine_kernel,
        grid=(N // TS,),
        in_specs=[pl.BlockSpec((TOP_K, TS, D), lambda i: (0, i, 0)),
                  pl.BlockSpec((TS, D), lambda i: (i, 0)),
                  pl.BlockSpec((TS, LANES), lambda i: (i, 0)),
                  pl.BlockSpec((None, 6, D), lambda i: (i // per_seq, 0, 0))],
        out_specs=pl.BlockSpec((TS, D), lambda i: (i, 0)),
        out_shape=jax.ShapeDtypeStruct((N, D), f32),
        compiler_params=pltpu.CompilerParams(dimension_semantics=("arbitrary",)),
        name="combine",
    )(yg, x1, wtok, mod6)


def _block_diag_mean(n, blk):
    m = np.kron(np.eye(n // blk, dtype=np.float32), np.full((blk, blk), 1.0 / blk, np.float32))
    return jnp.asarray(m, bf16)


def _layer(x, mod6, bias, l, w):
    B, T, D = x.shape
    N = B * T
    w_rg, w_re = w["w_router_group"][l], w["w_router_expert"][l]
    wr = jnp.zeros((D, ROUTER_COLS), f32)
    wr = wr.at[:, 0:N_GROUPS].set(w_rg).at[:, EXPERT_COL0:EXPERT_COL0 + N_EXPERTS].set(w_re)
    br = jnp.zeros((1, ROUTER_COLS), f32)
    br = br.at[0, 0:N_GROUPS].set(w["b_router_group"][l])
    br = br.at[0, EXPERT_COL0:EXPERT_COL0 + N_EXPERTS].set(w["b_router_expert"][l])
    wrh = wr.astype(bf16)
    wrl = (wr - wrh.astype(f32)).astype(bf16)
    p = dict(
        gmix=w["norm_mix_g"][l].reshape(1, D), gffn=w["norm_ffn_g"][l].reshape(1, D),
        w_in=w["w_in"][l].astype(bf16),
        dw=w["dw_kernel"][l], dwb=w["dw_bias"][l].reshape(1, D_CONV),
        lng=w["conv_ln_g"][l].reshape(1, D_CONV), lnb=w["conv_ln_b"][l].reshape(1, D_CONV),
        wco=w["w_conv_out"][l].astype(bf16), wao=w["w_attn_out"][l].astype(bf16),
        wout=w["w_out"][l].astype(bf16),
        qg=(jnp.tile(w["q_norm_g"][l], N_Q_HEADS) * (HEAD_DIM ** -0.5)).reshape(1, D_Q),
        kg=jnp.tile(w["k_norm_g"][l], N_KV_HEADS).reshape(1, D_KV),
        bq=_block_diag_mean(D_Q, HEAD_DIM), bk=_block_diag_mean(D_KV, HEAD_DIM),
        bias=bias, sinks=w["sinks"][l], wrh=wrh, wrl=wrl, br=br,
    )
    x1, h2, route, wtok, cnt = _mixer(x, mod6, p)

    counts = cnt[:, 0].astype(i32)
    pcounts = (counts + MOE_BLOCK - 1) // MOE_BLOCK * MOE_BLOCK
    pend = jnp.cumsum(pcounts)
    pstart = pend - pcounts
    n_blocks = -(-(N * TOP_K) // MOE_BLOCK) + N_EXPERTS
    blk0 = jnp.arange(n_blocks, dtype=i32) * MOE_BLOCK
    block_e = jnp.minimum(jnp.sum((pend[None, :] <= blk0[:, None]).astype(i32), axis=1), N_EXPERTS - 1)
    n_valid = jnp.clip((pstart + counts)[block_e] - blk0, 0, MOE_BLOCK).astype(i32)
    n_used = (pend[-1:] // MOE_BLOCK).astype(i32)

    dest = _dest_slots(pstart, route)
    xs = _sc_scatter(h2.reshape(N, D), dest, n_blocks * MOE_BLOCK)
    wgu = jnp.concatenate([w["w_exp_gate"][l], w["w_exp_up"][l]], axis=-1).astype(bf16)
    wd = w["w_exp_down"][l].astype(bf16)
    y = _experts(block_e, n_valid, n_used, xs, wgu, wd)
    yg = _sc_gather(y, dest, N)
    out = _combine(yg, x1.reshape(N, D), wtok, mod6, T)
    return out.reshape(B, T, D)


def kernel(x, c, w_ada, b_ada, norm_mix_g, w_in, dw_kernel, dw_bias, conv_ln_g, conv_ln_b,
           w_conv_out, q_norm_g, k_norm_g, sinks, w_attn_out, w_out, rel_bias_table, norm_ffn_g,
           w_router_group, b_router_group, w_router_expert, b_router_expert,
           w_exp_gate, w_exp_up, w_exp_down):
    w = dict(norm_mix_g=norm_mix_g, w_in=w_in, dw_kernel=dw_kernel, dw_bias=dw_bias,
             conv_ln_g=conv_ln_g, conv_ln_b=conv_ln_b, w_conv_out=w_conv_out, q_norm_g=q_norm_g,
             k_norm_g=k_norm_g, sinks=sinks, w_attn_out=w_attn_out, w_out=w_out,
             norm_ffn_g=norm_ffn_g, w_router_group=w_router_group, b_router_group=b_router_group,
             w_router_expert=w_router_expert, b_router_expert=b_router_expert,
             w_exp_gate=w_exp_gate, w_exp_up=w_exp_up, w_exp_down=w_exp_down)
    B = x.shape[0]
    bias = _bias_band(rel_bias_table).reshape(N_KV_HEADS, Q_PER_KV * WINDOW, 2 * WINDOW)
    for l in range(w_ada.shape[0]):
        mod6 = _modulation(c, w_ada[l], b_ada[l]).reshape(B, 6, D_MODEL)
        x = _layer(x, mod6, bias, l, w)
    return x
```

```python
import functools
import math

import jax
import jax.numpy as jnp
import numpy as np
from jax import lax
from jax.experimental import pallas as pl
from jax.experimental.pallas import tpu as pltpu
from jax.experimental.pallas import tpu_sc as plsc

D_MODEL = 1024
D_CONV = 512
CONV_WIDTH = 31
N_Q_HEADS = 8
N_KV_HEADS = 2
HEAD_DIM = 64
Q_PER_KV = N_Q_HEADS // N_KV_HEADS
WINDOW = 128
N_BUCKETS = 32
MAX_DISTANCE = 128
N_GROUPS = 4
EXPERTS_PER_GROUP = 8
N_EXPERTS = N_GROUPS * EXPERTS_PER_GROUP
TOP_K = 2
D_EXPERT = 256
D_Q = N_Q_HEADS * HEAD_DIM
D_KV = N_KV_HEADS * HEAD_DIM
EPS = 1e-6
NEG_INF = -1e30

LANES = 128
SEQ_TILE = 512
CONV_ROWS = 32
PROJ_CHUNK = 256
CONV_SPAN = 24
MOE_BLOCK = 512
DEST_TILE = 8192
COMBINE_TILE = 512
SC_CORES = 2
SC_SUBCORES = 16
SC_WORKERS = SC_CORES * SC_SUBCORES
SC_CHUNK = 32
ROUTER_COLS = LANES
EXPERT_COL0 = 8
VMEM_LIMIT = 56 * 1024 * 1024

f32 = jnp.float32
bf16 = jnp.bfloat16
i32 = jnp.int32


def _dot(a, b):
    return jnp.dot(a, b, preferred_element_type=f32)


def _split(a):
    hi = a.astype(bf16)
    lo = (a - hi.astype(f32)).astype(bf16)
    return hi, lo


def _dot3(a, b):
    ah, al = _split(a)
    bh, bl = _split(b)
    return _dot(ah, bh) + _dot(al, bh) + _dot(ah, bl)


def _mod_kernel(c_ref, w_ref, b_ref, o_ref):
    c = c_ref[...]
    s = c * jax.nn.sigmoid(c)
    o_ref[...] = _dot3(s, w_ref[...]) + b_ref[...]


def _modulation(c, w_ada, b_ada):
    B, D = c.shape
    n_out = w_ada.shape[1]
    return pl.pallas_call(
        _mod_kernel,
        grid=(n_out // D,),
        in_specs=[pl.BlockSpec((B, D), lambda j: (0, 0)),
                  pl.BlockSpec((D, D), lambda j: (0, j)),
                  pl.BlockSpec((1, D), lambda j: (0, j))],
        out_specs=pl.BlockSpec((B, D), lambda j: (0, j)),
        out_shape=jax.ShapeDtypeStruct((B, n_out), f32),
        name="modulation",
    )(c, w_ada, b_ada.reshape(1, n_out))


def _band_buckets():
    qi = np.arange(WINDOW)[:, None]
    kj = np.arange(2 * WINDOW)[None, :]
    dist = np.clip(qi + WINDOW - kj, 0, MAX_DISTANCE)
    max_exact = N_BUCKETS // 2
    d = np.maximum(dist, 1).astype(np.float32)
    large = max_exact + (np.log(d / np.float32(max_exact)) / np.float32(math.log(MAX_DISTANCE / max_exact))
                         * np.float32(N_BUCKETS - max_exact)).astype(np.int32)
    large = np.minimum(large, N_BUCKETS - 1)
    return np.where(dist < max_exact, dist, large).astype(np.int32)


def _bias_kernel(tab_ref, bucket_ref, o_ref):
    bk = bucket_ref[...]
    for h in range(N_Q_HEADS):
        acc = jnp.zeros(bk.shape, f32)
        for b in range(N_BUCKETS):
            acc = jnp.where(bk == b, tab_ref[b, h], acc)
        o_ref[h] = acc


def _bias_band(rel_bias_table):
    return pl.pallas_call(
        _bias_kernel,
        in_specs=[pl.BlockSpec(memory_space=pltpu.SMEM),
                  pl.BlockSpec(memory_space=pltpu.VMEM)],
        out_specs=pl.BlockSpec(memory_space=pltpu.VMEM),
        out_shape=jax.ShapeDtypeStruct((N_Q_HEADS, WINDOW, 2 * WINDOW), f32),
        name="bias_band",
    )(rel_bias_table, jnp.asarray(_band_buckets()))


def _mixer_kernel(x_ref, mod_ref, gmix_ref, gffn_ref, wab_ref, wproj_ref, dw_ref, dwb_ref, lng_ref, lnb_ref,
                  wco_ref, wao_ref, wout_ref, qg_ref, kg_ref, bq_ref, bk_ref, bias_ref, sink_ref,
                  wrh_ref, wrl_ref, br_ref, tri_ref,
                  x1_ref, h2_ref, route_ref, wtok_ref, cnt_ref,
                  uext, ush, conv_sc, proj_sc, k2, v2, osc):
    TM = x_ref.shape[0]
    HALO = 32
    b = pl.program_id(0)
    t = pl.program_id(1)
    first = t == 0

    @pl.when(first)
    def _():
        uext[0:HALO, :] = jnp.zeros((HALO, D_CONV), f32)
        k2[:, 0:WINDOW, :] = jnp.zeros((N_KV_HEADS, WINDOW, LANES), bf16)
        v2[:, 0:WINDOW, :] = jnp.zeros((N_KV_HEADS, WINDOW, LANES), bf16)

    @pl.when(first & (b == 0))
    def _():
        cnt_ref[...] = jnp.zeros(cnt_ref.shape, f32)

    x = x_ref[...]
    mod = mod_ref[...]
    sh1, sc1, g1 = mod[0:1, :], mod[1:2, :], mod[2:3, :]
    sh2, sc2, g2 = mod[3:4, :], mod[4:5, :], mod[5:6, :]
    del g2

    ms = jnp.mean(x * x, axis=-1, keepdims=True)
    h = (x * lax.rsqrt(ms + EPS)) * gmix_ref[...]
    h = h * (1.0 + sc1) + sh1
    hb = h.astype(bf16)

    ab = _dot(hb, wab_ref[...])
    u = ab[:, 0:D_CONV] * jax.nn.sigmoid(ab[:, D_CONV:2 * D_CONV])
    uext[HALO:HALO + TM, :] = u
    for r in range(1, 8):
        ush[r - 1] = uext[r:r + TM + CONV_SPAN, :]
    n_q = D_Q // PROJ_CHUNK
    n_g = D_MODEL // PROJ_CHUNK
    n_jobs = wproj_ref.shape[0]
    n_rb = TM // CONV_ROWS
    n_t8 = CONV_ROWS // 8
    job_at = {(i * n_rb) // n_jobs: i for i in range(n_jobs)}
    assert len(job_at) == n_jobs
    for rb in range(n_rb):
        if rb in job_at:
            proj_sc[job_at[rb]] = _dot(hb, wproj_ref[job_at[rb]])
        base = rb * CONV_ROWS
        accs = [None] * n_t8
        for j in range(CONV_WIDTH):
            a, r = divmod(j + 2, 8)
            tap = dw_ref[j]
            for s8 in range(n_t8):
                rows = slice(base + 8 * (a + s8), base + 8 * (a + s8) + 8)
                win = uext[rows, :] if r == 0 else ush[r - 1, rows, :]
                term = tap * win
                accs[s8] = term if accs[s8] is None else accs[s8] + term
        for s8 in range(n_t8):
            conv_sc[base + 8 * s8:base + 8 * s8 + 8, :] = accs[s8] + dwb_ref[...]
    conv = conv_sc[...]
    uext[0:HALO, :] = uext[TM:TM + HALO, :]
    mu = jnp.mean(conv, axis=-1, keepdims=True)
    dc = conv - mu
    var = jnp.mean(dc * dc, axis=-1, keepdims=True)
    yn = dc * lax.rsqrt(var + EPS) * lng_ref[...] + lnb_ref[...]
    act = yn * jax.nn.sigmoid(yn)
    y_conv = _dot(act.astype(bf16), wco_ref[...])

    q = jnp.concatenate([proj_sc[i] for i in range(n_q)], axis=1)
    k = proj_sc[n_q, :, 0:D_KV]
    v = proj_sc[n_q, :, D_KV:2 * D_KV]

    def head_norm(z, blk_ref, g):
        msq = _dot((z * z).astype(bf16), blk_ref[...])
        return z * lax.rsqrt(msq + EPS) * g

    qn = head_norm(q, bq_ref, qg_ref[...]).astype(bf16)
    kn = head_norm(k, bk_ref, kg_ref[...])
    lo_half = lax.broadcasted_iota(i32, (TM, LANES), 1) < HEAD_DIM
    kr = pltpu.roll(kn, HEAD_DIM, 1)
    vr = pltpu.roll(v, HEAD_DIM, 1)
    k2[0, WINDOW:WINDOW + TM, :] = jnp.where(lo_half, kn, kr).astype(bf16)
    k2[1, WINDOW:WINDOW + TM, :] = jnp.where(lo_half, kr, kn).astype(bf16)
    v2[0, WINDOW:WINDOW + TM, :] = jnp.where(lo_half, v, vr).astype(bf16)
    v2[1, WINDOW:WINDOW + TM, :] = jnp.where(lo_half, vr, v).astype(bf16)

    QROWS = Q_PER_KV * WINDOW
    row = lax.broadcasted_iota(i32, (QROWS, 2 * WINDOW), 0) & (WINDOW - 1)
    col = lax.broadcasted_iota(i32, (QROWS, 2 * WINDOW), 1)
    dist = row + WINDOW - col
    in_window = (dist >= 0) & (dist < WINDOW)
    first_mask = in_window & (col >= jnp.where(first, WINDOW, 0))
    hrow = lax.broadcasted_iota(i32, (QROWS, 1), 0) // WINDOW
    lo128 = lax.broadcasted_iota(i32, (WINDOW, LANES), 1) < HEAD_DIM
    zero_q = jnp.zeros((WINDOW, LANES), bf16)
    for g in range(N_KV_HEADS):
        sink = jnp.where(hrow == 0, sink_ref[4 * g],
                         jnp.where(hrow == 1, sink_ref[4 * g + 1],
                                   jnp.where(hrow == 2, sink_ref[4 * g + 2], sink_ref[4 * g + 3])))
        bias_g = bias_ref[g]
        for j in range(TM // WINDOW):
            rs = slice(j * WINDOW, (j + 1) * WINDOW)
            qa = qn[rs, 2 * LANES * g:2 * LANES * g + LANES]
            qb = qn[rs, 2 * LANES * g + LANES:2 * LANES * (g + 1)]
            qs = jnp.concatenate([jnp.where(lo128, qa, zero_q), jnp.where(lo128, zero_q, qa),
                                  jnp.where(lo128, qb, zero_q), jnp.where(lo128, zero_q, qb)], axis=0)
            kk = k2[g, j * WINDOW:(j + 2) * WINDOW, :]
            s = lax.dot_general(qs, kk, (((1,), (1,)), ((), ())), preferred_element_type=f32)
            msk = first_mask if j == 0 else in_window
            logits = jnp.where(msk, s + bias_g, NEG_INF)
            m = jnp.maximum(jnp.max(logits, axis=-1, keepdims=True), sink)
            p = jnp.exp(logits - m)
            den = jnp.sum(p, axis=-1, keepdims=True) + jnp.exp(sink - m)
            o2 = _dot(p.astype(bf16), v2[g, j * WINDOW:(j + 2) * WINDOW, :]) * (1.0 / den)
            osc[rs, 2 * LANES * g:2 * LANES * g + LANES] = jnp.where(
                lo128, o2[0:WINDOW], o2[WINDOW:2 * WINDOW]).astype(bf16)
            osc[rs, 2 * LANES * g + LANES:2 * LANES * (g + 1)] = jnp.where(
                lo128, o2[2 * WINDOW:3 * WINDOW], o2[3 * WINDOW:4 * WINDOW]).astype(bf16)
    k2[:, 0:WINDOW, :] = k2[:, TM:TM + WINDOW, :]
    v2[:, 0:WINDOW, :] = v2[:, TM:TM + WINDOW, :]
    y_attn = _dot(osc[...], wao_ref[...])

    merged = []
    for i in range(n_g):
        cs = slice(i * PROJ_CHUNK, (i + 1) * PROJ_CHUNK)
        g_conv = jax.nn.sigmoid(proj_sc[n_q + 1 + i])
        g_attn = jax.nn.sigmoid(proj_sc[n_q + 1 + n_g + i])
        merged.append((g_conv * y_conv[:, cs] + g_attn * y_attn[:, cs]).astype(bf16))
    merged = jnp.concatenate(merged, axis=1)
    x1 = x + g1 * _dot(merged, wout_ref[...])
    x1_ref[...] = x1

    ms2 = jnp.mean(x1 * x1, axis=-1, keepdims=True)
    h2 = (x1 * lax.rsqrt(ms2 + EPS)) * gffn_ref[...]
    h2 = h2 * (1.0 + sc2) + sh2
    h2_ref[...] = h2
    hh, hl = _split(h2)
    lg = _dot(hh, wrh_ref[...]) + _dot(hl, wrh_ref[...]) + _dot(hh, wrl_ref[...]) + br_ref[...]
    lt = lg.T
    gl = lt[0:N_GROUPS, :]
    grow = lax.broadcasted_iota(i32, (N_GROUPS, TM), 0)
    gmax = jnp.max(gl, axis=0, keepdims=True)
    gi = jnp.min(jnp.where(gl == gmax, grow, N_GROUPS), axis=0, keepdims=True)
    p_top = 1.0 / jnp.sum(jnp.exp(gl - gmax), axis=0, keepdims=True)
    sel = lt[EXPERT_COL0:EXPERT_COL0 + EXPERTS_PER_GROUP, :]
    for gg in range(1, N_GROUPS):
        lo_r = EXPERT_COL0 + gg * EXPERTS_PER_GROUP
        sel = jnp.where(gi == gg, lt[lo_r:lo_r + EXPERTS_PER_GROUP, :], sel)
    erow = lax.broadcasted_iota(i32, (EXPERTS_PER_GROUP, TM), 0)
    m1 = jnp.max(sel, axis=0, keepdims=True)
    i1 = jnp.min(jnp.where(sel == m1, erow, EXPERTS_PER_GROUP), axis=0, keepdims=True)
    rest = jnp.where(erow == i1, -jnp.inf, sel)
    m2 = jnp.max(rest, axis=0, keepdims=True)
    i2 = jnp.min(jnp.where(rest == m2, erow, EXPERTS_PER_GROUP), axis=0, keepdims=True)
    z = jnp.sum(jnp.exp(sel - m1), axis=0, keepdims=True)
    v1 = 1.0 / z
    v2nd = jnp.exp(m2 - m1) / z
    w1 = v1 / (v1 + v2nd) * p_top
    w2 = v2nd / (v1 + v2nd) * p_top
    e1 = gi * EXPERTS_PER_GROUP + i1
    e2 = gi * EXPERTS_PER_GROUP + i2

    xrow = lax.broadcasted_iota(i32, (N_EXPERTS, TM), 0)
    oh1 = xrow == e1
    oh2 = xrow == e2
    both = jnp.where(oh1 | oh2, 1.0, 0.0)
    prefix = _dot(both.astype(bf16), tri_ref[...]) + cnt_ref[:, 0:1]
    r1 = jnp.sum(jnp.where(oh1, prefix, 0.0), axis=0, keepdims=True)
    r2 = jnp.sum(jnp.where(oh2, prefix, 0.0), axis=0, keepdims=True)
    cnt_ref[...] = cnt_ref[...] + jnp.sum(both, axis=1, keepdims=True)

    route_ref[...] = jnp.concatenate(
        [e1, e2, r1.astype(i32), r2.astype(i32), jnp.zeros((4, TM), i32)], axis=0)
    wpad = jnp.concatenate([w1, w2, jnp.zeros((LANES - 2, TM), f32)], axis=0)
    wtok_ref[...] = wpad.T


def _mixer(x, mod6, p):
    B, T, D = x.shape
    TM = min(SEQ_TILE, T)
    nt = T // TM
    N = B * T
    const = lambda shape: pl.BlockSpec(shape, lambda b, t: (0,) * len(shape))
    in_specs = [
        pl.BlockSpec((None, TM, D), lambda b, t: (b, t, 0)),
        pl.BlockSpec((None, 6, D), lambda b, t: (b, 0, 0)),
        const((1, D)), const((1, D)),
        const(p["wab"].shape), const(p["wproj"].shape),
        const((CONV_WIDTH, 8, D_CONV)), const((1, D_CONV)), const((1, D_CONV)), const((1, D_CONV)),
        const((D_CONV, D)), const((D_Q, D)), const((D, D)),
        const((1, D_Q)), const((1, D_KV)),
        const((D_Q, D_Q)), const((D_KV, D_KV)),
        const((N_KV_HEADS, Q_PER_KV * WINDOW, 2 * WINDOW)),
        pl.BlockSpec(memory_space=pltpu.SMEM),
        const((D, ROUTER_COLS)), const((D, ROUTER_COLS)), const((1, ROUTER_COLS)),
        const((TM, TM)),
    ]
    out_specs = [
        pl.BlockSpec((None, TM, D), lambda b, t: (b, t, 0)),
        pl.BlockSpec((None, TM, D), lambda b, t: (b, t, 0)),
        pl.BlockSpec((8, TM), lambda b, t: (0, b * nt + t)),
        pl.BlockSpec((TM, LANES), lambda b, t: (b * nt + t, 0)),
        pl.BlockSpec((N_EXPERTS, LANES), lambda b, t: (0, 0)),
    ]
    out_shape = [
        jax.ShapeDtypeStruct((B, T, D), f32),
        jax.ShapeDtypeStruct((B, T, D), f32),
        jax.ShapeDtypeStruct((8, N), i32),
        jax.ShapeDtypeStruct((N, LANES), f32),
        jax.ShapeDtypeStruct((N_EXPERTS, LANES), f32),
    ]
    scratch = [
        pltpu.VMEM((TM + 32, D_CONV), f32),
        pltpu.VMEM((7, TM + CONV_SPAN, D_CONV), f32),
        pltpu.VMEM((TM, D_CONV), f32),
        pltpu.VMEM((p["wproj"].shape[0], TM, PROJ_CHUNK), f32),
        pltpu.VMEM((N_KV_HEADS, TM + WINDOW, LANES), bf16),
        pltpu.VMEM((N_KV_HEADS, TM + WINDOW, LANES), bf16),
        pltpu.VMEM((TM, D_Q), bf16),
    ]
    tri = jnp.asarray(np.triu(np.ones((TM, TM), np.float32), 1), bf16)
    return pl.pallas_call(
        _mixer_kernel,
        grid=(B, nt),
        in_specs=in_specs, out_specs=out_specs, out_shape=out_shape, scratch_shapes=scratch,
        compiler_params=pltpu.CompilerParams(
            dimension_semantics=("arbitrary", "arbitrary"), vmem_limit_bytes=VMEM_LIMIT),
        name="mixer_router",
    )(x, mod6, p["gmix"], p["gffn"], p["wab"], p["wproj"], p["dw"], p["dwb"], p["lng"], p["lnb"],
      p["wco"], p["wao"], p["wout"], p["qg"], p["kg"], p["bq"], p["bk"], p["bias"], p["sinks"],
      p["wrh"], p["wrl"], p["br"], tri)


def _dest_kernel(pstart_ref, route_ref, dest_ref):
    e = route_ref[0:2, :]
    base = jnp.zeros(e.shape, i32)
    for x in range(N_EXPERTS):
        base = jnp.where(e == x, pstart_ref[x], base)
    dest_ref[...] = base + route_ref[2:4, :]


def _dest_slots(pstart, route):
    N = route.shape[1]
    TS = min(DEST_TILE, N)
    return pl.pallas_call(
        _dest_kernel,
        grid=(N // TS,),
        in_specs=[pl.BlockSpec(memory_space=pltpu.SMEM),
                  pl.BlockSpec((8, TS), lambda i: (0, i))],
        out_specs=pl.BlockSpec((2, TS), lambda i: (0, i)),
        out_shape=jax.ShapeDtypeStruct((2, N), i32),
        name="dest_slots",
    )(pstart, route)


def _sc_mesh():
    return plsc.VectorSubcoreMesh(core_axis_name="c", subcore_axis_name="s")


def _sc_scatter(h2, dest, n_slots):
    N, D = h2.shape
    per_w = N // SC_WORKERS
    C = SC_CHUNK
    n_chunks = per_w // C
    dest4 = dest.reshape(TOP_K, SC_WORKERS, n_chunks, C)

    @functools.partial(
        pl.kernel, mesh=_sc_mesh(),
        out_type=jax.ShapeDtypeStruct((n_slots, D), f32),
        scratch_types=[pltpu.VMEM((n_chunks, C), i32), pltpu.VMEM((n_chunks, C), i32),
                       pltpu.VMEM((2, C, D), f32),
                       pltpu.SemaphoreType.DMA((2,)), pltpu.SemaphoreType.DMA((2,))],
        name="sc_dispatch_scatter",
    )
    def run(h2_hbm, dest_hbm, xs_hbm, idx0, idx1, buf, sem_in, sem_out):
        wid = lax.axis_index("s") * SC_CORES + lax.axis_index("c")
        base = wid * per_w
        pltpu.sync_copy(dest_hbm.at[0, wid], idx0)
        pltpu.sync_copy(dest_hbm.at[1, wid], idx1)

        def load(j, b):
            return pltpu.make_async_copy(h2_hbm.at[pl.ds(base + j * C, C)], buf.at[b], sem_in.at[b])

        def put(j, b, idx):
            return pltpu.make_async_copy(buf.at[b], xs_hbm.at[idx.at[j]], sem_out.at[b])

        load(0, 0).start()

        @pl.loop(0, n_chunks, step=2)
        def _(j0):
            for b in range(2):
                j = j0 + b
                load(j, b).wait()

                @pl.when(j >= 1)
                def _():
                    put(j - 1, 1 - b, idx0).wait()
                    put(j - 1, 1 - b, idx1).wait()

                @pl.when(j + 1 < n_chunks)
                def _():
                    load(j + 1, 1 - b).start()

                put(j, b, idx0).start()
                put(j, b, idx1).start()

        put(n_chunks - 1, 1, idx0).wait()
        put(n_chunks - 1, 1, idx1).wait()

    return run(h2, dest4)


def _sc_gather(y, dest, N):
    D = y.shape[1]
    per_w = N // SC_WORKERS
    C = SC_CHUNK
    n_chunks = per_w // C
    dest4 = dest.reshape(TOP_K, SC_WORKERS, n_chunks, C)

    @functools.partial(
        pl.kernel, mesh=_sc_mesh(),
        out_type=jax.ShapeDtypeStruct((TOP_K, N, D), f32),
        scratch_types=[pltpu.VMEM((n_chunks, C), i32), pltpu.VMEM((n_chunks, C), i32),
                       pltpu.VMEM((2, C, D), f32),
                       pltpu.SemaphoreType.DMA((2,)), pltpu.SemaphoreType.DMA((2,))],
        name="sc_combine_gather",
    )
    def run(y_hbm, dest_hbm, yg_hbm, idx0, idx1, buf, sem_in, sem_out):
        wid = lax.axis_index("s") * SC_CORES + lax.axis_index("c")
        base = wid * per_w
        pltpu.sync_copy(dest_hbm.at[0, wid], idx0)
        pltpu.sync_copy(dest_hbm.at[1, wid], idx1)
        idx = (idx0, idx1)

        def get(j, k):
            return pltpu.make_async_copy(y_hbm.at[idx[k].at[j]], buf.at[k], sem_in.at[k])

        def put(j, k):
            return pltpu.make_async_copy(buf.at[k], yg_hbm.at[k, pl.ds(base + j * C, C)], sem_out.at[k])

        get(0, 0).start()
        get(0, 1).start()

        @pl.loop(0, n_chunks)
        def _(j):
            for k in range(TOP_K):
                get(j, k).wait()
                put(j, k).start()
            for k in range(TOP_K):
                put(j, k).wait()

                @pl.when(j + 1 < n_chunks)
                def _():
                    get(j + 1, k).start()

    return run(y, dest4)


def _expert_kernel(be_ref, nv_ref, nu_ref, xs_ref, wgu_ref, wd_ref, y_ref):
    i = pl.program_id(0)

    @pl.when(i < nu_ref[0])
    def _():
        live = lax.broadcasted_iota(i32, (MOE_BLOCK, 1), 0) < nv_ref[i]
        xb = jnp.where(live, xs_ref[...], 0.0).astype(bf16)
        gu = _dot(xb, wgu_ref[...])
        gate = gu[:, 0:D_EXPERT]
        hid = (gate * jax.nn.sigmoid(gate)) * gu[:, D_EXPERT:2 * D_EXPERT]
        y_ref[...] = _dot(hid.astype(bf16), wd_ref[...])


def _experts(block_e, n_valid, n_used, xs, wgu, wd):
    n_slots, D = xs.shape
    n_blocks = n_slots // MOE_BLOCK
    last = lambda i, nu: jnp.minimum(i, nu[0] - 1)
    grid_spec = pltpu.PrefetchScalarGridSpec(
        num_scalar_prefetch=3,
        grid=(n_blocks,),
        in_specs=[pl.BlockSpec((MOE_BLOCK, D), lambda i, be, nv, nu: (last(i, nu), 0)),
                  pl.BlockSpec((None, D, 2 * D_EXPERT), lambda i, be, nv, nu: (be[last(i, nu)], 0, 0)),
                  pl.BlockSpec((None, D_EXPERT, D), lambda i, be, nv, nu: (be[last(i, nu)], 0, 0))],
        out_specs=pl.BlockSpec((MOE_BLOCK, D), lambda i, be, nv, nu: (last(i, nu), 0)),
    )
    return pl.pallas_call(
        _expert_kernel,
        grid_spec=grid_spec,
        out_shape=jax.ShapeDtypeStruct((n_slots, D), f32),
        compiler_params=pltpu.CompilerParams(dimension_semantics=("arbitrary",),
                                             vmem_limit_bytes=VMEM_LIMIT),
        name="experts",
    )(block_e, n_valid, n_used, xs, wgu, wd)


def _combine_kernel(yg_ref, x1_ref, wtok_ref, mod_ref, o_ref):
    w = wtok_ref[...]
    g2 = mod_ref[5:6, :]
    moe = w[:, 0:1] * yg_ref[0] + w[:, 1:2] * yg_ref[1]
    o_ref[...] = x1_ref[...] + g2 * moe


def _combine(yg, x1, wtok, mod6, T):
    N, D = x1.shape
    TS = min(COMBINE_TILE, T)
    per_seq = T // TS
    return pl.pallas_call(
        _combine_kernel,
        grid=(N // TS,),
        in_specs=[pl.BlockSpec((TOP_K, TS, D), lambda i: (0, i, 0)),
                  pl.BlockSpec((TS, D), lambda i: (i, 0)),
                  pl.BlockSpec((TS, LANES), lambda i: (i, 0)),
                  pl.BlockSpec((None, 6, D), lambda i: (i // per_seq, 0, 0))],
        out_specs=pl.BlockSpec((TS, D), lambda i: (i, 0)),
        out_shape=jax.ShapeDtypeStruct((N, D), f32),
        compiler_params=pltpu.CompilerParams(dimension_semantics=("arbitrary",)),
        name="combine",
    )(yg, x1, wtok, mod6)


def _block_diag_mean(n, blk):
    m = np.kron(np.eye(n // blk, dtype=np.float32), np.full((blk, blk), 1.0 / blk, np.float32))
    return jnp.asarray(m, bf16)


def _layer(x, mod6, bias, l, w):
    B, T, D = x.shape
    N = B * T
    w_rg, w_re = w["w_router_group"][l], w["w_router_expert"][l]
    wr = jnp.zeros((D, ROUTER_COLS), f32)
    wr = wr.at[:, 0:N_GROUPS].set(w_rg).at[:, EXPERT_COL0:EXPERT_COL0 + N_EXPERTS].set(w_re)
    br = jnp.zeros((1, ROUTER_COLS), f32)
    br = br.at[0, 0:N_GROUPS].set(w["b_router_group"][l])
    br = br.at[0, EXPERT_COL0:EXPERT_COL0 + N_EXPERTS].set(w["b_router_expert"][l])
    wrh = wr.astype(bf16)
    wrl = (wr - wrh.astype(f32)).astype(bf16)
    p = dict(
        gmix=w["norm_mix_g"][l].reshape(1, D), gffn=w["norm_ffn_g"][l].reshape(1, D),
        wab=w["w_in"][l][:, 0:2 * D_CONV].astype(bf16),
        wproj=w["w_in"][l][:, 2 * D_CONV:].astype(bf16).reshape(D, -1, PROJ_CHUNK).transpose(1, 0, 2),
        dw=jnp.broadcast_to(w["dw_kernel"][l][:, None, :], (CONV_WIDTH, 8, D_CONV)), dwb=w["dw_bias"][l].reshape(1, D_CONV),
        lng=w["conv_ln_g"][l].reshape(1, D_CONV), lnb=w["conv_ln_b"][l].reshape(1, D_CONV),
        wco=w["w_conv_out"][l].astype(bf16), wao=w["w_attn_out"][l].astype(bf16),
        wout=w["w_out"][l].astype(bf16),
        qg=(jnp.tile(w["q_norm_g"][l], N_Q_HEADS) * (HEAD_DIM ** -0.5)).reshape(1, D_Q),
        kg=jnp.tile(w["k_norm_g"][l], N_KV_HEADS).reshape(1, D_KV),
        bq=_block_diag_mean(D_Q, HEAD_DIM), bk=_block_diag_mean(D_KV, HEAD_DIM),
        bias=bias, sinks=w["sinks"][l], wrh=wrh, wrl=wrl, br=br,
    )
    x1, h2, route, wtok, cnt = _mixer(x, mod6, p)

    counts = cnt[:, 0].astype(i32)
    pcounts = (counts + MOE_BLOCK - 1) // MOE_BLOCK * MOE_BLOCK
    pend = jnp.cumsum(pcounts)
    pstart = pend - pcounts
    n_blocks = -(-(N * TOP_K) // MOE_BLOCK) + N_EXPERTS
    blk0 = jnp.arange(n_blocks, dtype=i32) * MOE_BLOCK
    block_e = jnp.minimum(jnp.sum((pend[None, :] <= blk0[:, None]).astype(i32), axis=1), N_EXPERTS - 1)
    n_valid = jnp.clip((pstart + counts)[block_e] - blk0, 0, MOE_BLOCK).astype(i32)
    n_used = (pend[-1:] // MOE_BLOCK).astype(i32)

    dest = _dest_slots(pstart, route)
    xs = _sc_scatter(h2.reshape(N, D), dest, n_blocks * MOE_BLOCK)
    wgu = jnp.concatenate([w["w_exp_gate"][l], w["w_exp_up"][l]], axis=-1).astype(bf16)
    wd = w["w_exp_down"][l].astype(bf16)
    y = _experts(block_e, n_valid, n_used, xs, wgu, wd)
    yg = _sc_gather(y, dest, N)
    out = _combine(yg, x1.reshape(N, D), wtok, mod6, T)
    return out.reshape(B, T, D)


def kernel(x, c, w_ada, b_ada, norm_mix_g, w_in, dw_kernel, dw_bias, conv_ln_g, conv_ln_b,
           w_conv_out, q_norm_g, k_norm_g, sinks, w_attn_out, w_out, rel_bias_table, norm_ffn_g,
           w_router_group, b_router_group, w_router_expert, b_router_expert,
           w_exp_gate, w_exp_up, w_exp_down):
    w = dict(norm_mix_g=norm_mix_g, w_in=w_in, dw_kernel=dw_kernel, dw_bias=dw_bias,
             conv_ln_g=conv_ln_g, conv_ln_b=conv_ln_b, w_conv_out=w_conv_out, q_norm_g=q_norm_g,
             k_norm_g=k_norm_g, sinks=sinks, w_attn_out=w_attn_out, w_out=w_out,
             norm_ffn_g=norm_ffn_g, w_router_group=w_router_group, b_router_group=b_router_group,
             w_router_expert=w_router_expert, b_router_expert=b_router_expert,
             w_exp_gate=w_exp_gate, w_exp_up=w_exp_up, w_exp_down=w_exp_down)
    B = x.shape[0]
    bias = _bias_band(rel_bias_table).reshape(N_KV_HEADS, Q_PER_KV * WINDOW, 2 * WINDOW)
    for l in range(w_ada.shape[0]):
        mod6 = _modulation(c, w_ada[l], b_ada[l]).reshape(B, 6, D_MODEL)
        x = _layer(x, mod6, bias, l, w)
    return x
```

```python
import functools
import math

import jax
import jax.numpy as jnp
import numpy as np
from jax import lax
from jax.experimental import pallas as pl
from jax.experimental.pallas import tpu as pltpu
from jax.experimental.pallas import tpu_sc as plsc

D_MODEL = 1024
D_CONV = 512
CONV_WIDTH = 31
N_Q_HEADS = 8
N_KV_HEADS = 2
HEAD_DIM = 64
Q_PER_KV = N_Q_HEADS // N_KV_HEADS
WINDOW = 128
N_BUCKETS = 32
MAX_DISTANCE = 128
N_GROUPS = 4
EXPERTS_PER_GROUP = 8
N_EXPERTS = N_GROUPS * EXPERTS_PER_GROUP
TOP_K = 2
D_EXPERT = 256
D_Q = N_Q_HEADS * HEAD_DIM
D_KV = N_KV_HEADS * HEAD_DIM
EPS = 1e-6
NEG_INF = -1e30

LANES = 128
SEQ_TILE = 512
CONV_ROWS = 32
PROJ_CHUNK = 256
CONV_SPAN = 24
MOE_BLOCK = 512
MOE_CHUNKS = 2
DEST_TILE = 8192
COMBINE_TILE = 512
SC_CORES = 2
SC_SUBCORES = 16
SC_WORKERS = SC_CORES * SC_SUBCORES
SC_CHUNK = 32
ROUTER_COLS = LANES
EXPERT_COL0 = 8
VMEM_LIMIT = 56 * 1024 * 1024

f32 = jnp.float32
bf16 = jnp.bfloat16
i32 = jnp.int32


def _dot(a, b):
    return jnp.dot(a, b, preferred_element_type=f32)


def _split(a):
    hi = a.astype(bf16)
    lo = (a - hi.astype(f32)).astype(bf16)
    return hi, lo


def _dot3(a, b):
    ah, al = _split(a)
    bh, bl = _split(b)
    return _dot(ah, bh) + _dot(al, bh) + _dot(ah, bl)


def _mod_kernel(c_ref, w_ref, b_ref, o_ref):
    c = c_ref[...]
    s = c * jax.nn.sigmoid(c)
    o_ref[...] = _dot3(s, w_ref[...]) + b_ref[...]


def _modulation(c, w_ada, b_ada):
    B, D = c.shape
    n_out = w_ada.shape[1]
    return pl.pallas_call(
        _mod_kernel,
        grid=(n_out // D,),
        in_specs=[pl.BlockSpec((B, D), lambda j: (0, 0)),
                  pl.BlockSpec((D, D), lambda j: (0, j)),
                  pl.BlockSpec((1, D), lambda j: (0, j))],
        out_specs=pl.BlockSpec((B, D), lambda j: (0, j)),
        out_shape=jax.ShapeDtypeStruct((B, n_out), f32),
        name="modulation",
    )(c, w_ada, b_ada.reshape(1, n_out))


def _band_buckets():
    qi = np.arange(WINDOW)[:, None]
    kj = np.arange(2 * WINDOW)[None, :]
    dist = np.clip(qi + WINDOW - kj, 0, MAX_DISTANCE)
    max_exact = N_BUCKETS // 2
    d = np.maximum(dist, 1).astype(np.float32)
    large = max_exact + (np.log(d / np.float32(max_exact)) / np.float32(math.log(MAX_DISTANCE / max_exact))
                         * np.float32(N_BUCKETS - max_exact)).astype(np.int32)
    large = np.minimum(large, N_BUCKETS - 1)
    return np.where(dist < max_exact, dist, large).astype(np.int32)


def _bias_kernel(tab_ref, bucket_ref, o_ref):
    bk = bucket_ref[...]
    for h in range(N_Q_HEADS):
        acc = jnp.zeros(bk.shape, f32)
        for b in range(N_BUCKETS):
            acc = jnp.where(bk == b, tab_ref[b, h], acc)
        o_ref[h] = acc


def _bias_band(rel_bias_table):
    return pl.pallas_call(
        _bias_kernel,
        in_specs=[pl.BlockSpec(memory_space=pltpu.SMEM),
                  pl.BlockSpec(memory_space=pltpu.VMEM)],
        out_specs=pl.BlockSpec(memory_space=pltpu.VMEM),
        out_shape=jax.ShapeDtypeStruct((N_Q_HEADS, WINDOW, 2 * WINDOW), f32),
        name="bias_band",
    )(rel_bias_table, jnp.asarray(_band_buckets()))


def _mixer_kernel(x_ref, mod_ref, gmix_ref, gffn_ref, wab_ref, wproj_ref, dw_ref, dwb_ref, lng_ref, lnb_ref,
                  wco_ref, wao_ref, wout_ref, qg_ref, kg_ref, bq_ref, bk_ref, bias_ref, sink_ref,
                  wrh_ref, wrl_ref, br_ref, tri_ref,
                  x1_ref, h2_ref, route_ref, wtok_ref, cnt_ref,
                  uext, ush, conv_sc, proj_sc, k2, v2, osc):
    TM = x_ref.shape[0]
    HALO = 32
    b = pl.program_id(0)
    t = pl.program_id(1)
    first = t == 0

    @pl.when(first)
    def _():
        uext[0:HALO, :] = jnp.zeros((HALO, D_CONV), f32)
        k2[:, 0:WINDOW, :] = jnp.zeros((N_KV_HEADS, WINDOW, LANES), bf16)
        v2[:, 0:WINDOW, :] = jnp.zeros((N_KV_HEADS, WINDOW, LANES), bf16)

    @pl.when(first & (b == 0))
    def _():
        cnt_ref[...] = jnp.zeros(cnt_ref.shape, f32)

    x = x_ref[...]
    mod = mod_ref[...]
    sh1, sc1, g1 = mod[0:1, :], mod[1:2, :], mod[2:3, :]
    sh2, sc2, g2 = mod[3:4, :], mod[4:5, :], mod[5:6, :]
    del g2

    ms = jnp.mean(x * x, axis=-1, keepdims=True)
    h = (x * lax.rsqrt(ms + EPS)) * gmix_ref[...]
    h = h * (1.0 + sc1) + sh1
    hb = h.astype(bf16)

    ab = _dot(hb, wab_ref[...])
    u = ab[:, 0:D_CONV] * jax.nn.sigmoid(ab[:, D_CONV:2 * D_CONV])
    uext[HALO:HALO + TM, :] = u
    for r in range(1, 8):
        ush[r - 1] = uext[r:r + TM + CONV_SPAN, :]
    n_q = D_Q // PROJ_CHUNK
    n_g = D_MODEL // PROJ_CHUNK
    n_jobs = wproj_ref.shape[0]
    n_rb = TM // CONV_ROWS
    n_t8 = CONV_ROWS // 8
    job_at = {(i * n_rb) // n_jobs: i for i in range(n_jobs)}
    assert len(job_at) == n_jobs
    for rb in range(n_rb):
        if rb in job_at:
            proj_sc[job_at[rb]] = _dot(hb, wproj_ref[job_at[rb]])
        base = rb * CONV_ROWS
        accs = [None] * n_t8
        for j in range(CONV_WIDTH):
            a, r = divmod(j + 2, 8)
            tap = dw_ref[j]
            for s8 in range(n_t8):
                rows = slice(base + 8 * (a + s8), base + 8 * (a + s8) + 8)
                win = uext[rows, :] if r == 0 else ush[r - 1, rows, :]
                term = tap * win
                accs[s8] = term if accs[s8] is None else accs[s8] + term
        for s8 in range(n_t8):
            conv_sc[base + 8 * s8:base + 8 * s8 + 8, :] = accs[s8] + dwb_ref[...]
    conv = conv_sc[...]
    uext[0:HALO, :] = uext[TM:TM + HALO, :]
    mu = jnp.mean(conv, axis=-1, keepdims=True)
    dc = conv - mu
    var = jnp.mean(dc * dc, axis=-1, keepdims=True)
    yn = dc * lax.rsqrt(var + EPS) * lng_ref[...] + lnb_ref[...]
    act = yn * jax.nn.sigmoid(yn)
    y_conv = _dot(act.astype(bf16), wco_ref[...])

    q = jnp.concatenate([proj_sc[i] for i in range(n_q)], axis=1)
    k = proj_sc[n_q, :, 0:D_KV]
    v = proj_sc[n_q, :, D_KV:2 * D_KV]

    def head_norm(z, blk_ref, g):
        msq = _dot((z * z).astype(bf16), blk_ref[...])
        return z * lax.rsqrt(msq + EPS) * g

    qn = head_norm(q, bq_ref, qg_ref[...]).astype(bf16)
    kn = head_norm(k, bk_ref, kg_ref[...])
    lo_half = lax.broadcasted_iota(i32, (TM, LANES), 1) < HEAD_DIM
    kr = pltpu.roll(kn, HEAD_DIM, 1)
    vr = pltpu.roll(v, HEAD_DIM, 1)
    k2[0, WINDOW:WINDOW + TM, :] = jnp.where(lo_half, kn, kr).astype(bf16)
    k2[1, WINDOW:WINDOW + TM, :] = jnp.where(lo_half, kr, kn).astype(bf16)
    v2[0, WINDOW:WINDOW + TM, :] = jnp.where(lo_half, v, vr).astype(bf16)
    v2[1, WINDOW:WINDOW + TM, :] = jnp.where(lo_half, vr, v).astype(bf16)

    QROWS = Q_PER_KV * WINDOW
    row = lax.broadcasted_iota(i32, (QROWS, 2 * WINDOW), 0) & (WINDOW - 1)
    col = lax.broadcasted_iota(i32, (QROWS, 2 * WINDOW), 1)
    dist = row + WINDOW - col
    in_window = (dist >= 0) & (dist < WINDOW)
    first_mask = in_window & (col >= jnp.where(first, WINDOW, 0))
    hrow = lax.broadcasted_iota(i32, (QROWS, 1), 0) // WINDOW
    lo128 = lax.broadcasted_iota(i32, (WINDOW, LANES), 1) < HEAD_DIM
    zero_q = jnp.zeros((WINDOW, LANES), bf16)
    for g in range(N_KV_HEADS):
        sink = jnp.where(hrow == 0, sink_ref[4 * g],
                         jnp.where(hrow == 1, sink_ref[4 * g + 1],
                                   jnp.where(hrow == 2, sink_ref[4 * g + 2], sink_ref[4 * g + 3])))
        bias_g = bias_ref[g]
        for j in range(TM // WINDOW):
            rs = slice(j * WINDOW, (j + 1) * WINDOW)
            qa = qn[rs, 2 * LANES * g:2 * LANES * g + LANES]
            qb = qn[rs, 2 * LANES * g + LANES:2 * LANES * (g + 1)]
            qs = jnp.concatenate([jnp.where(lo128, qa, zero_q), jnp.where(lo128, zero_q, qa),
                                  jnp.where(lo128, qb, zero_q), jnp.where(lo128, zero_q, qb)], axis=0)
            kk = k2[g, j * WINDOW:(j + 2) * WINDOW, :]
            s = lax.dot_general(qs, kk, (((1,), (1,)), ((), ())), preferred_element_type=f32)
            msk = first_mask if j == 0 else in_window
            logits = jnp.where(msk, s + bias_g, NEG_INF)
            m = jnp.maximum(jnp.max(logits, axis=-1, keepdims=True), sink)
            p = jnp.exp(logits - m)
            den = jnp.sum(p, axis=-1, keepdims=True) + jnp.exp(sink - m)
            o2 = _dot(p.astype(bf16), v2[g, j * WINDOW:(j + 2) * WINDOW, :]) * (1.0 / den)
            osc[rs, 2 * LANES * g:2 * LANES * g + LANES] = jnp.where(
                lo128, o2[0:WINDOW], o2[WINDOW:2 * WINDOW]).astype(bf16)
            osc[rs, 2 * LANES * g + LANES:2 * LANES * (g + 1)] = jnp.where(
                lo128, o2[2 * WINDOW:3 * WINDOW], o2[3 * WINDOW:4 * WINDOW]).astype(bf16)
    k2[:, 0:WINDOW, :] = k2[:, TM:TM + WINDOW, :]
    v2[:, 0:WINDOW, :] = v2[:, TM:TM + WINDOW, :]
    y_attn = _dot(osc[...], wao_ref[...])

    merged = []
    for i in range(n_g):
        cs = slice(i * PROJ_CHUNK, (i + 1) * PROJ_CHUNK)
        g_conv = jax.nn.sigmoid(proj_sc[n_q + 1 + i])
        g_attn = jax.nn.sigmoid(proj_sc[n_q + 1 + n_g + i])
        merged.append((g_conv * y_conv[:, cs] + g_attn * y_attn[:, cs]).astype(bf16))
    merged = jnp.concatenate(merged, axis=1)
    x1 = x + g1 * _dot(merged, wout_ref[...])
    x1_ref[...] = x1

    ms2 = jnp.mean(x1 * x1, axis=-1, keepdims=True)
    h2 = (x1 * lax.rsqrt(ms2 + EPS)) * gffn_ref[...]
    h2 = h2 * (1.0 + sc2) + sh2
    h2_ref[...] = h2
    hh, hl = _split(h2)
    lg = _dot(hh, wrh_ref[...]) + _dot(hl, wrh_ref[...]) + _dot(hh, wrl_ref[...]) + br_ref[...]
    lt = lg.T
    gl = lt[0:N_GROUPS, :]
    grow = lax.broadcasted_iota(i32, (N_GROUPS, TM), 0)
    gmax = jnp.max(gl, axis=0, keepdims=True)
    gi = jnp.min(jnp.where(gl == gmax, grow, N_GROUPS), axis=0, keepdims=True)
    p_top = 1.0 / jnp.sum(jnp.exp(gl - gmax), axis=0, keepdims=True)
    sel = lt[EXPERT_COL0:EXPERT_COL0 + EXPERTS_PER_GROUP, :]
    for gg in range(1, N_GROUPS):
        lo_r = EXPERT_COL0 + gg * EXPERTS_PER_GROUP
        sel = jnp.where(gi == gg, lt[lo_r:lo_r + EXPERTS_PER_GROUP, :], sel)
    erow = lax.broadcasted_iota(i32, (EXPERTS_PER_GROUP, TM), 0)
    m1 = jnp.max(sel, axis=0, keepdims=True)
    i1 = jnp.min(jnp.where(sel == m1, erow, EXPERTS_PER_GROUP), axis=0, keepdims=True)
    rest = jnp.where(erow == i1, -jnp.inf, sel)
    m2 = jnp.max(rest, axis=0, keepdims=True)
    i2 = jnp.min(jnp.where(rest == m2, erow, EXPERTS_PER_GROUP), axis=0, keepdims=True)
    z = jnp.sum(jnp.exp(sel - m1), axis=0, keepdims=True)
    v1 = 1.0 / z
    v2nd = jnp.exp(m2 - m1) / z
    w1 = v1 / (v1 + v2nd) * p_top
    w2 = v2nd / (v1 + v2nd) * p_top
    e1 = gi * EXPERTS_PER_GROUP + i1
    e2 = gi * EXPERTS_PER_GROUP + i2

    xrow = lax.broadcasted_iota(i32, (N_EXPERTS, TM), 0)
    oh1 = xrow == e1
    oh2 = xrow == e2
    both = jnp.where(oh1 | oh2, 1.0, 0.0)
    prefix = _dot(both.astype(bf16), tri_ref[...]) + cnt_ref[:, 0:1]
    r1 = jnp.sum(jnp.where(oh1, prefix, 0.0), axis=0, keepdims=True)
    r2 = jnp.sum(jnp.where(oh2, prefix, 0.0), axis=0, keepdims=True)
    cnt_ref[...] = cnt_ref[...] + jnp.sum(both, axis=1, keepdims=True)

    route_ref[...] = jnp.concatenate(
        [e1, e2, r1.astype(i32), r2.astype(i32), jnp.zeros((4, TM), i32)], axis=0)
    wpad = jnp.concatenate([w1, w2, jnp.zeros((LANES - 2, TM), f32)], axis=0)
    wtok_ref[...] = wpad.T


def _mixer(x, mod6, p, b0, B):
    _, T, D = x.shape
    TM = min(SEQ_TILE, T)
    nt = T // TM
    N = B * T
    const = lambda shape: pl.BlockSpec(shape, lambda b, t: (0,) * len(shape))
    in_specs = [
        pl.BlockSpec((None, TM, D), lambda b, t: (b0 + b, t, 0)),
        pl.BlockSpec((None, 6, D), lambda b, t: (b0 + b, 0, 0)),
        const((1, D)), const((1, D)),
        const(p["wab"].shape), const(p["wproj"].shape),
        const((CONV_WIDTH, 8, D_CONV)), const((1, D_CONV)), const((1, D_CONV)), const((1, D_CONV)),
        const((D_CONV, D)), const((D_Q, D)), const((D, D)),
        const((1, D_Q)), const((1, D_KV)),
        const((D_Q, D_Q)), const((D_KV, D_KV)),
        const((N_KV_HEADS, Q_PER_KV * WINDOW, 2 * WINDOW)),
        pl.BlockSpec(memory_space=pltpu.SMEM),
        const((D, ROUTER_COLS)), const((D, ROUTER_COLS)), const((1, ROUTER_COLS)),
        const((TM, TM)),
    ]
    out_specs = [
        pl.BlockSpec((None, TM, D), lambda b, t: (b, t, 0)),
        pl.BlockSpec((None, TM, D), lambda b, t: (b, t, 0)),
        pl.BlockSpec((8, TM), lambda b, t: (0, b * nt + t)),
        pl.BlockSpec((TM, LANES), lambda b, t: (b * nt + t, 0)),
        pl.BlockSpec((N_EXPERTS, LANES), lambda b, t: (0, 0)),
    ]
    out_shape = [
        jax.ShapeDtypeStruct((B, T, D), f32),
        jax.ShapeDtypeStruct((B, T, D), f32),
        jax.ShapeDtypeStruct((8, N), i32),
        jax.ShapeDtypeStruct((N, LANES), f32),
        jax.ShapeDtypeStruct((N_EXPERTS, LANES), f32),
    ]
    scratch = [
        pltpu.VMEM((TM + 32, D_CONV), f32),
        pltpu.VMEM((7, TM + CONV_SPAN, D_CONV), f32),
        pltpu.VMEM((TM, D_CONV), f32),
        pltpu.VMEM((p["wproj"].shape[0], TM, PROJ_CHUNK), f32),
        pltpu.VMEM((N_KV_HEADS, TM + WINDOW, LANES), bf16),
        pltpu.VMEM((N_KV_HEADS, TM + WINDOW, LANES), bf16),
        pltpu.VMEM((TM, D_Q), bf16),
    ]
    tri = jnp.asarray(np.triu(np.ones((TM, TM), np.float32), 1), bf16)
    return pl.pallas_call(
        _mixer_kernel,
        grid=(B, nt),
        in_specs=in_specs, out_specs=out_specs, out_shape=out_shape, scratch_shapes=scratch,
        compiler_params=pltpu.CompilerParams(
            dimension_semantics=("arbitrary", "arbitrary"), vmem_limit_bytes=VMEM_LIMIT),
        name="mixer_router",
    )(x, mod6, p["gmix"], p["gffn"], p["wab"], p["wproj"], p["dw"], p["dwb"], p["lng"], p["lnb"],
      p["wco"], p["wao"], p["wout"], p["qg"], p["kg"], p["bq"], p["bk"], p["bias"], p["sinks"],
      p["wrh"], p["wrl"], p["br"], tri)


def _dest_kernel(pstart_ref, route_ref, dest_ref):
    e = route_ref[0:2, :]
    base = jnp.zeros(e.shape, i32)
    for x in range(N_EXPERTS):
        base = jnp.where(e == x, pstart_ref[x], base)
    dest_ref[...] = base + route_ref[2:4, :]


def _dest_slots(pstart, route):
    N = route.shape[1]
    TS = min(DEST_TILE, N)
    return pl.pallas_call(
        _dest_kernel,
        grid=(N // TS,),
        in_specs=[pl.BlockSpec(memory_space=pltpu.SMEM),
                  pl.BlockSpec((8, TS), lambda i: (0, i))],
        out_specs=pl.BlockSpec((2, TS), lambda i: (0, i)),
        out_shape=jax.ShapeDtypeStruct((2, N), i32),
        name="dest_slots",
    )(pstart, route)


def _sc_mesh():
    return plsc.VectorSubcoreMesh(core_axis_name="c", subcore_axis_name="s")


def _sc_scatter(h2, dest, n_slots):
    N, D = h2.shape
    per_w = N // SC_WORKERS
    C = SC_CHUNK
    n_chunks = per_w // C
    dest4 = dest.reshape(TOP_K, SC_WORKERS, n_chunks, C)

    @functools.partial(
        pl.kernel, mesh=_sc_mesh(),
        out_type=jax.ShapeDtypeStruct((n_slots, D), f32),
        scratch_types=[pltpu.VMEM((n_chunks, C), i32), pltpu.VMEM((n_chunks, C), i32),
                       pltpu.VMEM((2, C, D), f32),
                       pltpu.SemaphoreType.DMA((2,)), pltpu.SemaphoreType.DMA((2,))],
        name="sc_dispatch_scatter",
    )
    def run(h2_hbm, dest_hbm, xs_hbm, idx0, idx1, buf, sem_in, sem_out):
        wid = lax.axis_index("s") * SC_CORES + lax.axis_index("c")
        base = wid * per_w
        pltpu.sync_copy(dest_hbm.at[0, wid], idx0)
        pltpu.sync_copy(dest_hbm.at[1, wid], idx1)

        def load(j, b):
            return pltpu.make_async_copy(h2_hbm.at[pl.ds(base + j * C, C)], buf.at[b], sem_in.at[b])

        def put(j, b, idx):
            return pltpu.make_async_copy(buf.at[b], xs_hbm.at[idx.at[j]], sem_out.at[b])

        load(0, 0).start()

        @pl.loop(0, n_chunks, step=2)
        def _(j0):
            for b in range(2):
                j = j0 + b
                load(j, b).wait()

                @pl.when(j >= 1)
                def _():
                    put(j - 1, 1 - b, idx0).wait()
                    put(j - 1, 1 - b, idx1).wait()

                @pl.when(j + 1 < n_chunks)
                def _():
                    load(j + 1, 1 - b).start()

                put(j, b, idx0).start()
                put(j, b, idx1).start()

        put(n_chunks - 1, 1, idx0).wait()
        put(n_chunks - 1, 1, idx1).wait()

    return run(h2, dest4)


def _sc_gather(y, dest, N):
    D = y.shape[1]
    per_w = N // SC_WORKERS
    C = SC_CHUNK
    n_chunks = per_w // C
    dest4 = dest.reshape(TOP_K, SC_WORKERS, n_chunks, C)

    @functools.partial(
        pl.kernel, mesh=_sc_mesh(),
        out_type=jax.ShapeDtypeStruct((TOP_K, N, D), f32),
        scratch_types=[pltpu.VMEM((n_chunks, C), i32), pltpu.VMEM((n_chunks, C), i32),
                       pltpu.VMEM((2, C, D), f32),
                       pltpu.SemaphoreType.DMA((2,)), pltpu.SemaphoreType.DMA((2,))],
        name="sc_combine_gather",
    )
    def run(y_hbm, dest_hbm, yg_hbm, idx0, idx1, buf, sem_in, sem_out):
        wid = lax.axis_index("s") * SC_CORES + lax.axis_index("c")
        base = wid * per_w
        pltpu.sync_copy(dest_hbm.at[0, wid], idx0)
        pltpu.sync_copy(dest_hbm.at[1, wid], idx1)
        idx = (idx0, idx1)

        def get(j, k):
            return pltpu.make_async_copy(y_hbm.at[idx[k].at[j]], buf.at[k], sem_in.at[k])

        def put(j, k):
            return pltpu.make_async_copy(buf.at[k], yg_hbm.at[k, pl.ds(base + j * C, C)], sem_out.at[k])

        get(0, 0).start()
        get(0, 1).start()

        @pl.loop(0, n_chunks)
        def _(j):
            for k in range(TOP_K):
                get(j, k).wait()
                put(j, k).start()
            for k in range(TOP_K):
                put(j, k).wait()

                @pl.when(j + 1 < n_chunks)
                def _():
                    get(j + 1, k).start()

    return run(y, dest4)


def _expert_kernel(be_ref, nv_ref, nu_ref, xs_ref, wgu_ref, wd_ref, y_ref):
    i = pl.program_id(0)

    @pl.when(i < nu_ref[0])
    def _():
        live = lax.broadcasted_iota(i32, (MOE_BLOCK, 1), 0) < nv_ref[i]
        xb = jnp.where(live, xs_ref[...], 0.0).astype(bf16)
        gu = _dot(xb, wgu_ref[...])
        gate = gu[:, 0:D_EXPERT]
        hid = (gate * jax.nn.sigmoid(gate)) * gu[:, D_EXPERT:2 * D_EXPERT]
        y_ref[...] = _dot(hid.astype(bf16), wd_ref[...])


def _experts(block_e, n_valid, n_used, xs, wgu, wd):
    n_slots, D = xs.shape
    n_blocks = n_slots // MOE_BLOCK
    last = lambda i, nu: jnp.minimum(i, nu[0] - 1)
    grid_spec = pltpu.PrefetchScalarGridSpec(
        num_scalar_prefetch=3,
        grid=(n_blocks,),
        in_specs=[pl.BlockSpec((MOE_BLOCK, D), lambda i, be, nv, nu: (last(i, nu), 0)),
                  pl.BlockSpec((None, D, 2 * D_EXPERT), lambda i, be, nv, nu: (be[last(i, nu)], 0, 0)),
                  pl.BlockSpec((None, D_EXPERT, D), lambda i, be, nv, nu: (be[last(i, nu)], 0, 0))],
        out_specs=pl.BlockSpec((MOE_BLOCK, D), lambda i, be, nv, nu: (last(i, nu), 0)),
    )
    return pl.pallas_call(
        _expert_kernel,
        grid_spec=grid_spec,
        out_shape=jax.ShapeDtypeStruct((n_slots, D), f32),
        compiler_params=pltpu.CompilerParams(dimension_semantics=("arbitrary",),
                                             vmem_limit_bytes=VMEM_LIMIT),
        name="experts",
    )(block_e, n_valid, n_used, xs, wgu, wd)


def _combine_kernel(yg_ref, x1_ref, wtok_ref, mod_ref, *rest):
    o_ref = rest[-1]
    w = wtok_ref[...]
    g2 = mod_ref[5:6, :]
    moe = w[:, 0:1] * yg_ref[0] + w[:, 1:2] * yg_ref[1]
    o_ref[...] = x1_ref[...] + g2 * moe


def _combine(yg, x1, wtok, mod6, T, b0, n_total, out_prev):
    Nc, D = x1.shape
    TS = min(COMBINE_TILE, T)
    per_seq = T // TS
    blk0 = b0 * per_seq
    in_specs = [pl.BlockSpec((TOP_K, TS, D), lambda i: (0, i, 0)),
                pl.BlockSpec((TS, D), lambda i: (i, 0)),
                pl.BlockSpec((TS, LANES), lambda i: (i, 0)),
                pl.BlockSpec((None, 6, D), lambda i: (b0 + i // per_seq, 0, 0))]
    args = [yg, x1, wtok, mod6]
    aliases = {}
    if out_prev is not None:
        in_specs.append(pl.BlockSpec(memory_space=pl.ANY))
        args.append(out_prev)
        aliases = {len(args) - 1: 0}
    return pl.pallas_call(
        _combine_kernel,
        grid=(Nc // TS,),
        in_specs=in_specs,
        out_specs=pl.BlockSpec((TS, D), lambda i: (blk0 + i, 0)),
        out_shape=jax.ShapeDtypeStruct((n_total, D), f32),
        input_output_aliases=aliases,
        compiler_params=pltpu.CompilerParams(dimension_semantics=("arbitrary",)),
        name="combine",
    )(*args)


def _block_diag_mean(n, blk):
    m = np.kron(np.eye(n // blk, dtype=np.float32), np.full((blk, blk), 1.0 / blk, np.float32))
    return jnp.asarray(m, bf16)


def _layer(x, mod6, bias, l, w):
    B, T, D = x.shape
    N = B * T
    w_rg, w_re = w["w_router_group"][l], w["w_router_expert"][l]
    wr = jnp.zeros((D, ROUTER_COLS), f32)
    wr = wr.at[:, 0:N_GROUPS].set(w_rg).at[:, EXPERT_COL0:EXPERT_COL0 + N_EXPERTS].set(w_re)
    br = jnp.zeros((1, ROUTER_COLS), f32)
    br = br.at[0, 0:N_GROUPS].set(w["b_router_group"][l])
    br = br.at[0, EXPERT_COL0:EXPERT_COL0 + N_EXPERTS].set(w["b_router_expert"][l])
    wrh = wr.astype(bf16)
    wrl = (wr - wrh.astype(f32)).astype(bf16)
    p = dict(
        gmix=w["norm_mix_g"][l].reshape(1, D), gffn=w["norm_ffn_g"][l].reshape(1, D),
        wab=w["w_in"][l][:, 0:2 * D_CONV].astype(bf16),
        wproj=w["w_in"][l][:, 2 * D_CONV:].astype(bf16).reshape(D, -1, PROJ_CHUNK).transpose(1, 0, 2),
        dw=jnp.broadcast_to(w["dw_kernel"][l][:, None, :], (CONV_WIDTH, 8, D_CONV)), dwb=w["dw_bias"][l].reshape(1, D_CONV),
        lng=w["conv_ln_g"][l].reshape(1, D_CONV), lnb=w["conv_ln_b"][l].reshape(1, D_CONV),
        wco=w["w_conv_out"][l].astype(bf16), wao=w["w_attn_out"][l].astype(bf16),
        wout=w["w_out"][l].astype(bf16),
        qg=(jnp.tile(w["q_norm_g"][l], N_Q_HEADS) * (HEAD_DIM ** -0.5)).reshape(1, D_Q),
        kg=jnp.tile(w["k_norm_g"][l], N_KV_HEADS).reshape(1, D_KV),
        bq=_block_diag_mean(D_Q, HEAD_DIM), bk=_block_diag_mean(D_KV, HEAD_DIM),
        bias=bias, sinks=w["sinks"][l], wrh=wrh, wrl=wrl, br=br,
    )
    wgu = jnp.concatenate([w["w_exp_gate"][l], w["w_exp_up"][l]], axis=-1).astype(bf16)
    wd = w["w_exp_down"][l].astype(bf16)

    n_chunks = MOE_CHUNKS if B % MOE_CHUNKS == 0 else 1
    Bc = B // n_chunks
    Nc = Bc * T
    n_blocks = -(-(Nc * TOP_K) // MOE_BLOCK) + N_EXPERTS
    blk0 = jnp.arange(n_blocks, dtype=i32) * MOE_BLOCK
    stage = []
    for ch in range(n_chunks):
        x1, h2, route, wtok, cnt = _mixer(x, mod6, p, ch * Bc, Bc)
        counts = cnt[:, 0].astype(i32)
        pcounts = (counts + MOE_BLOCK - 1) // MOE_BLOCK * MOE_BLOCK
        pend = jnp.cumsum(pcounts)
        pstart = pend - pcounts
        block_e = jnp.minimum(jnp.sum((pend[None, :] <= blk0[:, None]).astype(i32), axis=1), N_EXPERTS - 1)
        n_valid = jnp.clip((pstart + counts)[block_e] - blk0, 0, MOE_BLOCK).astype(i32)
        n_used = (pend[-1:] // MOE_BLOCK).astype(i32)
        dest = _dest_slots(pstart, route)
        xs = _sc_scatter(h2.reshape(Nc, D), dest, n_blocks * MOE_BLOCK)
        stage.append((x1, wtok, dest, xs, block_e, n_valid, n_used))
    ys = [_experts(be, nv, nu, xs, wgu, wd) for (_, _, _, xs, be, nv, nu) in stage]
    ygs = [_sc_gather(y, st[2], Nc) for y, st in zip(ys, stage)]
    out = None
    for ch in range(n_chunks):
        x1, wtok = stage[ch][0], stage[ch][1]
        out = _combine(ygs[ch], x1.reshape(Nc, D), wtok, mod6, T, ch * Bc, B * T, out)
    return out.reshape(B, T, D)


def kernel(x, c, w_ada, b_ada, norm_mix_g, w_in, dw_kernel, dw_bias, conv_ln_g, conv_ln_b,
           w_conv_out, q_norm_g, k_norm_g, sinks, w_attn_out, w_out, rel_bias_table, norm_ffn_g,
           w_router_group, b_router_group, w_router_expert, b_router_expert,
           w_exp_gate, w_exp_up, w_exp_down):
    w = dict(norm_mix_g=norm_mix_g, w_in=w_in, dw_kernel=dw_kernel, dw_bias=dw_bias,
             conv_ln_g=conv_ln_g, conv_ln_b=conv_ln_b, w_conv_out=w_conv_out, q_norm_g=q_norm_g,
             k_norm_g=k_norm_g, sinks=sinks, w_attn_out=w_attn_out, w_out=w_out,
             norm_ffn_g=norm_ffn_g, w_router_group=w_router_group, b_router_group=b_router_group,
             w_router_expert=w_router_expert, b_router_expert=b_router_expert,
             w_exp_gate=w_exp_gate, w_exp_up=w_exp_up, w_exp_down=w_exp_down)
    B = x.shape[0]
    bias = _bias_band(rel_bias_table).reshape(N_KV_HEADS, Q_PER_KV * WINDOW, 2 * WINDOW)
    for l in range(w_ada.shape[0]):
        mod6 = _modulation(c, w_ada[l], b_ada[l]).reshape(B, 6, D_MODEL)
        x = _layer(x, mod6, bias, l, w)
    return x
```

```python
import functools
import math

import jax
import jax.numpy as jnp
import numpy as np
from jax import lax
from jax.experimental import pallas as pl
from jax.experimental.pallas import tpu as pltpu
from jax.experimental.pallas import tpu_sc as plsc

D_MODEL = 1024
D_CONV = 512
CONV_WIDTH = 31
N_Q_HEADS = 8
N_KV_HEADS = 2
HEAD_DIM = 64
Q_PER_KV = N_Q_HEADS // N_KV_HEADS
WINDOW = 128
N_BUCKETS = 32
MAX_DISTANCE = 128
N_GROUPS = 4
EXPERTS_PER_GROUP = 8
N_EXPERTS = N_GROUPS * EXPERTS_PER_GROUP
TOP_K = 2
D_EXPERT = 256
D_Q = N_Q_HEADS * HEAD_DIM
D_KV = N_KV_HEADS * HEAD_DIM
EPS = 1e-6
NEG_INF = -1e30

LANES = 128
SEQ_TILE = 512
CONV_ROWS = 32
PROJ_CHUNK = 256
CONV_SPAN = 24
MOE_BLOCK = 512
MOE_CHUNKS = 2
DEST_TILE = 8192
COMBINE_TILE = 512
SC_CORES = 2
SC_SUBCORES = 16
SC_WORKERS = SC_CORES * SC_SUBCORES
SC_CHUNK = 32
ROUTER_COLS = LANES
EXPERT_COL0 = 8
VMEM_LIMIT = 56 * 1024 * 1024

f32 = jnp.float32
bf16 = jnp.bfloat16
i32 = jnp.int32


def _dot(a, b):
    return jnp.dot(a, b, preferred_element_type=f32)


def _split(a):
    hi = a.astype(bf16)
    lo = (a - hi.astype(f32)).astype(bf16)
    return hi, lo


def _pack_halves(x):
    c = x.shape[1] // 2
    hi = lax.bitcast_convert_type(x[:, 0:c].astype(bf16).astype(f32), jnp.uint32)
    lo = lax.bitcast_convert_type(x[:, c:2 * c].astype(bf16).astype(f32), jnp.uint32)
    word = (hi & jnp.uint32(0xFFFF0000)) | (lo >> jnp.uint32(16))
    return lax.bitcast_convert_type(word, i32)


def _unpack_halves(word):
    u = lax.bitcast_convert_type(word, jnp.uint32)
    hi = lax.bitcast_convert_type(u & jnp.uint32(0xFFFF0000), f32)
    lo = lax.bitcast_convert_type(u << jnp.uint32(16), f32)
    return hi, lo


def _dot3(a, b):
    ah, al = _split(a)
    bh, bl = _split(b)
    return _dot(ah, bh) + _dot(al, bh) + _dot(ah, bl)


def _mod_kernel(c_ref, w_ref, b_ref, o_ref):
    c = c_ref[...]
    s = c * jax.nn.sigmoid(c)
    o_ref[...] = _dot3(s, w_ref[...]) + b_ref[...]


def _modulation(c, w_ada, b_ada):
    B, D = c.shape
    n_out = w_ada.shape[1]
    return pl.pallas_call(
        _mod_kernel,
        grid=(n_out // D,),
        in_specs=[pl.BlockSpec((B, D), lambda j: (0, 0)),
                  pl.BlockSpec((D, D), lambda j: (0, j)),
                  pl.BlockSpec((1, D), lambda j: (0, j))],
        out_specs=pl.BlockSpec((B, D), lambda j: (0, j)),
        out_shape=jax.ShapeDtypeStruct((B, n_out), f32),
        name="modulation",
    )(c, w_ada, b_ada.reshape(1, n_out))


def _band_buckets():
    qi = np.arange(WINDOW)[:, None]
    kj = np.arange(2 * WINDOW)[None, :]
    dist = np.clip(qi + WINDOW - kj, 0, MAX_DISTANCE)
    max_exact = N_BUCKETS // 2
    d = np.maximum(dist, 1).astype(np.float32)
    large = max_exact + (np.log(d / np.float32(max_exact)) / np.float32(math.log(MAX_DISTANCE / max_exact))
                         * np.float32(N_BUCKETS - max_exact)).astype(np.int32)
    large = np.minimum(large, N_BUCKETS - 1)
    return np.where(dist < max_exact, dist, large).astype(np.int32)


def _bias_kernel(tab_ref, bucket_ref, o_ref):
    bk = bucket_ref[...]
    for h in range(N_Q_HEADS):
        acc = jnp.zeros(bk.shape, f32)
        for b in range(N_BUCKETS):
            acc = jnp.where(bk == b, tab_ref[b, h], acc)
        o_ref[h] = acc


def _bias_band(rel_bias_table):
    return pl.pallas_call(
        _bias_kernel,
        in_specs=[pl.BlockSpec(memory_space=pltpu.SMEM),
                  pl.BlockSpec(memory_space=pltpu.VMEM)],
        out_specs=pl.BlockSpec(memory_space=pltpu.VMEM),
        out_shape=jax.ShapeDtypeStruct((N_Q_HEADS, WINDOW, 2 * WINDOW), f32),
        name="bias_band",
    )(rel_bias_table, jnp.asarray(_band_buckets()))


def _mixer_kernel(x_ref, mod_ref, gmix_ref, gffn_ref, wab_ref, wproj_ref, dw_ref, dwb_ref, lng_ref, lnb_ref,
                  wco_ref, wao_ref, wout_ref, qg_ref, kg_ref, bq_ref, bk_ref, bias_ref, sink_ref,
                  wrh_ref, wrl_ref, br_ref, tri_ref,
                  x1_ref, h2_ref, route_ref, wtok_ref, cnt_ref,
                  uext, ush, conv_sc, proj_sc, k2, v2, osc):
    TM = x_ref.shape[0]
    HALO = 32
    b = pl.program_id(0)
    t = pl.program_id(1)
    first = t == 0

    @pl.when(first)
    def _():
        uext[0:HALO, :] = jnp.zeros((HALO, D_CONV), f32)
        k2[:, 0:WINDOW, :] = jnp.zeros((N_KV_HEADS, WINDOW, LANES), bf16)
        v2[:, 0:WINDOW, :] = jnp.zeros((N_KV_HEADS, WINDOW, LANES), bf16)

    @pl.when(first & (b == 0))
    def _():
        cnt_ref[...] = jnp.zeros(cnt_ref.shape, f32)

    x = x_ref[...]
    mod = mod_ref[...]
    sh1, sc1, g1 = mod[0:1, :], mod[1:2, :], mod[2:3, :]
    sh2, sc2, g2 = mod[3:4, :], mod[4:5, :], mod[5:6, :]
    del g2

    ms = jnp.mean(x * x, axis=-1, keepdims=True)
    h = (x * lax.rsqrt(ms + EPS)) * gmix_ref[...]
    h = h * (1.0 + sc1) + sh1
    hb = h.astype(bf16)

    ab = _dot(hb, wab_ref[...])
    u = ab[:, 0:D_CONV] * jax.nn.sigmoid(ab[:, D_CONV:2 * D_CONV])
    uext[HALO:HALO + TM, :] = u
    for r in range(1, 8):
        ush[r - 1] = uext[r:r + TM + CONV_SPAN, :]
    n_q = D_Q // PROJ_CHUNK
    n_g = D_MODEL // PROJ_CHUNK
    n_jobs = wproj_ref.shape[0]
    n_rb = TM // CONV_ROWS
    n_t8 = CONV_ROWS // 8
    job_at = {(i * n_rb) // n_jobs: i for i in range(n_jobs)}
    assert len(job_at) == n_jobs
    for rb in range(n_rb):
        if rb in job_at:
            proj_sc[job_at[rb]] = _dot(hb, wproj_ref[job_at[rb]])
        base = rb * CONV_ROWS
        accs = [None] * n_t8
        for j in range(CONV_WIDTH):
            a, r = divmod(j + 2, 8)
            tap = dw_ref[j]
            for s8 in range(n_t8):
                rows = slice(base + 8 * (a + s8), base + 8 * (a + s8) + 8)
                win = uext[rows, :] if r == 0 else ush[r - 1, rows, :]
                term = tap * win
                accs[s8] = term if accs[s8] is None else accs[s8] + term
        for s8 in range(n_t8):
            conv_sc[base + 8 * s8:base + 8 * s8 + 8, :] = accs[s8] + dwb_ref[...]
    conv = conv_sc[...]
    uext[0:HALO, :] = uext[TM:TM + HALO, :]
    mu = jnp.mean(conv, axis=-1, keepdims=True)
    dc = conv - mu
    var = jnp.mean(dc * dc, axis=-1, keepdims=True)
    yn = dc * lax.rsqrt(var + EPS) * lng_ref[...] + lnb_ref[...]
    act = yn * jax.nn.sigmoid(yn)
    y_conv = _dot(act.astype(bf16), wco_ref[...])

    q = jnp.concatenate([proj_sc[i] for i in range(n_q)], axis=1)
    k = proj_sc[n_q, :, 0:D_KV]
    v = proj_sc[n_q, :, D_KV:2 * D_KV]

    def head_norm(z, blk_ref, g):
        msq = _dot((z * z).astype(bf16), blk_ref[...])
        return z * lax.rsqrt(msq + EPS) * g

    qn = head_norm(q, bq_ref, qg_ref[...]).astype(bf16)
    kn = head_norm(k, bk_ref, kg_ref[...])
    lo_half = lax.broadcasted_iota(i32, (TM, LANES), 1) < HEAD_DIM
    kr = pltpu.roll(kn, HEAD_DIM, 1)
    vr = pltpu.roll(v, HEAD_DIM, 1)
    k2[0, WINDOW:WINDOW + TM, :] = jnp.where(lo_half, kn, kr).astype(bf16)
    k2[1, WINDOW:WINDOW + TM, :] = jnp.where(lo_half, kr, kn).astype(bf16)
    v2[0, WINDOW:WINDOW + TM, :] = jnp.where(lo_half, v, vr).astype(bf16)
    v2[1, WINDOW:WINDOW + TM, :] = jnp.where(lo_half, vr, v).astype(bf16)

    QROWS = Q_PER_KV * WINDOW
    row = lax.broadcasted_iota(i32, (QROWS, 2 * WINDOW), 0) & (WINDOW - 1)
    col = lax.broadcasted_iota(i32, (QROWS, 2 * WINDOW), 1)
    dist = row + WINDOW - col
    in_window = (dist >= 0) & (dist < WINDOW)
    first_mask = in_window & (col >= jnp.where(first, WINDOW, 0))
    hrow = lax.broadcasted_iota(i32, (QROWS, 1), 0) // WINDOW
    lo128 = lax.broadcasted_iota(i32, (WINDOW, LANES), 1) < HEAD_DIM
    zero_q = jnp.zeros((WINDOW, LANES), bf16)
    for g in range(N_KV_HEADS):
        sink = jnp.where(hrow == 0, sink_ref[4 * g],
                         jnp.where(hrow == 1, sink_ref[4 * g + 1],
                                   jnp.where(hrow == 2, sink_ref[4 * g + 2], sink_ref[4 * g + 3])))
        bias_g = bias_ref[g]
        for j in range(TM // WINDOW):
            rs = slice(j * WINDOW, (j + 1) * WINDOW)
            qa = qn[rs, 2 * LANES * g:2 * LANES * g + LANES]
            qb = qn[rs, 2 * LANES * g + LANES:2 * LANES * (g + 1)]
            qs = jnp.concatenate([jnp.where(lo128, qa, zero_q), jnp.where(lo128, zero_q, qa),
                                  jnp.where(lo128, qb, zero_q), jnp.where(lo128, zero_q, qb)], axis=0)
            kk = k2[g, j * WINDOW:(j + 2) * WINDOW, :]
            s = lax.dot_general(qs, kk, (((1,), (1,)), ((), ())), preferred_element_type=f32)
            msk = first_mask if j == 0 else in_window
            logits = jnp.where(msk, s + bias_g, NEG_INF)
            m = jnp.maximum(jnp.max(logits, axis=-1, keepdims=True), sink)
            p = jnp.exp(logits - m)
            den = jnp.sum(p, axis=-1, keepdims=True) + jnp.exp(sink - m)
            o2 = _dot(p.astype(bf16), v2[g, j * WINDOW:(j + 2) * WINDOW, :]) * (1.0 / den)
            osc[rs, 2 * LANES * g:2 * LANES * g + LANES] = jnp.where(
                lo128, o2[0:WINDOW], o2[WINDOW:2 * WINDOW]).astype(bf16)
            osc[rs, 2 * LANES * g + LANES:2 * LANES * (g + 1)] = jnp.where(
                lo128, o2[2 * WINDOW:3 * WINDOW], o2[3 * WINDOW:4 * WINDOW]).astype(bf16)
    k2[:, 0:WINDOW, :] = k2[:, TM:TM + WINDOW, :]
    v2[:, 0:WINDOW, :] = v2[:, TM:TM + WINDOW, :]
    y_attn = _dot(osc[...], wao_ref[...])

    merged = []
    for i in range(n_g):
        cs = slice(i * PROJ_CHUNK, (i + 1) * PROJ_CHUNK)
        g_conv = jax.nn.sigmoid(proj_sc[n_q + 1 + i])
        g_attn = jax.nn.sigmoid(proj_sc[n_q + 1 + n_g + i])
        merged.append((g_conv * y_conv[:, cs] + g_attn * y_attn[:, cs]).astype(bf16))
    merged = jnp.concatenate(merged, axis=1)
    x1 = x + g1 * _dot(merged, wout_ref[...])
    x1_ref[...] = x1

    ms2 = jnp.mean(x1 * x1, axis=-1, keepdims=True)
    h2 = (x1 * lax.rsqrt(ms2 + EPS)) * gffn_ref[...]
    h2 = h2 * (1.0 + sc2) + sh2
    h2_ref[...] = _pack_halves(h2)
    hh, hl = _split(h2)
    lg = _dot(hh, wrh_ref[...]) + _dot(hl, wrh_ref[...]) + _dot(hh, wrl_ref[...]) + br_ref[...]
    lt = lg.T
    gl = lt[0:N_GROUPS, :]
    grow = lax.broadcasted_iota(i32, (N_GROUPS, TM), 0)
    gmax = jnp.max(gl, axis=0, keepdims=True)
    gi = jnp.min(jnp.where(gl == gmax, grow, N_GROUPS), axis=0, keepdims=True)
    p_top = 1.0 / jnp.sum(jnp.exp(gl - gmax), axis=0, keepdims=True)
    sel = lt[EXPERT_COL0:EXPERT_COL0 + EXPERTS_PER_GROUP, :]
    for gg in range(1, N_GROUPS):
        lo_r = EXPERT_COL0 + gg * EXPERTS_PER_GROUP
        sel = jnp.where(gi == gg, lt[lo_r:lo_r + EXPERTS_PER_GROUP, :], sel)
    erow = lax.broadcasted_iota(i32, (EXPERTS_PER_GROUP, TM), 0)
    m1 = jnp.max(sel, axis=0, keepdims=True)
    i1 = jnp.min(jnp.where(sel == m1, erow, EXPERTS_PER_GROUP), axis=0, keepdims=True)
    rest = jnp.where(erow == i1, -jnp.inf, sel)
    m2 = jnp.max(rest, axis=0, keepdims=True)
    i2 = jnp.min(jnp.where(rest == m2, erow, EXPERTS_PER_GROUP), axis=0, keepdims=True)
    z = jnp.sum(jnp.exp(sel - m1), axis=0, keepdims=True)
    v1 = 1.0 / z
    v2nd = jnp.exp(m2 - m1) / z
    w1 = v1 / (v1 + v2nd) * p_top
    w2 = v2nd / (v1 + v2nd) * p_top
    e1 = gi * EXPERTS_PER_GROUP + i1
    e2 = gi * EXPERTS_PER_GROUP + i2

    xrow = lax.broadcasted_iota(i32, (N_EXPERTS, TM), 0)
    oh1 = xrow == e1
    oh2 = xrow == e2
    both = jnp.where(oh1 | oh2, 1.0, 0.0)
    prefix = _dot(both.astype(bf16), tri_ref[...]) + cnt_ref[:, 0:1]
    r1 = jnp.sum(jnp.where(oh1, prefix, 0.0), axis=0, keepdims=True)
    r2 = jnp.sum(jnp.where(oh2, prefix, 0.0), axis=0, keepdims=True)
    cnt_ref[...] = cnt_ref[...] + jnp.sum(both, axis=1, keepdims=True)

    route_ref[...] = jnp.concatenate(
        [e1, e2, r1.astype(i32), r2.astype(i32), jnp.zeros((4, TM), i32)], axis=0)
    wpad = jnp.concatenate([w1, w2, jnp.zeros((LANES - 2, TM), f32)], axis=0)
    wtok_ref[...] = wpad.T


def _mixer(x, mod6, p, b0, B):
    _, T, D = x.shape
    TM = min(SEQ_TILE, T)
    nt = T // TM
    N = B * T
    const = lambda shape: pl.BlockSpec(shape, lambda b, t: (0,) * len(shape))
    in_specs = [
        pl.BlockSpec((None, TM, D), lambda b, t: (b0 + b, t, 0)),
        pl.BlockSpec((None, 6, D), lambda b, t: (b0 + b, 0, 0)),
        const((1, D)), const((1, D)),
        const(p["wab"].shape), const(p["wproj"].shape),
        const((CONV_WIDTH, 8, D_CONV)), const((1, D_CONV)), const((1, D_CONV)), const((1, D_CONV)),
        const((D_CONV, D)), const((D_Q, D)), const((D, D)),
        const((1, D_Q)), const((1, D_KV)),
        const((D_Q, D_Q)), const((D_KV, D_KV)),
        const((N_KV_HEADS, Q_PER_KV * WINDOW, 2 * WINDOW)),
        pl.BlockSpec(memory_space=pltpu.SMEM),
        const((D, ROUTER_COLS)), const((D, ROUTER_COLS)), const((1, ROUTER_COLS)),
        const((TM, TM)),
    ]
    out_specs = [
        pl.BlockSpec((None, TM, D), lambda b, t: (b, t, 0)),
        pl.BlockSpec((None, TM, D // 2), lambda b, t: (b, t, 0)),
        pl.BlockSpec((8, TM), lambda b, t: (0, b * nt + t)),
        pl.BlockSpec((TM, LANES), lambda b, t: (b * nt + t, 0)),
        pl.BlockSpec((N_EXPERTS, LANES), lambda b, t: (0, 0)),
    ]
    out_shape = [
        jax.ShapeDtypeStruct((B, T, D), f32),
        jax.ShapeDtypeStruct((B, T, D // 2), i32),
        jax.ShapeDtypeStruct((8, N), i32),
        jax.ShapeDtypeStruct((N, LANES), f32),
        jax.ShapeDtypeStruct((N_EXPERTS, LANES), f32),
    ]
    scratch = [
        pltpu.VMEM((TM + 32, D_CONV), f32),
        pltpu.VMEM((7, TM + CONV_SPAN, D_CONV), f32),
        pltpu.VMEM((TM, D_CONV), f32),
        pltpu.VMEM((p["wproj"].shape[0], TM, PROJ_CHUNK), f32),
        pltpu.VMEM((N_KV_HEADS, TM + WINDOW, LANES), bf16),
        pltpu.VMEM((N_KV_HEADS, TM + WINDOW, LANES), bf16),
        pltpu.VMEM((TM, D_Q), bf16),
    ]
    tri = jnp.asarray(np.triu(np.ones((TM, TM), np.float32), 1), bf16)
    return pl.pallas_call(
        _mixer_kernel,
        grid=(B, nt),
        in_specs=in_specs, out_specs=out_specs, out_shape=out_shape, scratch_shapes=scratch,
        compiler_params=pltpu.CompilerParams(
            dimension_semantics=("arbitrary", "arbitrary"), vmem_limit_bytes=VMEM_LIMIT),
        name="mixer_router",
    )(x, mod6, p["gmix"], p["gffn"], p["wab"], p["wproj"], p["dw"], p["dwb"], p["lng"], p["lnb"],
      p["wco"], p["wao"], p["wout"], p["qg"], p["kg"], p["bq"], p["bk"], p["bias"], p["sinks"],
      p["wrh"], p["wrl"], p["br"], tri)


def _dest_kernel(pstart_ref, route_ref, dest_ref):
    e = route_ref[0:2, :]
    base = jnp.zeros(e.shape, i32)
    for x in range(N_EXPERTS):
        base = jnp.where(e == x, pstart_ref[x], base)
    dest_ref[...] = base + route_ref[2:4, :]


def _dest_slots(pstart, route):
    N = route.shape[1]
    TS = min(DEST_TILE, N)
    return pl.pallas_call(
        _dest_kernel,
        grid=(N // TS,),
        in_specs=[pl.BlockSpec(memory_space=pltpu.SMEM),
                  pl.BlockSpec((8, TS), lambda i: (0, i))],
        out_specs=pl.BlockSpec((2, TS), lambda i: (0, i)),
        out_shape=jax.ShapeDtypeStruct((2, N), i32),
        name="dest_slots",
    )(pstart, route)


def _sc_mesh():
    return plsc.VectorSubcoreMesh(core_axis_name="c", subcore_axis_name="s")


def _sc_scatter(h2, dest, n_slots):
    N, D = h2.shape
    per_w = N // SC_WORKERS
    C = SC_CHUNK
    n_chunks = per_w // C
    dest4 = dest.reshape(TOP_K, SC_WORKERS, n_chunks, C)

    @functools.partial(
        pl.kernel, mesh=_sc_mesh(),
        out_type=jax.ShapeDtypeStruct((n_slots, D), h2.dtype),
        scratch_types=[pltpu.VMEM((n_chunks, C), i32), pltpu.VMEM((n_chunks, C), i32),
                       pltpu.VMEM((2, C, D), h2.dtype),
                       pltpu.SemaphoreType.DMA((2,)), pltpu.SemaphoreType.DMA((2,))],
        name="sc_dispatch_scatter",
    )
    def run(h2_hbm, dest_hbm, xs_hbm, idx0, idx1, buf, sem_in, sem_out):
        wid = lax.axis_index("s") * SC_CORES + lax.axis_index("c")
        base = wid * per_w
        pltpu.sync_copy(dest_hbm.at[0, wid], idx0)
        pltpu.sync_copy(dest_hbm.at[1, wid], idx1)

        def load(j, b):
            return pltpu.make_async_copy(h2_hbm.at[pl.ds(base + j * C, C)], buf.at[b], sem_in.at[b])

        def put(j, b, idx):
            return pltpu.make_async_copy(buf.at[b], xs_hbm.at[idx.at[j]], sem_out.at[b])

        load(0, 0).start()

        @pl.loop(0, n_chunks, step=2)
        def _(j0):
            for b in range(2):
                j = j0 + b
                load(j, b).wait()

                @pl.when(j >= 1)
                def _():
                    put(j - 1, 1 - b, idx0).wait()
                    put(j - 1, 1 - b, idx1).wait()

                @pl.when(j + 1 < n_chunks)
                def _():
                    load(j + 1, 1 - b).start()

                put(j, b, idx0).start()
                put(j, b, idx1).start()

        put(n_chunks - 1, 1, idx0).wait()
        put(n_chunks - 1, 1, idx1).wait()

    return run(h2, dest4)


def _sc_gather(y, dest, N):
    D = y.shape[1]
    per_w = N // SC_WORKERS
    C = SC_CHUNK
    n_chunks = per_w // C
    dest4 = dest.reshape(TOP_K, SC_WORKERS, n_chunks, C)

    @functools.partial(
        pl.kernel, mesh=_sc_mesh(),
        out_type=jax.ShapeDtypeStruct((TOP_K, N, D), y.dtype),
        scratch_types=[pltpu.VMEM((n_chunks, C), i32), pltpu.VMEM((n_chunks, C), i32),
                       pltpu.VMEM((2, C, D), y.dtype),
                       pltpu.SemaphoreType.DMA((2,)), pltpu.SemaphoreType.DMA((2,))],
        name="sc_combine_gather",
    )
    def run(y_hbm, dest_hbm, yg_hbm, idx0, idx1, buf, sem_in, sem_out):
        wid = lax.axis_index("s") * SC_CORES + lax.axis_index("c")
        base = wid * per_w
        pltpu.sync_copy(dest_hbm.at[0, wid], idx0)
        pltpu.sync_copy(dest_hbm.at[1, wid], idx1)
        idx = (idx0, idx1)

        def get(j, k):
            return pltpu.make_async_copy(y_hbm.at[idx[k].at[j]], buf.at[k], sem_in.at[k])

        def put(j, k):
            return pltpu.make_async_copy(buf.at[k], yg_hbm.at[k, pl.ds(base + j * C, C)], sem_out.at[k])

        get(0, 0).start()
        get(0, 1).start()

        @pl.loop(0, n_chunks)
        def _(j):
            for k in range(TOP_K):
                get(j, k).wait()
                put(j, k).start()
            for k in range(TOP_K):
                put(j, k).wait()

                @pl.when(j + 1 < n_chunks)
                def _():
                    get(j + 1, k).start()

    return run(y, dest4)


def _expert_kernel(be_ref, nv_ref, nu_ref, xs_ref, wgu_ref, wd_ref, y_ref):
    i = pl.program_id(0)

    @pl.when(i < nu_ref[0])
    def _():
        live = lax.broadcasted_iota(i32, (MOE_BLOCK, 1), 0) < nv_ref[i]
        x_hi, x_lo = _unpack_halves(jnp.where(live, xs_ref[...], 0))
        half = x_hi.shape[1]
        gu = (_dot(x_hi.astype(bf16), wgu_ref[0:half, :]) +
              _dot(x_lo.astype(bf16), wgu_ref[half:2 * half, :]))
        gate = gu[:, 0:D_EXPERT]
        hid = (gate * jax.nn.sigmoid(gate)) * gu[:, D_EXPERT:2 * D_EXPERT]
        y_ref[...] = _pack_halves(_dot(hid.astype(bf16), wd_ref[...]))


def _experts(block_e, n_valid, n_used, xs, wgu, wd):
    n_slots, DH = xs.shape
    D = 2 * DH
    n_blocks = n_slots // MOE_BLOCK
    last = lambda i, nu: jnp.minimum(i, nu[0] - 1)
    grid_spec = pltpu.PrefetchScalarGridSpec(
        num_scalar_prefetch=3,
        grid=(n_blocks,),
        in_specs=[pl.BlockSpec((MOE_BLOCK, DH), lambda i, be, nv, nu: (last(i, nu), 0)),
                  pl.BlockSpec((None, D, 2 * D_EXPERT), lambda i, be, nv, nu: (be[last(i, nu)], 0, 0)),
                  pl.BlockSpec((None, D_EXPERT, D), lambda i, be, nv, nu: (be[last(i, nu)], 0, 0))],
        out_specs=pl.BlockSpec((MOE_BLOCK, DH), lambda i, be, nv, nu: (last(i, nu), 0)),
    )
    return pl.pallas_call(
        _expert_kernel,
        grid_spec=grid_spec,
        out_shape=jax.ShapeDtypeStruct((n_slots, DH), i32),
        compiler_params=pltpu.CompilerParams(dimension_semantics=("arbitrary",),
                                             vmem_limit_bytes=VMEM_LIMIT),
        name="experts",
    )(block_e, n_valid, n_used, xs, wgu, wd)


def _combine_kernel(yg_ref, x1_ref, wtok_ref, mod_ref, *rest):
    o_ref = rest[-1]
    w = wtok_ref[...]
    g2 = mod_ref[5:6, :]
    a_hi, a_lo = _unpack_halves(yg_ref[0])
    b_hi, b_lo = _unpack_halves(yg_ref[1])
    moe = jnp.concatenate([w[:, 0:1] * a_hi + w[:, 1:2] * b_hi,
                           w[:, 0:1] * a_lo + w[:, 1:2] * b_lo], axis=1)
    o_ref[...] = x1_ref[...] + g2 * moe


def _combine(yg, x1, wtok, mod6, T, b0, n_total, out_prev):
    Nc, D = x1.shape
    TS = min(COMBINE_TILE, T)
    per_seq = T // TS
    blk0 = b0 * per_seq
    in_specs = [pl.BlockSpec((TOP_K, TS, D // 2), lambda i: (0, i, 0)),
                pl.BlockSpec((TS, D), lambda i: (i, 0)),
                pl.BlockSpec((TS, LANES), lambda i: (i, 0)),
                pl.BlockSpec((None, 6, D), lambda i: (b0 + i // per_seq, 0, 0))]
    args = [yg, x1, wtok, mod6]
    aliases = {}
    if out_prev is not None:
        in_specs.append(pl.BlockSpec(memory_space=pl.ANY))
        args.append(out_prev)
        aliases = {len(args) - 1: 0}
    return pl.pallas_call(
        _combine_kernel,
        grid=(Nc // TS,),
        in_specs=in_specs,
        out_specs=pl.BlockSpec((TS, D), lambda i: (blk0 + i, 0)),
        out_shape=jax.ShapeDtypeStruct((n_total, D), f32),
        input_output_aliases=aliases,
        compiler_params=pltpu.CompilerParams(dimension_semantics=("arbitrary",)),
        name="combine",
    )(*args)


def _block_diag_mean(n, blk):
    m = np.kron(np.eye(n // blk, dtype=np.float32), np.full((blk, blk), 1.0 / blk, np.float32))
    return jnp.asarray(m, bf16)


def _layer(x, mod6, bias, l, w):
    B, T, D = x.shape
    N = B * T
    w_rg, w_re = w["w_router_group"][l], w["w_router_expert"][l]
    wr = jnp.zeros((D, ROUTER_COLS), f32)
    wr = wr.at[:, 0:N_GROUPS].set(w_rg).at[:, EXPERT_COL0:EXPERT_COL0 + N_EXPERTS].set(w_re)
    br = jnp.zeros((1, ROUTER_COLS), f32)
    br = br.at[0, 0:N_GROUPS].set(w["b_router_group"][l])
    br = br.at[0, EXPERT_COL0:EXPERT_COL0 + N_EXPERTS].set(w["b_router_expert"][l])
    wrh = wr.astype(bf16)
    wrl = (wr - wrh.astype(f32)).astype(bf16)
    p = dict(
        gmix=w["norm_mix_g"][l].reshape(1, D), gffn=w["norm_ffn_g"][l].reshape(1, D),
        wab=w["w_in"][l][:, 0:2 * D_CONV].astype(bf16),
        wproj=w["w_in"][l][:, 2 * D_CONV:].astype(bf16).reshape(D, -1, PROJ_CHUNK).transpose(1, 0, 2),
        dw=jnp.broadcast_to(w["dw_kernel"][l][:, None, :], (CONV_WIDTH, 8, D_CONV)), dwb=w["dw_bias"][l].reshape(1, D_CONV),
        lng=w["conv_ln_g"][l].reshape(1, D_CONV), lnb=w["conv_ln_b"][l].reshape(1, D_CONV),
        wco=w["w_conv_out"][l].astype(bf16), wao=w["w_attn_out"][l].astype(bf16),
        wout=w["w_out"][l].astype(bf16),
        qg=(jnp.tile(w["q_norm_g"][l], N_Q_HEADS) * (HEAD_DIM ** -0.5)).reshape(1, D_Q),
        kg=jnp.tile(w["k_norm_g"][l], N_KV_HEADS).reshape(1, D_KV),
        bq=_block_diag_mean(D_Q, HEAD_DIM), bk=_block_diag_mean(D_KV, HEAD_DIM),
        bias=bias, sinks=w["sinks"][l], wrh=wrh, wrl=wrl, br=br,
    )
    wgu = jnp.concatenate([w["w_exp_gate"][l], w["w_exp_up"][l]], axis=-1).astype(bf16)
    wd = w["w_exp_down"][l].astype(bf16)

    n_chunks = MOE_CHUNKS if B % MOE_CHUNKS == 0 else 1
    Bc = B // n_chunks
    Nc = Bc * T
    n_blocks = -(-(Nc * TOP_K) // MOE_BLOCK) + N_EXPERTS
    blk0 = jnp.arange(n_blocks, dtype=i32) * MOE_BLOCK
    stage = []
    for ch in range(n_chunks):
        x1, h2, route, wtok, cnt = _mixer(x, mod6, p, ch * Bc, Bc)
        counts = cnt[:, 0].astype(i32)
        pcounts = (counts + MOE_BLOCK - 1) // MOE_BLOCK * MOE_BLOCK
        pend = jnp.cumsum(pcounts)
        pstart = pend - pcounts
        block_e = jnp.minimum(jnp.sum((pend[None, :] <= blk0[:, None]).astype(i32), axis=1), N_EXPERTS - 1)
        n_valid = jnp.clip((pstart + counts)[block_e] - blk0, 0, MOE_BLOCK).astype(i32)
        n_used = (pend[-1:] // MOE_BLOCK).astype(i32)
        dest = _dest_slots(pstart, route)
        xs = _sc_scatter(h2.reshape(Nc, D // 2), dest, n_blocks * MOE_BLOCK)
        stage.append((x1, wtok, dest, xs, block_e, n_valid, n_used))
    ys = [_experts(be, nv, nu, xs, wgu, wd) for (_, _, _, xs, be, nv, nu) in stage]
    ygs = [_sc_gather(y, st[2], Nc) for y, st in zip(ys, stage)]
    out = None
    for ch in range(n_chunks):
        x1, wtok = stage[ch][0], stage[ch][1]
        out = _combine(ygs[ch], x1.reshape(Nc, D), wtok, mod6, T, ch * Bc, B * T, out)
    return out.reshape(B, T, D)


def kernel(x, c, w_ada, b_ada, norm_mix_g, w_in, dw_kernel, dw_bias, conv_ln_g, conv_ln_b,
           w_conv_out, q_norm_g, k_norm_g, sinks, w_attn_out, w_out, rel_bias_table, norm_ffn_g,
           w_router_group, b_router_group, w_router_expert, b_router_expert,
           w_exp_gate, w_exp_up, w_exp_down):
    w = dict(norm_mix_g=norm_mix_g, w_in=w_in, dw_kernel=dw_kernel, dw_bias=dw_bias,
             conv_ln_g=conv_ln_g, conv_ln_b=conv_ln_b, w_conv_out=w_conv_out, q_norm_g=q_norm_g,
             k_norm_g=k_norm_g, sinks=sinks, w_attn_out=w_attn_out, w_out=w_out,
             norm_ffn_g=norm_ffn_g, w_router_group=w_router_group, b_router_group=b_router_group,
             w_router_expert=w_router_expert, b_router_expert=b_router_expert,
             w_exp_gate=w_exp_gate, w_exp_up=w_exp_up, w_exp_down=w_exp_down)
    B = x.shape[0]
    bias = _bias_band(rel_bias_table).reshape(N_KV_HEADS, Q_PER_KV * WINDOW, 2 * WINDOW)
    for l in range(w_ada.shape[0]):
        mod6 = _modulation(c, w_ada[l], b_ada[l]).reshape(B, 6, D_MODEL)
        x = _layer(x, mod6, bias, l, w)
    return x
```

```python
import functools
import math

import jax
import jax.numpy as jnp
import numpy as np
from jax import lax
from jax.experimental import pallas as pl
from jax.experimental.pallas import tpu as pltpu
from jax.experimental.pallas import tpu_sc as plsc

D_MODEL = 1024
D_CONV = 512
CONV_WIDTH = 31
N_Q_HEADS = 8
N_KV_HEADS = 2
HEAD_DIM = 64
Q_PER_KV = N_Q_HEADS // N_KV_HEADS
WINDOW = 128
N_BUCKETS = 32
MAX_DISTANCE = 128
N_GROUPS = 4
EXPERTS_PER_GROUP = 8
N_EXPERTS = N_GROUPS * EXPERTS_PER_GROUP
TOP_K = 2
D_EXPERT = 256
D_Q = N_Q_HEADS * HEAD_DIM
D_KV = N_KV_HEADS * HEAD_DIM
EPS = 1e-6
NEG_INF = -1e30

LANES = 128
SEQ_TILE = 512
CONV_ROWS = 32
PROJ_CHUNK = 256
CONV_BLOCKS_PER_ITER = 4
PROJ_CHUNKS_PER_ITER = 2
CONV_SPAN = 24
MOE_BLOCK = 512
MOE_CHUNKS = 2
DEST_TILE = 8192
COMBINE_TILE = 512
SC_CORES = 2
SC_SUBCORES = 16
SC_WORKERS = SC_CORES * SC_SUBCORES
SC_CHUNK = 32
ROUTER_COLS = LANES
EXPERT_COL0 = 8
VMEM_LIMIT = 56 * 1024 * 1024

f32 = jnp.float32
bf16 = jnp.bfloat16
i32 = jnp.int32


def _dot(a, b):
    return jnp.dot(a, b, preferred_element_type=f32)


def _split(a):
    hi = a.astype(bf16)
    lo = (a - hi.astype(f32)).astype(bf16)
    return hi, lo


def _pack_halves(x):
    c = x.shape[1] // 2
    hi = lax.bitcast_convert_type(x[:, 0:c].astype(bf16).astype(f32), jnp.uint32)
    lo = lax.bitcast_convert_type(x[:, c:2 * c].astype(bf16).astype(f32), jnp.uint32)
    word = (hi & jnp.uint32(0xFFFF0000)) | (lo >> jnp.uint32(16))
    return lax.bitcast_convert_type(word, i32)


def _unpack_halves(word):
    u = lax.bitcast_convert_type(word, jnp.uint32)
    hi = lax.bitcast_convert_type(u & jnp.uint32(0xFFFF0000), f32)
    lo = lax.bitcast_convert_type(u << jnp.uint32(16), f32)
    return hi, lo


def _dot3(a, b):
    ah, al = _split(a)
    bh, bl = _split(b)
    return _dot(ah, bh) + _dot(al, bh) + _dot(ah, bl)


def _mod_kernel(c_ref, w_ref, b_ref, o_ref):
    c = c_ref[...]
    s = c * jax.nn.sigmoid(c)
    o_ref[...] = _dot3(s, w_ref[...]) + b_ref[...]


def _modulation(c, w_ada, b_ada):
    B, D = c.shape
    n_out = w_ada.shape[1]
    return pl.pallas_call(
        _mod_kernel,
        grid=(n_out // D,),
        in_specs=[pl.BlockSpec((B, D), lambda j: (0, 0)),
                  pl.BlockSpec((D, D), lambda j: (0, j)),
                  pl.BlockSpec((1, D), lambda j: (0, j))],
        out_specs=pl.BlockSpec((B, D), lambda j: (0, j)),
        out_shape=jax.ShapeDtypeStruct((B, n_out), f32),
        name="modulation",
    )(c, w_ada, b_ada.reshape(1, n_out))


def _band_buckets():
    qi = np.arange(WINDOW)[:, None]
    kj = np.arange(2 * WINDOW)[None, :]
    dist = np.clip(qi + WINDOW - kj, 0, MAX_DISTANCE)
    max_exact = N_BUCKETS // 2
    d = np.maximum(dist, 1).astype(np.float32)
    large = max_exact + (np.log(d / np.float32(max_exact)) / np.float32(math.log(MAX_DISTANCE / max_exact))
                         * np.float32(N_BUCKETS - max_exact)).astype(np.int32)
    large = np.minimum(large, N_BUCKETS - 1)
    return np.where(dist < max_exact, dist, large).astype(np.int32)


def _bias_kernel(tab_ref, bucket_ref, o_ref):
    bk = bucket_ref[...]
    for h in range(N_Q_HEADS):
        acc = jnp.zeros(bk.shape, f32)
        for b in range(N_BUCKETS):
            acc = jnp.where(bk == b, tab_ref[b, h], acc)
        o_ref[h] = acc


def _bias_band(rel_bias_table):
    return pl.pallas_call(
        _bias_kernel,
        in_specs=[pl.BlockSpec(memory_space=pltpu.SMEM),
                  pl.BlockSpec(memory_space=pltpu.VMEM)],
        out_specs=pl.BlockSpec(memory_space=pltpu.VMEM),
        out_shape=jax.ShapeDtypeStruct((N_Q_HEADS, WINDOW, 2 * WINDOW), f32),
        name="bias_band",
    )(rel_bias_table, jnp.asarray(_band_buckets()))


def _mixer_kernel(x_ref, mod_ref, gmix_ref, gffn_ref, wab_ref, wproj_ref, dw_ref, dwb_ref, lng_ref, lnb_ref,
                  wco_ref, wao_ref, wout_ref, qg_ref, kg_ref, bq_ref, bk_ref, bias_ref, sink_ref,
                  wrh_ref, wrl_ref, br_ref, tri_ref,
                  x1_ref, h2_ref, route_ref, wtok_ref, cnt_ref,
                  uext, ush, conv_sc, hb_sc, proj_sc, k2, v2, osc):
    TM = x_ref.shape[0]
    HALO = 32
    b = pl.program_id(0)
    t = pl.program_id(1)
    first = t == 0

    @pl.when(first)
    def _():
        uext[0:HALO, :] = jnp.zeros((HALO, D_CONV), f32)
        k2[:, 0:WINDOW, :] = jnp.zeros((N_KV_HEADS, WINDOW, LANES), bf16)
        v2[:, 0:WINDOW, :] = jnp.zeros((N_KV_HEADS, WINDOW, LANES), bf16)

    @pl.when(first & (b == 0))
    def _():
        cnt_ref[...] = jnp.zeros(cnt_ref.shape, f32)

    x = x_ref[...]
    mod = mod_ref[...]
    sh1, sc1, g1 = mod[0:1, :], mod[1:2, :], mod[2:3, :]
    sh2, sc2, g2 = mod[3:4, :], mod[4:5, :], mod[5:6, :]
    del g2

    ms = jnp.mean(x * x, axis=-1, keepdims=True)
    h = (x * lax.rsqrt(ms + EPS)) * gmix_ref[...]
    h = h * (1.0 + sc1) + sh1
    hb = h.astype(bf16)

    ab = _dot(hb, wab_ref[...])
    u = ab[:, 0:D_CONV] * jax.nn.sigmoid(ab[:, D_CONV:2 * D_CONV])
    uext[HALO:HALO + TM, :] = u
    for r in range(1, 8):
        ush[r - 1] = uext[r:r + TM + CONV_SPAN, :]
    n_q = D_Q // PROJ_CHUNK
    n_g = D_MODEL // PROJ_CHUNK
    n_jobs = wproj_ref.shape[0]
    n_rb = TM // CONV_ROWS
    n_t8 = CONV_ROWS // 8
    n_it = n_rb // CONV_BLOCKS_PER_ITER
    hb_sc[...] = hb

    def conv_block(rb):
        base = pl.multiple_of(rb * CONV_ROWS, CONV_ROWS)
        accs = [None] * n_t8
        for j in range(CONV_WIDTH):
            a, r = divmod(j + 2, 8)
            tap = dw_ref[j]
            for s8 in range(n_t8):
                rows = pl.ds(base + 8 * (a + s8), 8)
                win = uext[rows, :] if r == 0 else ush[r - 1, rows, :]
                term = tap * win
                accs[s8] = term if accs[s8] is None else accs[s8] + term
        for s8 in range(n_t8):
            conv_sc[pl.ds(base + 8 * s8, 8), :] = accs[s8] + dwb_ref[...]

    def conv_and_proj(it, carry):
        for c in range(PROJ_CHUNKS_PER_ITER):
            idx = it * PROJ_CHUNKS_PER_ITER + c
            proj_sc[idx] = _dot(hb_sc[...], wproj_ref[idx])
        for bb in range(CONV_BLOCKS_PER_ITER):
            conv_block(it * CONV_BLOCKS_PER_ITER + bb)
        return carry

    lax.fori_loop(0, n_it, conv_and_proj, 0)
    for idx in range(n_it * PROJ_CHUNKS_PER_ITER, n_jobs):
        proj_sc[idx] = _dot(hb, wproj_ref[idx])
    conv = conv_sc[...]
    uext[0:HALO, :] = uext[TM:TM + HALO, :]
    mu = jnp.mean(conv, axis=-1, keepdims=True)
    dc = conv - mu
    var = jnp.mean(dc * dc, axis=-1, keepdims=True)
    yn = dc * lax.rsqrt(var + EPS) * lng_ref[...] + lnb_ref[...]
    act = yn * jax.nn.sigmoid(yn)
    y_conv = _dot(act.astype(bf16), wco_ref[...])

    q = jnp.concatenate([proj_sc[i] for i in range(n_q)], axis=1)
    k = proj_sc[n_q, :, 0:D_KV]
    v = proj_sc[n_q, :, D_KV:2 * D_KV]

    def head_norm(z, blk_ref, g):
        msq = _dot((z * z).astype(bf16), blk_ref[...])
        return z * lax.rsqrt(msq + EPS) * g

    qn = head_norm(q, bq_ref, qg_ref[...]).astype(bf16)
    kn = head_norm(k, bk_ref, kg_ref[...])
    lo_half = lax.broadcasted_iota(i32, (TM, LANES), 1) < HEAD_DIM
    kr = pltpu.roll(kn, HEAD_DIM, 1)
    vr = pltpu.roll(v, HEAD_DIM, 1)
    k2[0, WINDOW:WINDOW + TM, :] = jnp.where(lo_half, kn, kr).astype(bf16)
    k2[1, WINDOW:WINDOW + TM, :] = jnp.where(lo_half, kr, kn).astype(bf16)
    v2[0, WINDOW:WINDOW + TM, :] = jnp.where(lo_half, v, vr).astype(bf16)
    v2[1, WINDOW:WINDOW + TM, :] = jnp.where(lo_half, vr, v).astype(bf16)

    QROWS = Q_PER_KV * WINDOW
    row = lax.broadcasted_iota(i32, (QROWS, 2 * WINDOW), 0) & (WINDOW - 1)
    col = lax.broadcasted_iota(i32, (QROWS, 2 * WINDOW), 1)
    dist = row + WINDOW - col
    in_window = (dist >= 0) & (dist < WINDOW)
    first_mask = in_window & (col >= jnp.where(first, WINDOW, 0))
    hrow = lax.broadcasted_iota(i32, (QROWS, 1), 0) // WINDOW
    lo128 = lax.broadcasted_iota(i32, (WINDOW, LANES), 1) < HEAD_DIM
    zero_q = jnp.zeros((WINDOW, LANES), bf16)
    for g in range(N_KV_HEADS):
        sink = jnp.where(hrow == 0, sink_ref[4 * g],
                         jnp.where(hrow == 1, sink_ref[4 * g + 1],
                                   jnp.where(hrow == 2, sink_ref[4 * g + 2], sink_ref[4 * g + 3])))
        bias_g = bias_ref[g]
        for j in range(TM // WINDOW):
            rs = slice(j * WINDOW, (j + 1) * WINDOW)
            qa = qn[rs, 2 * LANES * g:2 * LANES * g + LANES]
            qb = qn[rs, 2 * LANES * g + LANES:2 * LANES * (g + 1)]
            qs = jnp.concatenate([jnp.where(lo128, qa, zero_q), jnp.where(lo128, zero_q, qa),
                                  jnp.where(lo128, qb, zero_q), jnp.where(lo128, zero_q, qb)], axis=0)
            kk = k2[g, j * WINDOW:(j + 2) * WINDOW, :]
            s = lax.dot_general(qs, kk, (((1,), (1,)), ((), ())), preferred_element_type=f32)
            msk = first_mask if j == 0 else in_window
            logits = jnp.where(msk, s + bias_g, NEG_INF)
            m = jnp.maximum(jnp.max(logits, axis=-1, keepdims=True), sink)
            p = jnp.exp(logits - m)
            den = jnp.sum(p, axis=-1, keepdims=True) + jnp.exp(sink - m)
            o2 = _dot(p.astype(bf16), v2[g, j * WINDOW:(j + 2) * WINDOW, :]) * (1.0 / den)
            osc[rs, 2 * LANES * g:2 * LANES * g + LANES] = jnp.where(
                lo128, o2[0:WINDOW], o2[WINDOW:2 * WINDOW]).astype(bf16)
            osc[rs, 2 * LANES * g + LANES:2 * LANES * (g + 1)] = jnp.where(
                lo128, o2[2 * WINDOW:3 * WINDOW], o2[3 * WINDOW:4 * WINDOW]).astype(bf16)
    k2[:, 0:WINDOW, :] = k2[:, TM:TM + WINDOW, :]
    v2[:, 0:WINDOW, :] = v2[:, TM:TM + WINDOW, :]
    y_attn = _dot(osc[...], wao_ref[...])

    merged = []
    for i in range(n_g):
        cs = slice(i * PROJ_CHUNK, (i + 1) * PROJ_CHUNK)
        g_conv = jax.nn.sigmoid(proj_sc[n_q + 1 + i])
        g_attn = jax.nn.sigmoid(proj_sc[n_q + 1 + n_g + i])
        merged.append((g_conv * y_conv[:, cs] + g_attn * y_attn[:, cs]).astype(bf16))
    merged = jnp.concatenate(merged, axis=1)
    x1 = x + g1 * _dot(merged, wout_ref[...])
    x1_ref[...] = x1

    ms2 = jnp.mean(x1 * x1, axis=-1, keepdims=True)
    h2 = (x1 * lax.rsqrt(ms2 + EPS)) * gffn_ref[...]
    h2 = h2 * (1.0 + sc2) + sh2
    h2_ref[...] = _pack_halves(h2)
    hh, hl = _split(h2)
    lg = _dot(hh, wrh_ref[...]) + _dot(hl, wrh_ref[...]) + _dot(hh, wrl_ref[...]) + br_ref[...]
    lt = lg.T
    gl = lt[0:N_GROUPS, :]
    grow = lax.broadcasted_iota(i32, (N_GROUPS, TM), 0)
    gmax = jnp.max(gl, axis=0, keepdims=True)
    gi = jnp.min(jnp.where(gl == gmax, grow, N_GROUPS), axis=0, keepdims=True)
    p_top = 1.0 / jnp.sum(jnp.exp(gl - gmax), axis=0, keepdims=True)
    sel = lt[EXPERT_COL0:EXPERT_COL0 + EXPERTS_PER_GROUP, :]
    for gg in range(1, N_GROUPS):
        lo_r = EXPERT_COL0 + gg * EXPERTS_PER_GROUP
        sel = jnp.where(gi == gg, lt[lo_r:lo_r + EXPERTS_PER_GROUP, :], sel)
    erow = lax.broadcasted_iota(i32, (EXPERTS_PER_GROUP, TM), 0)
    m1 = jnp.max(sel, axis=0, keepdims=True)
    i1 = jnp.min(jnp.where(sel == m1, erow, EXPERTS_PER_GROUP), axis=0, keepdims=True)
    rest = jnp.where(erow == i1, -jnp.inf, sel)
    m2 = jnp.max(rest, axis=0, keepdims=True)
    i2 = jnp.min(jnp.where(rest == m2, erow, EXPERTS_PER_GROUP), axis=0, keepdims=True)
    z = jnp.sum(jnp.exp(sel - m1), axis=0, keepdims=True)
    v1 = 1.0 / z
    v2nd = jnp.exp(m2 - m1) / z
    w1 = v1 / (v1 + v2nd) * p_top
    w2 = v2nd / (v1 + v2nd) * p_top
    e1 = gi * EXPERTS_PER_GROUP + i1
    e2 = gi * EXPERTS_PER_GROUP + i2

    xrow = lax.broadcasted_iota(i32, (N_EXPERTS, TM), 0)
    oh1 = xrow == e1
    oh2 = xrow == e2
    both = jnp.where(oh1 | oh2, 1.0, 0.0)
    prefix = _dot(both.astype(bf16), tri_ref[...]) + cnt_ref[:, 0:1]
    r1 = jnp.sum(jnp.where(oh1, prefix, 0.0), axis=0, keepdims=True)
    r2 = jnp.sum(jnp.where(oh2, prefix, 0.0), axis=0, keepdims=True)
    cnt_ref[...] = cnt_ref[...] + jnp.sum(both, axis=1, keepdims=True)

    route_ref[...] = jnp.concatenate(
        [e1, e2, r1.astype(i32), r2.astype(i32), jnp.zeros((4, TM), i32)], axis=0)
    wpad = jnp.concatenate([w1, w2, jnp.zeros((LANES - 2, TM), f32)], axis=0)
    wtok_ref[...] = wpad.T


def _mixer(x, mod6, p, b0, B):
    _, T, D = x.shape
    TM = min(SEQ_TILE, T)
    nt = T // TM
    N = B * T
    const = lambda shape: pl.BlockSpec(shape, lambda b, t: (0,) * len(shape))
    in_specs = [
        pl.BlockSpec((None, TM, D), lambda b, t: (b0 + b, t, 0)),
        pl.BlockSpec((None, 6, D), lambda b, t: (b0 + b, 0, 0)),
        const((1, D)), const((1, D)),
        const(p["wab"].shape), const(p["wproj"].shape),
        const((CONV_WIDTH, 8, D_CONV)), const((1, D_CONV)), const((1, D_CONV)), const((1, D_CONV)),
        const((D_CONV, D)), const((D_Q, D)), const((D, D)),
        const((1, D_Q)), const((1, D_KV)),
        const((D_Q, D_Q)), const((D_KV, D_KV)),
        const((N_KV_HEADS, Q_PER_KV * WINDOW, 2 * WINDOW)),
        pl.BlockSpec(memory_space=pltpu.SMEM),
        const((D, ROUTER_COLS)), const((D, ROUTER_COLS)), const((1, ROUTER_COLS)),
        const((TM, TM)),
    ]
    out_specs = [
        pl.BlockSpec((None, TM, D), lambda b, t: (b, t, 0)),
        pl.BlockSpec((None, TM, D // 2), lambda b, t: (b, t, 0)),
        pl.BlockSpec((8, TM), lambda b, t: (0, b * nt + t)),
        pl.BlockSpec((TM, LANES), lambda b, t: (b * nt + t, 0)),
        pl.BlockSpec((N_EXPERTS, LANES), lambda b, t: (0, 0)),
    ]
    out_shape = [
        jax.ShapeDtypeStruct((B, T, D), f32),
        jax.ShapeDtypeStruct((B, T, D // 2), i32),
        jax.ShapeDtypeStruct((8, N), i32),
        jax.ShapeDtypeStruct((N, LANES), f32),
        jax.ShapeDtypeStruct((N_EXPERTS, LANES), f32),
    ]
    scratch = [
        pltpu.VMEM((TM + 32, D_CONV), f32),
        pltpu.VMEM((7, TM + CONV_SPAN, D_CONV), f32),
        pltpu.VMEM((TM, D_CONV), f32),
        pltpu.VMEM((TM, D), bf16),
        pltpu.VMEM((p["wproj"].shape[0], TM, PROJ_CHUNK), f32),
        pltpu.VMEM((N_KV_HEADS, TM + WINDOW, LANES), bf16),
        pltpu.VMEM((N_KV_HEADS, TM + WINDOW, LANES), bf16),
        pltpu.VMEM((TM, D_Q), bf16),
    ]
    tri = jnp.asarray(np.triu(np.ones((TM, TM), np.float32), 1), bf16)
    return pl.pallas_call(
        _mixer_kernel,
        grid=(B, nt),
        in_specs=in_specs, out_specs=out_specs, out_shape=out_shape, scratch_shapes=scratch,
        compiler_params=pltpu.CompilerParams(
            dimension_semantics=("arbitrary", "arbitrary"), vmem_limit_bytes=VMEM_LIMIT),
        name="mixer_router",
    )(x, mod6, p["gmix"], p["gffn"], p["wab"], p["wproj"], p["dw"], p["dwb"], p["lng"], p["lnb"],
      p["wco"], p["wao"], p["wout"], p["qg"], p["kg"], p["bq"], p["bk"], p["bias"], p["sinks"],
      p["wrh"], p["wrl"], p["br"], tri)


def _dest_kernel(pstart_ref, route_ref, dest_ref):
    e = route_ref[0:2, :]
    base = jnp.zeros(e.shape, i32)
    for x in range(N_EXPERTS):
        base = jnp.where(e == x, pstart_ref[x], base)
    dest_ref[...] = base + route_ref[2:4, :]


def _dest_slots(pstart, route):
    N = route.shape[1]
    TS = min(DEST_TILE, N)
    return pl.pallas_call(
        _dest_kernel,
        grid=(N // TS,),
        in_specs=[pl.BlockSpec(memory_space=pltpu.SMEM),
                  pl.BlockSpec((8, TS), lambda i: (0, i))],
        out_specs=pl.BlockSpec((2, TS), lambda i: (0, i)),
        out_shape=jax.ShapeDtypeStruct((2, N), i32),
        name="dest_slots",
    )(pstart, route)


def _sc_mesh():
    return plsc.VectorSubcoreMesh(core_axis_name="c", subcore_axis_name="s")


def _sc_scatter(h2, dest, n_slots):
    N, D = h2.shape
    per_w = N // SC_WORKERS
    C = SC_CHUNK
    n_chunks = per_w // C
    dest4 = dest.reshape(TOP_K, SC_WORKERS, n_chunks, C)

    @functools.partial(
        pl.kernel, mesh=_sc_mesh(),
        out_type=jax.ShapeDtypeStruct((n_slots, D), h2.dtype),
        scratch_types=[pltpu.VMEM((n_chunks, C), i32), pltpu.VMEM((n_chunks, C), i32),
                       pltpu.VMEM((2, C, D), h2.dtype),
                       pltpu.SemaphoreType.DMA((2,)), pltpu.SemaphoreType.DMA((2,))],
        name="sc_dispatch_scatter",
    )
    def run(h2_hbm, dest_hbm, xs_hbm, idx0, idx1, buf, sem_in, sem_out):
        wid = lax.axis_index("s") * SC_CORES + lax.axis_index("c")
        base = wid * per_w
        pltpu.sync_copy(dest_hbm.at[0, wid], idx0)
        pltpu.sync_copy(dest_hbm.at[1, wid], idx1)

        def load(j, b):
            return pltpu.make_async_copy(h2_hbm.at[pl.ds(base + j * C, C)], buf.at[b], sem_in.at[b])

        def put(j, b, idx):
            return pltpu.make_async_copy(buf.at[b], xs_hbm.at[idx.at[j]], sem_out.at[b])

        load(0, 0).start()

        @pl.loop(0, n_chunks, step=2)
        def _(j0):
            for b in range(2):
                j = j0 + b
                load(j, b).wait()

                @pl.when(j >= 1)
                def _():
                    put(j - 1, 1 - b, idx0).wait()
                    put(j - 1, 1 - b, idx1).wait()

                @pl.when(j + 1 < n_chunks)
                def _():
                    load(j + 1, 1 - b).start()

                put(j, b, idx0).start()
                put(j, b, idx1).start()

        put(n_chunks - 1, 1, idx0).wait()
        put(n_chunks - 1, 1, idx1).wait()

    return run(h2, dest4)


def _sc_gather(y, dest, N):
    D = y.shape[1]
    per_w = N // SC_WORKERS
    C = SC_CHUNK
    n_chunks = per_w // C
    dest4 = dest.reshape(TOP_K, SC_WORKERS, n_chunks, C)

    @functools.partial(
        pl.kernel, mesh=_sc_mesh(),
        out_type=jax.ShapeDtypeStruct((TOP_K, N, D), y.dtype),
        scratch_types=[pltpu.VMEM((n_chunks, C), i32), pltpu.VMEM((n_chunks, C), i32),
                       pltpu.VMEM((2, C, D), y.dtype),
                       pltpu.SemaphoreType.DMA((2,)), pltpu.SemaphoreType.DMA((2,))],
        name="sc_combine_gather",
    )
    def run(y_hbm, dest_hbm, yg_hbm, idx0, idx1, buf, sem_in, sem_out):
        wid = lax.axis_index("s") * SC_CORES + lax.axis_index("c")
        base = wid * per_w
        pltpu.sync_copy(dest_hbm.at[0, wid], idx0)
        pltpu.sync_copy(dest_hbm.at[1, wid], idx1)
        idx = (idx0, idx1)

        def get(j, k):
            return pltpu.make_async_copy(y_hbm.at[idx[k].at[j]], buf.at[k], sem_in.at[k])

        def put(j, k):
            return pltpu.make_async_copy(buf.at[k], yg_hbm.at[k, pl.ds(base + j * C, C)], sem_out.at[k])

        get(0, 0).start()
        get(0, 1).start()

        @pl.loop(0, n_chunks)
        def _(j):
            for k in range(TOP_K):
                get(j, k).wait()
                put(j, k).start()
            for k in range(TOP_K):
                put(j, k).wait()

                @pl.when(j + 1 < n_chunks)
                def _():
                    get(j + 1, k).start()

    return run(y, dest4)


def _expert_kernel(be_ref, nv_ref, nu_ref, xs_ref, wg_ref, wu_ref, wdn_ref, y_ref, wgu_ref, wd_ref):
    i = pl.program_id(0)
    in_use = i < nu_ref[0]

    @pl.when(in_use & ((i == 0) | (be_ref[i] != be_ref[jnp.maximum(i - 1, 0)])))
    def _():
        wgu_ref[:, 0:D_EXPERT] = wg_ref[...].astype(bf16)
        wgu_ref[:, D_EXPERT:2 * D_EXPERT] = wu_ref[...].astype(bf16)
        wd_ref[...] = wdn_ref[...].astype(bf16)

    @pl.when(in_use)
    def _():
        live = lax.broadcasted_iota(i32, (MOE_BLOCK, 1), 0) < nv_ref[i]
        x_hi, x_lo = _unpack_halves(jnp.where(live, xs_ref[...], 0))
        half = x_hi.shape[1]
        gu = (_dot(x_hi.astype(bf16), wgu_ref[0:half, :]) +
              _dot(x_lo.astype(bf16), wgu_ref[half:2 * half, :]))
        gate = gu[:, 0:D_EXPERT]
        hid = (gate * jax.nn.sigmoid(gate)) * gu[:, D_EXPERT:2 * D_EXPERT]
        y_ref[...] = _pack_halves(_dot(hid.astype(bf16), wd_ref[...]))


def _experts(block_e, n_valid, n_used, xs, w_gate, w_up, w_down):
    n_slots, DH = xs.shape
    D = 2 * DH
    n_blocks = n_slots // MOE_BLOCK
    last = lambda i, nu: jnp.minimum(i, nu[0] - 1)
    expert = lambda i, be, nv, nu: (be[last(i, nu)], 0, 0)
    grid_spec = pltpu.PrefetchScalarGridSpec(
        num_scalar_prefetch=3,
        grid=(n_blocks,),
        in_specs=[pl.BlockSpec((MOE_BLOCK, DH), lambda i, be, nv, nu: (last(i, nu), 0)),
                  pl.BlockSpec((None, D, D_EXPERT), expert),
                  pl.BlockSpec((None, D, D_EXPERT), expert),
                  pl.BlockSpec((None, D_EXPERT, D), expert)],
        out_specs=pl.BlockSpec((MOE_BLOCK, DH), lambda i, be, nv, nu: (last(i, nu), 0)),
        scratch_shapes=[pltpu.VMEM((D, 2 * D_EXPERT), bf16), pltpu.VMEM((D_EXPERT, D), bf16)],
    )
    return pl.pallas_call(
        _expert_kernel,
        grid_spec=grid_spec,
        out_shape=jax.ShapeDtypeStruct((n_slots, DH), i32),
        compiler_params=pltpu.CompilerParams(dimension_semantics=("arbitrary",),
                                             vmem_limit_bytes=VMEM_LIMIT),
        name="experts",
    )(block_e, n_valid, n_used, xs, w_gate, w_up, w_down)


def _combine_kernel(yg_ref, x1_ref, wtok_ref, mod_ref, *rest):
    o_ref = rest[-1]
    w = wtok_ref[...]
    g2 = mod_ref[5:6, :]
    a_hi, a_lo = _unpack_halves(yg_ref[0])
    b_hi, b_lo = _unpack_halves(yg_ref[1])
    moe = jnp.concatenate([w[:, 0:1] * a_hi + w[:, 1:2] * b_hi,
                           w[:, 0:1] * a_lo + w[:, 1:2] * b_lo], axis=1)
    o_ref[...] = x1_ref[...] + g2 * moe


def _combine(yg, x1, wtok, mod6, T, b0, n_total, out_prev):
    Nc, D = x1.shape
    TS = min(COMBINE_TILE, T)
    per_seq = T // TS
    blk0 = b0 * per_seq
    in_specs = [pl.BlockSpec((TOP_K, TS, D // 2), lambda i: (0, i, 0)),
                pl.BlockSpec((TS, D), lambda i: (i, 0)),
                pl.BlockSpec((TS, LANES), lambda i: (i, 0)),
                pl.BlockSpec((None, 6, D), lambda i: (b0 + i // per_seq, 0, 0))]
    args = [yg, x1, wtok, mod6]
    aliases = {}
    if out_prev is not None:
        in_specs.append(pl.BlockSpec(memory_space=pl.ANY))
        args.append(out_prev)
        aliases = {len(args) - 1: 0}
    return pl.pallas_call(
        _combine_kernel,
        grid=(Nc // TS,),
        in_specs=in_specs,
        out_specs=pl.BlockSpec((TS, D), lambda i: (blk0 + i, 0)),
        out_shape=jax.ShapeDtypeStruct((n_total, D), f32),
        input_output_aliases=aliases,
        compiler_params=pltpu.CompilerParams(dimension_semantics=("arbitrary",)),
        name="combine",
    )(*args)


def _block_diag_mean(n, blk):
    m = np.kron(np.eye(n // blk, dtype=np.float32), np.full((blk, blk), 1.0 / blk, np.float32))
    return jnp.asarray(m, bf16)


def _layer(x, mod6, bias, l, w):
    B, T, D = x.shape
    N = B * T
    w_rg, w_re = w["w_router_group"][l], w["w_router_expert"][l]
    wr = jnp.zeros((D, ROUTER_COLS), f32)
    wr = wr.at[:, 0:N_GROUPS].set(w_rg).at[:, EXPERT_COL0:EXPERT_COL0 + N_EXPERTS].set(w_re)
    br = jnp.zeros((1, ROUTER_COLS), f32)
    br = br.at[0, 0:N_GROUPS].set(w["b_router_group"][l])
    br = br.at[0, EXPERT_COL0:EXPERT_COL0 + N_EXPERTS].set(w["b_router_expert"][l])
    wrh = wr.astype(bf16)
    wrl = (wr - wrh.astype(f32)).astype(bf16)
    p = dict(
        gmix=w["norm_mix_g"][l].reshape(1, D), gffn=w["norm_ffn_g"][l].reshape(1, D),
        wab=w["w_in"][l][:, 0:2 * D_CONV].astype(bf16),
        wproj=w["w_in"][l][:, 2 * D_CONV:].astype(bf16).reshape(D, -1, PROJ_CHUNK).transpose(1, 0, 2),
        dw=jnp.broadcast_to(w["dw_kernel"][l][:, None, :], (CONV_WIDTH, 8, D_CONV)), dwb=w["dw_bias"][l].reshape(1, D_CONV),
        lng=w["conv_ln_g"][l].reshape(1, D_CONV), lnb=w["conv_ln_b"][l].reshape(1, D_CONV),
        wco=w["w_conv_out"][l].astype(bf16), wao=w["w_attn_out"][l].astype(bf16),
        wout=w["w_out"][l].astype(bf16),
        qg=(jnp.tile(w["q_norm_g"][l], N_Q_HEADS) * (HEAD_DIM ** -0.5)).reshape(1, D_Q),
        kg=jnp.tile(w["k_norm_g"][l], N_KV_HEADS).reshape(1, D_KV),
        bq=_block_diag_mean(D_Q, HEAD_DIM), bk=_block_diag_mean(D_KV, HEAD_DIM),
        bias=bias, sinks=w["sinks"][l], wrh=wrh, wrl=wrl, br=br,
    )
    w_gate, w_up, w_down = w["w_exp_gate"][l], w["w_exp_up"][l], w["w_exp_down"][l]

    n_chunks = MOE_CHUNKS if B % MOE_CHUNKS == 0 else 1
    Bc = B // n_chunks
    Nc = Bc * T
    n_blocks = -(-(Nc * TOP_K) // MOE_BLOCK) + N_EXPERTS
    blk0 = jnp.arange(n_blocks, dtype=i32) * MOE_BLOCK
    stage = []
    for ch in range(n_chunks):
        x1, h2, route, wtok, cnt = _mixer(x, mod6, p, ch * Bc, Bc)
        counts = cnt[:, 0].astype(i32)
        pcounts = (counts + MOE_BLOCK - 1) // MOE_BLOCK * MOE_BLOCK
        pend = jnp.cumsum(pcounts)
        pstart = pend - pcounts
        block_e = jnp.minimum(jnp.sum((pend[None, :] <= blk0[:, None]).astype(i32), axis=1), N_EXPERTS - 1)
        n_valid = jnp.clip((pstart + counts)[block_e] - blk0, 0, MOE_BLOCK).astype(i32)
        n_used = (pend[-1:] // MOE_BLOCK).astype(i32)
        dest = _dest_slots(pstart, route)
        xs = _sc_scatter(h2.reshape(Nc, D // 2), dest, n_blocks * MOE_BLOCK)
        stage.append((x1, wtok, dest, xs, block_e, n_valid, n_used))
    ys = [_experts(be, nv, nu, xs, w_gate, w_up, w_down) for (_, _, _, xs, be, nv, nu) in stage]
    ygs = [_sc_gather(y, st[2], Nc) for y, st in zip(ys, stage)]
    out = None
    for ch in range(n_chunks):
        x1, wtok = stage[ch][0], stage[ch][1]
        out = _combine(ygs[ch], x1.reshape(Nc, D), wtok, mod6, T, ch * Bc, B * T, out)
    return out.reshape(B, T, D)


def kernel(x, c, w_ada, b_ada, norm_mix_g, w_in, dw_kernel, dw_bias, conv_ln_g, conv_ln_b,
           w_conv_out, q_norm_g, k_norm_g, sinks, w_attn_out, w_out, rel_bias_table, norm_ffn_g,
           w_router_group, b_router_group, w_router_expert, b_router_expert,
           w_exp_gate, w_exp_up, w_exp_down):
    w = dict(norm_mix_g=norm_mix_g, w_in=w_in, dw_kernel=dw_kernel, dw_bias=dw_bias,
             conv_ln_g=conv_ln_g, conv_ln_b=conv_ln_b, w_conv_out=w_conv_out, q_norm_g=q_norm_g,
             k_norm_g=k_norm_g, sinks=sinks, w_attn_out=w_attn_out, w_out=w_out,
             norm_ffn_g=norm_ffn_g, w_router_group=w_router_group, b_router_group=b_router_group,
             w_router_expert=w_router_expert, b_router_expert=b_router_expert,
             w_exp_gate=w_exp_gate, w_exp_up=w_exp_up, w_exp_down=w_exp_down)
    B = x.shape[0]
    bias = _bias_band(rel_bias_table).reshape(N_KV_HEADS, Q_PER_KV * WINDOW, 2 * WINDOW)
    for l in range(w_ada.shape[0]):
        mod6 = _modulation(c, w_ada[l], b_ada[l]).reshape(B, 6, D_MODEL)
        x = _layer(x, mod6, bias, l, w)
    return x
```

```python
import functools
import math

import jax
import jax.numpy as jnp
import numpy as np
from jax import lax
from jax.experimental import pallas as pl
from jax.experimental.pallas import tpu as pltpu
from jax.experimental.pallas import tpu_sc as plsc

D_MODEL = 1024
D_CONV = 512
CONV_WIDTH = 31
N_Q_HEADS = 8
N_KV_HEADS = 2
HEAD_DIM = 64
Q_PER_KV = N_Q_HEADS // N_KV_HEADS
WINDOW = 128
N_BUCKETS = 32
MAX_DISTANCE = 128
N_GROUPS = 4
EXPERTS_PER_GROUP = 8
N_EXPERTS = N_GROUPS * EXPERTS_PER_GROUP
TOP_K = 2
D_EXPERT = 256
D_Q = N_Q_HEADS * HEAD_DIM
D_KV = N_KV_HEADS * HEAD_DIM
EPS = 1e-6
NEG_INF = -1e30

LANES = 128
SEQ_TILE = 512
CONV_ROWS = 32
PROJ_CHUNK = 256
CONV_SPAN = 24
MOE_BLOCK = 512
MOE_CHUNKS = 2
DEST_TILE = 8192
COMBINE_TILE = 512
SC_CORES = 2
SC_SUBCORES = 16
SC_WORKERS = SC_CORES * SC_SUBCORES
SC_CHUNK = 32
ROUTER_COLS = LANES
EXPERT_COL0 = 8
VMEM_LIMIT = 56 * 1024 * 1024

f32 = jnp.float32
bf16 = jnp.bfloat16
i32 = jnp.int32


def _dot(a, b):
    return jnp.dot(a, b, preferred_element_type=f32)


def _split(a):
    hi = a.astype(bf16)
    lo = (a - hi.astype(f32)).astype(bf16)
    return hi, lo


def _pack_halves(x):
    c = x.shape[1] // 2
    hi = lax.bitcast_convert_type(x[:, 0:c].astype(bf16).astype(f32), jnp.uint32)
    lo = lax.bitcast_convert_type(x[:, c:2 * c].astype(bf16).astype(f32), jnp.uint32)
    word = (hi & jnp.uint32(0xFFFF0000)) | (lo >> jnp.uint32(16))
    return lax.bitcast_convert_type(word, i32)


def _unpack_halves(word):
    u = lax.bitcast_convert_type(word, jnp.uint32)
    hi = lax.bitcast_convert_type(u & jnp.uint32(0xFFFF0000), f32)
    lo = lax.bitcast_convert_type(u << jnp.uint32(16), f32)
    return hi, lo


def _dot3(a, b):
    ah, al = _split(a)
    bh, bl = _split(b)
    return _dot(ah, bh) + _dot(al, bh) + _dot(ah, bl)


def _mod_kernel(c_ref, w_ref, b_ref, o_ref):
    c = c_ref[...]
    s = c * jax.nn.sigmoid(c)
    o_ref[...] = _dot3(s, w_ref[...]) + b_ref[...]


def _modulation(c, w_ada, b_ada):
    B, D = c.shape
    n_out = w_ada.shape[1]
    return pl.pallas_call(
        _mod_kernel,
        grid=(n_out // D,),
        in_specs=[pl.BlockSpec((B, D), lambda j: (0, 0)),
                  pl.BlockSpec((D, D), lambda j: (0, j)),
                  pl.BlockSpec((1, D), lambda j: (0, j))],
        out_specs=pl.BlockSpec((B, D), lambda j: (0, j)),
        out_shape=jax.ShapeDtypeStruct((B, n_out), f32),
        name="modulation",
    )(c, w_ada, b_ada.reshape(1, n_out))


def _band_buckets():
    qi = np.arange(WINDOW)[:, None]
    kj = np.arange(2 * WINDOW)[None, :]
    dist = np.clip(qi + WINDOW - kj, 0, MAX_DISTANCE)
    max_exact = N_BUCKETS // 2
    d = np.maximum(dist, 1).astype(np.float32)
    large = max_exact + (np.log(d / np.float32(max_exact)) / np.float32(math.log(MAX_DISTANCE / max_exact))
                         * np.float32(N_BUCKETS - max_exact)).astype(np.int32)
    large = np.minimum(large, N_BUCKETS - 1)
    return np.where(dist < max_exact, dist, large).astype(np.int32)


def _bias_kernel(tab_ref, bucket_ref, o_ref):
    bk = bucket_ref[...]
    for h in range(N_Q_HEADS):
        acc = jnp.zeros(bk.shape, f32)
        for b in range(N_BUCKETS):
            acc = jnp.where(bk == b, tab_ref[b, h], acc)
        o_ref[h] = acc


def _bias_band(rel_bias_table):
    return pl.pallas_call(
        _bias_kernel,
        in_specs=[pl.BlockSpec(memory_space=pltpu.SMEM),
                  pl.BlockSpec(memory_space=pltpu.VMEM)],
        out_specs=pl.BlockSpec(memory_space=pltpu.VMEM),
        out_shape=jax.ShapeDtypeStruct((N_Q_HEADS, WINDOW, 2 * WINDOW), f32),
        name="bias_band",
    )(rel_bias_table, jnp.asarray(_band_buckets()))


def _mixer_kernel(x_ref, mod_ref, gmix_ref, gffn_ref, wab_ref, wproj_ref, dw_ref, dwb_ref, lng_ref, lnb_ref,
                  wco_ref, wao_ref, wout_ref, qg_ref, kg_ref, bq_ref, bk_ref, bias_ref, sink_ref,
                  wrh_ref, wrl_ref, br_ref, tri_ref,
                  x1_ref, h2_ref, route_ref, wtok_ref, cnt_ref,
                  uext, ush, conv_sc, proj_sc, k2, v2, osc):
    TM = x_ref.shape[0]
    HALO = 32
    b = pl.program_id(0)
    t = pl.program_id(1)
    first = t == 0

    @pl.when(first)
    def _():
        uext[0:HALO, :] = jnp.zeros((HALO, D_CONV), f32)
        k2[:, 0:WINDOW, :] = jnp.zeros((N_KV_HEADS, WINDOW, LANES), bf16)
        v2[:, 0:WINDOW, :] = jnp.zeros((N_KV_HEADS, WINDOW, LANES), bf16)

    @pl.when(first & (b == 0))
    def _():
        cnt_ref[...] = jnp.zeros(cnt_ref.shape, f32)

    x = x_ref[...]
    mod = mod_ref[...]
    sh1, sc1, g1 = mod[0:1, :], mod[1:2, :], mod[2:3, :]
    sh2, sc2, g2 = mod[3:4, :], mod[4:5, :], mod[5:6, :]
    del g2

    ms = jnp.mean(x * x, axis=-1, keepdims=True)
    h = (x * lax.rsqrt(ms + EPS)) * gmix_ref[...]
    h = h * (1.0 + sc1) + sh1
    hb = h.astype(bf16)

    ab = _dot(hb, wab_ref[...])
    u = ab[:, 0:D_CONV] * jax.nn.sigmoid(ab[:, D_CONV:2 * D_CONV])
    uext[HALO:HALO + TM, :] = u
    for r in range(1, 8):
        ush[r - 1] = uext[r:r + TM + CONV_SPAN, :]
    n_q = D_Q // PROJ_CHUNK
    n_g = D_MODEL // PROJ_CHUNK
    n_jobs = wproj_ref.shape[0]
    n_rb = TM // CONV_ROWS
    n_t8 = CONV_ROWS // 8
    job_at = {(i * n_rb) // n_jobs: i for i in range(n_jobs)}
    assert len(job_at) == n_jobs
    for rb in range(n_rb):
        if rb in job_at:
            proj_sc[job_at[rb]] = _dot(hb, wproj_ref[job_at[rb]])
        base = rb * CONV_ROWS
        accs = [None] * n_t8
        for j in range(CONV_WIDTH):
            a, r = divmod(j + 2, 8)
            tap = dw_ref[j]
            for s8 in range(n_t8):
                rows = slice(base + 8 * (a + s8), base + 8 * (a + s8) + 8)
                win = uext[rows, :] if r == 0 else ush[r - 1, rows, :]
                term = tap * win
                accs[s8] = term if accs[s8] is None else accs[s8] + term
        for s8 in range(n_t8):
            conv_sc[base + 8 * s8:base + 8 * s8 + 8, :] = accs[s8] + dwb_ref[...]
    conv = conv_sc[...]
    uext[0:HALO, :] = uext[TM:TM + HALO, :]
    mu = jnp.mean(conv, axis=-1, keepdims=True)
    dc = conv - mu
    var = jnp.mean(dc * dc, axis=-1, keepdims=True)
    yn = dc * lax.rsqrt(var + EPS) * lng_ref[...] + lnb_ref[...]
    act = yn * jax.nn.sigmoid(yn)
    y_conv = _dot(act.astype(bf16), wco_ref[...])

    q = jnp.concatenate([proj_sc[i] for i in range(n_q)], axis=1)
    k = proj_sc[n_q, :, 0:D_KV]
    v = proj_sc[n_q, :, D_KV:2 * D_KV]

    def head_norm(z, blk_ref, g):
        msq = _dot((z * z).astype(bf16), blk_ref[...])
        return z * lax.rsqrt(msq + EPS) * g

    qn = head_norm(q, bq_ref, qg_ref[...]).astype(bf16)
    kn = head_norm(k, bk_ref, kg_ref[...])
    lo_half = lax.broadcasted_iota(i32, (TM, LANES), 1) < HEAD_DIM
    kr = pltpu.roll(kn, HEAD_DIM, 1)
    vr = pltpu.roll(v, HEAD_DIM, 1)
    k2[0, WINDOW:WINDOW + TM, :] = jnp.where(lo_half, kn, kr).astype(bf16)
    k2[1, WINDOW:WINDOW + TM, :] = jnp.where(lo_half, kr, kn).astype(bf16)
    v2[0, WINDOW:WINDOW + TM, :] = jnp.where(lo_half, v, vr).astype(bf16)
    v2[1, WINDOW:WINDOW + TM, :] = jnp.where(lo_half, vr, v).astype(bf16)

    QROWS = Q_PER_KV * WINDOW
    row = lax.broadcasted_iota(i32, (QROWS, 2 * WINDOW), 0) & (WINDOW - 1)
    col = lax.broadcasted_iota(i32, (QROWS, 2 * WINDOW), 1)
    dist = row + WINDOW - col
    in_window = (dist >= 0) & (dist < WINDOW)
    first_mask = in_window & (col >= jnp.where(first, WINDOW, 0))
    hrow = lax.broadcasted_iota(i32, (QROWS, 1), 0) // WINDOW
    lo128 = lax.broadcasted_iota(i32, (WINDOW, LANES), 1) < HEAD_DIM
    zero_q = jnp.zeros((WINDOW, LANES), bf16)
    for g in range(N_KV_HEADS):
        sink = jnp.where(hrow == 0, sink_ref[4 * g],
                         jnp.where(hrow == 1, sink_ref[4 * g + 1],
                                   jnp.where(hrow == 2, sink_ref[4 * g + 2], sink_ref[4 * g + 3])))
        bias_g = bias_ref[g]
        for j in range(TM // WINDOW):
            rs = slice(j * WINDOW, (j + 1) * WINDOW)
            qa = qn[rs, 2 * LANES * g:2 * LANES * g + LANES]
            qb = qn[rs, 2 * LANES * g + LANES:2 * LANES * (g + 1)]
            qs = jnp.concatenate([jnp.where(lo128, qa, zero_q), jnp.where(lo128, zero_q, qa),
                                  jnp.where(lo128, qb, zero_q), jnp.where(lo128, zero_q, qb)], axis=0)
            kk = k2[g, j * WINDOW:(j + 2) * WINDOW, :]
            s = lax.dot_general(qs, kk, (((1,), (1,)), ((), ())), preferred_element_type=f32)
            msk = first_mask if j == 0 else in_window
            logits = jnp.where(msk, s + bias_g, NEG_INF)
            m = jnp.maximum(jnp.max(logits, axis=-1, keepdims=True), sink)
            p = jnp.exp(logits - m)
            den = jnp.sum(p, axis=-1, keepdims=True) + jnp.exp(sink - m)
            o2 = _dot(p.astype(bf16), v2[g, j * WINDOW:(j + 2) * WINDOW, :]) * (1.0 / den)
            osc[rs, 2 * LANES * g:2 * LANES * g + LANES] = jnp.where(
                lo128, o2[0:WINDOW], o2[WINDOW:2 * WINDOW]).astype(bf16)
            osc[rs, 2 * LANES * g + LANES:2 * LANES * (g + 1)] = jnp.where(
                lo128, o2[2 * WINDOW:3 * WINDOW], o2[3 * WINDOW:4 * WINDOW]).astype(bf16)
    k2[:, 0:WINDOW, :] = k2[:, TM:TM + WINDOW, :]
    v2[:, 0:WINDOW, :] = v2[:, TM:TM + WINDOW, :]
    y_attn = _dot(osc[...], wao_ref[...])

    merged = []
    for i in range(n_g):
        cs = slice(i * PROJ_CHUNK, (i + 1) * PROJ_CHUNK)
        g_conv = jax.nn.sigmoid(proj_sc[n_q + 1 + i])
        g_attn = jax.nn.sigmoid(proj_sc[n_q + 1 + n_g + i])
        merged.append((g_conv * y_conv[:, cs] + g_attn * y_attn[:, cs]).astype(bf16))
    merged = jnp.concatenate(merged, axis=1)
    x1 = x + g1 * _dot(merged, wout_ref[...])
    x1_ref[...] = x1

    ms2 = jnp.mean(x1 * x1, axis=-1, keepdims=True)
    h2 = (x1 * lax.rsqrt(ms2 + EPS)) * gffn_ref[...]
    h2 = h2 * (1.0 + sc2) + sh2
    h2_ref[...] = _pack_halves(h2)
    hh, hl = _split(h2)
    lg = _dot(hh, wrh_ref[...]) + _dot(hl, wrh_ref[...]) + _dot(hh, wrl_ref[...]) + br_ref[...]
    lt = lg.T
    gl = lt[0:N_GROUPS, :]
    grow = lax.broadcasted_iota(i32, (N_GROUPS, TM), 0)
    gmax = jnp.max(gl, axis=0, keepdims=True)
    gi = jnp.min(jnp.where(gl == gmax, grow, N_GROUPS), axis=0, keepdims=True)
    p_top = 1.0 / jnp.sum(jnp.exp(gl - gmax), axis=0, keepdims=True)
    sel = lt[EXPERT_COL0:EXPERT_COL0 + EXPERTS_PER_GROUP, :]
    for gg in range(1, N_GROUPS):
        lo_r = EXPERT_COL0 + gg * EXPERTS_PER_GROUP
        sel = jnp.where(gi == gg, lt[lo_r:lo_r + EXPERTS_PER_GROUP, :], sel)
    erow = lax.broadcasted_iota(i32, (EXPERTS_PER_GROUP, TM), 0)
    m1 = jnp.max(sel, axis=0, keepdims=True)
    i1 = jnp.min(jnp.where(sel == m1, erow, EXPERTS_PER_GROUP), axis=0, keepdims=True)
    rest = jnp.where(erow == i1, -jnp.inf, sel)
    m2 = jnp.max(rest, axis=0, keepdims=True)
    i2 = jnp.min(jnp.where(rest == m2, erow, EXPERTS_PER_GROUP), axis=0, keepdims=True)
    z = jnp.sum(jnp.exp(sel - m1), axis=0, keepdims=True)
    v1 = 1.0 / z
    v2nd = jnp.exp(m2 - m1) / z
    w1 = v1 / (v1 + v2nd) * p_top
    w2 = v2nd / (v1 + v2nd) * p_top
    e1 = gi * EXPERTS_PER_GROUP + i1
    e2 = gi * EXPERTS_PER_GROUP + i2

    xrow = lax.broadcasted_iota(i32, (N_EXPERTS, TM), 0)
    oh1 = xrow == e1
    oh2 = xrow == e2
    both = jnp.where(oh1 | oh2, 1.0, 0.0)
    prefix = _dot(both.astype(bf16), tri_ref[...]) + cnt_ref[:, 0:1]
    r1 = jnp.sum(jnp.where(oh1, prefix, 0.0), axis=0, keepdims=True)
    r2 = jnp.sum(jnp.where(oh2, prefix, 0.0), axis=0, keepdims=True)
    cnt_ref[...] = cnt_ref[...] + jnp.sum(both, axis=1, keepdims=True)

    route_ref[...] = jnp.concatenate(
        [e1, e2, r1.astype(i32), r2.astype(i32), jnp.zeros((4, TM), i32)], axis=0)
    wpad = jnp.concatenate([w1, w2, jnp.zeros((LANES - 2, TM), f32)], axis=0)
    wtok_ref[...] = wpad.T


def _mixer(x, mod6, p, b0, B):
    _, T, D = x.shape
    TM = min(SEQ_TILE, T)
    nt = T // TM
    N = B * T
    const = lambda shape: pl.BlockSpec(shape, lambda b, t: (0,) * len(shape))
    in_specs = [
        pl.BlockSpec((None, TM, D), lambda b, t: (b0 + b, t, 0)),
        pl.BlockSpec((None, 6, D), lambda b, t: (b0 + b, 0, 0)),
        const((1, D)), const((1, D)),
        const(p["wab"].shape), const(p["wproj"].shape),
        const((CONV_WIDTH, 8, D_CONV)), const((1, D_CONV)), const((1, D_CONV)), const((1, D_CONV)),
        const((D_CONV, D)), const((D_Q, D)), const((D, D)),
        const((1, D_Q)), const((1, D_KV)),
        const((D_Q, D_Q)), const((D_KV, D_KV)),
        const((N_KV_HEADS, Q_PER_KV * WINDOW, 2 * WINDOW)),
        pl.BlockSpec(memory_space=pltpu.SMEM),
        const((D, ROUTER_COLS)), const((D, ROUTER_COLS)), const((1, ROUTER_COLS)),
        const((TM, TM)),
    ]
    out_specs = [
        pl.BlockSpec((None, TM, D), lambda b, t: (b, t, 0)),
        pl.BlockSpec((None, TM, D // 2), lambda b, t: (b, t, 0)),
        pl.BlockSpec((8, TM), lambda b, t: (0, b * nt + t)),
        pl.BlockSpec((TM, LANES), lambda b, t: (b * nt + t, 0)),
        pl.BlockSpec((N_EXPERTS, LANES), lambda b, t: (0, 0)),
    ]
    out_shape = [
        jax.ShapeDtypeStruct((B, T, D), f32),
        jax.ShapeDtypeStruct((B, T, D // 2), i32),
        jax.ShapeDtypeStruct((8, N), i32),
        jax.ShapeDtypeStruct((N, LANES), f32),
        jax.ShapeDtypeStruct((N_EXPERTS, LANES), f32),
    ]
    scratch = [
        pltpu.VMEM((TM + 32, D_CONV), f32),
        pltpu.VMEM((7, TM + CONV_SPAN, D_CONV), f32),
        pltpu.VMEM((TM, D_CONV), f32),
        pltpu.VMEM((p["wproj"].shape[0], TM, PROJ_CHUNK), f32),
        pltpu.VMEM((N_KV_HEADS, TM + WINDOW, LANES), bf16),
        pltpu.VMEM((N_KV_HEADS, TM + WINDOW, LANES), bf16),
        pltpu.VMEM((TM, D_Q), bf16),
    ]
    tri = jnp.asarray(np.triu(np.ones((TM, TM), np.float32), 1), bf16)
    return pl.pallas_call(
        _mixer_kernel,
        grid=(B, nt),
        in_specs=in_specs, out_specs=out_specs, out_shape=out_shape, scratch_shapes=scratch,
        compiler_params=pltpu.CompilerParams(
            dimension_semantics=("arbitrary", "arbitrary"), vmem_limit_bytes=VMEM_LIMIT),
        name="mixer_router",
    )(x, mod6, p["gmix"], p["gffn"], p["wab"], p["wproj"], p["dw"], p["dwb"], p["lng"], p["lnb"],
      p["wco"], p["wao"], p["wout"], p["qg"], p["kg"], p["bq"], p["bk"], p["bias"], p["sinks"],
      p["wrh"], p["wrl"], p["br"], tri)


def _dest_kernel(pstart_ref, route_ref, dest_ref):
    e = route_ref[0:2, :]
    base = jnp.zeros(e.shape, i32)
    for x in range(N_EXPERTS):
        base = jnp.where(e == x, pstart_ref[x], base)
    dest_ref[...] = base + route_ref[2:4, :]


def _dest_slots(pstart, route):
    N = route.shape[1]
    TS = min(DEST_TILE, N)
    return pl.pallas_call(
        _dest_kernel,
        grid=(N // TS,),
        in_specs=[pl.BlockSpec(memory_space=pltpu.SMEM),
                  pl.BlockSpec((8, TS), lambda i: (0, i))],
        out_specs=pl.BlockSpec((2, TS), lambda i: (0, i)),
        out_shape=jax.ShapeDtypeStruct((2, N), i32),
        name="dest_slots",
    )(pstart, route)


def _sc_mesh():
    return plsc.VectorSubcoreMesh(core_axis_name="c", subcore_axis_name="s")


def _sc_scatter(h2, dest, n_slots):
    N, D = h2.shape
    per_w = N // SC_WORKERS
    C = SC_CHUNK
    n_chunks = per_w // C
    dest4 = dest.reshape(TOP_K, SC_WORKERS, n_chunks, C)

    @functools.partial(
        pl.kernel, mesh=_sc_mesh(),
        out_type=jax.ShapeDtypeStruct((n_slots, D), h2.dtype),
        scratch_types=[pltpu.VMEM((n_chunks, C), i32), pltpu.VMEM((n_chunks, C), i32),
                       pltpu.VMEM((2, C, D), h2.dtype),
                       pltpu.SemaphoreType.DMA((2,)), pltpu.SemaphoreType.DMA((2,))],
        name="sc_dispatch_scatter",
    )
    def run(h2_hbm, dest_hbm, xs_hbm, idx0, idx1, buf, sem_in, sem_out):
        wid = lax.axis_index("s") * SC_CORES + lax.axis_index("c")
        base = wid * per_w
        pltpu.sync_copy(dest_hbm.at[0, wid], idx0)
        pltpu.sync_copy(dest_hbm.at[1, wid], idx1)

        def load(j, b):
            return pltpu.make_async_copy(h2_hbm.at[pl.ds(base + j * C, C)], buf.at[b], sem_in.at[b])

        def put(j, b, idx):
            return pltpu.make_async_copy(buf.at[b], xs_hbm.at[idx.at[j]], sem_out.at[b])

        load(0, 0).start()

        @pl.loop(0, n_chunks, step=2)
        def _(j0):
            for b in range(2):
                j = j0 + b
                load(j, b).wait()

                @pl.when(j >= 1)
                def _():
                    put(j - 1, 1 - b, idx0).wait()
                    put(j - 1, 1 - b, idx1).wait()

                @pl.when(j + 1 < n_chunks)
                def _():
                    load(j + 1, 1 - b).start()

                put(j, b, idx0).start()
                put(j, b, idx1).start()

        put(n_chunks - 1, 1, idx0).wait()
        put(n_chunks - 1, 1, idx1).wait()

    return run(h2, dest4)


def _sc_gather(y, dest, N):
    D = y.shape[1]
    per_w = N // SC_WORKERS
    C = SC_CHUNK
    n_chunks = per_w // C
    dest4 = dest.reshape(TOP_K, SC_WORKERS, n_chunks, C)

    @functools.partial(
        pl.kernel, mesh=_sc_mesh(),
        out_type=jax.ShapeDtypeStruct((TOP_K, N, D), y.dtype),
        scratch_types=[pltpu.VMEM((n_chunks, C), i32), pltpu.VMEM((n_chunks, C), i32),
                       pltpu.VMEM((2, C, D), y.dtype),
                       pltpu.SemaphoreType.DMA((2,)), pltpu.SemaphoreType.DMA((2,))],
        name="sc_combine_gather",
    )
    def run(y_hbm, dest_hbm, yg_hbm, idx0, idx1, buf, sem_in, sem_out):
        wid = lax.axis_index("s") * SC_CORES + lax.axis_index("c")
        base = wid * per_w
        pltpu.sync_copy(dest_hbm.at[0, wid], idx0)
        pltpu.sync_copy(dest_hbm.at[1, wid], idx1)
        idx = (idx0, idx1)

        def get(j, k):
            return pltpu.make_async_copy(y_hbm.at[idx[k].at[j]], buf.at[k], sem_in.at[k])

        def put(j, k):
            return pltpu.make_async_copy(buf.at[k], yg_hbm.at[k, pl.ds(base + j * C, C)], sem_out.at[k])

        get(0, 0).start()
        get(0, 1).start()

        @pl.loop(0, n_chunks)
        def _(j):
            for k in range(TOP_K):
                get(j, k).wait()
                put(j, k).start()
            for k in range(TOP_K):
                put(j, k).wait()

                @pl.when(j + 1 < n_chunks)
                def _():
                    get(j + 1, k).start()

    return run(y, dest4)


def _expert_kernel(be_ref, nv_ref, nu_ref, xs_ref, wg_ref, wu_ref, wdn_ref, y_ref, wgu_ref, wd_ref):
    i = pl.program_id(0)
    in_use = i < nu_ref[0]

    @pl.when(in_use & ((i == 0) | (be_ref[i] != be_ref[jnp.maximum(i - 1, 0)])))
    def _():
        wgu_ref[:, 0:D_EXPERT] = wg_ref[...].astype(bf16)
        wgu_ref[:, D_EXPERT:2 * D_EXPERT] = wu_ref[...].astype(bf16)
        wd_ref[...] = wdn_ref[...].astype(bf16)

    @pl.when(in_use)
    def _():
        live = lax.broadcasted_iota(i32, (MOE_BLOCK, 1), 0) < nv_ref[i]
        x_hi, x_lo = _unpack_halves(jnp.where(live, xs_ref[...], 0))
        half = x_hi.shape[1]
        gu = (_dot(x_hi.astype(bf16), wgu_ref[0:half, :]) +
              _dot(x_lo.astype(bf16), wgu_ref[half:2 * half, :]))
        gate = gu[:, 0:D_EXPERT]
        hid = (gate * jax.nn.sigmoid(gate)) * gu[:, D_EXPERT:2 * D_EXPERT]
        y_ref[...] = _pack_halves(_dot(hid.astype(bf16), wd_ref[...]))


def _experts(block_e, n_valid, n_used, xs, w_gate, w_up, w_down):
    n_slots, DH = xs.shape
    D = 2 * DH
    n_blocks = n_slots // MOE_BLOCK
    last = lambda i, nu: jnp.minimum(i, nu[0] - 1)
    expert = lambda i, be, nv, nu: (be[last(i, nu)], 0, 0)
    grid_spec = pltpu.PrefetchScalarGridSpec(
        num_scalar_prefetch=3,
        grid=(n_blocks,),
        in_specs=[pl.BlockSpec((MOE_BLOCK, DH), lambda i, be, nv, nu: (last(i, nu), 0)),
                  pl.BlockSpec((None, D, D_EXPERT), expert),
                  pl.BlockSpec((None, D, D_EXPERT), expert),
                  pl.BlockSpec((None, D_EXPERT, D), expert)],
        out_specs=pl.BlockSpec((MOE_BLOCK, DH), lambda i, be, nv, nu: (last(i, nu), 0)),
        scratch_shapes=[pltpu.VMEM((D, 2 * D_EXPERT), bf16), pltpu.VMEM((D_EXPERT, D), bf16)],
    )
    return pl.pallas_call(
        _expert_kernel,
        grid_spec=grid_spec,
        out_shape=jax.ShapeDtypeStruct((n_slots, DH), i32),
        compiler_params=pltpu.CompilerParams(dimension_semantics=("arbitrary",),
                                             vmem_limit_bytes=VMEM_LIMIT),
        name="experts",
    )(block_e, n_valid, n_used, xs, w_gate, w_up, w_down)


def _combine_kernel(yg_ref, x1_ref, wtok_ref, mod_ref, *rest):
    o_ref = rest[-1]
    w = wtok_ref[...]
    g2 = mod_ref[5:6, :]
    a_hi, a_lo = _unpack_halves(yg_ref[0])
    b_hi, b_lo = _unpack_halves(yg_ref[1])
    moe = jnp.concatenate([w[:, 0:1] * a_hi + w[:, 1:2] * b_hi,
                           w[:, 0:1] * a_lo + w[:, 1:2] * b_lo], axis=1)
    o_ref[...] = x1_ref[...] + g2 * moe


def _combine(yg, x1, wtok, mod6, T, b0, n_total, out_prev):
    Nc, D = x1.shape
    TS = min(COMBINE_TILE, T)
    per_seq = T // TS
    blk0 = b0 * per_seq
    in_specs = [pl.BlockSpec((TOP_K, TS, D // 2), lambda i: (0, i, 0)),
                pl.BlockSpec((TS, D), lambda i: (i, 0)),
                pl.BlockSpec((TS, LANES), lambda i: (i, 0)),
                pl.BlockSpec((None, 6, D), lambda i: (b0 + i // per_seq, 0, 0))]
    args = [yg, x1, wtok, mod6]
    aliases = {}
    if out_prev is not None:
        in_specs.append(pl.BlockSpec(memory_space=pl.ANY))
        args.append(out_prev)
        aliases = {len(args) - 1: 0}
    return pl.pallas_call(
        _combine_kernel,
        grid=(Nc // TS,),
        in_specs=in_specs,
        out_specs=pl.BlockSpec((TS, D), lambda i: (blk0 + i, 0)),
        out_shape=jax.ShapeDtypeStruct((n_total, D), f32),
        input_output_aliases=aliases,
        compiler_params=pltpu.CompilerParams(dimension_semantics=("arbitrary",)),
        name="combine",
    )(*args)


def _block_diag_mean(n, blk):
    m = np.kron(np.eye(n // blk, dtype=np.float32), np.full((blk, blk), 1.0 / blk, np.float32))
    return jnp.asarray(m, bf16)


def _layer(x, mod6, bias, l, w):
    B, T, D = x.shape
    N = B * T
    w_rg, w_re = w["w_router_group"][l], w["w_router_expert"][l]
    wr = jnp.zeros((D, ROUTER_COLS), f32)
    wr = wr.at[:, 0:N_GROUPS].set(w_rg).at[:, EXPERT_COL0:EXPERT_COL0 + N_EXPERTS].set(w_re)
    br = jnp.zeros((1, ROUTER_COLS), f32)
    br = br.at[0, 0:N_GROUPS].set(w["b_router_group"][l])
    br = br.at[0, EXPERT_COL0:EXPERT_COL0 + N_EXPERTS].set(w["b_router_expert"][l])
    wrh = wr.astype(bf16)
    wrl = (wr - wrh.astype(f32)).astype(bf16)
    p = dict(
        gmix=w["norm_mix_g"][l].reshape(1, D), gffn=w["norm_ffn_g"][l].reshape(1, D),
        wab=w["w_in"][l][:, 0:2 * D_CONV].astype(bf16),
        wproj=w["w_in"][l][:, 2 * D_CONV:].astype(bf16).reshape(D, -1, PROJ_CHUNK).transpose(1, 0, 2),
        dw=jnp.broadcast_to(w["dw_kernel"][l][:, None, :], (CONV_WIDTH, 8, D_CONV)), dwb=w["dw_bias"][l].reshape(1, D_CONV),
        lng=w["conv_ln_g"][l].reshape(1, D_CONV), lnb=w["conv_ln_b"][l].reshape(1, D_CONV),
        wco=w["w_conv_out"][l].astype(bf16), wao=w["w_attn_out"][l].astype(bf16),
        wout=w["w_out"][l].astype(bf16),
        qg=(jnp.tile(w["q_norm_g"][l], N_Q_HEADS) * (HEAD_DIM ** -0.5)).reshape(1, D_Q),
        kg=jnp.tile(w["k_norm_g"][l], N_KV_HEADS).reshape(1, D_KV),
        bq=_block_diag_mean(D_Q, HEAD_DIM), bk=_block_diag_mean(D_KV, HEAD_DIM),
        bias=bias, sinks=w["sinks"][l], wrh=wrh, wrl=wrl, br=br,
    )
    w_gate, w_up, w_down = w["w_exp_gate"][l], w["w_exp_up"][l], w["w_exp_down"][l]

    n_chunks = MOE_CHUNKS if B % MOE_CHUNKS == 0 else 1
    Bc = B // n_chunks
    Nc = Bc * T
    n_blocks = -(-(Nc * TOP_K) // MOE_BLOCK) + N_EXPERTS
    blk0 = jnp.arange(n_blocks, dtype=i32) * MOE_BLOCK
    stage = []
    for ch in range(n_chunks):
        x1, h2, route, wtok, cnt = _mixer(x, mod6, p, ch * Bc, Bc)
        counts = cnt[:, 0].astype(i32)
        pcounts = (counts + MOE_BLOCK - 1) // MOE_BLOCK * MOE_BLOCK
        pend = jnp.cumsum(pcounts)
        pstart = pend - pcounts
        block_e = jnp.minimum(jnp.sum((pend[None, :] <= blk0[:, None]).astype(i32), axis=1), N_EXPERTS - 1)
        n_valid = jnp.clip((pstart + counts)[block_e] - blk0, 0, MOE_BLOCK).astype(i32)
        n_used = (pend[-1:] // MOE_BLOCK).astype(i32)
        dest = _dest_slots(pstart, route)
        xs = _sc_scatter(h2.reshape(Nc, D // 2), dest, n_blocks * MOE_BLOCK)
        stage.append((x1, wtok, dest, xs, block_e, n_valid, n_used))
    ys = [_experts(be, nv, nu, xs, w_gate, w_up, w_down) for (_, _, _, xs, be, nv, nu) in stage]
    ygs = [_sc_gather(y, st[2], Nc) for y, st in zip(ys, stage)]
    out = None
    for ch in range(n_chunks):
        x1, wtok = stage[ch][0], stage[ch][1]
        out = _combine(ygs[ch], x1.reshape(Nc, D), wtok, mod6, T, ch * Bc, B * T, out)
    return out.reshape(B, T, D)


def kernel(x, c, w_ada, b_ada, norm_mix_g, w_in, dw_kernel, dw_bias, conv_ln_g, conv_ln_b,
           w_conv_out, q_norm_g, k_norm_g, sinks, w_attn_out, w_out, rel_bias_table, norm_ffn_g,
           w_router_group, b_router_group, w_router_expert, b_router_expert,
           w_exp_gate, w_exp_up, w_exp_down):
    w = dict(norm_mix_g=norm_mix_g, w_in=w_in, dw_kernel=dw_kernel, dw_bias=dw_bias,
             conv_ln_g=conv_ln_g, conv_ln_b=conv_ln_b, w_conv_out=w_conv_out, q_norm_g=q_norm_g,
             k_norm_g=k_norm_g, sinks=sinks, w_attn_out=w_attn_out, w_out=w_out,
             norm_ffn_g=norm_ffn_g, w_router_group=w_router_group, b_router_group=b_router_group,
             w_router_expert=w_router_expert, b_router_expert=b_router_expert,
             w_exp_gate=w_exp_gate, w_exp_up=w_exp_up, w_exp_down=w_exp_down)
    B = x.shape[0]
    bias = _bias_band(rel_bias_table).reshape(N_KV_HEADS, Q_PER_KV * WINDOW, 2 * WINDOW)
    for l in range(w_ada.shape[0]):
        mod6 = _modulation(c, w_ada[l], b_ada[l]).reshape(B, 6, D_MODEL)
        x = _layer(x, mod6, bias, l, w)
    return x
```

```python
import functools
import math

import jax
import jax.numpy as jnp
import numpy as np
from jax import lax
from jax.experimental import pallas as pl
from jax.experimental.pallas import tpu as pltpu
from jax.experimental.pallas import tpu_sc as plsc

D_MODEL = 1024
D_CONV = 512
CONV_WIDTH = 31
N_Q_HEADS = 8
N_KV_HEADS = 2
HEAD_DIM = 64
Q_PER_KV = N_Q_HEADS // N_KV_HEADS
WINDOW = 128
N_BUCKETS = 32
MAX_DISTANCE = 128
N_GROUPS = 4
EXPERTS_PER_GROUP = 8
N_EXPERTS = N_GROUPS * EXPERTS_PER_GROUP
TOP_K = 2
D_EXPERT = 256
D_Q = N_Q_HEADS * HEAD_DIM
D_KV = N_KV_HEADS * HEAD_DIM
EPS = 1e-6
NEG_INF = -1e30

LANES = 128
SEQ_TILE = 512
CONV_ROWS = 64
NORM_BLK = 256
PROJ_CHUNK = 256
CONV_SPAN = 24
MOE_BLOCK = 512
MOE_CHUNKS = 2
DEST_TILE = 8192
COMBINE_TILE = 512
SC_CORES = 2
SC_SUBCORES = 16
SC_WORKERS = SC_CORES * SC_SUBCORES
SC_CHUNK = 32
ROUTER_COLS = LANES
EXPERT_COL0 = 8
VMEM_LIMIT = 56 * 1024 * 1024

f32 = jnp.float32
bf16 = jnp.bfloat16
i32 = jnp.int32


def _dot(a, b):
    return jnp.dot(a, b, preferred_element_type=f32)


def _split(a):
    hi = a.astype(bf16)
    lo = (a - hi.astype(f32)).astype(bf16)
    return hi, lo


def _pack_halves(x):
    c = x.shape[1] // 2
    hi = lax.bitcast_convert_type(x[:, 0:c].astype(bf16).astype(f32), jnp.uint32)
    lo = lax.bitcast_convert_type(x[:, c:2 * c].astype(bf16).astype(f32), jnp.uint32)
    word = (hi & jnp.uint32(0xFFFF0000)) | (lo >> jnp.uint32(16))
    return lax.bitcast_convert_type(word, i32)


def _unpack_halves(word):
    u = lax.bitcast_convert_type(word, jnp.uint32)
    hi = lax.bitcast_convert_type(u & jnp.uint32(0xFFFF0000), f32)
    lo = lax.bitcast_convert_type(u << jnp.uint32(16), f32)
    return hi, lo


def _dot3(a, b):
    ah, al = _split(a)
    bh, bl = _split(b)
    return _dot(ah, bh) + _dot(al, bh) + _dot(ah, bl)


def _mod_kernel(c_ref, w_ref, b_ref, o_ref):
    c = c_ref[...]
    s = c * jax.nn.sigmoid(c)
    o_ref[...] = _dot3(s, w_ref[...]) + b_ref[...]


def _modulation(c, w_ada, b_ada):
    B, D = c.shape
    n_out = w_ada.shape[1]
    return pl.pallas_call(
        _mod_kernel,
        grid=(n_out // D,),
        in_specs=[pl.BlockSpec((B, D), lambda j: (0, 0)),
                  pl.BlockSpec((D, D), lambda j: (0, j)),
                  pl.BlockSpec((1, D), lambda j: (0, j))],
        out_specs=pl.BlockSpec((B, D), lambda j: (0, j)),
        out_shape=jax.ShapeDtypeStruct((B, n_out), f32),
        name="modulation",
    )(c, w_ada, b_ada.reshape(1, n_out))


def _band_buckets():
    qi = np.arange(WINDOW)[:, None]
    kj = np.arange(2 * WINDOW)[None, :]
    dist = np.clip(qi + WINDOW - kj, 0, MAX_DISTANCE)
    max_exact = N_BUCKETS // 2
    d = np.maximum(dist, 1).astype(np.float32)
    large = max_exact + (np.log(d / np.float32(max_exact)) / np.float32(math.log(MAX_DISTANCE / max_exact))
                         * np.float32(N_BUCKETS - max_exact)).astype(np.int32)
    large = np.minimum(large, N_BUCKETS - 1)
    return np.where(dist < max_exact, dist, large).astype(np.int32)


def _bias_kernel(tab_ref, bucket_ref, o_ref):
    bk = bucket_ref[...]
    for h in range(N_Q_HEADS):
        acc = jnp.zeros(bk.shape, f32)
        for b in range(N_BUCKETS):
            acc = jnp.where(bk == b, tab_ref[b, h], acc)
        o_ref[h] = acc


def _bias_band(rel_bias_table):
    return pl.pallas_call(
        _bias_kernel,
        in_specs=[pl.BlockSpec(memory_space=pltpu.SMEM),
                  pl.BlockSpec(memory_space=pltpu.VMEM)],
        out_specs=pl.BlockSpec(memory_space=pltpu.VMEM),
        out_shape=jax.ShapeDtypeStruct((N_Q_HEADS, WINDOW, 2 * WINDOW), f32),
        name="bias_band",
    )(rel_bias_table, jnp.asarray(_band_buckets()))


def _mixer_kernel(x_ref, mod_ref, gmix_ref, gffn_ref, wab_ref, wproj_ref, dw_ref, dwb_ref, lng_ref, lnb_ref,
                  wco_ref, wao_ref, wout_ref, qg_ref, kg_ref, bq_ref, bk_ref, bias_ref, sink_ref,
                  wrh_ref, wr2_ref, br_ref, tri_ref,
                  x1_ref, h2_ref, route_ref, wtok_ref, cnt_ref,
                  uext, ush, conv_sc, proj_sc, k2, v2, osc):
    TM = x_ref.shape[0]
    HALO = 32
    b = pl.program_id(0)
    t = pl.program_id(1)
    first = t == 0

    @pl.when(first)
    def _():
        uext[0:HALO, :] = jnp.zeros((HALO, D_CONV), f32)
        k2[:, 0:WINDOW, :] = jnp.zeros((N_KV_HEADS, WINDOW, LANES), bf16)
        v2[:, 0:WINDOW, :] = jnp.zeros((N_KV_HEADS, WINDOW, LANES), bf16)

    @pl.when(first & (b == 0))
    def _():
        cnt_ref[...] = jnp.zeros(cnt_ref.shape, f32)

    x = x_ref[...]
    mod = mod_ref[...]
    sh1, sc1, g1 = mod[0:1, :], mod[1:2, :], mod[2:3, :]
    sh2, sc2, g2 = mod[3:4, :], mod[4:5, :], mod[5:6, :]
    del g2

    ms = jnp.mean(x * x, axis=-1, keepdims=True)
    h = (x * lax.rsqrt(ms + EPS)) * gmix_ref[...]
    h = h * (1.0 + sc1) + sh1
    hb = h.astype(bf16)

    ab = _dot(hb, wab_ref[...])
    u = ab[:, 0:D_CONV] * jax.nn.sigmoid(ab[:, D_CONV:2 * D_CONV])
    uext[HALO:HALO + TM, :] = u
    for r in range(1, 8):
        ush[r - 1] = uext[r:r + TM + CONV_SPAN, :]
    n_q = D_Q // PROJ_CHUNK
    n_g = D_MODEL // PROJ_CHUNK
    n_jobs = wproj_ref.shape[0]
    n_rb = TM // CONV_ROWS
    n_t8 = CONV_ROWS // 8
    units = [(cb, rg) for cb in range(D_CONV // LANES) for rg in range(n_rb)]
    job_at = {(i * len(units)) // n_jobs: i for i in range(n_jobs)}
    assert len(job_at) == n_jobs
    for ui, (cb, rg) in enumerate(units):
        if ui in job_at:
            proj_sc[job_at[ui]] = _dot(hb, wproj_ref[job_at[ui]])
        cs = slice(cb * LANES, (cb + 1) * LANES)
        base = rg * CONV_ROWS
        accs = [None] * n_t8
        for r in range(8):
            taps = [(j, (j + 2) // 8) for j in range(CONV_WIDTH) if (j + 2) % 8 == r]
            t_lo = min(a for _, a in taps)
            t_hi = max(a for _, a in taps) + n_t8
            tiles = {}
            for t8 in range(t_lo, t_hi):
                rows = slice(base + 8 * t8, base + 8 * t8 + 8)
                tiles[t8] = uext[rows, cs] if r == 0 else ush[r - 1, rows, cs]
            for j, a in taps:
                tap = dw_ref[j, :, cs]
                for s8 in range(n_t8):
                    term = tap * tiles[a + s8]
                    accs[s8] = term if accs[s8] is None else accs[s8] + term
        for s8 in range(n_t8):
            conv_sc[base + 8 * s8:base + 8 * s8 + 8, cs] = accs[s8] + dwb_ref[:, cs]
    conv = conv_sc[...]
    uext[0:HALO, :] = uext[TM:TM + HALO, :]
    mu = jnp.mean(conv, axis=-1, keepdims=True)
    dc = conv - mu
    var = jnp.mean(dc * dc, axis=-1, keepdims=True)
    yn = dc * lax.rsqrt(var + EPS) * lng_ref[...] + lnb_ref[...]
    act = yn * jax.nn.sigmoid(yn)
    y_conv = _dot(act.astype(bf16), wco_ref[...])

    q = jnp.concatenate([proj_sc[i] for i in range(n_q)], axis=1)
    k = proj_sc[n_q, :, 0:D_KV]
    v = proj_sc[n_q, :, D_KV:2 * D_KV]

    def head_norm(z, blk_ref, g):
        sq = (z * z).astype(bf16)
        blk = blk_ref.shape[0]
        msq = jnp.concatenate([_dot(sq[:, c:c + blk], blk_ref[...]) for c in range(0, z.shape[1], blk)], axis=1)
        return z * lax.rsqrt(msq + EPS) * g

    qn = head_norm(q, bq_ref, qg_ref[...]).astype(bf16)
    kn = head_norm(k, bk_ref, kg_ref[...])
    lo_half = lax.broadcasted_iota(i32, (TM, LANES), 1) < HEAD_DIM
    kr = pltpu.roll(kn, HEAD_DIM, 1)
    vr = pltpu.roll(v, HEAD_DIM, 1)
    k2[0, WINDOW:WINDOW + TM, :] = jnp.where(lo_half, kn, kr).astype(bf16)
    k2[1, WINDOW:WINDOW + TM, :] = jnp.where(lo_half, kr, kn).astype(bf16)
    v2[0, WINDOW:WINDOW + TM, :] = jnp.where(lo_half, v, vr).astype(bf16)
    v2[1, WINDOW:WINDOW + TM, :] = jnp.where(lo_half, vr, v).astype(bf16)

    QROWS = Q_PER_KV * WINDOW
    row = lax.broadcasted_iota(i32, (QROWS, 2 * WINDOW), 0) & (WINDOW - 1)
    col = lax.broadcasted_iota(i32, (QROWS, 2 * WINDOW), 1)
    dist = row + WINDOW - col
    in_window = (dist >= 0) & (dist < WINDOW)
    first_mask = in_window & (col >= jnp.where(first, WINDOW, 0))
    hrow = lax.broadcasted_iota(i32, (QROWS, 1), 0) // WINDOW
    lo128 = lax.broadcasted_iota(i32, (WINDOW, LANES), 1) < HEAD_DIM
    zero_q = jnp.zeros((WINDOW, LANES), bf16)
    for g in range(N_KV_HEADS):
        sink = jnp.where(hrow == 0, sink_ref[4 * g],
                         jnp.where(hrow == 1, sink_ref[4 * g + 1],
                                   jnp.where(hrow == 2, sink_ref[4 * g + 2], sink_ref[4 * g + 3])))
        bias_g = bias_ref[g]
        for j in range(TM // WINDOW):
            rs = slice(j * WINDOW, (j + 1) * WINDOW)
            qa = qn[rs, 2 * LANES * g:2 * LANES * g + LANES]
            qb = qn[rs, 2 * LANES * g + LANES:2 * LANES * (g + 1)]
            qs = jnp.concatenate([jnp.where(lo128, qa, zero_q), jnp.where(lo128, zero_q, qa),
                                  jnp.where(lo128, qb, zero_q), jnp.where(lo128, zero_q, qb)], axis=0)
            kk = k2[g, j * WINDOW:(j + 2) * WINDOW, :]
            s = lax.dot_general(qs, kk, (((1,), (1,)), ((), ())), preferred_element_type=f32)
            msk = first_mask if j == 0 else in_window
            logits = jnp.where(msk, s + bias_g, NEG_INF)
            m = jnp.maximum(jnp.max(logits, axis=-1, keepdims=True), sink)
            p = jnp.exp(logits - m)
            den = jnp.sum(p, axis=-1, keepdims=True) + jnp.exp(sink - m)
            o2 = _dot(p.astype(bf16), v2[g, j * WINDOW:(j + 2) * WINDOW, :]) * (1.0 / den)
            osc[rs, 2 * LANES * g:2 * LANES * g + LANES] = jnp.where(
                lo128, o2[0:WINDOW], o2[WINDOW:2 * WINDOW]).astype(bf16)
            osc[rs, 2 * LANES * g + LANES:2 * LANES * (g + 1)] = jnp.where(
                lo128, o2[2 * WINDOW:3 * WINDOW], o2[3 * WINDOW:4 * WINDOW]).astype(bf16)
    k2[:, 0:WINDOW, :] = k2[:, TM:TM + WINDOW, :]
    v2[:, 0:WINDOW, :] = v2[:, TM:TM + WINDOW, :]
    y_attn = _dot(osc[...], wao_ref[...])

    merged = []
    for i in range(n_g):
        cs = slice(i * PROJ_CHUNK, (i + 1) * PROJ_CHUNK)
        g_conv = jax.nn.sigmoid(proj_sc[n_q + 1 + i])
        g_attn = jax.nn.sigmoid(proj_sc[n_q + 1 + n_g + i])
        merged.append((g_conv * y_conv[:, cs] + g_attn * y_attn[:, cs]).astype(bf16))
    merged = jnp.concatenate(merged, axis=1)
    x1 = x + g1 * _dot(merged, wout_ref[...])
    x1_ref[...] = x1

    ms2 = jnp.mean(x1 * x1, axis=-1, keepdims=True)
    h2 = (x1 * lax.rsqrt(ms2 + EPS)) * gffn_ref[...]
    h2 = h2 * (1.0 + sc2) + sh2
    h2_ref[...] = _pack_halves(h2)
    hh, hl = _split(h2)
    hw = _dot(hh, wr2_ref[...])
    lg = hw[:, 0:ROUTER_COLS] + hw[:, ROUTER_COLS:2 * ROUTER_COLS] + _dot(hl, wrh_ref[...]) + br_ref[...]
    lt = lg.T
    gl = lt[0:N_GROUPS, :]
    grow = lax.broadcasted_iota(i32, (N_GROUPS, TM), 0)
    gmax = jnp.max(gl, axis=0, keepdims=True)
    gi = jnp.min(jnp.where(gl == gmax, grow, N_GROUPS), axis=0, keepdims=True)
    p_top = 1.0 / jnp.sum(jnp.exp(gl - gmax), axis=0, keepdims=True)
    sel = lt[EXPERT_COL0:EXPERT_COL0 + EXPERTS_PER_GROUP, :]
    for gg in range(1, N_GROUPS):
        lo_r = EXPERT_COL0 + gg * EXPERTS_PER_GROUP
        sel = jnp.where(gi == gg, lt[lo_r:lo_r + EXPERTS_PER_GROUP, :], sel)
    erow = lax.broadcasted_iota(i32, (EXPERTS_PER_GROUP, TM), 0)
    m1 = jnp.max(sel, axis=0, keepdims=True)
    i1 = jnp.min(jnp.where(sel == m1, erow, EXPERTS_PER_GROUP), axis=0, keepdims=True)
    rest = jnp.where(erow == i1, -jnp.inf, sel)
    m2 = jnp.max(rest, axis=0, keepdims=True)
    i2 = jnp.min(jnp.where(rest == m2, erow, EXPERTS_PER_GROUP), axis=0, keepdims=True)
    z = jnp.sum(jnp.exp(sel - m1), axis=0, keepdims=True)
    v1 = 1.0 / z
    v2nd = jnp.exp(m2 - m1) / z
    w1 = v1 / (v1 + v2nd) * p_top
    w2 = v2nd / (v1 + v2nd) * p_top
    e1 = gi * EXPERTS_PER_GROUP + i1
    e2 = gi * EXPERTS_PER_GROUP + i2

    xrow = lax.broadcasted_iota(i32, (N_EXPERTS, TM), 0)
    oh1 = xrow == e1
    oh2 = xrow == e2
    both = jnp.where(oh1 | oh2, 1.0, 0.0)
    prefix = _dot(both.astype(bf16), tri_ref[...]) + cnt_ref[:, 0:1]
    r1 = jnp.sum(jnp.where(oh1, prefix, 0.0), axis=0, keepdims=True)
    r2 = jnp.sum(jnp.where(oh2, prefix, 0.0), axis=0, keepdims=True)
    cnt_ref[...] = cnt_ref[...] + jnp.sum(both, axis=1, keepdims=True)

    route_ref[...] = jnp.concatenate(
        [e1, e2, r1.astype(i32), r2.astype(i32), jnp.zeros((4, TM), i32)], axis=0)
    wpad = jnp.concatenate([w1, w2, jnp.zeros((LANES - 2, TM), f32)], axis=0)
    wtok_ref[...] = wpad.T


def _mixer(x, mod6, p, b0, B):
    _, T, D = x.shape
    TM = min(SEQ_TILE, T)
    nt = T // TM
    N = B * T
    const = lambda shape: pl.BlockSpec(shape, lambda b, t: (0,) * len(shape))
    in_specs = [
        pl.BlockSpec((None, TM, D), lambda b, t: (b0 + b, t, 0)),
        pl.BlockSpec((None, 6, D), lambda b, t: (b0 + b, 0, 0)),
        const((1, D)), const((1, D)),
        const(p["wab"].shape), const(p["wproj"].shape),
        const((CONV_WIDTH, 8, D_CONV)), const((8, D_CONV)), const((1, D_CONV)), const((1, D_CONV)),
        const((D_CONV, D)), const((D_Q, D)), const((D, D)),
        const((1, D_Q)), const((1, D_KV)),
        const((NORM_BLK, NORM_BLK)), const((D_KV, D_KV)),
        const((N_KV_HEADS, Q_PER_KV * WINDOW, 2 * WINDOW)),
        pl.BlockSpec(memory_space=pltpu.SMEM),
        const((D, ROUTER_COLS)), const((D, 2 * ROUTER_COLS)), const((1, ROUTER_COLS)),
        const((TM, TM)),
    ]
    out_specs = [
        pl.BlockSpec((None, TM, D), lambda b, t: (b, t, 0)),
        pl.BlockSpec((None, TM, D // 2), lambda b, t: (b, t, 0)),
        pl.BlockSpec((8, TM), lambda b, t: (0, b * nt + t)),
        pl.BlockSpec((TM, LANES), lambda b, t: (b * nt + t, 0)),
        pl.BlockSpec((N_EXPERTS, LANES), lambda b, t: (0, 0)),
    ]
    out_shape = [
        jax.ShapeDtypeStruct((B, T, D), f32),
        jax.ShapeDtypeStruct((B, T, D // 2), i32),
        jax.ShapeDtypeStruct((8, N), i32),
        jax.ShapeDtypeStruct((N, LANES), f32),
        jax.ShapeDtypeStruct((N_EXPERTS, LANES), f32),
    ]
    scratch = [
        pltpu.VMEM((TM + 32, D_CONV), f32),
        pltpu.VMEM((7, TM + CONV_SPAN, D_CONV), f32),
        pltpu.VMEM((TM, D_CONV), f32),
        pltpu.VMEM((p["wproj"].shape[0], TM, PROJ_CHUNK), f32),
        pltpu.VMEM((N_KV_HEADS, TM + WINDOW, LANES), bf16),
        pltpu.VMEM((N_KV_HEADS, TM + WINDOW, LANES), bf16),
        pltpu.VMEM((TM, D_Q), bf16),
    ]
    tri = jnp.asarray(np.triu(np.ones((TM, TM), np.float32), 1), bf16)
    return pl.pallas_call(
        _mixer_kernel,
        grid=(B, nt),
        in_specs=in_specs, out_specs=out_specs, out_shape=out_shape, scratch_shapes=scratch,
        compiler_params=pltpu.CompilerParams(
            dimension_semantics=("arbitrary", "arbitrary"), vmem_limit_bytes=VMEM_LIMIT),
        name="mixer_router",
    )(x, mod6, p["gmix"], p["gffn"], p["wab"], p["wproj"], p["dw"], p["dwb"], p["lng"], p["lnb"],
      p["wco"], p["wao"], p["wout"], p["qg"], p["kg"], p["bq"], p["bk"], p["bias"], p["sinks"],
      p["wrh"], p["wr2"], p["br"], tri)


def _dest_kernel(pstart_ref, route_ref, dest_ref):
    e = route_ref[0:2, :]
    base = jnp.zeros(e.shape, i32)
    for x in range(N_EXPERTS):
        base = jnp.where(e == x, pstart_ref[x], base)
    dest_ref[...] = base + route_ref[2:4, :]


def _dest_slots(pstart, route):
    N = route.shape[1]
    TS = min(DEST_TILE, N)
    return pl.pallas_call(
        _dest_kernel,
        grid=(N // TS,),
        in_specs=[pl.BlockSpec(memory_space=pltpu.SMEM),
                  pl.BlockSpec((8, TS), lambda i: (0, i))],
        out_specs=pl.BlockSpec((2, TS), lambda i: (0, i)),
        out_shape=jax.ShapeDtypeStruct((2, N), i32),
        name="dest_slots",
    )(pstart, route)


def _sc_mesh():
    return plsc.VectorSubcoreMesh(core_axis_name="c", subcore_axis_name="s")


def _sc_scatter(h2, dest, n_slots):
    N, D = h2.shape
    per_w = N // SC_WORKERS
    C = SC_CHUNK
    n_chunks = per_w // C
    dest4 = dest.reshape(TOP_K, SC_WORKERS, n_chunks, C)

    @functools.partial(
        pl.kernel, mesh=_sc_mesh(),
        out_type=jax.ShapeDtypeStruct((n_slots, D), h2.dtype),
        scratch_types=[pltpu.VMEM((n_chunks, C), i32), pltpu.VMEM((n_chunks, C), i32),
                       pltpu.VMEM((2, C, D), h2.dtype),
                       pltpu.SemaphoreType.DMA((2,)), pltpu.SemaphoreType.DMA((2,))],
        name="sc_dispatch_scatter",
    )
    def run(h2_hbm, dest_hbm, xs_hbm, idx0, idx1, buf, sem_in, sem_out):
        wid = lax.axis_index("s") * SC_CORES + lax.axis_index("c")
        base = wid * per_w
        pltpu.sync_copy(dest_hbm.at[0, wid], idx0)
        pltpu.sync_copy(dest_hbm.at[1, wid], idx1)

        def load(j, b):
            return pltpu.make_async_copy(h2_hbm.at[pl.ds(base + j * C, C)], buf.at[b], sem_in.at[b])

        def put(j, b, idx):
            return pltpu.make_async_copy(buf.at[b], xs_hbm.at[idx.at[j]], sem_out.at[b])

        load(0, 0).start()

        @pl.loop(0, n_chunks, step=2)
        def _(j0):
            for b in range(2):
                j = j0 + b
                load(j, b).wait()

                @pl.when(j >= 1)
                def _():
                    put(j - 1, 1 - b, idx0).wait()
                    put(j - 1, 1 - b, idx1).wait()

                @pl.when(j + 1 < n_chunks)
                def _():
                    load(j + 1, 1 - b).start()

                put(j, b, idx0).start()
                put(j, b, idx1).start()

        put(n_chunks - 1, 1, idx0).wait()
        put(n_chunks - 1, 1, idx1).wait()

    return run(h2, dest4)


def _sc_gather(y, dest, N):
    D = y.shape[1]
    per_w = N // SC_WORKERS
    C = SC_CHUNK
    n_chunks = per_w // C
    dest4 = dest.reshape(TOP_K, SC_WORKERS, n_chunks, C)

    @functools.partial(
        pl.kernel, mesh=_sc_mesh(),
        out_type=jax.ShapeDtypeStruct((TOP_K, N, D), y.dtype),
        scratch_types=[pltpu.VMEM((n_chunks, C), i32), pltpu.VMEM((n_chunks, C), i32),
                       pltpu.VMEM((2, C, D), y.dtype),
                       pltpu.SemaphoreType.DMA((2,)), pltpu.SemaphoreType.DMA((2,))],
        name="sc_combine_gather",
    )
    def run(y_hbm, dest_hbm, yg_hbm, idx0, idx1, buf, sem_in, sem_out):
        wid = lax.axis_index("s") * SC_CORES + lax.axis_index("c")
        base = wid * per_w
        pltpu.sync_copy(dest_hbm.at[0, wid], idx0)
        pltpu.sync_copy(dest_hbm.at[1, wid], idx1)
        idx = (idx0, idx1)

        def get(j, k):
            return pltpu.make_async_copy(y_hbm.at[idx[k].at[j]], buf.at[k], sem_in.at[k])

        def put(j, k):
            return pltpu.make_async_copy(buf.at[k], yg_hbm.at[k, pl.ds(base + j * C, C)], sem_out.at[k])

        get(0, 0).start()
        get(0, 1).start()

        @pl.loop(0, n_chunks)
        def _(j):
            for k in range(TOP_K):
                get(j, k).wait()
                put(j, k).start()
            for k in range(TOP_K):
                put(j, k).wait()

                @pl.when(j + 1 < n_chunks)
                def _():
                    get(j + 1, k).start()

    return run(y, dest4)


def _expert_kernel(be_ref, nv_ref, nu_ref, xs_ref, wg_ref, wu_ref, wdn_ref, y_ref, wgu_ref, wd_ref):
    i = pl.program_id(0)
    in_use = i < nu_ref[0]

    @pl.when(in_use & ((i == 0) | (be_ref[i] != be_ref[jnp.maximum(i - 1, 0)])))
    def _():
        wgu_ref[:, 0:D_EXPERT] = wg_ref[...].astype(bf16)
        wgu_ref[:, D_EXPERT:2 * D_EXPERT] = wu_ref[...].astype(bf16)
        wd_ref[...] = wdn_ref[...].astype(bf16)

    @pl.when(in_use)
    def _():
        live = lax.broadcasted_iota(i32, (MOE_BLOCK, 1), 0) < nv_ref[i]
        x_hi, x_lo = _unpack_halves(jnp.where(live, xs_ref[...], 0))
        half = x_hi.shape[1]
        gu = (_dot(x_hi.astype(bf16), wgu_ref[0:half, :]) +
              _dot(x_lo.astype(bf16), wgu_ref[half:2 * half, :]))
        gate = gu[:, 0:D_EXPERT]
        hid = (gate * jax.nn.sigmoid(gate)) * gu[:, D_EXPERT:2 * D_EXPERT]
        y_ref[...] = _pack_halves(_dot(hid.astype(bf16), wd_ref[...]))


def _experts(block_e, n_valid, n_used, xs, w_gate, w_up, w_down):
    n_slots, DH = xs.shape
    D = 2 * DH
    n_blocks = n_slots // MOE_BLOCK
    last = lambda i, nu: jnp.minimum(i, nu[0] - 1)
    expert = lambda i, be, nv, nu: (be[last(i, nu)], 0, 0)
    grid_spec = pltpu.PrefetchScalarGridSpec(
        num_scalar_prefetch=3,
        grid=(n_blocks,),
        in_specs=[pl.BlockSpec((MOE_BLOCK, DH), lambda i, be, nv, nu: (last(i, nu), 0)),
                  pl.BlockSpec((None, D, D_EXPERT), expert),
                  pl.BlockSpec((None, D, D_EXPERT), expert),
                  pl.BlockSpec((None, D_EXPERT, D), expert)],
        out_specs=pl.BlockSpec((MOE_BLOCK, DH), lambda i, be, nv, nu: (last(i, nu), 0)),
        scratch_shapes=[pltpu.VMEM((D, 2 * D_EXPERT), bf16), pltpu.VMEM((D_EXPERT, D), bf16)],
    )
    return pl.pallas_call(
        _expert_kernel,
        grid_spec=grid_spec,
        out_shape=jax.ShapeDtypeStruct((n_slots, DH), i32),
        compiler_params=pltpu.CompilerParams(dimension_semantics=("arbitrary",),
                                             vmem_limit_bytes=VMEM_LIMIT),
        name="experts",
    )(block_e, n_valid, n_used, xs, w_gate, w_up, w_down)


def _combine_kernel(yg_ref, x1_ref, wtok_ref, mod_ref, *rest):
    o_ref = rest[-1]
    w = wtok_ref[...]
    g2 = mod_ref[5:6, :]
    a_hi, a_lo = _unpack_halves(yg_ref[0])
    b_hi, b_lo = _unpack_halves(yg_ref[1])
    moe = jnp.concatenate([w[:, 0:1] * a_hi + w[:, 1:2] * b_hi,
                           w[:, 0:1] * a_lo + w[:, 1:2] * b_lo], axis=1)
    o_ref[...] = x1_ref[...] + g2 * moe


def _combine(yg, x1, wtok, mod6, T, b0, n_total, out_prev):
    Nc, D = x1.shape
    TS = min(COMBINE_TILE, T)
    per_seq = T // TS
    blk0 = b0 * per_seq
    in_specs = [pl.BlockSpec((TOP_K, TS, D // 2), lambda i: (0, i, 0)),
                pl.BlockSpec((TS, D), lambda i: (i, 0)),
                pl.BlockSpec((TS, LANES), lambda i: (i, 0)),
                pl.BlockSpec((None, 6, D), lambda i: (b0 + i // per_seq, 0, 0))]
    args = [yg, x1, wtok, mod6]
    aliases = {}
    if out_prev is not None:
        in_specs.append(pl.BlockSpec(memory_space=pl.ANY))
        args.append(out_prev)
        aliases = {len(args) - 1: 0}
    return pl.pallas_call(
        _combine_kernel,
        grid=(Nc // TS,),
        in_specs=in_specs,
        out_specs=pl.BlockSpec((TS, D), lambda i: (blk0 + i, 0)),
        out_shape=jax.ShapeDtypeStruct((n_total, D), f32),
        input_output_aliases=aliases,
        compiler_params=pltpu.CompilerParams(dimension_semantics=("arbitrary",)),
        name="combine",
    )(*args)


def _block_diag_mean(n, blk):
    m = np.kron(np.eye(n // blk, dtype=np.float32), np.full((blk, blk), 1.0 / blk, np.float32))
    return jnp.asarray(m, bf16)


def _layer(x, mod6, bias, l, w):
    B, T, D = x.shape
    N = B * T
    w_rg, w_re = w["w_router_group"][l], w["w_router_expert"][l]
    wr = jnp.zeros((D, ROUTER_COLS), f32)
    wr = wr.at[:, 0:N_GROUPS].set(w_rg).at[:, EXPERT_COL0:EXPERT_COL0 + N_EXPERTS].set(w_re)
    br = jnp.zeros((1, ROUTER_COLS), f32)
    br = br.at[0, 0:N_GROUPS].set(w["b_router_group"][l])
    br = br.at[0, EXPERT_COL0:EXPERT_COL0 + N_EXPERTS].set(w["b_router_expert"][l])
    wrh = wr.astype(bf16)
    wrl = (wr - wrh.astype(f32)).astype(bf16)
    p = dict(
        gmix=w["norm_mix_g"][l].reshape(1, D), gffn=w["norm_ffn_g"][l].reshape(1, D),
        wab=w["w_in"][l][:, 0:2 * D_CONV].astype(bf16),
        wproj=w["w_in"][l][:, 2 * D_CONV:].astype(bf16).reshape(D, -1, PROJ_CHUNK).transpose(1, 0, 2),
        dw=jnp.broadcast_to(w["dw_kernel"][l][:, None, :], (CONV_WIDTH, 8, D_CONV)), dwb=jnp.broadcast_to(w["dw_bias"][l][None, :], (8, D_CONV)),
        lng=w["conv_ln_g"][l].reshape(1, D_CONV), lnb=w["conv_ln_b"][l].reshape(1, D_CONV),
        wco=w["w_conv_out"][l].astype(bf16), wao=w["w_attn_out"][l].astype(bf16),
        wout=w["w_out"][l].astype(bf16),
        qg=(jnp.tile(w["q_norm_g"][l], N_Q_HEADS) * (HEAD_DIM ** -0.5)).reshape(1, D_Q),
        kg=jnp.tile(w["k_norm_g"][l], N_KV_HEADS).reshape(1, D_KV),
        bq=_block_diag_mean(NORM_BLK, HEAD_DIM), bk=_block_diag_mean(D_KV, HEAD_DIM),
        bias=bias, sinks=w["sinks"][l], wrh=wrh, wr2=jnp.concatenate([wrh, wrl], axis=1), br=br,
    )
    w_gate, w_up, w_down = w["w_exp_gate"][l], w["w_exp_up"][l], w["w_exp_down"][l]

    n_chunks = MOE_CHUNKS if B % MOE_CHUNKS == 0 else 1
    Bc = B // n_chunks
    Nc = Bc * T
    n_blocks = -(-(Nc * TOP_K) // MOE_BLOCK) + N_EXPERTS
    blk0 = jnp.arange(n_blocks, dtype=i32) * MOE_BLOCK
    stage = []
    for ch in range(n_chunks):
        x1, h2, route, wtok, cnt = _mixer(x, mod6, p, ch * Bc, Bc)
        counts = cnt[:, 0].astype(i32)
        pcounts = (counts + MOE_BLOCK - 1) // MOE_BLOCK * MOE_BLOCK
        pend = jnp.cumsum(pcounts)
        pstart = pend - pcounts
        block_e = jnp.minimum(jnp.sum((pend[None, :] <= blk0[:, None]).astype(i32), axis=1), N_EXPERTS - 1)
        n_valid = jnp.clip((pstart + counts)[block_e] - blk0, 0, MOE_BLOCK).astype(i32)
        n_used = (pend[-1:] // MOE_BLOCK).astype(i32)
        dest = _dest_slots(pstart, route)
        xs = _sc_scatter(h2.reshape(Nc, D // 2), dest, n_blocks * MOE_BLOCK)
        stage.append((x1, wtok, dest, xs, block_e, n_valid, n_used))
    ys = [_experts(be, nv, nu, xs, w_gate, w_up, w_down) for (_, _, _, xs, be, nv, nu) in stage]
    ygs = [_sc_gather(y, st[2], Nc) for y, st in zip(ys, stage)]
    out = None
    for ch in range(n_chunks):
        x1, wtok = stage[ch][0], stage[ch][1]
        out = _combine(ygs[ch], x1.reshape(Nc, D), wtok, mod6, T, ch * Bc, B * T, out)
    return out.reshape(B, T, D)


def kernel(x, c, w_ada, b_ada, norm_mix_g, w_in, dw_kernel, dw_bias, conv_ln_g, conv_ln_b,
           w_conv_out, q_norm_g, k_norm_g, sinks, w_attn_out, w_out, rel_bias_table, norm_ffn_g,
           w_router_group, b_router_group, w_router_expert, b_router_expert,
           w_exp_gate, w_exp_up, w_exp_down):
    w = dict(norm_mix_g=norm_mix_g, w_in=w_in, dw_kernel=dw_kernel, dw_bias=dw_bias,
             conv_ln_g=conv_ln_g, conv_ln_b=conv_ln_b, w_conv_out=w_conv_out, q_norm_g=q_norm_g,
             k_norm_g=k_norm_g, sinks=sinks, w_attn_out=w_attn_out, w_out=w_out,
             norm_ffn_g=norm_ffn_g, w_router_group=w_router_group, b_router_group=b_router_group,
             w_router_expert=w_router_expert, b_router_expert=b_router_expert,
             w_exp_gate=w_exp_gate, w_exp_up=w_exp_up, w_exp_down=w_exp_down)
    B = x.shape[0]
    bias = _bias_band(rel_bias_table).reshape(N_KV_HEADS, Q_PER_KV * WINDOW, 2 * WINDOW)
    for l in range(w_ada.shape[0]):
        mod6 = _modulation(c, w_ada[l], b_ada[l]).reshape(B, 6, D_MODEL)
        x = _layer(x, mod6, bias, l, w)
    return x
```

```python
import functools
import math

import jax
import jax.numpy as jnp
import numpy as np
from jax import lax
from jax.experimental import pallas as pl
from jax.experimental.pallas import tpu as pltpu
from jax.experimental.pallas import tpu_sc as plsc

D_MODEL = 1024
D_CONV = 512
CONV_WIDTH = 31
N_Q_HEADS = 8
N_KV_HEADS = 2
HEAD_DIM = 64
Q_PER_KV = N_Q_HEADS // N_KV_HEADS
WINDOW = 128
N_BUCKETS = 32
MAX_DISTANCE = 128
N_GROUPS = 4
EXPERTS_PER_GROUP = 8
N_EXPERTS = N_GROUPS * EXPERTS_PER_GROUP
TOP_K = 2
D_EXPERT = 256
D_Q = N_Q_HEADS * HEAD_DIM
D_KV = N_KV_HEADS * HEAD_DIM
EPS = 1e-6
NEG_INF = -1e30

LANES = 128
SEQ_TILE = 512
CONV_ROWS = 64
NORM_BLK = 256
PROJ_CHUNK = 256
CONV_SPAN = 24
MOE_BLOCK = 512
MOE_CHUNKS = 2
DEST_TILE = 8192
COMBINE_TILE = 512
SC_CORES = 2
SC_SUBCORES = 16
SC_WORKERS = SC_CORES * SC_SUBCORES
SC_CHUNK = 32
ROUTER_COLS = LANES
EXPERT_COL0 = 8
VMEM_LIMIT = 56 * 1024 * 1024

f32 = jnp.float32
bf16 = jnp.bfloat16
i32 = jnp.int32


def _dot(a, b):
    return jnp.dot(a, b, preferred_element_type=f32)


def _split(a):
    hi = a.astype(bf16)
    lo = (a - hi.astype(f32)).astype(bf16)
    return hi, lo


def _pack_halves(x):
    c = x.shape[1] // 2
    hi = lax.bitcast_convert_type(x[:, 0:c].astype(bf16).astype(f32), jnp.uint32)
    lo = lax.bitcast_convert_type(x[:, c:2 * c].astype(bf16).astype(f32), jnp.uint32)
    word = (hi & jnp.uint32(0xFFFF0000)) | (lo >> jnp.uint32(16))
    return lax.bitcast_convert_type(word, i32)


def _unpack_halves(word):
    u = lax.bitcast_convert_type(word, jnp.uint32)
    hi = lax.bitcast_convert_type(u & jnp.uint32(0xFFFF0000), f32)
    lo = lax.bitcast_convert_type(u << jnp.uint32(16), f32)
    return hi, lo


def _dot3(a, b):
    ah, al = _split(a)
    bh, bl = _split(b)
    return _dot(ah, bh) + _dot(al, bh) + _dot(ah, bl)


def _mod_kernel(c_ref, w_ref, b_ref, o_ref):
    c = c_ref[...]
    s = c * jax.nn.sigmoid(c)
    o_ref[...] = _dot3(s, w_ref[...]) + b_ref[...]


def _modulation(c, w_ada, b_ada):
    B, D = c.shape
    n_out = w_ada.shape[1]
    return pl.pallas_call(
        _mod_kernel,
        grid=(n_out // D,),
        in_specs=[pl.BlockSpec((B, D), lambda j: (0, 0)),
                  pl.BlockSpec((D, D), lambda j: (0, j)),
                  pl.BlockSpec((1, D), lambda j: (0, j))],
        out_specs=pl.BlockSpec((B, D), lambda j: (0, j)),
        out_shape=jax.ShapeDtypeStruct((B, n_out), f32),
        name="modulation",
    )(c, w_ada, b_ada.reshape(1, n_out))


def _band_buckets():
    qi = np.arange(WINDOW)[:, None]
    kj = np.arange(2 * WINDOW)[None, :]
    dist = np.clip(qi + WINDOW - kj, 0, MAX_DISTANCE)
    max_exact = N_BUCKETS // 2
    d = np.maximum(dist, 1).astype(np.float32)
    large = max_exact + (np.log(d / np.float32(max_exact)) / np.float32(math.log(MAX_DISTANCE / max_exact))
                         * np.float32(N_BUCKETS - max_exact)).astype(np.int32)
    large = np.minimum(large, N_BUCKETS - 1)
    bucket = np.where(dist < max_exact, dist, large).astype(np.int32)
    raw = qi + WINDOW - kj
    return np.where((raw >= 0) & (raw < WINDOW), bucket, -1).astype(np.int32)


def _bias_kernel(tab_ref, bucket_ref, o_ref):
    bk = bucket_ref[...]
    for h in range(N_Q_HEADS):
        acc = jnp.full(bk.shape, NEG_INF, f32)
        for b in range(N_BUCKETS):
            acc = jnp.where(bk == b, tab_ref[b, h], acc)
        o_ref[h] = acc


def _bias_band(rel_bias_table):
    return pl.pallas_call(
        _bias_kernel,
        in_specs=[pl.BlockSpec(memory_space=pltpu.SMEM),
                  pl.BlockSpec(memory_space=pltpu.VMEM)],
        out_specs=pl.BlockSpec(memory_space=pltpu.VMEM),
        out_shape=jax.ShapeDtypeStruct((N_Q_HEADS, WINDOW, 2 * WINDOW), f32),
        name="bias_band",
    )(rel_bias_table, jnp.asarray(_band_buckets()))


def _mixer_kernel(x_ref, mod_ref, gmix_ref, gffn_ref, win_ref, dw_ref, dwb_ref, lng_ref, lnb_ref,
                  wco_ref, wao_ref, wout_ref, qg_ref, kg_ref, bq_ref, bk_ref, bias_ref, sink_ref,
                  wrh_ref, wr2_ref, br_ref, tri_ref,
                  x1_ref, h2_ref, route_ref, wtok_ref, cnt_ref,
                  uext, ush, conv_sc, proj_sc, k2, v2, osc):
    TM = x_ref.shape[0]
    HALO = 32
    b = pl.program_id(0)
    t = pl.program_id(1)
    first = t == 0

    @pl.when(first)
    def _():
        uext[0:HALO, :] = jnp.zeros((HALO, D_CONV), f32)
        k2[:, 0:WINDOW, :] = jnp.zeros((N_KV_HEADS, WINDOW, LANES), bf16)
        v2[:, 0:WINDOW, :] = jnp.zeros((N_KV_HEADS, WINDOW, LANES), bf16)

    @pl.when(first & (b == 0))
    def _():
        cnt_ref[...] = jnp.zeros(cnt_ref.shape, f32)

    x = x_ref[...]
    mod = mod_ref[...]
    sh1, sc1, g1 = mod[0:1, :], mod[1:2, :], mod[2:3, :]
    sh2, sc2, g2 = mod[3:4, :], mod[4:5, :], mod[5:6, :]
    del g2

    ms = jnp.mean(x * x, axis=-1, keepdims=True)
    h = (x * lax.rsqrt(ms + EPS)) * (gmix_ref[...] * (1.0 + sc1)) + sh1
    hb = h.astype(bf16)

    ab = _dot(hb, win_ref[:, 0:2 * D_CONV])
    u = ab[:, 0:D_CONV] * jax.nn.sigmoid(ab[:, D_CONV:2 * D_CONV])
    uext[HALO:HALO + TM, :] = u
    for r in range(1, 8):
        ush[r - 1] = uext[r:r + TM + CONV_SPAN, :]
    n_q = D_Q // PROJ_CHUNK
    n_g = D_MODEL // PROJ_CHUNK
    n_jobs = (win_ref.shape[1] - 2 * D_CONV) // PROJ_CHUNK
    n_rb = TM // CONV_ROWS
    n_t8 = CONV_ROWS // 8
    units = [(cb, rg) for cb in range(D_CONV // LANES) for rg in range(n_rb)]
    job_at = {(i * len(units)) // n_jobs: i for i in range(n_jobs)}
    assert len(job_at) == n_jobs
    for ui, (cb, rg) in enumerate(units):
        if ui in job_at:
            c_lo = 2 * D_CONV + job_at[ui] * PROJ_CHUNK
            proj_sc[job_at[ui]] = _dot(hb, win_ref[:, c_lo:c_lo + PROJ_CHUNK])
        cs = slice(cb * LANES, (cb + 1) * LANES)
        base = rg * CONV_ROWS
        accs = [None] * n_t8
        for r in range(8):
            taps = [(j, (j + 2) // 8) for j in range(CONV_WIDTH) if (j + 2) % 8 == r]
            t_lo = min(a for _, a in taps)
            t_hi = max(a for _, a in taps) + n_t8
            tiles = {}
            for t8 in range(t_lo, t_hi):
                rows = slice(base + 8 * t8, base + 8 * t8 + 8)
                tiles[t8] = uext[rows, cs] if r == 0 else ush[r - 1, rows, cs]
            for j, a in taps:
                tap = dw_ref[j, :, cs]
                for s8 in range(n_t8):
                    term = tap * tiles[a + s8]
                    accs[s8] = term if accs[s8] is None else accs[s8] + term
        for s8 in range(n_t8):
            conv_sc[base + 8 * s8:base + 8 * s8 + 8, cs] = accs[s8] + dwb_ref[:, cs]
    conv = conv_sc[...]
    uext[0:HALO, :] = uext[TM:TM + HALO, :]
    mu = jnp.mean(conv, axis=-1, keepdims=True)
    dc = conv - mu
    var = jnp.mean(dc * dc, axis=-1, keepdims=True)
    yn = dc * lax.rsqrt(var + EPS) * lng_ref[...] + lnb_ref[...]
    act = yn * jax.nn.sigmoid(yn)
    y_conv = _dot(act.astype(bf16), wco_ref[...])

    q = jnp.concatenate([proj_sc[i] for i in range(n_q)], axis=1)
    k = proj_sc[n_q, :, 0:D_KV]
    v = proj_sc[n_q, :, D_KV:2 * D_KV]

    def head_norm(z, blk_ref, g):
        sq = (z * z).astype(bf16)
        blk = blk_ref.shape[0]
        msq = jnp.concatenate([_dot(sq[:, c:c + blk], blk_ref[...]) for c in range(0, z.shape[1], blk)], axis=1)
        return z * lax.rsqrt(msq + EPS) * g

    qn = head_norm(q, bq_ref, qg_ref[...]).astype(bf16)
    kn = head_norm(k, bk_ref, kg_ref[...])
    lo_half = lax.broadcasted_iota(i32, (TM, LANES), 1) < HEAD_DIM
    kr = pltpu.roll(kn, HEAD_DIM, 1)
    vr = pltpu.roll(v, HEAD_DIM, 1)
    k2[0, WINDOW:WINDOW + TM, :] = jnp.where(lo_half, kn, kr).astype(bf16)
    k2[1, WINDOW:WINDOW + TM, :] = jnp.where(lo_half, kr, kn).astype(bf16)
    v2[0, WINDOW:WINDOW + TM, :] = jnp.where(lo_half, v, vr).astype(bf16)
    v2[1, WINDOW:WINDOW + TM, :] = jnp.where(lo_half, vr, v).astype(bf16)

    QROWS = Q_PER_KV * WINDOW
    col = lax.broadcasted_iota(i32, (QROWS, 2 * WINDOW), 1)
    has_prev = col >= jnp.where(first, WINDOW, 0)
    hrow = lax.broadcasted_iota(i32, (QROWS, 1), 0) // WINDOW
    lo128 = lax.broadcasted_iota(i32, (WINDOW, LANES), 1) < HEAD_DIM
    zero_q = jnp.zeros((WINDOW, LANES), bf16)
    for g in range(N_KV_HEADS):
        sink = jnp.where(hrow == 0, sink_ref[4 * g],
                         jnp.where(hrow == 1, sink_ref[4 * g + 1],
                                   jnp.where(hrow == 2, sink_ref[4 * g + 2], sink_ref[4 * g + 3])))
        bias_g = bias_ref[g]
        for j in range(TM // WINDOW):
            rs = slice(j * WINDOW, (j + 1) * WINDOW)
            qa = qn[rs, 2 * LANES * g:2 * LANES * g + LANES]
            qb = qn[rs, 2 * LANES * g + LANES:2 * LANES * (g + 1)]
            qs = jnp.concatenate([jnp.where(lo128, qa, zero_q), jnp.where(lo128, zero_q, qa),
                                  jnp.where(lo128, qb, zero_q), jnp.where(lo128, zero_q, qb)], axis=0)
            kk = k2[g, j * WINDOW:(j + 2) * WINDOW, :]
            s = lax.dot_general(qs, kk, (((1,), (1,)), ((), ())), preferred_element_type=f32)
            logits = jnp.where(has_prev, s + bias_g, NEG_INF) if j == 0 else s + bias_g
            m = jnp.maximum(jnp.max(logits, axis=-1, keepdims=True), sink)
            p = jnp.exp(logits - m)
            den = jnp.sum(p, axis=-1, keepdims=True) + jnp.exp(sink - m)
            o2 = _dot(p.astype(bf16), v2[g, j * WINDOW:(j + 2) * WINDOW, :]) * (1.0 / den)
            osc[rs, 2 * LANES * g:2 * LANES * g + LANES] = jnp.where(
                lo128, o2[0:WINDOW], o2[WINDOW:2 * WINDOW]).astype(bf16)
            osc[rs, 2 * LANES * g + LANES:2 * LANES * (g + 1)] = jnp.where(
                lo128, o2[2 * WINDOW:3 * WINDOW], o2[3 * WINDOW:4 * WINDOW]).astype(bf16)
    k2[:, 0:WINDOW, :] = k2[:, TM:TM + WINDOW, :]
    v2[:, 0:WINDOW, :] = v2[:, TM:TM + WINDOW, :]
    y_attn = _dot(osc[...], wao_ref[...])

    merged = []
    for i in range(n_g):
        cs = slice(i * PROJ_CHUNK, (i + 1) * PROJ_CHUNK)
        g_conv = jax.nn.sigmoid(proj_sc[n_q + 1 + i])
        g_attn = jax.nn.sigmoid(proj_sc[n_q + 1 + n_g + i])
        merged.append((g_conv * y_conv[:, cs] + g_attn * y_attn[:, cs]).astype(bf16))
    merged = jnp.concatenate(merged, axis=1)
    x1 = x + g1 * _dot(merged, wout_ref[...])
    x1_ref[...] = x1

    ms2 = jnp.mean(x1 * x1, axis=-1, keepdims=True)
    h2 = (x1 * lax.rsqrt(ms2 + EPS)) * (gffn_ref[...] * (1.0 + sc2)) + sh2
    h2_ref[...] = _pack_halves(h2)
    hh, hl = _split(h2)
    hw = _dot(hh, wr2_ref[...])
    lg = hw[:, 0:ROUTER_COLS] + hw[:, ROUTER_COLS:2 * ROUTER_COLS] + _dot(hl, wrh_ref[...]) + br_ref[...]
    lt = lg.T
    gl = lt[0:N_GROUPS, :]
    grow = lax.broadcasted_iota(i32, (N_GROUPS, TM), 0)
    gmax = jnp.max(gl, axis=0, keepdims=True)
    gi = jnp.min(jnp.where(gl == gmax, grow, N_GROUPS), axis=0, keepdims=True)
    p_top = 1.0 / jnp.sum(jnp.exp(gl - gmax), axis=0, keepdims=True)
    sel = lt[EXPERT_COL0:EXPERT_COL0 + EXPERTS_PER_GROUP, :]
    for gg in range(1, N_GROUPS):
        lo_r = EXPERT_COL0 + gg * EXPERTS_PER_GROUP
        sel = jnp.where(gi == gg, lt[lo_r:lo_r + EXPERTS_PER_GROUP, :], sel)
    erow = lax.broadcasted_iota(i32, (EXPERTS_PER_GROUP, TM), 0)
    m1 = jnp.max(sel, axis=0, keepdims=True)
    i1 = jnp.min(jnp.where(sel == m1, erow, EXPERTS_PER_GROUP), axis=0, keepdims=True)
    rest = jnp.where(erow == i1, -jnp.inf, sel)
    m2 = jnp.max(rest, axis=0, keepdims=True)
    i2 = jnp.min(jnp.where(rest == m2, erow, EXPERTS_PER_GROUP), axis=0, keepdims=True)
    z = jnp.sum(jnp.exp(sel - m1), axis=0, keepdims=True)
    v1 = 1.0 / z
    v2nd = jnp.exp(m2 - m1) / z
    w1 = v1 / (v1 + v2nd) * p_top
    w2 = v2nd / (v1 + v2nd) * p_top
    e1 = gi * EXPERTS_PER_GROUP + i1
    e2 = gi * EXPERTS_PER_GROUP + i2

    xrow = lax.broadcasted_iota(i32, (N_EXPERTS, TM), 0)
    oh1 = xrow == e1
    oh2 = xrow == e2
    both = jnp.where(oh1 | oh2, 1.0, 0.0)
    prefix = _dot(both.astype(bf16), tri_ref[...]) + cnt_ref[:, 0:1]
    r1 = jnp.sum(jnp.where(oh1, prefix, 0.0), axis=0, keepdims=True)
    r2 = jnp.sum(jnp.where(oh2, prefix, 0.0), axis=0, keepdims=True)
    cnt_ref[...] = cnt_ref[...] + jnp.sum(both, axis=1, keepdims=True)

    route_ref[...] = jnp.concatenate(
        [e1, e2, r1.astype(i32), r2.astype(i32), jnp.zeros((4, TM), i32)], axis=0)
    wpad = jnp.concatenate([w1, w2, jnp.zeros((LANES - 2, TM), f32)], axis=0)
    wtok_ref[...] = wpad.T


def _mixer(x, mod6, p, b0, B):
    _, T, D = x.shape
    TM = min(SEQ_TILE, T)
    nt = T // TM
    N = B * T
    const = lambda shape: pl.BlockSpec(shape, lambda b, t: (0,) * len(shape))
    in_specs = [
        pl.BlockSpec((None, TM, D), lambda b, t: (b0 + b, t, 0)),
        pl.BlockSpec((None, 6, D), lambda b, t: (b0 + b, 0, 0)),
        const((1, D)), const((1, D)),
        const(p["w_in"].shape),
        const((CONV_WIDTH, 8, D_CONV)), const((8, D_CONV)), const((1, D_CONV)), const((1, D_CONV)),
        const((D_CONV, D)), const((D_Q, D)), const((D, D)),
        const((1, D_Q)), const((1, D_KV)),
        const((NORM_BLK, NORM_BLK)), const((D_KV, D_KV)),
        const((N_KV_HEADS, Q_PER_KV * WINDOW, 2 * WINDOW)),
        pl.BlockSpec(memory_space=pltpu.SMEM),
        const((D, ROUTER_COLS)), const((D, 2 * ROUTER_COLS)), const((1, ROUTER_COLS)),
        const((TM, TM)),
    ]
    out_specs = [
        pl.BlockSpec((None, TM, D), lambda b, t: (b, t, 0)),
        pl.BlockSpec((None, TM, D // 2), lambda b, t: (b, t, 0)),
        pl.BlockSpec((8, TM), lambda b, t: (0, b * nt + t)),
        pl.BlockSpec((TM, LANES), lambda b, t: (b * nt + t, 0)),
        pl.BlockSpec((N_EXPERTS, LANES), lambda b, t: (0, 0)),
    ]
    out_shape = [
        jax.ShapeDtypeStruct((B, T, D), f32),
        jax.ShapeDtypeStruct((B, T, D // 2), i32),
        jax.ShapeDtypeStruct((8, N), i32),
        jax.ShapeDtypeStruct((N, LANES), f32),
        jax.ShapeDtypeStruct((N_EXPERTS, LANES), f32),
    ]
    scratch = [
        pltpu.VMEM((TM + 32, D_CONV), f32),
        pltpu.VMEM((7, TM + CONV_SPAN, D_CONV), f32),
        pltpu.VMEM((TM, D_CONV), f32),
        pltpu.VMEM(((p["w_in"].shape[1] - 2 * D_CONV) // PROJ_CHUNK, TM, PROJ_CHUNK), f32),
        pltpu.VMEM((N_KV_HEADS, TM + WINDOW, LANES), bf16),
        pltpu.VMEM((N_KV_HEADS, TM + WINDOW, LANES), bf16),
        pltpu.VMEM((TM, D_Q), bf16),
    ]
    tri = jnp.asarray(np.triu(np.ones((TM, TM), np.float32), 1), bf16)
    return pl.pallas_call(
        _mixer_kernel,
        grid=(B, nt),
        in_specs=in_specs, out_specs=out_specs, out_shape=out_shape, scratch_shapes=scratch,
        compiler_params=pltpu.CompilerParams(
            dimension_semantics=("arbitrary", "arbitrary"), vmem_limit_bytes=VMEM_LIMIT),
        name="mixer_router",
    )(x, mod6, p["gmix"], p["gffn"], p["w_in"], p["dw"], p["dwb"], p["lng"], p["lnb"],
      p["wco"], p["wao"], p["wout"], p["qg"], p["kg"], p["bq"], p["bk"], p["bias"], p["sinks"],
      p["wrh"], p["wr2"], p["br"], tri)


def _dest_kernel(pstart_ref, route_ref, dest_ref):
    e = route_ref[0:2, :]
    base = jnp.zeros(e.shape, i32)
    for x in range(N_EXPERTS):
        base = jnp.where(e == x, pstart_ref[x], base)
    dest_ref[...] = base + route_ref[2:4, :]


def _dest_slots(pstart, route):
    N = route.shape[1]
    TS = min(DEST_TILE, N)
    return pl.pallas_call(
        _dest_kernel,
        grid=(N // TS,),
        in_specs=[pl.BlockSpec(memory_space=pltpu.SMEM),
                  pl.BlockSpec((8, TS), lambda i: (0, i))],
        out_specs=pl.BlockSpec((2, TS), lambda i: (0, i)),
        out_shape=jax.ShapeDtypeStruct((2, N), i32),
        name="dest_slots",
    )(pstart, route)


def _sc_mesh():
    return plsc.VectorSubcoreMesh(core_axis_name="c", subcore_axis_name="s")


def _sc_scatter(h2, dest, n_slots):
    N, D = h2.shape
    per_w = N // SC_WORKERS
    C = SC_CHUNK
    n_chunks = per_w // C
    dest4 = dest.reshape(TOP_K, SC_WORKERS, n_chunks, C)

    @functools.partial(
        pl.kernel, mesh=_sc_mesh(),
        out_type=jax.ShapeDtypeStruct((n_slots, D), h2.dtype),
        scratch_types=[pltpu.VMEM((n_chunks, C), i32), pltpu.VMEM((n_chunks, C), i32),
                       pltpu.VMEM((2, C, D), h2.dtype),
                       pltpu.SemaphoreType.DMA((2,)), pltpu.SemaphoreType.DMA((2,))],
        name="sc_dispatch_scatter",
    )
    def run(h2_hbm, dest_hbm, xs_hbm, idx0, idx1, buf, sem_in, sem_out):
        wid = lax.axis_index("s") * SC_CORES + lax.axis_index("c")
        base = wid * per_w
        pltpu.sync_copy(dest_hbm.at[0, wid], idx0)
        pltpu.sync_copy(dest_hbm.at[1, wid], idx1)

        def load(j, b):
            return pltpu.make_async_copy(h2_hbm.at[pl.ds(base + j * C, C)], buf.at[b], sem_in.at[b])

        def put(j, b, idx):
            return pltpu.make_async_copy(buf.at[b], xs_hbm.at[idx.at[j]], sem_out.at[b])

        load(0, 0).start()

        @pl.loop(0, n_chunks, step=2)
        def _(j0):
            for b in range(2):
                j = j0 + b
                load(j, b).wait()

                @pl.when(j >= 1)
                def _():
                    put(j - 1, 1 - b, idx0).wait()
                    put(j - 1, 1 - b, idx1).wait()

                @pl.when(j + 1 < n_chunks)
                def _():
                    load(j + 1, 1 - b).start()

                put(j, b, idx0).start()
                put(j, b, idx1).start()

        put(n_chunks - 1, 1, idx0).wait()
        put(n_chunks - 1, 1, idx1).wait()

    return run(h2, dest4)


def _sc_gather(y, dest, N):
    D = y.shape[1]
    per_w = N // SC_WORKERS
    C = SC_CHUNK
    n_chunks = per_w // C
    dest4 = dest.reshape(TOP_K, SC_WORKERS, n_chunks, C)

    @functools.partial(
        pl.kernel, mesh=_sc_mesh(),
        out_type=jax.ShapeDtypeStruct((TOP_K, N, D), y.dtype),
        scratch_types=[pltpu.VMEM((n_chunks, C), i32), pltpu.VMEM((n_chunks, C), i32),
                       pltpu.VMEM((2, C, D), y.dtype),
                       pltpu.SemaphoreType.DMA((2,)), pltpu.SemaphoreType.DMA((2,))],
        name="sc_combine_gather",
    )
    def run(y_hbm, dest_hbm, yg_hbm, idx0, idx1, buf, sem_in, sem_out):
        wid = lax.axis_index("s") * SC_CORES + lax.axis_index("c")
        base = wid * per_w
        pltpu.sync_copy(dest_hbm.at[0, wid], idx0)
        pltpu.sync_copy(dest_hbm.at[1, wid], idx1)
        idx = (idx0, idx1)

        def get(j, k):
            return pltpu.make_async_copy(y_hbm.at[idx[k].at[j]], buf.at[k], sem_in.at[k])

        def put(j, k):
            return pltpu.make_async_copy(buf.at[k], yg_hbm.at[k, pl.ds(base + j * C, C)], sem_out.at[k])

        get(0, 0).start()
        get(0, 1).start()

        @pl.loop(0, n_chunks)
        def _(j):
            for k in range(TOP_K):
                get(j, k).wait()
                put(j, k).start()
            for k in range(TOP_K):
                put(j, k).wait()

                @pl.when(j + 1 < n_chunks)
                def _():
                    get(j + 1, k).start()

    return run(y, dest4)


def _expert_kernel(be_ref, nv_ref, nu_ref, xs_ref, wg_ref, wu_ref, wdn_ref, y_ref, wgu_ref, wd_ref):
    i = pl.program_id(0)
    in_use = i < nu_ref[0]

    @pl.when(in_use & ((i == 0) | (be_ref[i] != be_ref[jnp.maximum(i - 1, 0)])))
    def _():
        wgu_ref[:, 0:D_EXPERT] = wg_ref[...].astype(bf16)
        wgu_ref[:, D_EXPERT:2 * D_EXPERT] = wu_ref[...].astype(bf16)
        wd_ref[...] = wdn_ref[...].astype(bf16)

    @pl.when(in_use)
    def _():
        live = lax.broadcasted_iota(i32, (MOE_BLOCK, 1), 0) < nv_ref[i]
        x_hi, x_lo = _unpack_halves(jnp.where(live, xs_ref[...], 0))
        half = x_hi.shape[1]
        gu = (_dot(x_hi.astype(bf16), wgu_ref[0:half, :]) +
              _dot(x_lo.astype(bf16), wgu_ref[half:2 * half, :]))
        gate = gu[:, 0:D_EXPERT]
        hid = (gate * jax.nn.sigmoid(gate)) * gu[:, D_EXPERT:2 * D_EXPERT]
        y_ref[...] = _pack_halves(_dot(hid.astype(bf16), wd_ref[...]))


def _experts(block_e, n_valid, n_used, xs, w_gate, w_up, w_down):
    n_slots, DH = xs.shape
    D = 2 * DH
    n_blocks = n_slots // MOE_BLOCK
    last = lambda i, nu: jnp.minimum(i, nu[0] - 1)
    expert = lambda i, be, nv, nu: (be[last(i, nu)], 0, 0)
    grid_spec = pltpu.PrefetchScalarGridSpec(
        num_scalar_prefetch=3,
        grid=(n_blocks,),
        in_specs=[pl.BlockSpec((MOE_BLOCK, DH), lambda i, be, nv, nu: (last(i, nu), 0)),
                  pl.BlockSpec((None, D, D_EXPERT), expert),
                  pl.BlockSpec((None, D, D_EXPERT), expert),
                  pl.BlockSpec((None, D_EXPERT, D), expert)],
        out_specs=pl.BlockSpec((MOE_BLOCK, DH), lambda i, be, nv, nu: (last(i, nu), 0)),
        scratch_shapes=[pltpu.VMEM((D, 2 * D_EXPERT), bf16), pltpu.VMEM((D_EXPERT, D), bf16)],
    )
    return pl.pallas_call(
        _expert_kernel,
        grid_spec=grid_spec,
        out_shape=jax.ShapeDtypeStruct((n_slots, DH), i32),
        compiler_params=pltpu.CompilerParams(dimension_semantics=("arbitrary",),
                                             vmem_limit_bytes=VMEM_LIMIT),
        name="experts",
    )(block_e, n_valid, n_used, xs, w_gate, w_up, w_down)


def _combine_kernel(yg_ref, x1_ref, wtok_ref, mod_ref, *rest):
    o_ref = rest[-1]
    w = wtok_ref[...]
    g2 = mod_ref[5:6, :]
    a_hi, a_lo = _unpack_halves(yg_ref[0])
    b_hi, b_lo = _unpack_halves(yg_ref[1])
    moe = jnp.concatenate([w[:, 0:1] * a_hi + w[:, 1:2] * b_hi,
                           w[:, 0:1] * a_lo + w[:, 1:2] * b_lo], axis=1)
    o_ref[...] = x1_ref[...] + g2 * moe


def _combine(yg, x1, wtok, mod6, T, b0, n_total, out_prev):
    Nc, D = x1.shape
    TS = min(COMBINE_TILE, T)
    per_seq = T // TS
    blk0 = b0 * per_seq
    in_specs = [pl.BlockSpec((TOP_K, TS, D // 2), lambda i: (0, i, 0)),
                pl.BlockSpec((TS, D), lambda i: (i, 0)),
                pl.BlockSpec((TS, LANES), lambda i: (i, 0)),
                pl.BlockSpec((None, 6, D), lambda i: (b0 + i // per_seq, 0, 0))]
    args = [yg, x1, wtok, mod6]
    aliases = {}
    if out_prev is not None:
        in_specs.append(pl.BlockSpec(memory_space=pl.ANY))
        args.append(out_prev)
        aliases = {len(args) - 1: 0}
    return pl.pallas_call(
        _combine_kernel,
        grid=(Nc // TS,),
        in_specs=in_specs,
        out_specs=pl.BlockSpec((TS, D), lambda i: (blk0 + i, 0)),
        out_shape=jax.ShapeDtypeStruct((n_total, D), f32),
        input_output_aliases=aliases,
        compiler_params=pltpu.CompilerParams(dimension_semantics=("arbitrary",)),
        name="combine",
    )(*args)


def _block_diag_mean(n, blk):
    m = np.kron(np.eye(n // blk, dtype=np.float32), np.full((blk, blk), 1.0 / blk, np.float32))
    return jnp.asarray(m, bf16)


def _layer(x, mod6, bias, l, w):
    B, T, D = x.shape
    N = B * T
    w_rg, w_re = w["w_router_group"][l], w["w_router_expert"][l]
    wr = jnp.zeros((D, ROUTER_COLS), f32)
    wr = wr.at[:, 0:N_GROUPS].set(w_rg).at[:, EXPERT_COL0:EXPERT_COL0 + N_EXPERTS].set(w_re)
    br = jnp.zeros((1, ROUTER_COLS), f32)
    br = br.at[0, 0:N_GROUPS].set(w["b_router_group"][l])
    br = br.at[0, EXPERT_COL0:EXPERT_COL0 + N_EXPERTS].set(w["b_router_expert"][l])
    wrh = wr.astype(bf16)
    wrl = (wr - wrh.astype(f32)).astype(bf16)
    p = dict(
        gmix=w["norm_mix_g"][l].reshape(1, D), gffn=w["norm_ffn_g"][l].reshape(1, D),
        w_in=w["w_in"][l].astype(bf16),
        dw=jnp.broadcast_to(w["dw_kernel"][l][:, None, :], (CONV_WIDTH, 8, D_CONV)), dwb=jnp.broadcast_to(w["dw_bias"][l][None, :], (8, D_CONV)),
        lng=w["conv_ln_g"][l].reshape(1, D_CONV), lnb=w["conv_ln_b"][l].reshape(1, D_CONV),
        wco=w["w_conv_out"][l].astype(bf16), wao=w["w_attn_out"][l].astype(bf16),
        wout=w["w_out"][l].astype(bf16),
        qg=(jnp.tile(w["q_norm_g"][l], N_Q_HEADS) * (HEAD_DIM ** -0.5)).reshape(1, D_Q),
        kg=jnp.tile(w["k_norm_g"][l], N_KV_HEADS).reshape(1, D_KV),
        bq=_block_diag_mean(NORM_BLK, HEAD_DIM), bk=_block_diag_mean(D_KV, HEAD_DIM),
        bias=bias, sinks=w["sinks"][l], wrh=wrh, wr2=jnp.concatenate([wrh, wrl], axis=1), br=br,
    )
    w_gate, w_up, w_down = w["w_exp_gate"][l], w["w_exp_up"][l], w["w_exp_down"][l]

    n_chunks = MOE_CHUNKS if B % MOE_CHUNKS == 0 else 1
    Bc = B // n_chunks
    Nc = Bc * T
    n_blocks = -(-(Nc * TOP_K) // MOE_BLOCK) + N_EXPERTS
    blk0 = jnp.arange(n_blocks, dtype=i32) * MOE_BLOCK
    stage = []
    for ch in range(n_chunks):
        x1, h2, route, wtok, cnt = _mixer(x, mod6, p, ch * Bc, Bc)
        counts = cnt[:, 0].astype(i32)
        pcounts = (counts + MOE_BLOCK - 1) // MOE_BLOCK * MOE_BLOCK
        pend = jnp.cumsum(pcounts)
        pstart = pend - pcounts
        block_e = jnp.minimum(jnp.sum((pend[None, :] <= blk0[:, None]).astype(i32), axis=1), N_EXPERTS - 1)
        n_valid = jnp.clip((pstart + counts)[block_e] - blk0, 0, MOE_BLOCK).astype(i32)
        n_used = (pend[-1:] // MOE_BLOCK).astype(i32)
        dest = _dest_slots(pstart, route)
        xs = _sc_scatter(h2.reshape(Nc, D // 2), dest, n_blocks * MOE_BLOCK)
        stage.append((x1, wtok, dest, xs, block_e, n_valid, n_used))
    ys = [_experts(be, nv, nu, xs, w_gate, w_up, w_down) for (_, _, _, xs, be, nv, nu) in stage]
    ygs = [_sc_gather(y, st[2], Nc) for y, st in zip(ys, stage)]
    out = None
    for ch in range(n_chunks):
        x1, wtok = stage[ch][0], stage[ch][1]
        out = _combine(ygs[ch], x1.reshape(Nc, D), wtok, mod6, T, ch * Bc, B * T, out)
    return out.reshape(B, T, D)


def kernel(x, c, w_ada, b_ada, norm_mix_g, w_in, dw_kernel, dw_bias, conv_ln_g, conv_ln_b,
           w_conv_out, q_norm_g, k_norm_g, sinks, w_attn_out, w_out, rel_bias_table, norm_ffn_g,
           w_router_group, b_router_group, w_router_expert, b_router_expert,
           w_exp_gate, w_exp_up, w_exp_down):
    w = dict(norm_mix_g=norm_mix_g, w_in=w_in, dw_kernel=dw_kernel, dw_bias=dw_bias,
             conv_ln_g=conv_ln_g, conv_ln_b=conv_ln_b, w_conv_out=w_conv_out, q_norm_g=q_norm_g,
             k_norm_g=k_norm_g, sinks=sinks, w_attn_out=w_attn_out, w_out=w_out,
             norm_ffn_g=norm_ffn_g, w_router_group=w_router_group, b_router_group=b_router_group,
             w_router_expert=w_router_expert, b_router_expert=b_router_expert,
             w_exp_gate=w_exp_gate, w_exp_up=w_exp_up, w_exp_down=w_exp_down)
    B = x.shape[0]
    bias = _bias_band(rel_bias_table).reshape(N_KV_HEADS, Q_PER_KV * WINDOW, 2 * WINDOW)
    for l in range(w_ada.shape[0]):
        mod6 = _modulation(c, w_ada[l], b_ada[l]).reshape(B, 6, D_MODEL)
        x = _layer(x, mod6, bias, l, w)
    return x
```

```python
import functools
import math

import jax
import jax.numpy as jnp
import numpy as np
from jax import lax
from jax.experimental import pallas as pl
from jax.experimental.pallas import tpu as pltpu
from jax.experimental.pallas import tpu_sc as plsc

D_MODEL = 1024
D_CONV = 512
CONV_WIDTH = 31
N_Q_HEADS = 8
N_KV_HEADS = 2
HEAD_DIM = 64
Q_PER_KV = N_Q_HEADS // N_KV_HEADS
WINDOW = 128
N_BUCKETS = 32
MAX_DISTANCE = 128
N_GROUPS = 4
EXPERTS_PER_GROUP = 8
N_EXPERTS = N_GROUPS * EXPERTS_PER_GROUP
TOP_K = 2
D_EXPERT = 256
D_Q = N_Q_HEADS * HEAD_DIM
D_KV = N_KV_HEADS * HEAD_DIM
EPS = 1e-6
NEG_INF = -1e30
LOG2E = math.log2(math.e)

LANES = 128
SEQ_TILE = 512
CONV_ROWS = 64
NORM_BLK = 256
PROJ_CHUNK = 256
CONV_SPAN = 24
MOE_BLOCK = 512
MOE_CHUNKS = 2
DEST_TILE = 8192
COMBINE_TILE = 1024
SC_CORES = 2
SC_SUBCORES = 16
SC_WORKERS = SC_CORES * SC_SUBCORES
SC_CHUNK = 32
ROUTER_COLS = LANES
EXPERT_COL0 = 8
VMEM_LIMIT = 56 * 1024 * 1024

f32 = jnp.float32
bf16 = jnp.bfloat16
i32 = jnp.int32


def _dot(a, b):
    return jnp.dot(a, b, preferred_element_type=f32)


def _split(a):
    hi = a.astype(bf16)
    lo = (a - hi.astype(f32)).astype(bf16)
    return hi, lo


def _pack_halves(x):
    c = x.shape[1] // 2
    hi = lax.bitcast_convert_type(x[:, 0:c].astype(bf16).astype(f32), jnp.uint32)
    lo = lax.bitcast_convert_type(x[:, c:2 * c].astype(bf16).astype(f32), jnp.uint32)
    word = (hi & jnp.uint32(0xFFFF0000)) | (lo >> jnp.uint32(16))
    return lax.bitcast_convert_type(word, i32)


def _unpack_halves(word):
    u = lax.bitcast_convert_type(word, jnp.uint32)
    hi = lax.bitcast_convert_type(u & jnp.uint32(0xFFFF0000), f32)
    lo = lax.bitcast_convert_type(u << jnp.uint32(16), f32)
    return hi, lo


def _dot3(a, b):
    ah, al = _split(a)
    bh, bl = _split(b)
    return _dot(ah, bh) + _dot(al, bh) + _dot(ah, bl)


def _mod_kernel(c_ref, w_ref, b_ref, o_ref):
    c = c_ref[...]
    s = c * jax.nn.sigmoid(c)
    o_ref[...] = _dot3(s, w_ref[...]) + b_ref[...]


def _modulation(c, w_ada, b_ada):
    B, D = c.shape
    n_out = w_ada.shape[1]
    return pl.pallas_call(
        _mod_kernel,
        grid=(n_out // D,),
        in_specs=[pl.BlockSpec((B, D), lambda j: (0, 0)),
                  pl.BlockSpec((D, D), lambda j: (0, j)),
                  pl.BlockSpec((1, D), lambda j: (0, j))],
        out_specs=pl.BlockSpec((B, D), lambda j: (0, j)),
        out_shape=jax.ShapeDtypeStruct((B, n_out), f32),
        name="modulation",
    )(c, w_ada, b_ada.reshape(1, n_out))


def _band_buckets():
    qi = np.arange(WINDOW)[:, None]
    kj = np.arange(2 * WINDOW)[None, :]
    dist = np.clip(qi + WINDOW - kj, 0, MAX_DISTANCE)
    max_exact = N_BUCKETS // 2
    d = np.maximum(dist, 1).astype(np.float32)
    large = max_exact + (np.log(d / np.float32(max_exact)) / np.float32(math.log(MAX_DISTANCE / max_exact))
                         * np.float32(N_BUCKETS - max_exact)).astype(np.int32)
    large = np.minimum(large, N_BUCKETS - 1)
    bucket = np.where(dist < max_exact, dist, large).astype(np.int32)
    raw = qi + WINDOW - kj
    return np.where((raw >= 0) & (raw < WINDOW), bucket, -1).astype(np.int32)


def _bias_kernel(tab_ref, bucket_ref, o_ref):
    bk = bucket_ref[...]
    for h in range(N_Q_HEADS):
        acc = jnp.full(bk.shape, NEG_INF, f32)
        for b in range(N_BUCKETS):
            acc = jnp.where(bk == b, tab_ref[b, h] * LOG2E, acc)
        o_ref[h] = acc


def _bias_band(rel_bias_table):
    return pl.pallas_call(
        _bias_kernel,
        in_specs=[pl.BlockSpec(memory_space=pltpu.SMEM),
                  pl.BlockSpec(memory_space=pltpu.VMEM)],
        out_specs=pl.BlockSpec(memory_space=pltpu.VMEM),
        out_shape=jax.ShapeDtypeStruct((N_Q_HEADS, WINDOW, 2 * WINDOW), f32),
        name="bias_band",
    )(rel_bias_table, jnp.asarray(_band_buckets()))


def _mixer_kernel(x_ref, mod_ref, gmix_ref, gffn_ref, win_ref, dw_ref, dwb_ref, lng_ref, lnb_ref,
                  wco_ref, wao_ref, wout_ref, qg_ref, kg_ref, bq_ref, bk_ref, bias_ref, sink_ref,
                  wrh_ref, wr2_ref, br_ref, tri_ref,
                  x1_ref, h2_ref, route_ref, wtok_ref, cnt_ref,
                  uext, ush, conv_sc, proj_sc, k2, v2, osc):
    TM = x_ref.shape[0]
    HALO = 32
    b = pl.program_id(0)
    t = pl.program_id(1)
    first = t == 0

    @pl.when(first)
    def _():
        uext[0:HALO, :] = jnp.zeros((HALO, D_CONV), f32)
        k2[:, 0:WINDOW, :] = jnp.zeros((N_KV_HEADS, WINDOW, LANES), bf16)
        v2[:, 0:WINDOW, :] = jnp.zeros((N_KV_HEADS, WINDOW, LANES), bf16)

    @pl.when(first & (b == 0))
    def _():
        cnt_ref[...] = jnp.zeros(cnt_ref.shape, f32)

    x = x_ref[...]
    mod = mod_ref[...]
    sh1, sc1, g1 = mod[0:1, :], mod[1:2, :], mod[2:3, :]
    sh2, sc2, g2 = mod[3:4, :], mod[4:5, :], mod[5:6, :]
    del g2

    ms = jnp.mean(x * x, axis=-1, keepdims=True)
    h = (x * lax.rsqrt(ms + EPS)) * (gmix_ref[...] * (1.0 + sc1)) + sh1
    hb = h.astype(bf16)

    ab = _dot(hb, win_ref[:, 0:2 * D_CONV])
    u = ab[:, 0:D_CONV] * jax.nn.sigmoid(ab[:, D_CONV:2 * D_CONV])
    uext[HALO:HALO + TM, :] = u
    for r in range(1, 8):
        ush[r - 1] = uext[r:r + TM + CONV_SPAN, :]
    n_q = D_Q // PROJ_CHUNK
    n_g = D_MODEL // PROJ_CHUNK
    n_jobs = (win_ref.shape[1] - 2 * D_CONV) // PROJ_CHUNK
    n_rb = TM // CONV_ROWS
    n_t8 = CONV_ROWS // 8
    units = [(cb, rg) for cb in range(D_CONV // LANES) for rg in range(n_rb)]
    job_at = {(i * len(units)) // n_jobs: i for i in range(n_jobs)}
    assert len(job_at) == n_jobs
    for ui, (cb, rg) in enumerate(units):
        if ui in job_at:
            c_lo = 2 * D_CONV + job_at[ui] * PROJ_CHUNK
            proj_sc[job_at[ui]] = _dot(hb, win_ref[:, c_lo:c_lo + PROJ_CHUNK])
        cs = slice(cb * LANES, (cb + 1) * LANES)
        base = rg * CONV_ROWS
        accs = [None] * n_t8
        for r in range(8):
            taps = [(j, (j + 2) // 8) for j in range(CONV_WIDTH) if (j + 2) % 8 == r]
            t_lo = min(a for _, a in taps)
            t_hi = max(a for _, a in taps) + n_t8
            tiles = {}
            for t8 in range(t_lo, t_hi):
                rows = slice(base + 8 * t8, base + 8 * t8 + 8)
                tiles[t8] = uext[rows, cs] if r == 0 else ush[r - 1, rows, cs]
            for j, a in taps:
                tap = dw_ref[j, :, cs]
                for s8 in range(n_t8):
                    term = tap * tiles[a + s8]
                    accs[s8] = term if accs[s8] is None else accs[s8] + term
        for s8 in range(n_t8):
            conv_sc[base + 8 * s8:base + 8 * s8 + 8, cs] = accs[s8] + dwb_ref[:, cs]
    conv = conv_sc[...]
    uext[0:HALO, :] = uext[TM:TM + HALO, :]
    mu = jnp.mean(conv, axis=-1, keepdims=True)
    dc = conv - mu
    var = jnp.mean(dc * dc, axis=-1, keepdims=True)
    yn = dc * lax.rsqrt(var + EPS) * lng_ref[...] + lnb_ref[...]
    act = yn * jax.nn.sigmoid(yn)
    y_conv = _dot(act.astype(bf16), wco_ref[...])

    q = jnp.concatenate([proj_sc[i] for i in range(n_q)], axis=1)
    k = proj_sc[n_q, :, 0:D_KV]
    v = proj_sc[n_q, :, D_KV:2 * D_KV]

    def head_norm(z, blk_ref, g):
        sq = (z * z).astype(bf16)
        blk = blk_ref.shape[0]
        msq = jnp.concatenate([_dot(sq[:, c:c + blk], blk_ref[...]) for c in range(0, z.shape[1], blk)], axis=1)
        return z * lax.rsqrt(msq + EPS) * g

    qn = head_norm(q, bq_ref, qg_ref[...]).astype(bf16)
    kn = head_norm(k, bk_ref, kg_ref[...])
    lo_half = lax.broadcasted_iota(i32, (TM, LANES), 1) < HEAD_DIM
    kr = pltpu.roll(kn, HEAD_DIM, 1)
    vr = pltpu.roll(v, HEAD_DIM, 1)
    k2[0, WINDOW:WINDOW + TM, :] = jnp.where(lo_half, kn, kr).astype(bf16)
    k2[1, WINDOW:WINDOW + TM, :] = jnp.where(lo_half, kr, kn).astype(bf16)
    v2[0, WINDOW:WINDOW + TM, :] = jnp.where(lo_half, v, vr).astype(bf16)
    v2[1, WINDOW:WINDOW + TM, :] = jnp.where(lo_half, vr, v).astype(bf16)

    QROWS = Q_PER_KV * WINDOW
    col = lax.broadcasted_iota(i32, (QROWS, 2 * WINDOW), 1)
    has_prev = col >= jnp.where(first, WINDOW, 0)
    hrow = lax.broadcasted_iota(i32, (QROWS, 1), 0) // WINDOW
    lo128 = lax.broadcasted_iota(i32, (WINDOW, LANES), 1) < HEAD_DIM
    zero_q = jnp.zeros((WINDOW, LANES), bf16)
    for g in range(N_KV_HEADS):
        sink = LOG2E * jnp.where(hrow == 0, sink_ref[4 * g],
                                 jnp.where(hrow == 1, sink_ref[4 * g + 1],
                                           jnp.where(hrow == 2, sink_ref[4 * g + 2], sink_ref[4 * g + 3])))
        bias_g = bias_ref[g]
        for j in range(TM // WINDOW):
            rs = slice(j * WINDOW, (j + 1) * WINDOW)
            qa = qn[rs, 2 * LANES * g:2 * LANES * g + LANES]
            qb = qn[rs, 2 * LANES * g + LANES:2 * LANES * (g + 1)]
            qs = jnp.concatenate([jnp.where(lo128, qa, zero_q), jnp.where(lo128, zero_q, qa),
                                  jnp.where(lo128, qb, zero_q), jnp.where(lo128, zero_q, qb)], axis=0)
            kk = k2[g, j * WINDOW:(j + 2) * WINDOW, :]
            s = lax.dot_general(qs, kk, (((1,), (1,)), ((), ())), preferred_element_type=f32)
            logits = jnp.where(has_prev, s + bias_g, NEG_INF) if j == 0 else s + bias_g
            m = jnp.maximum(jnp.max(logits, axis=-1, keepdims=True), sink)
            p = jnp.exp2(logits - m)
            den = jnp.sum(p, axis=-1, keepdims=True) + jnp.exp2(sink - m)
            o2 = _dot(p.astype(bf16), v2[g, j * WINDOW:(j + 2) * WINDOW, :]) * (1.0 / den)
            osc[rs, 2 * LANES * g:2 * LANES * g + LANES] = jnp.where(
                lo128, o2[0:WINDOW], o2[WINDOW:2 * WINDOW]).astype(bf16)
            osc[rs, 2 * LANES * g + LANES:2 * LANES * (g + 1)] = jnp.where(
                lo128, o2[2 * WINDOW:3 * WINDOW], o2[3 * WINDOW:4 * WINDOW]).astype(bf16)
    k2[:, 0:WINDOW, :] = k2[:, TM:TM + WINDOW, :]
    v2[:, 0:WINDOW, :] = v2[:, TM:TM + WINDOW, :]
    y_attn = _dot(osc[...], wao_ref[...])

    merged = []
    for i in range(n_g):
        cs = slice(i * PROJ_CHUNK, (i + 1) * PROJ_CHUNK)
        g_conv = jax.nn.sigmoid(proj_sc[n_q + 1 + i])
        g_attn = jax.nn.sigmoid(proj_sc[n_q + 1 + n_g + i])
        merged.append((g_conv * y_conv[:, cs] + g_attn * y_attn[:, cs]).astype(bf16))
    merged = jnp.concatenate(merged, axis=1)
    x1 = x + g1 * _dot(merged, wout_ref[...])
    x1_ref[...] = x1

    ms2 = jnp.mean(x1 * x1, axis=-1, keepdims=True)
    h2 = (x1 * lax.rsqrt(ms2 + EPS)) * (gffn_ref[...] * (1.0 + sc2)) + sh2
    h2_ref[...] = _pack_halves(h2)
    hh, hl = _split(h2)
    hw = _dot(hh, wr2_ref[...])
    lg = hw[:, 0:ROUTER_COLS] + hw[:, ROUTER_COLS:2 * ROUTER_COLS] + _dot(hl, wrh_ref[...]) + br_ref[...]
    lt = lg.T
    gl = lt[0:N_GROUPS, :]
    grow = lax.broadcasted_iota(i32, (N_GROUPS, TM), 0)
    gmax = jnp.max(gl, axis=0, keepdims=True)
    gi = jnp.min(jnp.where(gl == gmax, grow, N_GROUPS), axis=0, keepdims=True)
    p_top = 1.0 / jnp.sum(jnp.exp(gl - gmax), axis=0, keepdims=True)
    sel = lt[EXPERT_COL0:EXPERT_COL0 + EXPERTS_PER_GROUP, :]
    for gg in range(1, N_GROUPS):
        lo_r = EXPERT_COL0 + gg * EXPERTS_PER_GROUP
        sel = jnp.where(gi == gg, lt[lo_r:lo_r + EXPERTS_PER_GROUP, :], sel)
    erow = lax.broadcasted_iota(i32, (EXPERTS_PER_GROUP, TM), 0)
    m1 = jnp.max(sel, axis=0, keepdims=True)
    i1 = jnp.min(jnp.where(sel == m1, erow, EXPERTS_PER_GROUP), axis=0, keepdims=True)
    rest = jnp.where(erow == i1, -jnp.inf, sel)
    m2 = jnp.max(rest, axis=0, keepdims=True)
    i2 = jnp.min(jnp.where(rest == m2, erow, EXPERTS_PER_GROUP), axis=0, keepdims=True)
    z = jnp.sum(jnp.exp(sel - m1), axis=0, keepdims=True)
    v1 = 1.0 / z
    v2nd = jnp.exp(m2 - m1) / z
    w1 = v1 / (v1 + v2nd) * p_top
    w2 = v2nd / (v1 + v2nd) * p_top
    e1 = gi * EXPERTS_PER_GROUP + i1
    e2 = gi * EXPERTS_PER_GROUP + i2

    xrow = lax.broadcasted_iota(i32, (N_EXPERTS, TM), 0)
    oh1 = xrow == e1
    oh2 = xrow == e2
    both = jnp.where(oh1 | oh2, 1.0, 0.0)
    prefix = _dot(both.astype(bf16), tri_ref[...]) + cnt_ref[:, 0:1]
    r1 = jnp.sum(jnp.where(oh1, prefix, 0.0), axis=0, keepdims=True)
    r2 = jnp.sum(jnp.where(oh2, prefix, 0.0), axis=0, keepdims=True)
    cnt_ref[...] = cnt_ref[...] + jnp.sum(both, axis=1, keepdims=True)

    route_ref[...] = jnp.concatenate(
        [e1, e2, r1.astype(i32), r2.astype(i32), jnp.zeros((4, TM), i32)], axis=0)
    wpad = jnp.concatenate([w1, w2, jnp.zeros((LANES - 2, TM), f32)], axis=0)
    wtok_ref[...] = wpad.T


def _mixer(x, mod6, p, b0, B):
    _, T, D = x.shape
    TM = min(SEQ_TILE, T)
    nt = T // TM
    N = B * T
    const = lambda shape: pl.BlockSpec(shape, lambda b, t: (0,) * len(shape))
    in_specs = [
        pl.BlockSpec((None, TM, D), lambda b, t: (b0 + b, t, 0)),
        pl.BlockSpec((None, 6, D), lambda b, t: (b0 + b, 0, 0)),
        const((1, D)), const((1, D)),
        const(p["w_in"].shape),
        const((CONV_WIDTH, 8, D_CONV)), const((8, D_CONV)), const((1, D_CONV)), const((1, D_CONV)),
        const((D_CONV, D)), const((D_Q, D)), const((D, D)),
        const((1, D_Q)), const((1, D_KV)),
        const((NORM_BLK, NORM_BLK)), const((D_KV, D_KV)),
        const((N_KV_HEADS, Q_PER_KV * WINDOW, 2 * WINDOW)),
        pl.BlockSpec(memory_space=pltpu.SMEM),
        const((D, ROUTER_COLS)), const((D, 2 * ROUTER_COLS)), const((1, ROUTER_COLS)),
        const((TM, TM)),
    ]
    out_specs = [
        pl.BlockSpec((None, TM, D), lambda b, t: (b, t, 0)),
        pl.BlockSpec((None, TM, D // 2), lambda b, t: (b, t, 0)),
        pl.BlockSpec((8, TM), lambda b, t: (0, b * nt + t)),
        pl.BlockSpec((TM, LANES), lambda b, t: (b * nt + t, 0)),
        pl.BlockSpec((N_EXPERTS, LANES), lambda b, t: (0, 0)),
    ]
    out_shape = [
        jax.ShapeDtypeStruct((B, T, D), f32),
        jax.ShapeDtypeStruct((B, T, D // 2), i32),
        jax.ShapeDtypeStruct((8, N), i32),
        jax.ShapeDtypeStruct((N, LANES), f32),
        jax.ShapeDtypeStruct((N_EXPERTS, LANES), f32),
    ]
    scratch = [
        pltpu.VMEM((TM + 32, D_CONV), f32),
        pltpu.VMEM((7, TM + CONV_SPAN, D_CONV), f32),
        pltpu.VMEM((TM, D_CONV), f32),
        pltpu.VMEM(((p["w_in"].shape[1] - 2 * D_CONV) // PROJ_CHUNK, TM, PROJ_CHUNK), f32),
        pltpu.VMEM((N_KV_HEADS, TM + WINDOW, LANES), bf16),
        pltpu.VMEM((N_KV_HEADS, TM + WINDOW, LANES), bf16),
        pltpu.VMEM((TM, D_Q), bf16),
    ]
    tri = jnp.asarray(np.triu(np.ones((TM, TM), np.float32), 1), bf16)
    return pl.pallas_call(
        _mixer_kernel,
        grid=(B, nt),
        in_specs=in_specs, out_specs=out_specs, out_shape=out_shape, scratch_shapes=scratch,
        compiler_params=pltpu.CompilerParams(
            dimension_semantics=("arbitrary", "arbitrary"), vmem_limit_bytes=VMEM_LIMIT),
        name="mixer_router",
    )(x, mod6, p["gmix"], p["gffn"], p["w_in"], p["dw"], p["dwb"], p["lng"], p["lnb"],
      p["wco"], p["wao"], p["wout"], p["qg"], p["kg"], p["bq"], p["bk"], p["bias"], p["sinks"],
      p["wrh"], p["wr2"], p["br"], tri)


def _dest_kernel(pstart_ref, route_ref, dest_ref):
    e = route_ref[0:2, :]
    base = jnp.zeros(e.shape, i32)
    for x in range(N_EXPERTS):
        base = jnp.where(e == x, pstart_ref[x], base)
    dest_ref[...] = base + route_ref[2:4, :]


def _dest_slots(pstart, route):
    N = route.shape[1]
    TS = min(DEST_TILE, N)
    return pl.pallas_call(
        _dest_kernel,
        grid=(N // TS,),
        in_specs=[pl.BlockSpec(memory_space=pltpu.SMEM),
                  pl.BlockSpec((8, TS), lambda i: (0, i))],
        out_specs=pl.BlockSpec((2, TS), lambda i: (0, i)),
        out_shape=jax.ShapeDtypeStruct((2, N), i32),
        name="dest_slots",
    )(pstart, route)


def _sc_mesh():
    return plsc.VectorSubcoreMesh(core_axis_name="c", subcore_axis_name="s")


def _sc_scatter(h2, dest, n_slots):
    N, D = h2.shape
    per_w = N // SC_WORKERS
    C = SC_CHUNK
    n_chunks = per_w // C
    dest4 = dest.reshape(TOP_K, SC_WORKERS, n_chunks, C)

    @functools.partial(
        pl.kernel, mesh=_sc_mesh(),
        out_type=jax.ShapeDtypeStruct((n_slots, D), h2.dtype),
        scratch_types=[pltpu.VMEM((n_chunks, C), i32), pltpu.VMEM((n_chunks, C), i32),
                       pltpu.VMEM((2, C, D), h2.dtype),
                       pltpu.SemaphoreType.DMA((2,)), pltpu.SemaphoreType.DMA((2,))],
        name="sc_dispatch_scatter",
    )
    def run(h2_hbm, dest_hbm, xs_hbm, idx0, idx1, buf, sem_in, sem_out):
        wid = lax.axis_index("s") * SC_CORES + lax.axis_index("c")
        base = wid * per_w
        pltpu.sync_copy(dest_hbm.at[0, wid], idx0)
        pltpu.sync_copy(dest_hbm.at[1, wid], idx1)

        def load(j, b):
            return pltpu.make_async_copy(h2_hbm.at[pl.ds(base + j * C, C)], buf.at[b], sem_in.at[b])

        def put(j, b, idx):
            return pltpu.make_async_copy(buf.at[b], xs_hbm.at[idx.at[j]], sem_out.at[b])

        load(0, 0).start()

        @pl.loop(0, n_chunks, step=2)
        def _(j0):
            for b in range(2):
                j = j0 + b
                load(j, b).wait()

                @pl.when(j >= 1)
                def _():
                    put(j - 1, 1 - b, idx0).wait()
                    put(j - 1, 1 - b, idx1).wait()

                @pl.when(j + 1 < n_chunks)
                def _():
                    load(j + 1, 1 - b).start()

                put(j, b, idx0).start()
                put(j, b, idx1).start()

        put(n_chunks - 1, 1, idx0).wait()
        put(n_chunks - 1, 1, idx1).wait()

    return run(h2, dest4)


def _sc_gather(y, dest, N):
    D = y.shape[1]
    per_w = N // SC_WORKERS
    C = SC_CHUNK
    n_chunks = per_w // C
    dest4 = dest.reshape(TOP_K, SC_WORKERS, n_chunks, C)

    @functools.partial(
        pl.kernel, mesh=_sc_mesh(),
        out_type=jax.ShapeDtypeStruct((TOP_K, N, D), y.dtype),
        scratch_types=[pltpu.VMEM((n_chunks, C), i32), pltpu.VMEM((n_chunks, C), i32),
                       pltpu.VMEM((2, C, D), y.dtype),
                       pltpu.SemaphoreType.DMA((2,)), pltpu.SemaphoreType.DMA((2,))],
        name="sc_combine_gather",
    )
    def run(y_hbm, dest_hbm, yg_hbm, idx0, idx1, buf, sem_in, sem_out):
        wid = lax.axis_index("s") * SC_CORES + lax.axis_index("c")
        base = wid * per_w
        pltpu.sync_copy(dest_hbm.at[0, wid], idx0)
        pltpu.sync_copy(dest_hbm.at[1, wid], idx1)
        idx = (idx0, idx1)

        def get(j, k):
            return pltpu.make_async_copy(y_hbm.at[idx[k].at[j]], buf.at[k], sem_in.at[k])

        def put(j, k):
            return pltpu.make_async_copy(buf.at[k], yg_hbm.at[k, pl.ds(base + j * C, C)], sem_out.at[k])

        get(0, 0).start()
        get(0, 1).start()

        @pl.loop(0, n_chunks)
        def _(j):
            for k in range(TOP_K):
                get(j, k).wait()
                put(j, k).start()
            for k in range(TOP_K):
                put(j, k).wait()

                @pl.when(j + 1 < n_chunks)
                def _():
                    get(j + 1, k).start()

    return run(y, dest4)


def _expert_kernel(be_ref, nv_ref, nu_ref, xs_ref, wg_ref, wu_ref, wdn_ref, y_ref, wgu_ref, wd_ref):
    i = pl.program_id(0)
    in_use = i < nu_ref[0]

    @pl.when(in_use & ((i == 0) | (be_ref[i] != be_ref[jnp.maximum(i - 1, 0)])))
    def _():
        wgu_ref[:, 0:D_EXPERT] = wg_ref[...].astype(bf16)
        wgu_ref[:, D_EXPERT:2 * D_EXPERT] = wu_ref[...].astype(bf16)
        wd_ref[...] = wdn_ref[...].astype(bf16)

    @pl.when(in_use)
    def _():
        live = lax.broadcasted_iota(i32, (MOE_BLOCK, 1), 0) < nv_ref[i]
        x_hi, x_lo = _unpack_halves(jnp.where(live, xs_ref[...], 0))
        half = x_hi.shape[1]
        gu = (_dot(x_hi.astype(bf16), wgu_ref[0:half, :]) +
              _dot(x_lo.astype(bf16), wgu_ref[half:2 * half, :]))
        gate = gu[:, 0:D_EXPERT]
        hid = (gate * jax.nn.sigmoid(gate)) * gu[:, D_EXPERT:2 * D_EXPERT]
        y_ref[...] = _pack_halves(_dot(hid.astype(bf16), wd_ref[...]))


def _experts(block_e, n_valid, n_used, xs, w_gate, w_up, w_down):
    n_slots, DH = xs.shape
    D = 2 * DH
    n_blocks = n_slots // MOE_BLOCK
    last = lambda i, nu: jnp.minimum(i, nu[0] - 1)
    expert = lambda i, be, nv, nu: (be[last(i, nu)], 0, 0)
    grid_spec = pltpu.PrefetchScalarGridSpec(
        num_scalar_prefetch=3,
        grid=(n_blocks,),
        in_specs=[pl.BlockSpec((MOE_BLOCK, DH), lambda i, be, nv, nu: (last(i, nu), 0)),
                  pl.BlockSpec((None, D, D_EXPERT), expert),
                  pl.BlockSpec((None, D, D_EXPERT), expert),
                  pl.BlockSpec((None, D_EXPERT, D), expert)],
        out_specs=pl.BlockSpec((MOE_BLOCK, DH), lambda i, be, nv, nu: (last(i, nu), 0)),
        scratch_shapes=[pltpu.VMEM((D, 2 * D_EXPERT), bf16), pltpu.VMEM((D_EXPERT, D), bf16)],
    )
    return pl.pallas_call(
        _expert_kernel,
        grid_spec=grid_spec,
        out_shape=jax.ShapeDtypeStruct((n_slots, DH), i32),
        compiler_params=pltpu.CompilerParams(dimension_semantics=("arbitrary",),
                                             vmem_limit_bytes=VMEM_LIMIT),
        name="experts",
    )(block_e, n_valid, n_used, xs, w_gate, w_up, w_down)


def _combine_kernel(yg_ref, x1_ref, wtok_ref, mod_ref, *rest):
    o_ref = rest[-1]
    w = wtok_ref[...]
    g2 = mod_ref[5:6, :]
    a_hi, a_lo = _unpack_halves(yg_ref[0])
    b_hi, b_lo = _unpack_halves(yg_ref[1])
    moe = jnp.concatenate([w[:, 0:1] * a_hi + w[:, 1:2] * b_hi,
                           w[:, 0:1] * a_lo + w[:, 1:2] * b_lo], axis=1)
    o_ref[...] = x1_ref[...] + g2 * moe


def _combine(yg, x1, wtok, mod6, T, b0, n_total, out_prev):
    Nc, D = x1.shape
    TS = min(COMBINE_TILE, T)
    per_seq = T // TS
    blk0 = b0 * per_seq
    in_specs = [pl.BlockSpec((TOP_K, TS, D // 2), lambda i: (0, i, 0)),
                pl.BlockSpec((TS, D), lambda i: (i, 0)),
                pl.BlockSpec((TS, LANES), lambda i: (i, 0)),
                pl.BlockSpec((None, 6, D), lambda i: (b0 + i // per_seq, 0, 0))]
    args = [yg, x1, wtok, mod6]
    aliases = {}
    if out_prev is not None:
        in_specs.append(pl.BlockSpec(memory_space=pl.ANY))
        args.append(out_prev)
        aliases = {len(args) - 1: 0}
    return pl.pallas_call(
        _combine_kernel,
        grid=(Nc // TS,),
        in_specs=in_specs,
        out_specs=pl.BlockSpec((TS, D), lambda i: (blk0 + i, 0)),
        out_shape=jax.ShapeDtypeStruct((n_total, D), f32),
        input_output_aliases=aliases,
        compiler_params=pltpu.CompilerParams(dimension_semantics=("arbitrary",),
                                             vmem_limit_bytes=VMEM_LIMIT),
        name="combine",
    )(*args)


def _block_diag_mean(n, blk):
    m = np.kron(np.eye(n // blk, dtype=np.float32), np.full((blk, blk), 1.0 / blk, np.float32))
    return jnp.asarray(m, bf16)


def _layer(x, mod6, bias, l, w):
    B, T, D = x.shape
    N = B * T
    w_rg, w_re = w["w_router_group"][l], w["w_router_expert"][l]
    wr = jnp.zeros((D, ROUTER_COLS), f32)
    wr = wr.at[:, 0:N_GROUPS].set(w_rg).at[:, EXPERT_COL0:EXPERT_COL0 + N_EXPERTS].set(w_re)
    br = jnp.zeros((1, ROUTER_COLS), f32)
    br = br.at[0, 0:N_GROUPS].set(w["b_router_group"][l])
    br = br.at[0, EXPERT_COL0:EXPERT_COL0 + N_EXPERTS].set(w["b_router_expert"][l])
    wrh = wr.astype(bf16)
    wrl = (wr - wrh.astype(f32)).astype(bf16)
    p = dict(
        gmix=w["norm_mix_g"][l].reshape(1, D), gffn=w["norm_ffn_g"][l].reshape(1, D),
        w_in=w["w_in"][l].astype(bf16),
        dw=jnp.broadcast_to(w["dw_kernel"][l][:, None, :], (CONV_WIDTH, 8, D_CONV)), dwb=jnp.broadcast_to(w["dw_bias"][l][None, :], (8, D_CONV)),
        lng=w["conv_ln_g"][l].reshape(1, D_CONV), lnb=w["conv_ln_b"][l].reshape(1, D_CONV),
        wco=w["w_conv_out"][l].astype(bf16), wao=w["w_attn_out"][l].astype(bf16),
        wout=w["w_out"][l].astype(bf16),
        qg=(jnp.tile(w["q_norm_g"][l], N_Q_HEADS) * (HEAD_DIM ** -0.5 * LOG2E)).reshape(1, D_Q),
        kg=jnp.tile(w["k_norm_g"][l], N_KV_HEADS).reshape(1, D_KV),
        bq=_block_diag_mean(NORM_BLK, HEAD_DIM), bk=_block_diag_mean(D_KV, HEAD_DIM),
        bias=bias, sinks=w["sinks"][l], wrh=wrh, wr2=jnp.concatenate([wrh, wrl], axis=1), br=br,
    )
    w_gate, w_up, w_down = w["w_exp_gate"][l], w["w_exp_up"][l], w["w_exp_down"][l]

    n_chunks = MOE_CHUNKS if B % MOE_CHUNKS == 0 else 1
    Bc = B // n_chunks
    Nc = Bc * T
    n_blocks = -(-(Nc * TOP_K) // MOE_BLOCK) + N_EXPERTS
    blk0 = jnp.arange(n_blocks, dtype=i32) * MOE_BLOCK
    stage = []
    for ch in range(n_chunks):
        x1, h2, route, wtok, cnt = _mixer(x, mod6, p, ch * Bc, Bc)
        counts = cnt[:, 0].astype(i32)
        pcounts = (counts + MOE_BLOCK - 1) // MOE_BLOCK * MOE_BLOCK
        pend = jnp.cumsum(pcounts)
        pstart = pend - pcounts
        block_e = jnp.minimum(jnp.sum((pend[None, :] <= blk0[:, None]).astype(i32), axis=1), N_EXPERTS - 1)
        n_valid = jnp.clip((pstart + counts)[block_e] - blk0, 0, MOE_BLOCK).astype(i32)
        n_used = (pend[-1:] // MOE_BLOCK).astype(i32)
        dest = _dest_slots(pstart, route)
        xs = _sc_scatter(h2.reshape(Nc, D // 2), dest, n_blocks * MOE_BLOCK)
        stage.append((x1, wtok, dest, xs, block_e, n_valid, n_used))
    ys = [_experts(be, nv, nu, xs, w_gate, w_up, w_down) for (_, _, _, xs, be, nv, nu) in stage]
    ygs = [_sc_gather(y, st[2], Nc) for y, st in zip(ys, stage)]
    out = None
    for ch in range(n_chunks):
        x1, wtok = stage[ch][0], stage[ch][1]
        out = _combine(ygs[ch], x1.reshape(Nc, D), wtok, mod6, T, ch * Bc, B * T, out)
    return out.reshape(B, T, D)


def kernel(x, c, w_ada, b_ada, norm_mix_g, w_in, dw_kernel, dw_bias, conv_ln_g, conv_ln_b,
           w_conv_out, q_norm_g, k_norm_g, sinks, w_attn_out, w_out, rel_bias_table, norm_ffn_g,
           w_router_group, b_router_group, w_router_expert, b_router_expert,
           w_exp_gate, w_exp_up, w_exp_down):
    w = dict(norm_mix_g=norm_mix_g, w_in=w_in, dw_kernel=dw_kernel, dw_bias=dw_bias,
             conv_ln_g=conv_ln_g, conv_ln_b=conv_ln_b, w_conv_out=w_conv_out, q_norm_g=q_norm_g,
             k_norm_g=k_norm_g, sinks=sinks, w_attn_out=w_attn_out, w_out=w_out,
             norm_ffn_g=norm_ffn_g, w_router_group=w_router_group, b_router_group=b_router_group,
             w_router_expert=w_router_expert, b_router_expert=b_router_expert,
             w_exp_gate=w_exp_gate, w_exp_up=w_exp_up, w_exp_down=w_exp_down)
    B = x.shape[0]
    bias = _bias_band(rel_bias_table).reshape(N_KV_HEADS, Q_PER_KV * WINDOW, 2 * WINDOW)
    for l in range(w_ada.shape[0]):
        mod6 = _modulation(c, w_ada[l], b_ada[l]).reshape(B, 6, D_MODEL)
        x = _layer(x, mod6, bias, l, w)
    return x
```

```python
import functools
import math

import jax
import jax.numpy as jnp
import numpy as np
from jax import lax
from jax.experimental import pallas as pl
from jax.experimental.pallas import tpu as pltpu
from jax.experimental.pallas import tpu_sc as plsc

D_MODEL = 1024
D_CONV = 512
CONV_WIDTH = 31
N_Q_HEADS = 8
N_KV_HEADS = 2
HEAD_DIM = 64
Q_PER_KV = N_Q_HEADS // N_KV_HEADS
WINDOW = 128
N_BUCKETS = 32
MAX_DISTANCE = 128
N_GROUPS = 4
EXPERTS_PER_GROUP = 8
N_EXPERTS = N_GROUPS * EXPERTS_PER_GROUP
TOP_K = 2
D_EXPERT = 256
D_Q = N_Q_HEADS * HEAD_DIM
D_KV = N_KV_HEADS * HEAD_DIM
EPS = 1e-6
NEG_INF = -1e30
LOG2E = math.log2(math.e)

LANES = 128
SEQ_TILE = 512
CONV_ROWS = 64
NORM_BLK = 256
PROJ_CHUNK = 256
CONV_SPAN = 24
MOE_BLOCK = 512
MOE_CHUNKS = 2
DEST_TILE = 8192
COMBINE_TILE = 1024
SC_CORES = 2
SC_SUBCORES = 16
SC_WORKERS = SC_CORES * SC_SUBCORES
SC_CHUNK = 32
ROUTER_COLS = LANES
EXPERT_COL0 = 8
VMEM_LIMIT = 56 * 1024 * 1024

f32 = jnp.float32
bf16 = jnp.bfloat16
i32 = jnp.int32


def _dot(a, b):
    return jnp.dot(a, b, preferred_element_type=f32)


def _split(a):
    hi = a.astype(bf16)
    lo = (a - hi.astype(f32)).astype(bf16)
    return hi, lo


def _pack_halves(x):
    c = x.shape[1] // 2
    hi = lax.bitcast_convert_type(x[:, 0:c].astype(bf16).astype(f32), jnp.uint32)
    lo = lax.bitcast_convert_type(x[:, c:2 * c].astype(bf16).astype(f32), jnp.uint32)
    word = (hi & jnp.uint32(0xFFFF0000)) | (lo >> jnp.uint32(16))
    return lax.bitcast_convert_type(word, i32)


def _unpack_halves(word):
    u = lax.bitcast_convert_type(word, jnp.uint32)
    hi = lax.bitcast_convert_type(u & jnp.uint32(0xFFFF0000), f32)
    lo = lax.bitcast_convert_type(u << jnp.uint32(16), f32)
    return hi, lo


def _dot3(a, b):
    ah, al = _split(a)
    bh, bl = _split(b)
    return _dot(ah, bh) + _dot(al, bh) + _dot(ah, bl)


def _mod_kernel(c_ref, w_ref, b_ref, o_ref):
    c = c_ref[...]
    s = c * jax.nn.sigmoid(c)
    o_ref[...] = _dot3(s, w_ref[...]) + b_ref[...]


def _modulation(c, w_ada, b_ada):
    B, D = c.shape
    n_out = w_ada.shape[1]
    return pl.pallas_call(
        _mod_kernel,
        grid=(n_out // D,),
        in_specs=[pl.BlockSpec((B, D), lambda j: (0, 0)),
                  pl.BlockSpec((D, D), lambda j: (0, j)),
                  pl.BlockSpec((1, D), lambda j: (0, j))],
        out_specs=pl.BlockSpec((B, D), lambda j: (0, j)),
        out_shape=jax.ShapeDtypeStruct((B, n_out), f32),
        name="modulation",
    )(c, w_ada, b_ada.reshape(1, n_out))


def _band_buckets():
    qi = np.arange(WINDOW)[:, None]
    kj = np.arange(2 * WINDOW)[None, :]
    dist = np.clip(qi + WINDOW - kj, 0, MAX_DISTANCE)
    max_exact = N_BUCKETS // 2
    d = np.maximum(dist, 1).astype(np.float32)
    large = max_exact + (np.log(d / np.float32(max_exact)) / np.float32(math.log(MAX_DISTANCE / max_exact))
                         * np.float32(N_BUCKETS - max_exact)).astype(np.int32)
    large = np.minimum(large, N_BUCKETS - 1)
    bucket = np.where(dist < max_exact, dist, large).astype(np.int32)
    raw = qi + WINDOW - kj
    return np.where((raw >= 0) & (raw < WINDOW), bucket, -1).astype(np.int32)


def _bias_kernel(tab_ref, bucket_ref, o_ref):
    bk = bucket_ref[...]
    for h in range(N_Q_HEADS):
        acc = jnp.full(bk.shape, NEG_INF, f32)
        for b in range(N_BUCKETS):
            acc = jnp.where(bk == b, tab_ref[b, h] * LOG2E, acc)
        o_ref[h] = acc


def _bias_band(rel_bias_table):
    return pl.pallas_call(
        _bias_kernel,
        in_specs=[pl.BlockSpec(memory_space=pltpu.SMEM),
                  pl.BlockSpec(memory_space=pltpu.VMEM)],
        out_specs=pl.BlockSpec(memory_space=pltpu.VMEM),
        out_shape=jax.ShapeDtypeStruct((N_Q_HEADS, WINDOW, 2 * WINDOW), f32),
        name="bias_band",
    )(rel_bias_table, jnp.asarray(_band_buckets()))


def _mixer_kernel(x_ref, mod_ref, gmix_ref, gffn_ref, win_ref, dw_ref, dwb_ref, lng_ref, lnb_ref,
                  wco_ref, wao_ref, wout_ref, qg_ref, kg_ref, bq_ref, bk_ref, bias_ref, sink_ref,
                  wrh_ref, wr2_ref, br_ref, tri_ref,
                  x1_ref, h2_ref, route_ref, wtok_ref, cnt_ref,
                  uext, ush, conv_sc, proj_sc, k2, v2, osc):
    TM = x_ref.shape[0]
    HALO = 32
    b = pl.program_id(0)
    t = pl.program_id(1)
    first = t == 0

    @pl.when(first)
    def _():
        uext[0:HALO, :] = jnp.zeros((HALO, D_CONV), f32)
        k2[:, 0:WINDOW, :] = jnp.zeros((N_KV_HEADS, WINDOW, LANES), bf16)
        v2[:, 0:WINDOW, :] = jnp.zeros((N_KV_HEADS, WINDOW, LANES), bf16)

    @pl.when(first & (b == 0))
    def _():
        cnt_ref[...] = jnp.zeros(cnt_ref.shape, f32)

    x = x_ref[...]
    mod = mod_ref[...]
    sh1, sc1, g1 = mod[0:1, :], mod[1:2, :], mod[2:3, :]
    sh2, sc2, g2 = mod[3:4, :], mod[4:5, :], mod[5:6, :]
    del g2

    ms = jnp.mean(x * x, axis=-1, keepdims=True)
    h = (x * lax.rsqrt(ms + EPS)) * (gmix_ref[...] * (1.0 + sc1)) + sh1
    hb = h.astype(bf16)

    ab = _dot(hb, win_ref[:, 0:2 * D_CONV])
    u = ab[:, 0:D_CONV] * jax.nn.sigmoid(ab[:, D_CONV:2 * D_CONV])
    uext[HALO:HALO + TM, :] = u
    for r in range(1, 8):
        ush[r - 1] = uext[r:r + TM + CONV_SPAN, :]
    n_q = D_Q // PROJ_CHUNK
    n_g = D_MODEL // PROJ_CHUNK
    n_jobs = (win_ref.shape[1] - 2 * D_CONV) // PROJ_CHUNK
    n_rb = TM // CONV_ROWS
    n_t8 = CONV_ROWS // 8
    units = [(cb, rg) for cb in range(D_CONV // LANES) for rg in range(n_rb)]
    job_at = {(i * len(units)) // n_jobs: i for i in range(n_jobs)}
    assert len(job_at) == n_jobs
    for ui, (cb, rg) in enumerate(units):
        if ui in job_at:
            c_lo = 2 * D_CONV + job_at[ui] * PROJ_CHUNK
            proj_sc[job_at[ui]] = _dot(hb, win_ref[:, c_lo:c_lo + PROJ_CHUNK])
        cs = slice(cb * LANES, (cb + 1) * LANES)
        base = rg * CONV_ROWS
        accs = [None] * n_t8
        for r in range(8):
            taps = [(j, (j + 2) // 8) for j in range(CONV_WIDTH) if (j + 2) % 8 == r]
            t_lo = min(a for _, a in taps)
            t_hi = max(a for _, a in taps) + n_t8
            tiles = {}
            for t8 in range(t_lo, t_hi):
                rows = slice(base + 8 * t8, base + 8 * t8 + 8)
                tiles[t8] = uext[rows, cs] if r == 0 else ush[r - 1, rows, cs]
            for j, a in taps:
                tap = dw_ref[j, :, cs]
                for s8 in range(n_t8):
                    term = tap * tiles[a + s8]
                    accs[s8] = term if accs[s8] is None else accs[s8] + term
        for s8 in range(n_t8):
            conv_sc[base + 8 * s8:base + 8 * s8 + 8, cs] = accs[s8] + dwb_ref[:, cs]
    conv = conv_sc[...]
    uext[0:HALO, :] = uext[TM:TM + HALO, :]
    mu = jnp.mean(conv, axis=-1, keepdims=True)
    dc = conv - mu
    var = jnp.mean(dc * dc, axis=-1, keepdims=True)
    yn = dc * lax.rsqrt(var + EPS) * lng_ref[...] + lnb_ref[...]
    act = yn * jax.nn.sigmoid(yn)
    y_conv = _dot(act.astype(bf16), wco_ref[...])

    q = jnp.concatenate([proj_sc[i] for i in range(n_q)], axis=1)
    k = proj_sc[n_q, :, 0:D_KV]
    v = proj_sc[n_q, :, D_KV:2 * D_KV]

    def head_norm(z, blk_ref, g):
        sq = (z * z).astype(bf16)
        blk = blk_ref.shape[0]
        msq = jnp.concatenate([_dot(sq[:, c:c + blk], blk_ref[...]) for c in range(0, z.shape[1], blk)], axis=1)
        return z * lax.rsqrt(msq + EPS) * g

    qn = head_norm(q, bq_ref, qg_ref[...]).astype(bf16)
    kn = head_norm(k, bk_ref, kg_ref[...])
    lo_half = lax.broadcasted_iota(i32, (TM, LANES), 1) < HEAD_DIM
    kr = pltpu.roll(kn, HEAD_DIM, 1)
    vr = pltpu.roll(v, HEAD_DIM, 1)
    k2[0, WINDOW:WINDOW + TM, :] = jnp.where(lo_half, kn, kr).astype(bf16)
    k2[1, WINDOW:WINDOW + TM, :] = jnp.where(lo_half, kr, kn).astype(bf16)
    v2[0, WINDOW:WINDOW + TM, :] = jnp.where(lo_half, v, vr).astype(bf16)
    v2[1, WINDOW:WINDOW + TM, :] = jnp.where(lo_half, vr, v).astype(bf16)

    QROWS = Q_PER_KV * WINDOW
    col = lax.broadcasted_iota(i32, (QROWS, 2 * WINDOW), 1)
    has_prev = col >= jnp.where(first, WINDOW, 0)
    hrow = lax.broadcasted_iota(i32, (QROWS, 1), 0) // WINDOW
    lo128 = lax.broadcasted_iota(i32, (WINDOW, LANES), 1) < HEAD_DIM
    zero_q = jnp.zeros((WINDOW, LANES), bf16)
    for g in range(N_KV_HEADS):
        sink = LOG2E * jnp.where(hrow == 0, sink_ref[4 * g],
                                 jnp.where(hrow == 1, sink_ref[4 * g + 1],
                                           jnp.where(hrow == 2, sink_ref[4 * g + 2], sink_ref[4 * g + 3])))
        bias_g = bias_ref[g]
        for j in range(TM // WINDOW):
            rs = slice(j * WINDOW, (j + 1) * WINDOW)
            qa = qn[rs, 2 * LANES * g:2 * LANES * g + LANES]
            qb = qn[rs, 2 * LANES * g + LANES:2 * LANES * (g + 1)]
            qs = jnp.concatenate([jnp.where(lo128, qa, zero_q), jnp.where(lo128, zero_q, qa),
                                  jnp.where(lo128, qb, zero_q), jnp.where(lo128, zero_q, qb)], axis=0)
            kk = k2[g, j * WINDOW:(j + 2) * WINDOW, :]
            s = lax.dot_general(qs, kk, (((1,), (1,)), ((), ())), preferred_element_type=f32)
            logits = jnp.where(has_prev, s + bias_g, NEG_INF) if j == 0 else s + bias_g
            m = jnp.maximum(jnp.max(logits, axis=-1, keepdims=True), sink)
            p = jnp.exp2(logits - m)
            den = jnp.sum(p, axis=-1, keepdims=True) + jnp.exp2(sink - m)
            o2 = _dot(p.astype(bf16), v2[g, j * WINDOW:(j + 2) * WINDOW, :]) * (1.0 / den)
            osc[rs, 2 * LANES * g:2 * LANES * g + LANES] = jnp.where(
                lo128, o2[0:WINDOW], o2[WINDOW:2 * WINDOW]).astype(bf16)
            osc[rs, 2 * LANES * g + LANES:2 * LANES * (g + 1)] = jnp.where(
                lo128, o2[2 * WINDOW:3 * WINDOW], o2[3 * WINDOW:4 * WINDOW]).astype(bf16)
    k2[:, 0:WINDOW, :] = k2[:, TM:TM + WINDOW, :]
    v2[:, 0:WINDOW, :] = v2[:, TM:TM + WINDOW, :]
    y_attn = _dot(osc[...], wao_ref[...])

    merged = []
    for i in range(n_g):
        cs = slice(i * PROJ_CHUNK, (i + 1) * PROJ_CHUNK)
        g_conv = jax.nn.sigmoid(proj_sc[n_q + 1 + i])
        g_attn = jax.nn.sigmoid(proj_sc[n_q + 1 + n_g + i])
        merged.append((g_conv * y_conv[:, cs] + g_attn * y_attn[:, cs]).astype(bf16))
    merged = jnp.concatenate(merged, axis=1)
    x1 = x + g1 * _dot(merged, wout_ref[...])
    x1_ref[...] = x1

    ms2 = jnp.mean(x1 * x1, axis=-1, keepdims=True)
    h2 = (x1 * lax.rsqrt(ms2 + EPS)) * (gffn_ref[...] * (1.0 + sc2)) + sh2
    h2_ref[...] = _pack_halves(h2)
    hh, hl = _split(h2)
    hw = _dot(hh, wr2_ref[...])
    lg = hw[:, 0:ROUTER_COLS] + hw[:, ROUTER_COLS:2 * ROUTER_COLS] + _dot(hl, wrh_ref[...]) + br_ref[...]
    lt = lg.T
    gl = lt[0:N_GROUPS, :]
    grow = lax.broadcasted_iota(i32, (N_GROUPS, TM), 0)
    gmax = jnp.max(gl, axis=0, keepdims=True)
    gi = jnp.min(jnp.where(gl == gmax, grow, N_GROUPS), axis=0, keepdims=True)
    p_top = 1.0 / jnp.sum(jnp.exp(gl - gmax), axis=0, keepdims=True)
    sel = lt[EXPERT_COL0:EXPERT_COL0 + EXPERTS_PER_GROUP, :]
    for gg in range(1, N_GROUPS):
        lo_r = EXPERT_COL0 + gg * EXPERTS_PER_GROUP
        sel = jnp.where(gi == gg, lt[lo_r:lo_r + EXPERTS_PER_GROUP, :], sel)
    erow = lax.broadcasted_iota(i32, (EXPERTS_PER_GROUP, TM), 0)
    m1 = jnp.max(sel, axis=0, keepdims=True)
    i1 = jnp.min(jnp.where(sel == m1, erow, EXPERTS_PER_GROUP), axis=0, keepdims=True)
    rest = jnp.where(erow == i1, -jnp.inf, sel)
    m2 = jnp.max(rest, axis=0, keepdims=True)
    i2 = jnp.min(jnp.where(rest == m2, erow, EXPERTS_PER_GROUP), axis=0, keepdims=True)
    z = jnp.sum(jnp.exp(sel - m1), axis=0, keepdims=True)
    v1 = 1.0 / z
    v2nd = jnp.exp(m2 - m1) / z
    w1 = v1 / (v1 + v2nd) * p_top
    w2 = v2nd / (v1 + v2nd) * p_top
    e1 = gi * EXPERTS_PER_GROUP + i1
    e2 = gi * EXPERTS_PER_GROUP + i2

    xrow = lax.broadcasted_iota(i32, (N_EXPERTS, TM), 0)
    oh1 = xrow == e1
    oh2 = xrow == e2
    both = jnp.where(oh1 | oh2, 1.0, 0.0)
    prefix = _dot(both.astype(bf16), tri_ref[...]) + cnt_ref[:, 0:1]
    r1 = jnp.sum(jnp.where(oh1, prefix, 0.0), axis=0, keepdims=True)
    r2 = jnp.sum(jnp.where(oh2, prefix, 0.0), axis=0, keepdims=True)
    cnt_ref[...] = cnt_ref[...] + jnp.sum(both, axis=1, keepdims=True)

    route_ref[...] = jnp.concatenate(
        [e1, e2, r1.astype(i32), r2.astype(i32), jnp.zeros((4, TM), i32)], axis=0)
    wpad = jnp.concatenate([w1, w2, jnp.zeros((LANES - 2, TM), f32)], axis=0)
    wtok_ref[...] = wpad.T


def _mixer(x, mod6, p, b0, B):
    _, T, D = x.shape
    TM = min(SEQ_TILE, T)
    nt = T // TM
    N = B * T
    const = lambda shape: pl.BlockSpec(shape, lambda b, t: (0,) * len(shape))
    in_specs = [
        pl.BlockSpec((None, TM, D), lambda b, t: (b0 + b, t, 0)),
        pl.BlockSpec((None, 6, D), lambda b, t: (b0 + b, 0, 0)),
        const((1, D)), const((1, D)),
        const(p["w_in"].shape),
        const((CONV_WIDTH, 8, D_CONV)), const((8, D_CONV)), const((1, D_CONV)), const((1, D_CONV)),
        const((D_CONV, D)), const((D_Q, D)), const((D, D)),
        const((1, D_Q)), const((1, D_KV)),
        const((NORM_BLK, NORM_BLK)), const((D_KV, D_KV)),
        const((N_KV_HEADS, Q_PER_KV * WINDOW, 2 * WINDOW)),
        pl.BlockSpec(memory_space=pltpu.SMEM),
        const((D, ROUTER_COLS)), const((D, 2 * ROUTER_COLS)), const((1, ROUTER_COLS)),
        const((TM, TM)),
    ]
    out_specs = [
        pl.BlockSpec((None, TM, D), lambda b, t: (b, t, 0)),
        pl.BlockSpec((None, TM, D // 2), lambda b, t: (b, t, 0)),
        pl.BlockSpec((8, TM), lambda b, t: (0, b * nt + t)),
        pl.BlockSpec((TM, LANES), lambda b, t: (b * nt + t, 0)),
        pl.BlockSpec((N_EXPERTS, LANES), lambda b, t: (0, 0)),
    ]
    out_shape = [
        jax.ShapeDtypeStruct((B, T, D), f32),
        jax.ShapeDtypeStruct((B, T, D // 2), i32),
        jax.ShapeDtypeStruct((8, N), i32),
        jax.ShapeDtypeStruct((N, LANES), f32),
        jax.ShapeDtypeStruct((N_EXPERTS, LANES), f32),
    ]
    scratch = [
        pltpu.VMEM((TM + 32, D_CONV), f32),
        pltpu.VMEM((7, TM + CONV_SPAN, D_CONV), f32),
        pltpu.VMEM((TM, D_CONV), f32),
        pltpu.VMEM(((p["w_in"].shape[1] - 2 * D_CONV) // PROJ_CHUNK, TM, PROJ_CHUNK), f32),
        pltpu.VMEM((N_KV_HEADS, TM + WINDOW, LANES), bf16),
        pltpu.VMEM((N_KV_HEADS, TM + WINDOW, LANES), bf16),
        pltpu.VMEM((TM, D_Q), bf16),
    ]
    tri = jnp.asarray(np.triu(np.ones((TM, TM), np.float32), 1), bf16)
    return pl.pallas_call(
        _mixer_kernel,
        grid=(B, nt),
        in_specs=in_specs, out_specs=out_specs, out_shape=out_shape, scratch_shapes=scratch,
        compiler_params=pltpu.CompilerParams(
            dimension_semantics=("arbitrary", "arbitrary"), vmem_limit_bytes=VMEM_LIMIT),
        name="mixer_router",
    )(x, mod6, p["gmix"], p["gffn"], p["w_in"], p["dw"], p["dwb"], p["lng"], p["lnb"],
      p["wco"], p["wao"], p["wout"], p["qg"], p["kg"], p["bq"], p["bk"], p["bias"], p["sinks"],
      p["wrh"], p["wr2"], p["br"], tri)


def _dest_kernel(pstart_ref, route_ref, dest_ref):
    e = route_ref[0:2, :]
    base = jnp.zeros(e.shape, i32)
    for x in range(N_EXPERTS):
        base = jnp.where(e == x, pstart_ref[x], base)
    dest_ref[...] = base + route_ref[2:4, :]


def _dest_slots(pstart, route):
    N = route.shape[1]
    TS = min(DEST_TILE, N)
    return pl.pallas_call(
        _dest_kernel,
        grid=(N // TS,),
        in_specs=[pl.BlockSpec(memory_space=pltpu.SMEM),
                  pl.BlockSpec((8, TS), lambda i: (0, i))],
        out_specs=pl.BlockSpec((2, TS), lambda i: (0, i)),
        out_shape=jax.ShapeDtypeStruct((2, N), i32),
        name="dest_slots",
    )(pstart, route)


def _sc_mesh():
    return plsc.VectorSubcoreMesh(core_axis_name="c", subcore_axis_name="s")


def _sc_scatter(h2, dest, n_slots):
    N, D = h2.shape
    per_w = N // SC_WORKERS
    C = SC_CHUNK
    n_chunks = per_w // C
    dest4 = dest.reshape(TOP_K, SC_WORKERS, n_chunks, C)

    @functools.partial(
        pl.kernel, mesh=_sc_mesh(),
        out_type=jax.ShapeDtypeStruct((n_slots, D), h2.dtype),
        scratch_types=[pltpu.VMEM((n_chunks, C), i32), pltpu.VMEM((n_chunks, C), i32),
                       pltpu.VMEM((2, C, D), h2.dtype),
                       pltpu.SemaphoreType.DMA((2,)), pltpu.SemaphoreType.DMA((2,))],
        name="sc_dispatch_scatter",
    )
    def run(h2_hbm, dest_hbm, xs_hbm, idx0, idx1, buf, sem_in, sem_out):
        wid = lax.axis_index("s") * SC_CORES + lax.axis_index("c")
        base = wid * per_w
        pltpu.sync_copy(dest_hbm.at[0, wid], idx0)
        pltpu.sync_copy(dest_hbm.at[1, wid], idx1)

        def load(j, b):
            return pltpu.make_async_copy(h2_hbm.at[pl.ds(base + j * C, C)], buf.at[b], sem_in.at[b])

        def put(j, b, idx):
            return pltpu.make_async_copy(buf.at[b], xs_hbm.at[idx.at[j]], sem_out.at[b])

        load(0, 0).start()

        @pl.loop(0, n_chunks, step=2)
        def _(j0):
            for b in range(2):
                j = j0 + b
                load(j, b).wait()

                @pl.when(j >= 1)
                def _():
                    put(j - 1, 1 - b, idx0).wait()
                    put(j - 1, 1 - b, idx1).wait()

                @pl.when(j + 1 < n_chunks)
                def _():
                    load(j + 1, 1 - b).start()

                put(j, b, idx0).start()
                put(j, b, idx1).start()

        put(n_chunks - 1, 1, idx0).wait()
        put(n_chunks - 1, 1, idx1).wait()

    return run(h2, dest4)


def _sc_gather(y, dest, N):
    D = y.shape[1]
    per_w = N // SC_WORKERS
    C = SC_CHUNK
    n_chunks = per_w // C
    dest4 = dest.reshape(TOP_K, SC_WORKERS, n_chunks, C)

    @functools.partial(
        pl.kernel, mesh=_sc_mesh(),
        out_type=jax.ShapeDtypeStruct((TOP_K, N, D), y.dtype),
        scratch_types=[pltpu.VMEM((n_chunks, C), i32), pltpu.VMEM((n_chunks, C), i32),
                       pltpu.VMEM((2, C, D), y.dtype),
                       pltpu.SemaphoreType.DMA((2,)), pltpu.SemaphoreType.DMA((2,))],
        name="sc_combine_gather",
    )
    def run(y_hbm, dest_hbm, yg_hbm, idx0, idx1, buf, sem_in, sem_out):
        wid = lax.axis_index("s") * SC_CORES + lax.axis_index("c")
        base = wid * per_w
        pltpu.sync_copy(dest_hbm.at[0, wid], idx0)
        pltpu.sync_copy(dest_hbm.at[1, wid], idx1)
        idx = (idx0, idx1)

        def get(j, k):
            return pltpu.make_async_copy(y_hbm.at[idx[k].at[j]], buf.at[k], sem_in.at[k])

        def put(j, k):
            return pltpu.make_async_copy(buf.at[k], yg_hbm.at[k, pl.ds(base + j * C, C)], sem_out.at[k])

        get(0, 0).start()
        get(0, 1).start()

        @pl.loop(0, n_chunks)
        def _(j):
            for k in range(TOP_K):
                get(j, k).wait()
                put(j, k).start()
            for k in range(TOP_K):
                put(j, k).wait()

                @pl.when(j + 1 < n_chunks)
                def _():
                    get(j + 1, k).start()

    return run(y, dest4)


def _expert_kernel(be_ref, nv_ref, nu_ref, first_ref, slot_ref, nxt_ref,
                   xs_ref, wg_hbm, wu_hbm, wdn_hbm, y_ref,
                   stg_g, stg_u, stg_d, wgu_ref, wd_ref, sems):
    i = pl.program_id(0)
    in_use = i < nu_ref[0]

    def fetch(e, s):
        return (pltpu.make_async_copy(wg_hbm.at[e], stg_g.at[s], sems.at[s, 0]),
                pltpu.make_async_copy(wu_hbm.at[e], stg_u.at[s], sems.at[s, 1]),
                pltpu.make_async_copy(wdn_hbm.at[e], stg_d.at[s], sems.at[s, 2]))

    @pl.when(in_use & (first_ref[i] == 1))
    def _():
        s = slot_ref[i]

        @pl.when(i == 0)
        def _():
            for cp in fetch(be_ref[0], 0):
                cp.start()

        for cp in fetch(be_ref[i], s):
            cp.wait()
        wgu_ref[:, 0:D_EXPERT] = stg_g[s].astype(bf16)
        wgu_ref[:, D_EXPERT:2 * D_EXPERT] = stg_u[s].astype(bf16)
        wd_ref[...] = stg_d[s].astype(bf16)

        @pl.when(nxt_ref[i] >= 0)
        def _():
            for cp in fetch(nxt_ref[i], 1 - s):
                cp.start()

    @pl.when(in_use)
    def _():
        live = lax.broadcasted_iota(i32, (MOE_BLOCK, 1), 0) < nv_ref[i]
        x_hi, x_lo = _unpack_halves(jnp.where(live, xs_ref[...], 0))
        half = x_hi.shape[1]
        gu = (_dot(x_hi.astype(bf16), wgu_ref[0:half, :]) +
              _dot(x_lo.astype(bf16), wgu_ref[half:2 * half, :]))
        gate = gu[:, 0:D_EXPERT]
        hid = (gate * jax.nn.sigmoid(gate)) * gu[:, D_EXPERT:2 * D_EXPERT]
        y_ref[...] = _pack_halves(_dot(hid.astype(bf16), wd_ref[...]))


def _experts(block_e, n_valid, n_used, counts, xs, w_gate, w_up, w_down):
    n_slots, DH = xs.shape
    D = 2 * DH
    n_blocks = n_slots // MOE_BLOCK
    idx = jnp.arange(n_blocks, dtype=i32)
    prev_e = jnp.concatenate([jnp.full((1,), -1, i32), block_e[:-1]])
    first = ((block_e != prev_e) & (idx < n_used[0])).astype(i32)
    slot = ((jnp.cumsum(first) - 1) % 2).astype(i32)
    e_ids = jnp.arange(N_EXPERTS, dtype=i32)
    later = (e_ids[None, :] > e_ids[:, None]) & (counts[None, :] > 0)
    nxt_of_e = jnp.min(jnp.where(later, e_ids[None, :], N_EXPERTS), axis=1)
    nxt = jnp.where(nxt_of_e[block_e] < N_EXPERTS, nxt_of_e[block_e], -1).astype(i32)
    last = lambda i, nu: jnp.minimum(i, nu[0] - 1)
    slot_map = lambda i, be, nv, nu, fi, sl, nx: (last(i, nu), 0)
    grid_spec = pltpu.PrefetchScalarGridSpec(
        num_scalar_prefetch=6,
        grid=(n_blocks,),
        in_specs=[pl.BlockSpec((MOE_BLOCK, DH), slot_map),
                  pl.BlockSpec(memory_space=pl.ANY),
                  pl.BlockSpec(memory_space=pl.ANY),
                  pl.BlockSpec(memory_space=pl.ANY)],
        out_specs=pl.BlockSpec((MOE_BLOCK, DH), slot_map),
        scratch_shapes=[pltpu.VMEM((2, D, D_EXPERT), f32), pltpu.VMEM((2, D, D_EXPERT), f32),
                        pltpu.VMEM((2, D_EXPERT, D), f32),
                        pltpu.VMEM((D, 2 * D_EXPERT), bf16), pltpu.VMEM((D_EXPERT, D), bf16),
                        pltpu.SemaphoreType.DMA((2, 3))],
    )
    return pl.pallas_call(
        _expert_kernel,
        grid_spec=grid_spec,
        out_shape=jax.ShapeDtypeStruct((n_slots, DH), i32),
        compiler_params=pltpu.CompilerParams(dimension_semantics=("arbitrary",),
                                             vmem_limit_bytes=VMEM_LIMIT),
        name="experts",
    )(block_e, n_valid, n_used, first, slot, nxt, xs, w_gate, w_up, w_down)


def _combine_kernel(yg_ref, x1_ref, wtok_ref, mod_ref, *rest):
    o_ref = rest[-1]
    w = wtok_ref[...]
    g2 = mod_ref[5:6, :]
    a_hi, a_lo = _unpack_halves(yg_ref[0])
    b_hi, b_lo = _unpack_halves(yg_ref[1])
    moe = jnp.concatenate([w[:, 0:1] * a_hi + w[:, 1:2] * b_hi,
                           w[:, 0:1] * a_lo + w[:, 1:2] * b_lo], axis=1)
    o_ref[...] = x1_ref[...] + g2 * moe


def _combine(yg, x1, wtok, mod6, T, b0, n_total, out_prev):
    Nc, D = x1.shape
    TS = min(COMBINE_TILE, T)
    per_seq = T // TS
    blk0 = b0 * per_seq
    in_specs = [pl.BlockSpec((TOP_K, TS, D // 2), lambda i: (0, i, 0)),
                pl.BlockSpec((TS, D), lambda i: (i, 0)),
                pl.BlockSpec((TS, LANES), lambda i: (i, 0)),
                pl.BlockSpec((None, 6, D), lambda i: (b0 + i // per_seq, 0, 0))]
    args = [yg, x1, wtok, mod6]
    aliases = {}
    if out_prev is not None:
        in_specs.append(pl.BlockSpec(memory_space=pl.ANY))
        args.append(out_prev)
        aliases = {len(args) - 1: 0}
    return pl.pallas_call(
        _combine_kernel,
        grid=(Nc // TS,),
        in_specs=in_specs,
        out_specs=pl.BlockSpec((TS, D), lambda i: (blk0 + i, 0)),
        out_shape=jax.ShapeDtypeStruct((n_total, D), f32),
        input_output_aliases=aliases,
        compiler_params=pltpu.CompilerParams(dimension_semantics=("arbitrary",),
                                             vmem_limit_bytes=VMEM_LIMIT),
        name="combine",
    )(*args)


def _block_diag_mean(n, blk):
    m = np.kron(np.eye(n // blk, dtype=np.float32), np.full((blk, blk), 1.0 / blk, np.float32))
    return jnp.asarray(m, bf16)


def _layer(x, mod6, bias, l, w):
    B, T, D = x.shape
    N = B * T
    w_rg, w_re = w["w_router_group"][l], w["w_router_expert"][l]
    wr = jnp.zeros((D, ROUTER_COLS), f32)
    wr = wr.at[:, 0:N_GROUPS].set(w_rg).at[:, EXPERT_COL0:EXPERT_COL0 + N_EXPERTS].set(w_re)
    br = jnp.zeros((1, ROUTER_COLS), f32)
    br = br.at[0, 0:N_GROUPS].set(w["b_router_group"][l])
    br = br.at[0, EXPERT_COL0:EXPERT_COL0 + N_EXPERTS].set(w["b_router_expert"][l])
    wrh = wr.astype(bf16)
    wrl = (wr - wrh.astype(f32)).astype(bf16)
    p = dict(
        gmix=w["norm_mix_g"][l].reshape(1, D), gffn=w["norm_ffn_g"][l].reshape(1, D),
        w_in=w["w_in"][l].astype(bf16),
        dw=jnp.broadcast_to(w["dw_kernel"][l][:, None, :], (CONV_WIDTH, 8, D_CONV)), dwb=jnp.broadcast_to(w["dw_bias"][l][None, :], (8, D_CONV)),
        lng=w["conv_ln_g"][l].reshape(1, D_CONV), lnb=w["conv_ln_b"][l].reshape(1, D_CONV),
        wco=w["w_conv_out"][l].astype(bf16), wao=w["w_attn_out"][l].astype(bf16),
        wout=w["w_out"][l].astype(bf16),
        qg=(jnp.tile(w["q_norm_g"][l], N_Q_HEADS) * (HEAD_DIM ** -0.5 * LOG2E)).reshape(1, D_Q),
        kg=jnp.tile(w["k_norm_g"][l], N_KV_HEADS).reshape(1, D_KV),
        bq=_block_diag_mean(NORM_BLK, HEAD_DIM), bk=_block_diag_mean(D_KV, HEAD_DIM),
        bias=bias, sinks=w["sinks"][l], wrh=wrh, wr2=jnp.concatenate([wrh, wrl], axis=1), br=br,
    )
    w_gate, w_up, w_down = w["w_exp_gate"][l], w["w_exp_up"][l], w["w_exp_down"][l]

    n_chunks = MOE_CHUNKS if B % MOE_CHUNKS == 0 else 1
    Bc = B // n_chunks
    Nc = Bc * T
    n_blocks = -(-(Nc * TOP_K) // MOE_BLOCK) + N_EXPERTS
    blk0 = jnp.arange(n_blocks, dtype=i32) * MOE_BLOCK
    stage = []
    for ch in range(n_chunks):
        x1, h2, route, wtok, cnt = _mixer(x, mod6, p, ch * Bc, Bc)
        counts = cnt[:, 0].astype(i32)
        pcounts = (counts + MOE_BLOCK - 1) // MOE_BLOCK * MOE_BLOCK
        pend = jnp.cumsum(pcounts)
        pstart = pend - pcounts
        block_e = jnp.minimum(jnp.sum((pend[None, :] <= blk0[:, None]).astype(i32), axis=1), N_EXPERTS - 1)
        n_valid = jnp.clip((pstart + counts)[block_e] - blk0, 0, MOE_BLOCK).astype(i32)
        n_used = (pend[-1:] // MOE_BLOCK).astype(i32)
        dest = _dest_slots(pstart, route)
        xs = _sc_scatter(h2.reshape(Nc, D // 2), dest, n_blocks * MOE_BLOCK)
        stage.append((x1, wtok, dest, xs, block_e, n_valid, n_used, counts))
    ys = [_experts(be, nv, nu, cn, xs, w_gate, w_up, w_down) for (_, _, _, xs, be, nv, nu, cn) in stage]
    ygs = [_sc_gather(y, st[2], Nc) for y, st in zip(ys, stage)]
    out = None
    for ch in range(n_chunks):
        x1, wtok = stage[ch][0], stage[ch][1]
        out = _combine(ygs[ch], x1.reshape(Nc, D), wtok, mod6, T, ch * Bc, B * T, out)
    return out.reshape(B, T, D)


def kernel(x, c, w_ada, b_ada, norm_mix_g, w_in, dw_kernel, dw_bias, conv_ln_g, conv_ln_b,
           w_conv_out, q_norm_g, k_norm_g, sinks, w_attn_out, w_out, rel_bias_table, norm_ffn_g,
           w_router_group, b_router_group, w_router_expert, b_router_expert,
           w_exp_gate, w_exp_up, w_exp_down):
    w = dict(norm_mix_g=norm_mix_g, w_in=w_in, dw_kernel=dw_kernel, dw_bias=dw_bias,
             conv_ln_g=conv_ln_g, conv_ln_b=conv_ln_b, w_conv_out=w_conv_out, q_norm_g=q_norm_g,
             k_norm_g=k_norm_g, sinks=sinks, w_attn_out=w_attn_out, w_out=w_out,
             norm_ffn_g=norm_ffn_g, w_router_group=w_router_group, b_router_group=b_router_group,
             w_router_expert=w_router_expert, b_router_expert=b_router_expert,
             w_exp_gate=w_exp_gate, w_exp_up=w_exp_up, w_exp_down=w_exp_down)
    B = x.shape[0]
    bias = _bias_band(rel_bias_table).reshape(N_KV_HEADS, Q_PER_KV * WINDOW, 2 * WINDOW)
    for l in range(w_ada.shape[0]):
        mod6 = _modulation(c, w_ada[l], b_ada[l]).reshape(B, 6, D_MODEL)
        x = _layer(x, mod6, bias, l, w)
    return x
```

```python
import functools
import math

import jax
import jax.numpy as jnp
import numpy as np
from jax import lax
from jax.experimental import pallas as pl
from jax.experimental.pallas import tpu as pltpu
from jax.experimental.pallas import tpu_sc as plsc

D_MODEL = 1024
D_CONV = 512
CONV_WIDTH = 31
N_Q_HEADS = 8
N_KV_HEADS = 2
HEAD_DIM = 64
Q_PER_KV = N_Q_HEADS // N_KV_HEADS
WINDOW = 128
N_BUCKETS = 32
MAX_DISTANCE = 128
N_GROUPS = 4
EXPERTS_PER_GROUP = 8
N_EXPERTS = N_GROUPS * EXPERTS_PER_GROUP
TOP_K = 2
D_EXPERT = 256
D_Q = N_Q_HEADS * HEAD_DIM
D_KV = N_KV_HEADS * HEAD_DIM
EPS = 1e-6
NEG_INF = -1e30
LOG2E = math.log2(math.e)

LANES = 128
SEQ_TILE = 512
CONV_ROWS = 64
NORM_BLK = 256
PROJ_CHUNK = 256
CONV_SPAN = 24
MOE_BLOCK = 512
MOE_CHUNKS = 2
DEST_TILE = 8192
COMBINE_TILE = 1024
SC_CORES = 2
SC_SUBCORES = 16
SC_WORKERS = SC_CORES * SC_SUBCORES
SC_CHUNK = 64
ROUTER_COLS = LANES
EXPERT_COL0 = 8
VMEM_LIMIT = 56 * 1024 * 1024

f32 = jnp.float32
bf16 = jnp.bfloat16
i32 = jnp.int32


def _dot(a, b):
    return jnp.dot(a, b, preferred_element_type=f32)


def _split(a):
    hi = a.astype(bf16)
    lo = (a - hi.astype(f32)).astype(bf16)
    return hi, lo


def _pack_halves(x):
    c = x.shape[1] // 2
    hi = lax.bitcast_convert_type(x[:, 0:c].astype(bf16).astype(f32), jnp.uint32)
    lo = lax.bitcast_convert_type(x[:, c:2 * c].astype(bf16).astype(f32), jnp.uint32)
    word = (hi & jnp.uint32(0xFFFF0000)) | (lo >> jnp.uint32(16))
    return lax.bitcast_convert_type(word, i32)


def _unpack_halves(word):
    u = lax.bitcast_convert_type(word, jnp.uint32)
    hi = lax.bitcast_convert_type(u & jnp.uint32(0xFFFF0000), f32)
    lo = lax.bitcast_convert_type(u << jnp.uint32(16), f32)
    return hi, lo


def _dot3(a, b):
    ah, al = _split(a)
    bh, bl = _split(b)
    return _dot(ah, bh) + _dot(al, bh) + _dot(ah, bl)


def _mod_kernel(c_ref, w_ref, b_ref, o_ref):
    c = c_ref[...]
    s = c * jax.nn.sigmoid(c)
    o_ref[...] = _dot3(s, w_ref[...]) + b_ref[...]


def _modulation(c, w_ada, b_ada):
    B, D = c.shape
    n_out = w_ada.shape[1]
    return pl.pallas_call(
        _mod_kernel,
        grid=(n_out // D,),
        in_specs=[pl.BlockSpec((B, D), lambda j: (0, 0)),
                  pl.BlockSpec((D, D), lambda j: (0, j)),
                  pl.BlockSpec((1, D), lambda j: (0, j))],
        out_specs=pl.BlockSpec((B, D), lambda j: (0, j)),
        out_shape=jax.ShapeDtypeStruct((B, n_out), f32),
        name="modulation",
    )(c, w_ada, b_ada.reshape(1, n_out))


def _band_buckets():
    qi = np.arange(WINDOW)[:, None]
    kj = np.arange(2 * WINDOW)[None, :]
    dist = np.clip(qi + WINDOW - kj, 0, MAX_DISTANCE)
    max_exact = N_BUCKETS // 2
    d = np.maximum(dist, 1).astype(np.float32)
    large = max_exact + (np.log(d / np.float32(max_exact)) / np.float32(math.log(MAX_DISTANCE / max_exact))
                         * np.float32(N_BUCKETS - max_exact)).astype(np.int32)
    large = np.minimum(large, N_BUCKETS - 1)
    bucket = np.where(dist < max_exact, dist, large).astype(np.int32)
    raw = qi + WINDOW - kj
    return np.where((raw >= 0) & (raw < WINDOW), bucket, -1).astype(np.int32)


def _bias_kernel(tab_ref, bucket_ref, o_ref):
    bk = bucket_ref[...]
    for h in range(N_Q_HEADS):
        acc = jnp.full(bk.shape, NEG_INF, f32)
        for b in range(N_BUCKETS):
            acc = jnp.where(bk == b, tab_ref[b, h] * LOG2E, acc)
        o_ref[h] = acc


def _bias_band(rel_bias_table):
    return pl.pallas_call(
        _bias_kernel,
        in_specs=[pl.BlockSpec(memory_space=pltpu.SMEM),
                  pl.BlockSpec(memory_space=pltpu.VMEM)],
        out_specs=pl.BlockSpec(memory_space=pltpu.VMEM),
        out_shape=jax.ShapeDtypeStruct((N_Q_HEADS, WINDOW, 2 * WINDOW), f32),
        name="bias_band",
    )(rel_bias_table, jnp.asarray(_band_buckets()))


def _mixer_kernel(x_ref, mod_ref, gmix_ref, gffn_ref, win_ref, dw_ref, dwb_ref, lng_ref, lnb_ref,
                  wco_ref, wao_ref, wout_ref, qg_ref, kg_ref, bq_ref, bk_ref, bias_ref, sink_ref,
                  wrh_ref, wr2_ref, br_ref, tri_ref,
                  x1_ref, h2_ref, route_ref, wtok_ref, cnt_ref,
                  uext, ush, conv_sc, proj_sc, k2, v2, osc):
    TM = x_ref.shape[0]
    HALO = 32
    b = pl.program_id(0)
    t = pl.program_id(1)
    first = t == 0

    @pl.when(first)
    def _():
        uext[0:HALO, :] = jnp.zeros((HALO, D_CONV), f32)
        k2[:, 0:WINDOW, :] = jnp.zeros((N_KV_HEADS, WINDOW, LANES), bf16)
        v2[:, 0:WINDOW, :] = jnp.zeros((N_KV_HEADS, WINDOW, LANES), bf16)

    @pl.when(first & (b == 0))
    def _():
        cnt_ref[...] = jnp.zeros(cnt_ref.shape, f32)

    x = x_ref[...]
    mod = mod_ref[...]
    sh1, sc1, g1 = mod[0:1, :], mod[1:2, :], mod[2:3, :]
    sh2, sc2, g2 = mod[3:4, :], mod[4:5, :], mod[5:6, :]
    del g2

    ms = jnp.mean(x * x, axis=-1, keepdims=True)
    h = (x * lax.rsqrt(ms + EPS)) * (gmix_ref[...] * (1.0 + sc1)) + sh1
    hb = h.astype(bf16)

    ab = _dot(hb, win_ref[:, 0:2 * D_CONV])
    u = ab[:, 0:D_CONV] * jax.nn.sigmoid(ab[:, D_CONV:2 * D_CONV])
    uext[HALO:HALO + TM, :] = u
    for r in range(1, 8):
        ush[r - 1] = uext[r:r + TM + CONV_SPAN, :]
    n_q = D_Q // PROJ_CHUNK
    n_g = D_MODEL // PROJ_CHUNK
    n_jobs = (win_ref.shape[1] - 2 * D_CONV) // PROJ_CHUNK
    n_rb = TM // CONV_ROWS
    n_t8 = CONV_ROWS // 8
    units = [(cb, rg) for cb in range(D_CONV // LANES) for rg in range(n_rb)]
    job_at = {(i * len(units)) // n_jobs: i for i in range(n_jobs)}
    assert len(job_at) == n_jobs
    for ui, (cb, rg) in enumerate(units):
        if ui in job_at:
            c_lo = 2 * D_CONV + job_at[ui] * PROJ_CHUNK
            proj_sc[job_at[ui]] = _dot(hb, win_ref[:, c_lo:c_lo + PROJ_CHUNK])
        cs = slice(cb * LANES, (cb + 1) * LANES)
        base = rg * CONV_ROWS
        accs = [None] * n_t8
        for r in range(8):
            taps = [(j, (j + 2) // 8) for j in range(CONV_WIDTH) if (j + 2) % 8 == r]
            t_lo = min(a for _, a in taps)
            t_hi = max(a for _, a in taps) + n_t8
            tiles = {}
            for t8 in range(t_lo, t_hi):
                rows = slice(base + 8 * t8, base + 8 * t8 + 8)
                tiles[t8] = uext[rows, cs] if r == 0 else ush[r - 1, rows, cs]
            for j, a in taps:
                tap = dw_ref[j, :, cs]
                for s8 in range(n_t8):
                    term = tap * tiles[a + s8]
                    accs[s8] = term if accs[s8] is None else accs[s8] + term
        for s8 in range(n_t8):
            conv_sc[base + 8 * s8:base + 8 * s8 + 8, cs] = accs[s8] + dwb_ref[:, cs]
    conv = conv_sc[...]
    uext[0:HALO, :] = uext[TM:TM + HALO, :]
    mu = jnp.mean(conv, axis=-1, keepdims=True)
    dc = conv - mu
    var = jnp.mean(dc * dc, axis=-1, keepdims=True)
    yn = dc * lax.rsqrt(var + EPS) * lng_ref[...] + lnb_ref[...]
    act = yn * jax.nn.sigmoid(yn)
    y_conv = _dot(act.astype(bf16), wco_ref[...])

    q = jnp.concatenate([proj_sc[i] for i in range(n_q)], axis=1)
    k = proj_sc[n_q, :, 0:D_KV]
    v = proj_sc[n_q, :, D_KV:2 * D_KV]

    def head_norm(z, blk_ref, g):
        sq = (z * z).astype(bf16)
        blk = blk_ref.shape[0]
        msq = jnp.concatenate([_dot(sq[:, c:c + blk], blk_ref[...]) for c in range(0, z.shape[1], blk)], axis=1)
        return z * lax.rsqrt(msq + EPS) * g

    qn = head_norm(q, bq_ref, qg_ref[...]).astype(bf16)
    kn = head_norm(k, bk_ref, kg_ref[...])
    lo_half = lax.broadcasted_iota(i32, (TM, LANES), 1) < HEAD_DIM
    kr = pltpu.roll(kn, HEAD_DIM, 1)
    vr = pltpu.roll(v, HEAD_DIM, 1)
    k2[0, WINDOW:WINDOW + TM, :] = jnp.where(lo_half, kn, kr).astype(bf16)
    k2[1, WINDOW:WINDOW + TM, :] = jnp.where(lo_half, kr, kn).astype(bf16)
    v2[0, WINDOW:WINDOW + TM, :] = jnp.where(lo_half, v, vr).astype(bf16)
    v2[1, WINDOW:WINDOW + TM, :] = jnp.where(lo_half, vr, v).astype(bf16)

    QROWS = Q_PER_KV * WINDOW
    col = lax.broadcasted_iota(i32, (QROWS, 2 * WINDOW), 1)
    has_prev = col >= jnp.where(first, WINDOW, 0)
    hrow = lax.broadcasted_iota(i32, (QROWS, 1), 0) // WINDOW
    lo128 = lax.broadcasted_iota(i32, (WINDOW, LANES), 1) < HEAD_DIM
    zero_q = jnp.zeros((WINDOW, LANES), bf16)
    for g in range(N_KV_HEADS):
        sink = LOG2E * jnp.where(hrow == 0, sink_ref[4 * g],
                                 jnp.where(hrow == 1, sink_ref[4 * g + 1],
                                           jnp.where(hrow == 2, sink_ref[4 * g + 2], sink_ref[4 * g + 3])))
        bias_g = bias_ref[g]
        for j in range(TM // WINDOW):
            rs = slice(j * WINDOW, (j + 1) * WINDOW)
            qa = qn[rs, 2 * LANES * g:2 * LANES * g + LANES]
            qb = qn[rs, 2 * LANES * g + LANES:2 * LANES * (g + 1)]
            qs = jnp.concatenate([jnp.where(lo128, qa, zero_q), jnp.where(lo128, zero_q, qa),
                                  jnp.where(lo128, qb, zero_q), jnp.where(lo128, zero_q, qb)], axis=0)
            kk = k2[g, j * WINDOW:(j + 2) * WINDOW, :]
            s = lax.dot_general(qs, kk, (((1,), (1,)), ((), ())), preferred_element_type=f32)
            logits = jnp.where(has_prev, s + bias_g, NEG_INF) if j == 0 else s + bias_g
            m = jnp.maximum(jnp.max(logits, axis=-1, keepdims=True), sink)
            p = jnp.exp2(logits - m)
            den = jnp.sum(p, axis=-1, keepdims=True) + jnp.exp2(sink - m)
            o2 = _dot(p.astype(bf16), v2[g, j * WINDOW:(j + 2) * WINDOW, :]) * (1.0 / den)
            osc[rs, 2 * LANES * g:2 * LANES * g + LANES] = jnp.where(
                lo128, o2[0:WINDOW], o2[WINDOW:2 * WINDOW]).astype(bf16)
            osc[rs, 2 * LANES * g + LANES:2 * LANES * (g + 1)] = jnp.where(
                lo128, o2[2 * WINDOW:3 * WINDOW], o2[3 * WINDOW:4 * WINDOW]).astype(bf16)
    k2[:, 0:WINDOW, :] = k2[:, TM:TM + WINDOW, :]
    v2[:, 0:WINDOW, :] = v2[:, TM:TM + WINDOW, :]
    y_attn = _dot(osc[...], wao_ref[...])

    merged = []
    for i in range(n_g):
        cs = slice(i * PROJ_CHUNK, (i + 1) * PROJ_CHUNK)
        g_conv = jax.nn.sigmoid(proj_sc[n_q + 1 + i])
        g_attn = jax.nn.sigmoid(proj_sc[n_q + 1 + n_g + i])
        merged.append((g_conv * y_conv[:, cs] + g_attn * y_attn[:, cs]).astype(bf16))
    merged = jnp.concatenate(merged, axis=1)
    x1 = x + g1 * _dot(merged, wout_ref[...])
    x1_ref[...] = x1

    ms2 = jnp.mean(x1 * x1, axis=-1, keepdims=True)
    h2 = (x1 * lax.rsqrt(ms2 + EPS)) * (gffn_ref[...] * (1.0 + sc2)) + sh2
    h2_ref[...] = _pack_halves(h2)
    hh, hl = _split(h2)
    hw = _dot(hh, wr2_ref[...])
    lg = hw[:, 0:ROUTER_COLS] + hw[:, ROUTER_COLS:2 * ROUTER_COLS] + _dot(hl, wrh_ref[...]) + br_ref[...]
    lt = lg.T
    gl = lt[0:N_GROUPS, :]
    grow = lax.broadcasted_iota(i32, (N_GROUPS, TM), 0)
    gmax = jnp.max(gl, axis=0, keepdims=True)
    gi = jnp.min(jnp.where(gl == gmax, grow, N_GROUPS), axis=0, keepdims=True)
    p_top = 1.0 / jnp.sum(jnp.exp(gl - gmax), axis=0, keepdims=True)
    sel = lt[EXPERT_COL0:EXPERT_COL0 + EXPERTS_PER_GROUP, :]
    for gg in range(1, N_GROUPS):
        lo_r = EXPERT_COL0 + gg * EXPERTS_PER_GROUP
        sel = jnp.where(gi == gg, lt[lo_r:lo_r + EXPERTS_PER_GROUP, :], sel)
    erow = lax.broadcasted_iota(i32, (EXPERTS_PER_GROUP, TM), 0)
    m1 = jnp.max(sel, axis=0, keepdims=True)
    i1 = jnp.min(jnp.where(sel == m1, erow, EXPERTS_PER_GROUP), axis=0, keepdims=True)
    rest = jnp.where(erow == i1, -jnp.inf, sel)
    m2 = jnp.max(rest, axis=0, keepdims=True)
    i2 = jnp.min(jnp.where(rest == m2, erow, EXPERTS_PER_GROUP), axis=0, keepdims=True)
    z = jnp.sum(jnp.exp(sel - m1), axis=0, keepdims=True)
    v1 = 1.0 / z
    v2nd = jnp.exp(m2 - m1) / z
    w1 = v1 / (v1 + v2nd) * p_top
    w2 = v2nd / (v1 + v2nd) * p_top
    e1 = gi * EXPERTS_PER_GROUP + i1
    e2 = gi * EXPERTS_PER_GROUP + i2

    xrow = lax.broadcasted_iota(i32, (N_EXPERTS, TM), 0)
    oh1 = xrow == e1
    oh2 = xrow == e2
    both = jnp.where(oh1 | oh2, 1.0, 0.0)
    prefix = _dot(both.astype(bf16), tri_ref[...]) + cnt_ref[:, 0:1]
    r1 = jnp.sum(jnp.where(oh1, prefix, 0.0), axis=0, keepdims=True)
    r2 = jnp.sum(jnp.where(oh2, prefix, 0.0), axis=0, keepdims=True)
    cnt_ref[...] = cnt_ref[...] + jnp.sum(both, axis=1, keepdims=True)

    route_ref[...] = jnp.concatenate(
        [e1, e2, r1.astype(i32), r2.astype(i32), jnp.zeros((4, TM), i32)], axis=0)
    wpad = jnp.concatenate([w1, w2, jnp.zeros((LANES - 2, TM), f32)], axis=0)
    wtok_ref[...] = wpad.T


def _mixer(x, mod6, p, b0, B):
    _, T, D = x.shape
    TM = min(SEQ_TILE, T)
    nt = T // TM
    N = B * T
    const = lambda shape: pl.BlockSpec(shape, lambda b, t: (0,) * len(shape))
    in_specs = [
        pl.BlockSpec((None, TM, D), lambda b, t: (b0 + b, t, 0)),
        pl.BlockSpec((None, 6, D), lambda b, t: (b0 + b, 0, 0)),
        const((1, D)), const((1, D)),
        const(p["w_in"].shape),
        const((CONV_WIDTH, 8, D_CONV)), const((8, D_CONV)), const((1, D_CONV)), const((1, D_CONV)),
        const((D_CONV, D)), const((D_Q, D)), const((D, D)),
        const((1, D_Q)), const((1, D_KV)),
        const((NORM_BLK, NORM_BLK)), const((D_KV, D_KV)),
        const((N_KV_HEADS, Q_PER_KV * WINDOW, 2 * WINDOW)),
        pl.BlockSpec(memory_space=pltpu.SMEM),
        const((D, ROUTER_COLS)), const((D, 2 * ROUTER_COLS)), const((1, ROUTER_COLS)),
        const((TM, TM)),
    ]
    out_specs = [
        pl.BlockSpec((None, TM, D), lambda b, t: (b, t, 0)),
        pl.BlockSpec((None, TM, D // 2), lambda b, t: (b, t, 0)),
        pl.BlockSpec((8, TM), lambda b, t: (0, b * nt + t)),
        pl.BlockSpec((TM, LANES), lambda b, t: (b * nt + t, 0)),
        pl.BlockSpec((N_EXPERTS, LANES), lambda b, t: (0, 0)),
    ]
    out_shape = [
        jax.ShapeDtypeStruct((B, T, D), f32),
        jax.ShapeDtypeStruct((B, T, D // 2), i32),
        jax.ShapeDtypeStruct((8, N), i32),
        jax.ShapeDtypeStruct((N, LANES), f32),
        jax.ShapeDtypeStruct((N_EXPERTS, LANES), f32),
    ]
    scratch = [
        pltpu.VMEM((TM + 32, D_CONV), f32),
        pltpu.VMEM((7, TM + CONV_SPAN, D_CONV), f32),
        pltpu.VMEM((TM, D_CONV), f32),
        pltpu.VMEM(((p["w_in"].shape[1] - 2 * D_CONV) // PROJ_CHUNK, TM, PROJ_CHUNK), f32),
        pltpu.VMEM((N_KV_HEADS, TM + WINDOW, LANES), bf16),
        pltpu.VMEM((N_KV_HEADS, TM + WINDOW, LANES), bf16),
        pltpu.VMEM((TM, D_Q), bf16),
    ]
    tri = jnp.asarray(np.triu(np.ones((TM, TM), np.float32), 1), bf16)
    return pl.pallas_call(
        _mixer_kernel,
        grid=(B, nt),
        in_specs=in_specs, out_specs=out_specs, out_shape=out_shape, scratch_shapes=scratch,
        compiler_params=pltpu.CompilerParams(
            dimension_semantics=("arbitrary", "arbitrary"), vmem_limit_bytes=VMEM_LIMIT),
        name="mixer_router",
    )(x, mod6, p["gmix"], p["gffn"], p["w_in"], p["dw"], p["dwb"], p["lng"], p["lnb"],
      p["wco"], p["wao"], p["wout"], p["qg"], p["kg"], p["bq"], p["bk"], p["bias"], p["sinks"],
      p["wrh"], p["wr2"], p["br"], tri)


def _dest_kernel(pstart_ref, route_ref, dest_ref):
    e = route_ref[0:2, :]
    base = jnp.zeros(e.shape, i32)
    for x in range(N_EXPERTS):
        base = jnp.where(e == x, pstart_ref[x], base)
    dest_ref[...] = base + route_ref[2:4, :]


def _dest_slots(pstart, route):
    N = route.shape[1]
    TS = min(DEST_TILE, N)
    return pl.pallas_call(
        _dest_kernel,
        grid=(N // TS,),
        in_specs=[pl.BlockSpec(memory_space=pltpu.SMEM),
                  pl.BlockSpec((8, TS), lambda i: (0, i))],
        out_specs=pl.BlockSpec((2, TS), lambda i: (0, i)),
        out_shape=jax.ShapeDtypeStruct((2, N), i32),
        name="dest_slots",
    )(pstart, route)


def _sc_mesh():
    return plsc.VectorSubcoreMesh(core_axis_name="c", subcore_axis_name="s")


def _sc_scatter(h2, dest, n_slots):
    N, D = h2.shape
    per_w = N // SC_WORKERS
    C = SC_CHUNK
    n_chunks = per_w // C
    dest4 = dest.reshape(TOP_K, SC_WORKERS, n_chunks, C)

    @functools.partial(
        pl.kernel, mesh=_sc_mesh(),
        out_type=jax.ShapeDtypeStruct((n_slots, D), h2.dtype),
        scratch_types=[pltpu.VMEM((n_chunks, C), i32), pltpu.VMEM((n_chunks, C), i32),
                       pltpu.VMEM((2, C, D), h2.dtype),
                       pltpu.SemaphoreType.DMA((2,)), pltpu.SemaphoreType.DMA((2,))],
        name="sc_dispatch_scatter",
    )
    def run(h2_hbm, dest_hbm, xs_hbm, idx0, idx1, buf, sem_in, sem_out):
        wid = lax.axis_index("s") * SC_CORES + lax.axis_index("c")
        base = wid * per_w
        pltpu.sync_copy(dest_hbm.at[0, wid], idx0)
        pltpu.sync_copy(dest_hbm.at[1, wid], idx1)

        def load(j, b):
            return pltpu.make_async_copy(h2_hbm.at[pl.ds(base + j * C, C)], buf.at[b], sem_in.at[b])

        def put(j, b, idx):
            return pltpu.make_async_copy(buf.at[b], xs_hbm.at[idx.at[j]], sem_out.at[b])

        load(0, 0).start()

        @pl.loop(0, n_chunks, step=2)
        def _(j0):
            for b in range(2):
                j = j0 + b
                load(j, b).wait()

                @pl.when(j >= 1)
                def _():
                    put(j - 1, 1 - b, idx0).wait()
                    put(j - 1, 1 - b, idx1).wait()

                @pl.when(j + 1 < n_chunks)
                def _():
                    load(j + 1, 1 - b).start()

                put(j, b, idx0).start()
                put(j, b, idx1).start()

        put(n_chunks - 1, 1, idx0).wait()
        put(n_chunks - 1, 1, idx1).wait()

    return run(h2, dest4)


def _sc_gather(y, dest, N):
    D = y.shape[1]
    per_w = N // SC_WORKERS
    C = SC_CHUNK
    n_chunks = per_w // C
    dest4 = dest.reshape(TOP_K, SC_WORKERS, n_chunks, C)

    @functools.partial(
        pl.kernel, mesh=_sc_mesh(),
        out_type=jax.ShapeDtypeStruct((TOP_K, N, D), y.dtype),
        scratch_types=[pltpu.VMEM((n_chunks, C), i32), pltpu.VMEM((n_chunks, C), i32),
                       pltpu.VMEM((2, C, D), y.dtype),
                       pltpu.SemaphoreType.DMA((2,)), pltpu.SemaphoreType.DMA((2,))],
        name="sc_combine_gather",
    )
    def run(y_hbm, dest_hbm, yg_hbm, idx0, idx1, buf, sem_in, sem_out):
        wid = lax.axis_index("s") * SC_CORES + lax.axis_index("c")
        base = wid * per_w
        pltpu.sync_copy(dest_hbm.at[0, wid], idx0)
        pltpu.sync_copy(dest_hbm.at[1, wid], idx1)
        idx = (idx0, idx1)

        def get(j, k):
            return pltpu.make_async_copy(y_hbm.at[idx[k].at[j]], buf.at[k], sem_in.at[k])

        def put(j, k):
            return pltpu.make_async_copy(buf.at[k], yg_hbm.at[k, pl.ds(base + j * C, C)], sem_out.at[k])

        get(0, 0).start()
        get(0, 1).start()

        @pl.loop(0, n_chunks)
        def _(j):
            for k in range(TOP_K):
                get(j, k).wait()
                put(j, k).start()
            for k in range(TOP_K):
                put(j, k).wait()

                @pl.when(j + 1 < n_chunks)
                def _():
                    get(j + 1, k).start()

    return run(y, dest4)


def _expert_kernel(be_ref, nv_ref, nu_ref, first_ref, slot_ref, nxt_ref,
                   xs_ref, wg_hbm, wu_hbm, wdn_hbm, y_ref,
                   stg_g, stg_u, stg_d, wgu_ref, wd_ref, sems):
    i = pl.program_id(0)
    in_use = i < nu_ref[0]

    def fetch(e, s):
        return (pltpu.make_async_copy(wg_hbm.at[e], stg_g.at[s], sems.at[s, 0]),
                pltpu.make_async_copy(wu_hbm.at[e], stg_u.at[s], sems.at[s, 1]),
                pltpu.make_async_copy(wdn_hbm.at[e], stg_d.at[s], sems.at[s, 2]))

    @pl.when(in_use & (first_ref[i] == 1))
    def _():
        s = slot_ref[i]

        @pl.when(i == 0)
        def _():
            for cp in fetch(be_ref[0], 0):
                cp.start()

        for cp in fetch(be_ref[i], s):
            cp.wait()
        wgu_ref[:, 0:D_EXPERT] = stg_g[s].astype(bf16)
        wgu_ref[:, D_EXPERT:2 * D_EXPERT] = stg_u[s].astype(bf16)
        wd_ref[...] = stg_d[s].astype(bf16)

        @pl.when(nxt_ref[i] >= 0)
        def _():
            for cp in fetch(nxt_ref[i], 1 - s):
                cp.start()

    @pl.when(in_use)
    def _():
        live = lax.broadcasted_iota(i32, (MOE_BLOCK, 1), 0) < nv_ref[i]
        x_hi, x_lo = _unpack_halves(jnp.where(live, xs_ref[...], 0))
        half = x_hi.shape[1]
        gu = (_dot(x_hi.astype(bf16), wgu_ref[0:half, :]) +
              _dot(x_lo.astype(bf16), wgu_ref[half:2 * half, :]))
        gate = gu[:, 0:D_EXPERT]
        hid = (gate * jax.nn.sigmoid(gate)) * gu[:, D_EXPERT:2 * D_EXPERT]
        y_ref[...] = _pack_halves(_dot(hid.astype(bf16), wd_ref[...]))


def _experts(block_e, n_valid, n_used, counts, xs, w_gate, w_up, w_down):
    n_slots, DH = xs.shape
    D = 2 * DH
    n_blocks = n_slots // MOE_BLOCK
    idx = jnp.arange(n_blocks, dtype=i32)
    prev_e = jnp.concatenate([jnp.full((1,), -1, i32), block_e[:-1]])
    first = ((block_e != prev_e) & (idx < n_used[0])).astype(i32)
    slot = ((jnp.cumsum(first) - 1) % 2).astype(i32)
    e_ids = jnp.arange(N_EXPERTS, dtype=i32)
    later = (e_ids[None, :] > e_ids[:, None]) & (counts[None, :] > 0)
    nxt_of_e = jnp.min(jnp.where(later, e_ids[None, :], N_EXPERTS), axis=1)
    nxt = jnp.where(nxt_of_e[block_e] < N_EXPERTS, nxt_of_e[block_e], -1).astype(i32)
    last = lambda i, nu: jnp.minimum(i, nu[0] - 1)
    slot_map = lambda i, be, nv, nu, fi, sl, nx: (last(i, nu), 0)
    grid_spec = pltpu.PrefetchScalarGridSpec(
        num_scalar_prefetch=6,
        grid=(n_blocks,),
        in_specs=[pl.BlockSpec((MOE_BLOCK, DH), slot_map),
                  pl.BlockSpec(memory_space=pl.ANY),
                  pl.BlockSpec(memory_space=pl.ANY),
                  pl.BlockSpec(memory_space=pl.ANY)],
        out_specs=pl.BlockSpec((MOE_BLOCK, DH), slot_map),
        scratch_shapes=[pltpu.VMEM((2, D, D_EXPERT), f32), pltpu.VMEM((2, D, D_EXPERT), f32),
                        pltpu.VMEM((2, D_EXPERT, D), f32),
                        pltpu.VMEM((D, 2 * D_EXPERT), bf16), pltpu.VMEM((D_EXPERT, D), bf16),
                        pltpu.SemaphoreType.DMA((2, 3))],
    )
    return pl.pallas_call(
        _expert_kernel,
        grid_spec=grid_spec,
        out_shape=jax.ShapeDtypeStruct((n_slots, DH), i32),
        compiler_params=pltpu.CompilerParams(dimension_semantics=("arbitrary",),
                                             vmem_limit_bytes=VMEM_LIMIT),
        name="experts",
    )(block_e, n_valid, n_used, first, slot, nxt, xs, w_gate, w_up, w_down)


def _combine_kernel(yg_ref, x1_ref, wtok_ref, mod_ref, *rest):
    o_ref = rest[-1]
    w = wtok_ref[...]
    g2 = mod_ref[5:6, :]
    a_hi, a_lo = _unpack_halves(yg_ref[0])
    b_hi, b_lo = _unpack_halves(yg_ref[1])
    moe = jnp.concatenate([w[:, 0:1] * a_hi + w[:, 1:2] * b_hi,
                           w[:, 0:1] * a_lo + w[:, 1:2] * b_lo], axis=1)
    o_ref[...] = x1_ref[...] + g2 * moe


def _combine(yg, x1, wtok, mod6, T, b0, n_total, out_prev):
    Nc, D = x1.shape
    TS = min(COMBINE_TILE, T)
    per_seq = T // TS
    blk0 = b0 * per_seq
    in_specs = [pl.BlockSpec((TOP_K, TS, D // 2), lambda i: (0, i, 0)),
                pl.BlockSpec((TS, D), lambda i: (i, 0)),
                pl.BlockSpec((TS, LANES), lambda i: (i, 0)),
                pl.BlockSpec((None, 6, D), lambda i: (b0 + i // per_seq, 0, 0))]
    args = [yg, x1, wtok, mod6]
    aliases = {}
    if out_prev is not None:
        in_specs.append(pl.BlockSpec(memory_space=pl.ANY))
        args.append(out_prev)
        aliases = {len(args) - 1: 0}
    return pl.pallas_call(
        _combine_kernel,
        grid=(Nc // TS,),
        in_specs=in_specs,
        out_specs=pl.BlockSpec((TS, D), lambda i: (blk0 + i, 0)),
        out_shape=jax.ShapeDtypeStruct((n_total, D), f32),
        input_output_aliases=aliases,
        compiler_params=pltpu.CompilerParams(dimension_semantics=("arbitrary",),
                                             vmem_limit_bytes=VMEM_LIMIT),
        name="combine",
    )(*args)


def _block_diag_mean(n, blk):
    m = np.kron(np.eye(n // blk, dtype=np.float32), np.full((blk, blk), 1.0 / blk, np.float32))
    return jnp.asarray(m, bf16)


def _layer(x, mod6, bias, l, w):
    B, T, D = x.shape
    N = B * T
    w_rg, w_re = w["w_router_group"][l], w["w_router_expert"][l]
    wr = jnp.zeros((D, ROUTER_COLS), f32)
    wr = wr.at[:, 0:N_GROUPS].set(w_rg).at[:, EXPERT_COL0:EXPERT_COL0 + N_EXPERTS].set(w_re)
    br = jnp.zeros((1, ROUTER_COLS), f32)
    br = br.at[0, 0:N_GROUPS].set(w["b_router_group"][l])
    br = br.at[0, EXPERT_COL0:EXPERT_COL0 + N_EXPERTS].set(w["b_router_expert"][l])
    wrh = wr.astype(bf16)
    wrl = (wr - wrh.astype(f32)).astype(bf16)
    p = dict(
        gmix=w["norm_mix_g"][l].reshape(1, D), gffn=w["norm_ffn_g"][l].reshape(1, D),
        w_in=w["w_in"][l].astype(bf16),
        dw=jnp.broadcast_to(w["dw_kernel"][l][:, None, :], (CONV_WIDTH, 8, D_CONV)), dwb=jnp.broadcast_to(w["dw_bias"][l][None, :], (8, D_CONV)),
        lng=w["conv_ln_g"][l].reshape(1, D_CONV), lnb=w["conv_ln_b"][l].reshape(1, D_CONV),
        wco=w["w_conv_out"][l].astype(bf16), wao=w["w_attn_out"][l].astype(bf16),
        wout=w["w_out"][l].astype(bf16),
        qg=(jnp.tile(w["q_norm_g"][l], N_Q_HEADS) * (HEAD_DIM ** -0.5 * LOG2E)).reshape(1, D_Q),
        kg=jnp.tile(w["k_norm_g"][l], N_KV_HEADS).reshape(1, D_KV),
        bq=_block_diag_mean(NORM_BLK, HEAD_DIM), bk=_block_diag_mean(D_KV, HEAD_DIM),
        bias=bias, sinks=w["sinks"][l], wrh=wrh, wr2=jnp.concatenate([wrh, wrl], axis=1), br=br,
    )
    w_gate, w_up, w_down = w["w_exp_gate"][l], w["w_exp_up"][l], w["w_exp_down"][l]

    n_chunks = MOE_CHUNKS if B % MOE_CHUNKS == 0 else 1
    Bc = B // n_chunks
    Nc = Bc * T
    n_blocks = -(-(Nc * TOP_K) // MOE_BLOCK) + N_EXPERTS
    blk0 = jnp.arange(n_blocks, dtype=i32) * MOE_BLOCK
    stage = []
    for ch in range(n_chunks):
        x1, h2, route, wtok, cnt = _mixer(x, mod6, p, ch * Bc, Bc)
        counts = cnt[:, 0].astype(i32)
        pcounts = (counts + MOE_BLOCK - 1) // MOE_BLOCK * MOE_BLOCK
        pend = jnp.cumsum(pcounts)
        pstart = pend - pcounts
        block_e = jnp.minimum(jnp.sum((pend[None, :] <= blk0[:, None]).astype(i32), axis=1), N_EXPERTS - 1)
        n_valid = jnp.clip((pstart + counts)[block_e] - blk0, 0, MOE_BLOCK).astype(i32)
        n_used = (pend[-1:] // MOE_BLOCK).astype(i32)
        dest = _dest_slots(pstart, route)
        xs = _sc_scatter(h2.reshape(Nc, D // 2), dest, n_blocks * MOE_BLOCK)
        stage.append((x1, wtok, dest, xs, block_e, n_valid, n_used, counts))
    ys = [_experts(be, nv, nu, cn, xs, w_gate, w_up, w_down) for (_, _, _, xs, be, nv, nu, cn) in stage]
    ygs = [_sc_gather(y, st[2], Nc) for y, st in zip(ys, stage)]
    out = None
    for ch in range(n_chunks):
        x1, wtok = stage[ch][0], stage[ch][1]
        out = _combine(ygs[ch], x1.reshape(Nc, D), wtok, mod6, T, ch * Bc, B * T, out)
    return out.reshape(B, T, D)


def kernel(x, c, w_ada, b_ada, norm_mix_g, w_in, dw_kernel, dw_bias, conv_ln_g, conv_ln_b,
           w_conv_out, q_norm_g, k_norm_g, sinks, w_attn_out, w_out, rel_bias_table, norm_ffn_g,
           w_router_group, b_router_group, w_router_expert, b_router_expert,
           w_exp_gate, w_exp_up, w_exp_down):
    w = dict(norm_mix_g=norm_mix_g, w_in=w_in, dw_kernel=dw_kernel, dw_bias=dw_bias,
             conv_ln_g=conv_ln_g, conv_ln_b=conv_ln_b, w_conv_out=w_conv_out, q_norm_g=q_norm_g,
             k_norm_g=k_norm_g, sinks=sinks, w_attn_out=w_attn_out, w_out=w_out,
             norm_ffn_g=norm_ffn_g, w_router_group=w_router_group, b_router_group=b_router_group,
             w_router_expert=w_router_expert, b_router_expert=b_router_expert,
             w_exp_gate=w_exp_gate, w_exp_up=w_exp_up, w_exp_down=w_exp_down)
    B = x.shape[0]
    bias = _bias_band(rel_bias_table).reshape(N_KV_HEADS, Q_PER_KV * WINDOW, 2 * WINDOW)
    for l in range(w_ada.shape[0]):
        mod6 = _modulation(c, w_ada[l], b_ada[l]).reshape(B, 6, D_MODEL)
        x = _layer(x, mod6, bias, l, w)
    return x
```

```python
import functools
import math

import jax
import jax.numpy as jnp
import numpy as np
from jax import lax
from jax.experimental import pallas as pl
from jax.experimental.pallas import tpu as pltpu
from jax.experimental.pallas import tpu_sc as plsc

D_MODEL = 1024
D_CONV = 512
CONV_WIDTH = 31
N_Q_HEADS = 8
N_KV_HEADS = 2
HEAD_DIM = 64
Q_PER_KV = N_Q_HEADS // N_KV_HEADS
WINDOW = 128
N_BUCKETS = 32
MAX_DISTANCE = 128
N_GROUPS = 4
EXPERTS_PER_GROUP = 8
N_EXPERTS = N_GROUPS * EXPERTS_PER_GROUP
TOP_K = 2
D_EXPERT = 256
D_Q = N_Q_HEADS * HEAD_DIM
D_KV = N_KV_HEADS * HEAD_DIM
EPS = 1e-6
NEG_INF = -1e30
LOG2E = math.log2(math.e)

LANES = 128
SEQ_TILE = 512
CONV_ROWS = 64
NORM_BLK = 256
PROJ_CHUNK = 256
CONV_SPAN = 24
MOE_BLOCK = 512
MOE_CHUNKS = 4
DEST_TILE = 8192
COMBINE_TILE = 1024
SC_CORES = 2
SC_SUBCORES = 16
SC_WORKERS = SC_CORES * SC_SUBCORES
SC_CHUNK = 32
ROUTER_COLS = LANES
EXPERT_COL0 = 8
VMEM_LIMIT = 56 * 1024 * 1024

f32 = jnp.float32
bf16 = jnp.bfloat16
i32 = jnp.int32


def _dot(a, b):
    return jnp.dot(a, b, preferred_element_type=f32)


def _split(a):
    hi = a.astype(bf16)
    lo = (a - hi.astype(f32)).astype(bf16)
    return hi, lo


def _pack_halves(x):
    c = x.shape[1] // 2
    hi = lax.bitcast_convert_type(x[:, 0:c].astype(bf16).astype(f32), jnp.uint32)
    lo = lax.bitcast_convert_type(x[:, c:2 * c].astype(bf16).astype(f32), jnp.uint32)
    word = (hi & jnp.uint32(0xFFFF0000)) | (lo >> jnp.uint32(16))
    return lax.bitcast_convert_type(word, i32)


def _unpack_halves(word):
    u = lax.bitcast_convert_type(word, jnp.uint32)
    hi = lax.bitcast_convert_type(u & jnp.uint32(0xFFFF0000), f32)
    lo = lax.bitcast_convert_type(u << jnp.uint32(16), f32)
    return hi, lo


def _dot3(a, b):
    ah, al = _split(a)
    bh, bl = _split(b)
    return _dot(ah, bh) + _dot(al, bh) + _dot(ah, bl)


def _mod_kernel(c_ref, w_ref, b_ref, o_ref):
    c = c_ref[...]
    s = c * jax.nn.sigmoid(c)
    o_ref[...] = _dot3(s, w_ref[...]) + b_ref[...]


def _modulation(c, w_ada, b_ada):
    B, D = c.shape
    n_out = w_ada.shape[1]
    return pl.pallas_call(
        _mod_kernel,
        grid=(n_out // D,),
        in_specs=[pl.BlockSpec((B, D), lambda j: (0, 0)),
                  pl.BlockSpec((D, D), lambda j: (0, j)),
                  pl.BlockSpec((1, D), lambda j: (0, j))],
        out_specs=pl.BlockSpec((B, D), lambda j: (0, j)),
        out_shape=jax.ShapeDtypeStruct((B, n_out), f32),
        name="modulation",
    )(c, w_ada, b_ada.reshape(1, n_out))


def _band_buckets():
    qi = np.arange(WINDOW)[:, None]
    kj = np.arange(2 * WINDOW)[None, :]
    dist = np.clip(qi + WINDOW - kj, 0, MAX_DISTANCE)
    max_exact = N_BUCKETS // 2
    d = np.maximum(dist, 1).astype(np.float32)
    large = max_exact + (np.log(d / np.float32(max_exact)) / np.float32(math.log(MAX_DISTANCE / max_exact))
                         * np.float32(N_BUCKETS - max_exact)).astype(np.int32)
    large = np.minimum(large, N_BUCKETS - 1)
    bucket = np.where(dist < max_exact, dist, large).astype(np.int32)
    raw = qi + WINDOW - kj
    return np.where((raw >= 0) & (raw < WINDOW), bucket, -1).astype(np.int32)


def _bias_kernel(tab_ref, bucket_ref, o_ref):
    bk = bucket_ref[...]
    for h in range(N_Q_HEADS):
        acc = jnp.full(bk.shape, NEG_INF, f32)
        for b in range(N_BUCKETS):
            acc = jnp.where(bk == b, tab_ref[b, h] * LOG2E, acc)
        o_ref[h] = acc


def _bias_band(rel_bias_table):
    return pl.pallas_call(
        _bias_kernel,
        in_specs=[pl.BlockSpec(memory_space=pltpu.SMEM),
                  pl.BlockSpec(memory_space=pltpu.VMEM)],
        out_specs=pl.BlockSpec(memory_space=pltpu.VMEM),
        out_shape=jax.ShapeDtypeStruct((N_Q_HEADS, WINDOW, 2 * WINDOW), f32),
        name="bias_band",
    )(rel_bias_table, jnp.asarray(_band_buckets()))


def _mixer_kernel(x_ref, mod_ref, gmix_ref, gffn_ref, win_ref, dw_ref, dwb_ref, lng_ref, lnb_ref,
                  wco_ref, wao_ref, wout_ref, qg_ref, kg_ref, bq_ref, bk_ref, bias_ref, sink_ref,
                  wrh_ref, wr2_ref, br_ref, tri_ref,
                  x1_ref, h2_ref, route_ref, wtok_ref, cnt_ref,
                  uext, ush, conv_sc, proj_sc, k2, v2, osc):
    TM = x_ref.shape[0]
    HALO = 32
    b = pl.program_id(0)
    t = pl.program_id(1)
    first = t == 0

    @pl.when(first)
    def _():
        uext[0:HALO, :] = jnp.zeros((HALO, D_CONV), f32)
        k2[:, 0:WINDOW, :] = jnp.zeros((N_KV_HEADS, WINDOW, LANES), bf16)
        v2[:, 0:WINDOW, :] = jnp.zeros((N_KV_HEADS, WINDOW, LANES), bf16)

    @pl.when(first & (b == 0))
    def _():
        cnt_ref[...] = jnp.zeros(cnt_ref.shape, f32)

    x = x_ref[...]
    mod = mod_ref[...]
    sh1, sc1, g1 = mod[0:1, :], mod[1:2, :], mod[2:3, :]
    sh2, sc2, g2 = mod[3:4, :], mod[4:5, :], mod[5:6, :]
    del g2

    ms = jnp.mean(x * x, axis=-1, keepdims=True)
    h = (x * lax.rsqrt(ms + EPS)) * (gmix_ref[...] * (1.0 + sc1)) + sh1
    hb = h.astype(bf16)

    ab = _dot(hb, win_ref[:, 0:2 * D_CONV])
    u = ab[:, 0:D_CONV] * jax.nn.sigmoid(ab[:, D_CONV:2 * D_CONV])
    uext[HALO:HALO + TM, :] = u
    for r in range(1, 8):
        ush[r - 1] = uext[r:r + TM + CONV_SPAN, :]
    n_q = D_Q // PROJ_CHUNK
    n_g = D_MODEL // PROJ_CHUNK
    n_jobs = (win_ref.shape[1] - 2 * D_CONV) // PROJ_CHUNK
    n_rb = TM // CONV_ROWS
    n_t8 = CONV_ROWS // 8
    units = [(cb, rg) for cb in range(D_CONV // LANES) for rg in range(n_rb)]
    job_at = {(i * len(units)) // n_jobs: i for i in range(n_jobs)}
    assert len(job_at) == n_jobs
    for ui, (cb, rg) in enumerate(units):
        if ui in job_at:
            c_lo = 2 * D_CONV + job_at[ui] * PROJ_CHUNK
            proj_sc[job_at[ui]] = _dot(hb, win_ref[:, c_lo:c_lo + PROJ_CHUNK])
        cs = slice(cb * LANES, (cb + 1) * LANES)
        base = rg * CONV_ROWS
        accs = [None] * n_t8
        for r in range(8):
            taps = [(j, (j + 2) // 8) for j in range(CONV_WIDTH) if (j + 2) % 8 == r]
            t_lo = min(a for _, a in taps)
            t_hi = max(a for _, a in taps) + n_t8
            tiles = {}
            for t8 in range(t_lo, t_hi):
                rows = slice(base + 8 * t8, base + 8 * t8 + 8)
                tiles[t8] = uext[rows, cs] if r == 0 else ush[r - 1, rows, cs]
            for j, a in taps:
                tap = dw_ref[j, :, cs]
                for s8 in range(n_t8):
                    term = tap * tiles[a + s8]
                    accs[s8] = term if accs[s8] is None else accs[s8] + term
        for s8 in range(n_t8):
            conv_sc[base + 8 * s8:base + 8 * s8 + 8, cs] = accs[s8] + dwb_ref[:, cs]
    conv = conv_sc[...]
    uext[0:HALO, :] = uext[TM:TM + HALO, :]
    mu = jnp.mean(conv, axis=-1, keepdims=True)
    dc = conv - mu
    var = jnp.mean(dc * dc, axis=-1, keepdims=True)
    yn = dc * lax.rsqrt(var + EPS) * lng_ref[...] + lnb_ref[...]
    act = yn * jax.nn.sigmoid(yn)
    y_conv = _dot(act.astype(bf16), wco_ref[...])

    q = jnp.concatenate([proj_sc[i] for i in range(n_q)], axis=1)
    k = proj_sc[n_q, :, 0:D_KV]
    v = proj_sc[n_q, :, D_KV:2 * D_KV]

    def head_norm(z, blk_ref, g):
        sq = (z * z).astype(bf16)
        blk = blk_ref.shape[0]
        msq = jnp.concatenate([_dot(sq[:, c:c + blk], blk_ref[...]) for c in range(0, z.shape[1], blk)], axis=1)
        return z * lax.rsqrt(msq + EPS) * g

    qn = head_norm(q, bq_ref, qg_ref[...]).astype(bf16)
    kn = head_norm(k, bk_ref, kg_ref[...])
    lo_half = lax.broadcasted_iota(i32, (TM, LANES), 1) < HEAD_DIM
    kr = pltpu.roll(kn, HEAD_DIM, 1)
    vr = pltpu.roll(v, HEAD_DIM, 1)
    k2[0, WINDOW:WINDOW + TM, :] = jnp.where(lo_half, kn, kr).astype(bf16)
    k2[1, WINDOW:WINDOW + TM, :] = jnp.where(lo_half, kr, kn).astype(bf16)
    v2[0, WINDOW:WINDOW + TM, :] = jnp.where(lo_half, v, vr).astype(bf16)
    v2[1, WINDOW:WINDOW + TM, :] = jnp.where(lo_half, vr, v).astype(bf16)

    QROWS = Q_PER_KV * WINDOW
    col = lax.broadcasted_iota(i32, (QROWS, 2 * WINDOW), 1)
    has_prev = col >= jnp.where(first, WINDOW, 0)
    hrow = lax.broadcasted_iota(i32, (QROWS, 1), 0) // WINDOW
    lo128 = lax.broadcasted_iota(i32, (WINDOW, LANES), 1) < HEAD_DIM
    zero_q = jnp.zeros((WINDOW, LANES), bf16)
    for g in range(N_KV_HEADS):
        sink = LOG2E * jnp.where(hrow == 0, sink_ref[4 * g],
                                 jnp.where(hrow == 1, sink_ref[4 * g + 1],
                                           jnp.where(hrow == 2, sink_ref[4 * g + 2], sink_ref[4 * g + 3])))
        bias_g = bias_ref[g]
        for j in range(TM // WINDOW):
            rs = slice(j * WINDOW, (j + 1) * WINDOW)
            qa = qn[rs, 2 * LANES * g:2 * LANES * g + LANES]
            qb = qn[rs, 2 * LANES * g + LANES:2 * LANES * (g + 1)]
            qs = jnp.concatenate([jnp.where(lo128, qa, zero_q), jnp.where(lo128, zero_q, qa),
                                  jnp.where(lo128, qb, zero_q), jnp.where(lo128, zero_q, qb)], axis=0)
            kk = k2[g, j * WINDOW:(j + 2) * WINDOW, :]
            s = lax.dot_general(qs, kk, (((1,), (1,)), ((), ())), preferred_element_type=f32)
            logits = jnp.where(has_prev, s + bias_g, NEG_INF) if j == 0 else s + bias_g
            m = jnp.maximum(jnp.max(logits, axis=-1, keepdims=True), sink)
            p = jnp.exp2(logits - m)
            den = jnp.sum(p, axis=-1, keepdims=True) + jnp.exp2(sink - m)
            o2 = _dot(p.astype(bf16), v2[g, j * WINDOW:(j + 2) * WINDOW, :]) * (1.0 / den)
            osc[rs, 2 * LANES * g:2 * LANES * g + LANES] = jnp.where(
                lo128, o2[0:WINDOW], o2[WINDOW:2 * WINDOW]).astype(bf16)
            osc[rs, 2 * LANES * g + LANES:2 * LANES * (g + 1)] = jnp.where(
                lo128, o2[2 * WINDOW:3 * WINDOW], o2[3 * WINDOW:4 * WINDOW]).astype(bf16)
    k2[:, 0:WINDOW, :] = k2[:, TM:TM + WINDOW, :]
    v2[:, 0:WINDOW, :] = v2[:, TM:TM + WINDOW, :]
    y_attn = _dot(osc[...], wao_ref[...])

    merged = []
    for i in range(n_g):
        cs = slice(i * PROJ_CHUNK, (i + 1) * PROJ_CHUNK)
        g_conv = jax.nn.sigmoid(proj_sc[n_q + 1 + i])
        g_attn = jax.nn.sigmoid(proj_sc[n_q + 1 + n_g + i])
        merged.append((g_conv * y_conv[:, cs] + g_attn * y_attn[:, cs]).astype(bf16))
    merged = jnp.concatenate(merged, axis=1)
    x1 = x + g1 * _dot(merged, wout_ref[...])
    x1_ref[...] = x1

    ms2 = jnp.mean(x1 * x1, axis=-1, keepdims=True)
    h2 = (x1 * lax.rsqrt(ms2 + EPS)) * (gffn_ref[...] * (1.0 + sc2)) + sh2
    h2_ref[...] = _pack_halves(h2)
    hh, hl = _split(h2)
    hw = _dot(hh, wr2_ref[...])
    lg = hw[:, 0:ROUTER_COLS] + hw[:, ROUTER_COLS:2 * ROUTER_COLS] + _dot(hl, wrh_ref[...]) + br_ref[...]
    lt = lg.T
    gl = lt[0:N_GROUPS, :]
    grow = lax.broadcasted_iota(i32, (N_GROUPS, TM), 0)
    gmax = jnp.max(gl, axis=0, keepdims=True)
    gi = jnp.min(jnp.where(gl == gmax, grow, N_GROUPS), axis=0, keepdims=True)
    p_top = 1.0 / jnp.sum(jnp.exp(gl - gmax), axis=0, keepdims=True)
    sel = lt[EXPERT_COL0:EXPERT_COL0 + EXPERTS_PER_GROUP, :]
    for gg in range(1, N_GROUPS):
        lo_r = EXPERT_COL0 + gg * EXPERTS_PER_GROUP
        sel = jnp.where(gi == gg, lt[lo_r:lo_r + EXPERTS_PER_GROUP, :], sel)
    erow = lax.broadcasted_iota(i32, (EXPERTS_PER_GROUP, TM), 0)
    m1 = jnp.max(sel, axis=0, keepdims=True)
    i1 = jnp.min(jnp.where(sel == m1, erow, EXPERTS_PER_GROUP), axis=0, keepdims=True)
    rest = jnp.where(erow == i1, -jnp.inf, sel)
    m2 = jnp.max(rest, axis=0, keepdims=True)
    i2 = jnp.min(jnp.where(rest == m2, erow, EXPERTS_PER_GROUP), axis=0, keepdims=True)
    z = jnp.sum(jnp.exp(sel - m1), axis=0, keepdims=True)
    v1 = 1.0 / z
    v2nd = jnp.exp(m2 - m1) / z
    w1 = v1 / (v1 + v2nd) * p_top
    w2 = v2nd / (v1 + v2nd) * p_top
    e1 = gi * EXPERTS_PER_GROUP + i1
    e2 = gi * EXPERTS_PER_GROUP + i2

    xrow = lax.broadcasted_iota(i32, (N_EXPERTS, TM), 0)
    oh1 = xrow == e1
    oh2 = xrow == e2
    both = jnp.where(oh1 | oh2, 1.0, 0.0)
    prefix = _dot(both.astype(bf16), tri_ref[...]) + cnt_ref[:, 0:1]
    r1 = jnp.sum(jnp.where(oh1, prefix, 0.0), axis=0, keepdims=True)
    r2 = jnp.sum(jnp.where(oh2, prefix, 0.0), axis=0, keepdims=True)
    cnt_ref[...] = cnt_ref[...] + jnp.sum(both, axis=1, keepdims=True)

    route_ref[...] = jnp.concatenate(
        [e1, e2, r1.astype(i32), r2.astype(i32), jnp.zeros((4, TM), i32)], axis=0)
    wpad = jnp.concatenate([w1, w2, jnp.zeros((LANES - 2, TM), f32)], axis=0)
    wtok_ref[...] = wpad.T


def _mixer(x, mod6, p, b0, B):
    _, T, D = x.shape
    TM = min(SEQ_TILE, T)
    nt = T // TM
    N = B * T
    const = lambda shape: pl.BlockSpec(shape, lambda b, t: (0,) * len(shape))
    in_specs = [
        pl.BlockSpec((None, TM, D), lambda b, t: (b0 + b, t, 0)),
        pl.BlockSpec((None, 6, D), lambda b, t: (b0 + b, 0, 0)),
        const((1, D)), const((1, D)),
        const(p["w_in"].shape),
        const((CONV_WIDTH, 8, D_CONV)), const((8, D_CONV)), const((1, D_CONV)), const((1, D_CONV)),
        const((D_CONV, D)), const((D_Q, D)), const((D, D)),
        const((1, D_Q)), const((1, D_KV)),
        const((NORM_BLK, NORM_BLK)), const((D_KV, D_KV)),
        const((N_KV_HEADS, Q_PER_KV * WINDOW, 2 * WINDOW)),
        pl.BlockSpec(memory_space=pltpu.SMEM),
        const((D, ROUTER_COLS)), const((D, 2 * ROUTER_COLS)), const((1, ROUTER_COLS)),
        const((TM, TM)),
    ]
    out_specs = [
        pl.BlockSpec((None, TM, D), lambda b, t: (b, t, 0)),
        pl.BlockSpec((None, TM, D // 2), lambda b, t: (b, t, 0)),
        pl.BlockSpec((8, TM), lambda b, t: (0, b * nt + t)),
        pl.BlockSpec((TM, LANES), lambda b, t: (b * nt + t, 0)),
        pl.BlockSpec((N_EXPERTS, LANES), lambda b, t: (0, 0)),
    ]
    out_shape = [
        jax.ShapeDtypeStruct((B, T, D), f32),
        jax.ShapeDtypeStruct((B, T, D // 2), i32),
        jax.ShapeDtypeStruct((8, N), i32),
        jax.ShapeDtypeStruct((N, LANES), f32),
        jax.ShapeDtypeStruct((N_EXPERTS, LANES), f32),
    ]
    scratch = [
        pltpu.VMEM((TM + 32, D_CONV), f32),
        pltpu.VMEM((7, TM + CONV_SPAN, D_CONV), f32),
        pltpu.VMEM((TM, D_CONV), f32),
        pltpu.VMEM(((p["w_in"].shape[1] - 2 * D_CONV) // PROJ_CHUNK, TM, PROJ_CHUNK), f32),
        pltpu.VMEM((N_KV_HEADS, TM + WINDOW, LANES), bf16),
        pltpu.VMEM((N_KV_HEADS, TM + WINDOW, LANES), bf16),
        pltpu.VMEM((TM, D_Q), bf16),
    ]
    tri = jnp.asarray(np.triu(np.ones((TM, TM), np.float32), 1), bf16)
    return pl.pallas_call(
        _mixer_kernel,
        grid=(B, nt),
        in_specs=in_specs, out_specs=out_specs, out_shape=out_shape, scratch_shapes=scratch,
        compiler_params=pltpu.CompilerParams(
            dimension_semantics=("arbitrary", "arbitrary"), vmem_limit_bytes=VMEM_LIMIT),
        name="mixer_router",
    )(x, mod6, p["gmix"], p["gffn"], p["w_in"], p["dw"], p["dwb"], p["lng"], p["lnb"],
      p["wco"], p["wao"], p["wout"], p["qg"], p["kg"], p["bq"], p["bk"], p["bias"], p["sinks"],
      p["wrh"], p["wr2"], p["br"], tri)


def _dest_kernel(pstart_ref, route_ref, dest_ref):
    e = route_ref[0:2, :]
    base = jnp.zeros(e.shape, i32)
    for x in range(N_EXPERTS):
        base = jnp.where(e == x, pstart_ref[x], base)
    dest_ref[...] = base + route_ref[2:4, :]


def _dest_slots(pstart, route):
    N = route.shape[1]
    TS = min(DEST_TILE, N)
    return pl.pallas_call(
        _dest_kernel,
        grid=(N // TS,),
        in_specs=[pl.BlockSpec(memory_space=pltpu.SMEM),
                  pl.BlockSpec((8, TS), lambda i: (0, i))],
        out_specs=pl.BlockSpec((2, TS), lambda i: (0, i)),
        out_shape=jax.ShapeDtypeStruct((2, N), i32),
        name="dest_slots",
    )(pstart, route)


def _sc_mesh():
    return plsc.VectorSubcoreMesh(core_axis_name="c", subcore_axis_name="s")


def _sc_scatter(h2, dest, n_slots):
    N, D = h2.shape
    per_w = N // SC_WORKERS
    C = SC_CHUNK
    n_chunks = per_w // C
    dest4 = dest.reshape(TOP_K, SC_WORKERS, n_chunks, C)

    @functools.partial(
        pl.kernel, mesh=_sc_mesh(),
        out_type=jax.ShapeDtypeStruct((n_slots, D), h2.dtype),
        scratch_types=[pltpu.VMEM((n_chunks, C), i32), pltpu.VMEM((n_chunks, C), i32),
                       pltpu.VMEM((2, C, D), h2.dtype),
                       pltpu.SemaphoreType.DMA((2,)), pltpu.SemaphoreType.DMA((2,))],
        name="sc_dispatch_scatter",
    )
    def run(h2_hbm, dest_hbm, xs_hbm, idx0, idx1, buf, sem_in, sem_out):
        wid = lax.axis_index("s") * SC_CORES + lax.axis_index("c")
        base = wid * per_w
        pltpu.sync_copy(dest_hbm.at[0, wid], idx0)
        pltpu.sync_copy(dest_hbm.at[1, wid], idx1)

        def load(j, b):
            return pltpu.make_async_copy(h2_hbm.at[pl.ds(base + j * C, C)], buf.at[b], sem_in.at[b])

        def put(j, b, idx):
            return pltpu.make_async_copy(buf.at[b], xs_hbm.at[idx.at[j]], sem_out.at[b])

        load(0, 0).start()

        @pl.loop(0, n_chunks, step=2)
        def _(j0):
            for b in range(2):
                j = j0 + b
                load(j, b).wait()

                @pl.when(j >= 1)
                def _():
                    put(j - 1, 1 - b, idx0).wait()
                    put(j - 1, 1 - b, idx1).wait()

                @pl.when(j + 1 < n_chunks)
                def _():
                    load(j + 1, 1 - b).start()

                put(j, b, idx0).start()
                put(j, b, idx1).start()

        put(n_chunks - 1, 1, idx0).wait()
        put(n_chunks - 1, 1, idx1).wait()

    return run(h2, dest4)


def _sc_gather(y, dest, N):
    D = y.shape[1]
    per_w = N // SC_WORKERS
    C = SC_CHUNK
    n_chunks = per_w // C
    dest4 = dest.reshape(TOP_K, SC_WORKERS, n_chunks, C)

    @functools.partial(
        pl.kernel, mesh=_sc_mesh(),
        out_type=jax.ShapeDtypeStruct((TOP_K, N, D), y.dtype),
        scratch_types=[pltpu.VMEM((n_chunks, C), i32), pltpu.VMEM((n_chunks, C), i32),
                       pltpu.VMEM((2, C, D), y.dtype),
                       pltpu.SemaphoreType.DMA((2,)), pltpu.SemaphoreType.DMA((2,))],
        name="sc_combine_gather",
    )
    def run(y_hbm, dest_hbm, yg_hbm, idx0, idx1, buf, sem_in, sem_out):
        wid = lax.axis_index("s") * SC_CORES + lax.axis_index("c")
        base = wid * per_w
        pltpu.sync_copy(dest_hbm.at[0, wid], idx0)
        pltpu.sync_copy(dest_hbm.at[1, wid], idx1)
        idx = (idx0, idx1)

        def get(j, k):
            return pltpu.make_async_copy(y_hbm.at[idx[k].at[j]], buf.at[k], sem_in.at[k])

        def put(j, k):
            return pltpu.make_async_copy(buf.at[k], yg_hbm.at[k, pl.ds(base + j * C, C)], sem_out.at[k])

        get(0, 0).start()
        get(0, 1).start()

        @pl.loop(0, n_chunks)
        def _(j):
            for k in range(TOP_K):
                get(j, k).wait()
                put(j, k).start()
            for k in range(TOP_K):
                put(j, k).wait()

                @pl.when(j + 1 < n_chunks)
                def _():
                    get(j + 1, k).start()

    return run(y, dest4)


def _expert_kernel(be_ref, nv_ref, nu_ref, first_ref, slot_ref, nxt_ref,
                   xs_ref, wg_hbm, wu_hbm, wdn_hbm, y_ref,
                   stg_g, stg_u, stg_d, wgu_ref, wd_ref, sems):
    i = pl.program_id(0)
    in_use = i < nu_ref[0]

    def fetch(e, s):
        return (pltpu.make_async_copy(wg_hbm.at[e], stg_g.at[s], sems.at[s, 0]),
                pltpu.make_async_copy(wu_hbm.at[e], stg_u.at[s], sems.at[s, 1]),
                pltpu.make_async_copy(wdn_hbm.at[e], stg_d.at[s], sems.at[s, 2]))

    @pl.when(in_use & (first_ref[i] == 1))
    def _():
        s = slot_ref[i]

        @pl.when(i == 0)
        def _():
            for cp in fetch(be_ref[0], 0):
                cp.start()

        for cp in fetch(be_ref[i], s):
            cp.wait()
        wgu_ref[:, 0:D_EXPERT] = stg_g[s].astype(bf16)
        wgu_ref[:, D_EXPERT:2 * D_EXPERT] = stg_u[s].astype(bf16)
        wd_ref[...] = stg_d[s].astype(bf16)

        @pl.when(nxt_ref[i] >= 0)
        def _():
            for cp in fetch(nxt_ref[i], 1 - s):
                cp.start()

    @pl.when(in_use)
    def _():
        live = lax.broadcasted_iota(i32, (MOE_BLOCK, 1), 0) < nv_ref[i]
        x_hi, x_lo = _unpack_halves(jnp.where(live, xs_ref[...], 0))
        half = x_hi.shape[1]
        gu = (_dot(x_hi.astype(bf16), wgu_ref[0:half, :]) +
              _dot(x_lo.astype(bf16), wgu_ref[half:2 * half, :]))
        gate = gu[:, 0:D_EXPERT]
        hid = (gate * jax.nn.sigmoid(gate)) * gu[:, D_EXPERT:2 * D_EXPERT]
        y_ref[...] = _pack_halves(_dot(hid.astype(bf16), wd_ref[...]))


def _experts(block_e, n_valid, n_used, counts, xs, w_gate, w_up, w_down):
    n_slots, DH = xs.shape
    D = 2 * DH
    n_blocks = n_slots // MOE_BLOCK
    idx = jnp.arange(n_blocks, dtype=i32)
    prev_e = jnp.concatenate([jnp.full((1,), -1, i32), block_e[:-1]])
    first = ((block_e != prev_e) & (idx < n_used[0])).astype(i32)
    slot = ((jnp.cumsum(first) - 1) % 2).astype(i32)
    e_ids = jnp.arange(N_EXPERTS, dtype=i32)
    later = (e_ids[None, :] > e_ids[:, None]) & (counts[None, :] > 0)
    nxt_of_e = jnp.min(jnp.where(later, e_ids[None, :], N_EXPERTS), axis=1)
    nxt = jnp.where(nxt_of_e[block_e] < N_EXPERTS, nxt_of_e[block_e], -1).astype(i32)
    last = lambda i, nu: jnp.minimum(i, nu[0] - 1)
    slot_map = lambda i, be, nv, nu, fi, sl, nx: (last(i, nu), 0)
    grid_spec = pltpu.PrefetchScalarGridSpec(
        num_scalar_prefetch=6,
        grid=(n_blocks,),
        in_specs=[pl.BlockSpec((MOE_BLOCK, DH), slot_map),
                  pl.BlockSpec(memory_space=pl.ANY),
                  pl.BlockSpec(memory_space=pl.ANY),
                  pl.BlockSpec(memory_space=pl.ANY)],
        out_specs=pl.BlockSpec((MOE_BLOCK, DH), slot_map),
        scratch_shapes=[pltpu.VMEM((2, D, D_EXPERT), f32), pltpu.VMEM((2, D, D_EXPERT), f32),
                        pltpu.VMEM((2, D_EXPERT, D), f32),
                        pltpu.VMEM((D, 2 * D_EXPERT), bf16), pltpu.VMEM((D_EXPERT, D), bf16),
                        pltpu.SemaphoreType.DMA((2, 3))],
    )
    return pl.pallas_call(
        _expert_kernel,
        grid_spec=grid_spec,
        out_shape=jax.ShapeDtypeStruct((n_slots, DH), i32),
        compiler_params=pltpu.CompilerParams(dimension_semantics=("arbitrary",),
                                             vmem_limit_bytes=VMEM_LIMIT),
        name="experts",
    )(block_e, n_valid, n_used, first, slot, nxt, xs, w_gate, w_up, w_down)


def _combine_kernel(yg_ref, x1_ref, wtok_ref, mod_ref, *rest):
    o_ref = rest[-1]
    w = wtok_ref[...]
    g2 = mod_ref[5:6, :]
    a_hi, a_lo = _unpack_halves(yg_ref[0])
    b_hi, b_lo = _unpack_halves(yg_ref[1])
    moe = jnp.concatenate([w[:, 0:1] * a_hi + w[:, 1:2] * b_hi,
                           w[:, 0:1] * a_lo + w[:, 1:2] * b_lo], axis=1)
    o_ref[...] = x1_ref[...] + g2 * moe


def _combine(yg, x1, wtok, mod6, T, b0, n_total, out_prev):
    Nc, D = x1.shape
    TS = min(COMBINE_TILE, T)
    per_seq = T // TS
    blk0 = b0 * per_seq
    in_specs = [pl.BlockSpec((TOP_K, TS, D // 2), lambda i: (0, i, 0)),
                pl.BlockSpec((TS, D), lambda i: (i, 0)),
                pl.BlockSpec((TS, LANES), lambda i: (i, 0)),
                pl.BlockSpec((None, 6, D), lambda i: (b0 + i // per_seq, 0, 0))]
    args = [yg, x1, wtok, mod6]
    aliases = {}
    if out_prev is not None:
        in_specs.append(pl.BlockSpec(memory_space=pl.ANY))
        args.append(out_prev)
        aliases = {len(args) - 1: 0}
    return pl.pallas_call(
        _combine_kernel,
        grid=(Nc // TS,),
        in_specs=in_specs,
        out_specs=pl.BlockSpec((TS, D), lambda i: (blk0 + i, 0)),
        out_shape=jax.ShapeDtypeStruct((n_total, D), f32),
        input_output_aliases=aliases,
        compiler_params=pltpu.CompilerParams(dimension_semantics=("arbitrary",),
                                             vmem_limit_bytes=VMEM_LIMIT),
        name="combine",
    )(*args)


def _block_diag_mean(n, blk):
    m = np.kron(np.eye(n // blk, dtype=np.float32), np.full((blk, blk), 1.0 / blk, np.float32))
    return jnp.asarray(m, bf16)


def _layer(x, mod6, bias, l, w):
    B, T, D = x.shape
    N = B * T
    w_rg, w_re = w["w_router_group"][l], w["w_router_expert"][l]
    wr = jnp.zeros((D, ROUTER_COLS), f32)
    wr = wr.at[:, 0:N_GROUPS].set(w_rg).at[:, EXPERT_COL0:EXPERT_COL0 + N_EXPERTS].set(w_re)
    br = jnp.zeros((1, ROUTER_COLS), f32)
    br = br.at[0, 0:N_GROUPS].set(w["b_router_group"][l])
    br = br.at[0, EXPERT_COL0:EXPERT_COL0 + N_EXPERTS].set(w["b_router_expert"][l])
    wrh = wr.astype(bf16)
    wrl = (wr - wrh.astype(f32)).astype(bf16)
    p = dict(
        gmix=w["norm_mix_g"][l].reshape(1, D), gffn=w["norm_ffn_g"][l].reshape(1, D),
        w_in=w["w_in"][l].astype(bf16),
        dw=jnp.broadcast_to(w["dw_kernel"][l][:, None, :], (CONV_WIDTH, 8, D_CONV)), dwb=jnp.broadcast_to(w["dw_bias"][l][None, :], (8, D_CONV)),
        lng=w["conv_ln_g"][l].reshape(1, D_CONV), lnb=w["conv_ln_b"][l].reshape(1, D_CONV),
        wco=w["w_conv_out"][l].astype(bf16), wao=w["w_attn_out"][l].astype(bf16),
        wout=w["w_out"][l].astype(bf16),
        qg=(jnp.tile(w["q_norm_g"][l], N_Q_HEADS) * (HEAD_DIM ** -0.5 * LOG2E)).reshape(1, D_Q),
        kg=jnp.tile(w["k_norm_g"][l], N_KV_HEADS).reshape(1, D_KV),
        bq=_block_diag_mean(NORM_BLK, HEAD_DIM), bk=_block_diag_mean(D_KV, HEAD_DIM),
        bias=bias, sinks=w["sinks"][l], wrh=wrh, wr2=jnp.concatenate([wrh, wrl], axis=1), br=br,
    )
    w_gate, w_up, w_down = w["w_exp_gate"][l], w["w_exp_up"][l], w["w_exp_down"][l]

    n_chunks = MOE_CHUNKS if B % MOE_CHUNKS == 0 else 1
    Bc = B // n_chunks
    Nc = Bc * T
    n_blocks = -(-(Nc * TOP_K) // MOE_BLOCK) + N_EXPERTS
    blk0 = jnp.arange(n_blocks, dtype=i32) * MOE_BLOCK
    stage = []
    for ch in range(n_chunks):
        x1, h2, route, wtok, cnt = _mixer(x, mod6, p, ch * Bc, Bc)
        counts = cnt[:, 0].astype(i32)
        pcounts = (counts + MOE_BLOCK - 1) // MOE_BLOCK * MOE_BLOCK
        pend = jnp.cumsum(pcounts)
        pstart = pend - pcounts
        block_e = jnp.minimum(jnp.sum((pend[None, :] <= blk0[:, None]).astype(i32), axis=1), N_EXPERTS - 1)
        n_valid = jnp.clip((pstart + counts)[block_e] - blk0, 0, MOE_BLOCK).astype(i32)
        n_used = (pend[-1:] // MOE_BLOCK).astype(i32)
        dest = _dest_slots(pstart, route)
        xs = _sc_scatter(h2.reshape(Nc, D // 2), dest, n_blocks * MOE_BLOCK)
        stage.append((x1, wtok, dest, xs, block_e, n_valid, n_used, counts))
    ys = [_experts(be, nv, nu, cn, xs, w_gate, w_up, w_down) for (_, _, _, xs, be, nv, nu, cn) in stage]
    ygs = [_sc_gather(y, st[2], Nc) for y, st in zip(ys, stage)]
    out = None
    for ch in range(n_chunks):
        x1, wtok = stage[ch][0], stage[ch][1]
        out = _combine(ygs[ch], x1.reshape(Nc, D), wtok, mod6, T, ch * Bc, B * T, out)
    return out.reshape(B, T, D)


def kernel(x, c, w_ada, b_ada, norm_mix_g, w_in, dw_kernel, dw_bias, conv_ln_g, conv_ln_b,
           w_conv_out, q_norm_g, k_norm_g, sinks, w_attn_out, w_out, rel_bias_table, norm_ffn_g,
           w_router_group, b_router_group, w_router_expert, b_router_expert,
           w_exp_gate, w_exp_up, w_exp_down):
    w = dict(norm_mix_g=norm_mix_g, w_in=w_in, dw_kernel=dw_kernel, dw_bias=dw_bias,
             conv_ln_g=conv_ln_g, conv_ln_b=conv_ln_b, w_conv_out=w_conv_out, q_norm_g=q_norm_g,
             k_norm_g=k_norm_g, sinks=sinks, w_attn_out=w_attn_out, w_out=w_out,
             norm_ffn_g=norm_ffn_g, w_router_group=w_router_group, b_router_group=b_router_group,
             w_router_expert=w_router_expert, b_router_expert=b_router_expert,
             w_exp_gate=w_exp_gate, w_exp_up=w_exp_up, w_exp_down=w_exp_down)
    B = x.shape[0]
    bias = _bias_band(rel_bias_table).reshape(N_KV_HEADS, Q_PER_KV * WINDOW, 2 * WINDOW)
    for l in range(w_ada.shape[0]):
        mod6 = _modulation(c, w_ada[l], b_ada[l]).reshape(B, 6, D_MODEL)
        x = _layer(x, mod6, bias, l, w)
    return x
```

```python
import functools
import math

import jax
import jax.numpy as jnp
import numpy as np
from jax import lax
from jax.experimental import pallas as pl
from jax.experimental.pallas import tpu as pltpu
from jax.experimental.pallas import tpu_sc as plsc

D_MODEL = 1024
D_CONV = 512
CONV_WIDTH = 31
N_Q_HEADS = 8
N_KV_HEADS = 2
HEAD_DIM = 64
Q_PER_KV = N_Q_HEADS // N_KV_HEADS
WINDOW = 128
N_BUCKETS = 32
MAX_DISTANCE = 128
N_GROUPS = 4
EXPERTS_PER_GROUP = 8
N_EXPERTS = N_GROUPS * EXPERTS_PER_GROUP
TOP_K = 2
D_EXPERT = 256
D_Q = N_Q_HEADS * HEAD_DIM
D_KV = N_KV_HEADS * HEAD_DIM
EPS = 1e-6
NEG_INF = -1e30
LOG2E = math.log2(math.e)

LANES = 128
SEQ_TILE = 512
CONV_ROWS = 64
NORM_BLK = 256
PROJ_CHUNK = 256
CONV_SPAN = 24
MOE_BLOCK = 512
MOE_CHUNKS = 2
DEST_TILE = 8192
COMBINE_TILE = 1024
SC_CORES = 2
SC_SUBCORES = 16
SC_WORKERS = SC_CORES * SC_SUBCORES
SC_CHUNK = 32
ROUTER_COLS = LANES
EXPERT_COL0 = 8
VMEM_LIMIT = 56 * 1024 * 1024

f32 = jnp.float32
bf16 = jnp.bfloat16
i32 = jnp.int32


def _dot(a, b):
    return jnp.dot(a, b, preferred_element_type=f32)


def _split(a):
    hi = a.astype(bf16)
    lo = (a - hi.astype(f32)).astype(bf16)
    return hi, lo


def _pack_halves(x):
    c = x.shape[1] // 2
    hi = lax.bitcast_convert_type(x[:, 0:c].astype(bf16).astype(f32), jnp.uint32)
    lo = lax.bitcast_convert_type(x[:, c:2 * c].astype(bf16).astype(f32), jnp.uint32)
    word = (hi & jnp.uint32(0xFFFF0000)) | (lo >> jnp.uint32(16))
    return lax.bitcast_convert_type(word, i32)


def _unpack_halves(word):
    u = lax.bitcast_convert_type(word, jnp.uint32)
    hi = lax.bitcast_convert_type(u & jnp.uint32(0xFFFF0000), f32)
    lo = lax.bitcast_convert_type(u << jnp.uint32(16), f32)
    return hi, lo


def _dot3(a, b):
    ah, al = _split(a)
    bh, bl = _split(b)
    return _dot(ah, bh) + _dot(al, bh) + _dot(ah, bl)


def _mod_kernel(c_ref, w_ref, b_ref, o_ref):
    c = c_ref[...]
    s = c * jax.nn.sigmoid(c)
    o_ref[...] = _dot3(s, w_ref[...]) + b_ref[...]


def _modulation(c, w_ada, b_ada):
    B, D = c.shape
    n_out = w_ada.shape[1]
    return pl.pallas_call(
        _mod_kernel,
        grid=(n_out // D,),
        in_specs=[pl.BlockSpec((B, D), lambda j: (0, 0)),
                  pl.BlockSpec((D, D), lambda j: (0, j)),
                  pl.BlockSpec((1, D), lambda j: (0, j))],
        out_specs=pl.BlockSpec((B, D), lambda j: (0, j)),
        out_shape=jax.ShapeDtypeStruct((B, n_out), f32),
        name="modulation",
    )(c, w_ada, b_ada.reshape(1, n_out))


def _band_buckets():
    qi = np.arange(WINDOW)[:, None]
    kj = np.arange(2 * WINDOW)[None, :]
    dist = np.clip(qi + WINDOW - kj, 0, MAX_DISTANCE)
    max_exact = N_BUCKETS // 2
    d = np.maximum(dist, 1).astype(np.float32)
    large = max_exact + (np.log(d / np.float32(max_exact)) / np.float32(math.log(MAX_DISTANCE / max_exact))
                         * np.float32(N_BUCKETS - max_exact)).astype(np.int32)
    large = np.minimum(large, N_BUCKETS - 1)
    bucket = np.where(dist < max_exact, dist, large).astype(np.int32)
    raw = qi + WINDOW - kj
    return np.where((raw >= 0) & (raw < WINDOW), bucket, -1).astype(np.int32)


def _bias_kernel(tab_ref, bucket_ref, o_ref):
    bk = bucket_ref[...]
    for h in range(N_Q_HEADS):
        acc = jnp.full(bk.shape, NEG_INF, f32)
        for b in range(N_BUCKETS):
            acc = jnp.where(bk == b, tab_ref[b, h] * LOG2E, acc)
        o_ref[h] = acc


def _bias_band(rel_bias_table):
    return pl.pallas_call(
        _bias_kernel,
        in_specs=[pl.BlockSpec(memory_space=pltpu.SMEM),
                  pl.BlockSpec(memory_space=pltpu.VMEM)],
        out_specs=pl.BlockSpec(memory_space=pltpu.VMEM),
        out_shape=jax.ShapeDtypeStruct((N_Q_HEADS, WINDOW, 2 * WINDOW), f32),
        name="bias_band",
    )(rel_bias_table, jnp.asarray(_band_buckets()))


def _mixer_kernel(x_ref, mod_ref, gmix_ref, gffn_ref, win_ref, dw_ref, dwb_ref, lng_ref, lnb_ref,
                  wco_ref, wao_ref, wout_ref, qg_ref, kg_ref, bq_ref, bk_ref, bias_ref, sink_ref,
                  wrh_ref, wr2_ref, br_ref, tri_ref,
                  x1_ref, h2_ref, route_ref, wtok_ref, cnt_ref,
                  uext, ush, conv_sc, proj_sc, k2, v2, osc):
    TM = x_ref.shape[0]
    HALO = 32
    b = pl.program_id(0)
    t = pl.program_id(1)
    first = t == 0

    @pl.when(first)
    def _():
        uext[0:HALO, :] = jnp.zeros((HALO, D_CONV), f32)
        k2[:, 0:WINDOW, :] = jnp.zeros((N_KV_HEADS, WINDOW, LANES), bf16)
        v2[:, 0:WINDOW, :] = jnp.zeros((N_KV_HEADS, WINDOW, LANES), bf16)

    @pl.when(first & (b == 0))
    def _():
        cnt_ref[...] = jnp.zeros(cnt_ref.shape, f32)

    x = x_ref[...]
    mod = mod_ref[...]
    sh1, sc1, g1 = mod[0:1, :], mod[1:2, :], mod[2:3, :]
    sh2, sc2, g2 = mod[3:4, :], mod[4:5, :], mod[5:6, :]
    del g2

    ms = jnp.mean(x * x, axis=-1, keepdims=True)
    h = (x * lax.rsqrt(ms + EPS)) * (gmix_ref[...] * (1.0 + sc1)) + sh1
    hb = h.astype(bf16)

    ab = _dot(hb, win_ref[:, 0:2 * D_CONV])
    u = ab[:, 0:D_CONV] * jax.nn.sigmoid(ab[:, D_CONV:2 * D_CONV])
    uext[HALO:HALO + TM, :] = u
    u_all = uext[...]
    for r in range(1, 8):
        ush[r - 1] = pltpu.roll(u_all, HALO + TM - r, 0)[0:TM + CONV_SPAN, :]
    n_q = D_Q // PROJ_CHUNK
    n_g = D_MODEL // PROJ_CHUNK
    n_jobs = (win_ref.shape[1] - 2 * D_CONV) // PROJ_CHUNK
    n_rb = TM // CONV_ROWS
    n_t8 = CONV_ROWS // 8
    units = [(cb, rg) for cb in range(D_CONV // LANES) for rg in range(n_rb)]
    job_at = {(i * len(units)) // n_jobs: i for i in range(n_jobs)}
    assert len(job_at) == n_jobs
    for ui, (cb, rg) in enumerate(units):
        if ui in job_at:
            c_lo = 2 * D_CONV + job_at[ui] * PROJ_CHUNK
            proj_sc[job_at[ui]] = _dot(hb, win_ref[:, c_lo:c_lo + PROJ_CHUNK])
        cs = slice(cb * LANES, (cb + 1) * LANES)
        base = rg * CONV_ROWS
        accs = [None] * n_t8
        for r in range(8):
            taps = [(j, (j + 2) // 8) for j in range(CONV_WIDTH) if (j + 2) % 8 == r]
            t_lo = min(a for _, a in taps)
            t_hi = max(a for _, a in taps) + n_t8
            tiles = {}
            for t8 in range(t_lo, t_hi):
                rows = slice(base + 8 * t8, base + 8 * t8 + 8)
                tiles[t8] = uext[rows, cs] if r == 0 else ush[r - 1, rows, cs]
            for j, a in taps:
                tap = dw_ref[j, :, cs]
                for s8 in range(n_t8):
                    term = tap * tiles[a + s8]
                    accs[s8] = term if accs[s8] is None else accs[s8] + term
        for s8 in range(n_t8):
            conv_sc[base + 8 * s8:base + 8 * s8 + 8, cs] = accs[s8] + dwb_ref[:, cs]
    conv = conv_sc[...]
    uext[0:HALO, :] = uext[TM:TM + HALO, :]
    mu = jnp.mean(conv, axis=-1, keepdims=True)
    dc = conv - mu
    var = jnp.mean(dc * dc, axis=-1, keepdims=True)
    yn = dc * lax.rsqrt(var + EPS) * lng_ref[...] + lnb_ref[...]
    act = yn * jax.nn.sigmoid(yn)
    y_conv = _dot(act.astype(bf16), wco_ref[...])

    q = jnp.concatenate([proj_sc[i] for i in range(n_q)], axis=1)
    k = proj_sc[n_q, :, 0:D_KV]
    v = proj_sc[n_q, :, D_KV:2 * D_KV]

    def head_norm(z, blk_ref, g):
        sq = (z * z).astype(bf16)
        blk = blk_ref.shape[0]
        msq = jnp.concatenate([_dot(sq[:, c:c + blk], blk_ref[...]) for c in range(0, z.shape[1], blk)], axis=1)
        return z * lax.rsqrt(msq + EPS) * g

    qn = head_norm(q, bq_ref, qg_ref[...]).astype(bf16)
    kn = head_norm(k, bk_ref, kg_ref[...])
    lo_half = lax.broadcasted_iota(i32, (TM, LANES), 1) < HEAD_DIM
    kr = pltpu.roll(kn, HEAD_DIM, 1)
    vr = pltpu.roll(v, HEAD_DIM, 1)
    k2[0, WINDOW:WINDOW + TM, :] = jnp.where(lo_half, kn, kr).astype(bf16)
    k2[1, WINDOW:WINDOW + TM, :] = jnp.where(lo_half, kr, kn).astype(bf16)
    v2[0, WINDOW:WINDOW + TM, :] = jnp.where(lo_half, v, vr).astype(bf16)
    v2[1, WINDOW:WINDOW + TM, :] = jnp.where(lo_half, vr, v).astype(bf16)

    QROWS = Q_PER_KV * WINDOW
    col = lax.broadcasted_iota(i32, (QROWS, 2 * WINDOW), 1)
    has_prev = col >= jnp.where(first, WINDOW, 0)
    hrow = lax.broadcasted_iota(i32, (QROWS, 1), 0) // WINDOW
    lo128 = lax.broadcasted_iota(i32, (WINDOW, LANES), 1) < HEAD_DIM
    zero_q = jnp.zeros((WINDOW, LANES), bf16)
    for g in range(N_KV_HEADS):
        sink = LOG2E * jnp.where(hrow == 0, sink_ref[4 * g],
                                 jnp.where(hrow == 1, sink_ref[4 * g + 1],
                                           jnp.where(hrow == 2, sink_ref[4 * g + 2], sink_ref[4 * g + 3])))
        bias_g = bias_ref[g]
        for j in range(TM // WINDOW):
            rs = slice(j * WINDOW, (j + 1) * WINDOW)
            qa = qn[rs, 2 * LANES * g:2 * LANES * g + LANES]
            qb = qn[rs, 2 * LANES * g + LANES:2 * LANES * (g + 1)]
            qs = jnp.concatenate([jnp.where(lo128, qa, zero_q), jnp.where(lo128, zero_q, qa),
                                  jnp.where(lo128, qb, zero_q), jnp.where(lo128, zero_q, qb)], axis=0)
            kk = k2[g, j * WINDOW:(j + 2) * WINDOW, :]
            s = lax.dot_general(qs, kk, (((1,), (1,)), ((), ())), preferred_element_type=f32)
            logits = jnp.where(has_prev, s + bias_g, NEG_INF) if j == 0 else s + bias_g
            m = jnp.maximum(jnp.max(logits, axis=-1, keepdims=True), sink)
            p = jnp.exp2(logits - m)
            den = jnp.sum(p, axis=-1, keepdims=True) + jnp.exp2(sink - m)
            o2 = _dot(p.astype(bf16), v2[g, j * WINDOW:(j + 2) * WINDOW, :]) * (1.0 / den)
            osc[rs, 2 * LANES * g:2 * LANES * g + LANES] = jnp.where(
                lo128, o2[0:WINDOW], o2[WINDOW:2 * WINDOW]).astype(bf16)
            osc[rs, 2 * LANES * g + LANES:2 * LANES * (g + 1)] = jnp.where(
                lo128, o2[2 * WINDOW:3 * WINDOW], o2[3 * WINDOW:4 * WINDOW]).astype(bf16)
    k2[:, 0:WINDOW, :] = k2[:, TM:TM + WINDOW, :]
    v2[:, 0:WINDOW, :] = v2[:, TM:TM + WINDOW, :]
    y_attn = _dot(osc[...], wao_ref[...])

    merged = []
    for i in range(n_g):
        cs = slice(i * PROJ_CHUNK, (i + 1) * PROJ_CHUNK)
        g_conv = jax.nn.sigmoid(proj_sc[n_q + 1 + i])
        g_attn = jax.nn.sigmoid(proj_sc[n_q + 1 + n_g + i])
        merged.append((g_conv * y_conv[:, cs] + g_attn * y_attn[:, cs]).astype(bf16))
    merged = jnp.concatenate(merged, axis=1)
    x1 = x + g1 * _dot(merged, wout_ref[...])
    x1_ref[...] = x1

    ms2 = jnp.mean(x1 * x1, axis=-1, keepdims=True)
    h2 = (x1 * lax.rsqrt(ms2 + EPS)) * (gffn_ref[...] * (1.0 + sc2)) + sh2
    h2_ref[...] = _pack_halves(h2)
    hh, hl = _split(h2)
    hw = _dot(hh, wr2_ref[...])
    lg = hw[:, 0:ROUTER_COLS] + hw[:, ROUTER_COLS:2 * ROUTER_COLS] + _dot(hl, wrh_ref[...]) + br_ref[...]
    lt = lg.T
    gl = lt[0:N_GROUPS, :]
    grow = lax.broadcasted_iota(i32, (N_GROUPS, TM), 0)
    gmax = jnp.max(gl, axis=0, keepdims=True)
    gi = jnp.min(jnp.where(gl == gmax, grow, N_GROUPS), axis=0, keepdims=True)
    p_top = 1.0 / jnp.sum(jnp.exp(gl - gmax), axis=0, keepdims=True)
    sel = lt[EXPERT_COL0:EXPERT_COL0 + EXPERTS_PER_GROUP, :]
    for gg in range(1, N_GROUPS):
        lo_r = EXPERT_COL0 + gg * EXPERTS_PER_GROUP
        sel = jnp.where(gi == gg, lt[lo_r:lo_r + EXPERTS_PER_GROUP, :], sel)
    erow = lax.broadcasted_iota(i32, (EXPERTS_PER_GROUP, TM), 0)
    m1 = jnp.max(sel, axis=0, keepdims=True)
    i1 = jnp.min(jnp.where(sel == m1, erow, EXPERTS_PER_GROUP), axis=0, keepdims=True)
    rest = jnp.where(erow == i1, -jnp.inf, sel)
    m2 = jnp.max(rest, axis=0, keepdims=True)
    i2 = jnp.min(jnp.where(rest == m2, erow, EXPERTS_PER_GROUP), axis=0, keepdims=True)
    z = jnp.sum(jnp.exp(sel - m1), axis=0, keepdims=True)
    v1 = 1.0 / z
    v2nd = jnp.exp(m2 - m1) / z
    w1 = v1 / (v1 + v2nd) * p_top
    w2 = v2nd / (v1 + v2nd) * p_top
    e1 = gi * EXPERTS_PER_GROUP + i1
    e2 = gi * EXPERTS_PER_GROUP + i2

    xrow = lax.broadcasted_iota(i32, (N_EXPERTS, TM), 0)
    oh1 = xrow == e1
    oh2 = xrow == e2
    both = jnp.where(oh1 | oh2, 1.0, 0.0)
    prefix = _dot(both.astype(bf16), tri_ref[...]) + cnt_ref[:, 0:1]
    r1 = jnp.sum(jnp.where(oh1, prefix, 0.0), axis=0, keepdims=True)
    r2 = jnp.sum(jnp.where(oh2, prefix, 0.0), axis=0, keepdims=True)
    cnt_ref[...] = cnt_ref[...] + jnp.sum(both, axis=1, keepdims=True)

    route_ref[...] = jnp.concatenate(
        [e1, e2, r1.astype(i32), r2.astype(i32), jnp.zeros((4, TM), i32)], axis=0)
    wpad = jnp.concatenate([w1, w2, jnp.zeros((LANES - 2, TM), f32)], axis=0)
    wtok_ref[...] = wpad.T


def _mixer(x, mod6, p, b0, B):
    _, T, D = x.shape
    TM = min(SEQ_TILE, T)
    nt = T // TM
    N = B * T
    const = lambda shape: pl.BlockSpec(shape, lambda b, t: (0,) * len(shape))
    in_specs = [
        pl.BlockSpec((None, TM, D), lambda b, t: (b0 + b, t, 0)),
        pl.BlockSpec((None, 6, D), lambda b, t: (b0 + b, 0, 0)),
        const((1, D)), const((1, D)),
        const(p["w_in"].shape),
        const((CONV_WIDTH, 8, D_CONV)), const((8, D_CONV)), const((1, D_CONV)), const((1, D_CONV)),
        const((D_CONV, D)), const((D_Q, D)), const((D, D)),
        const((1, D_Q)), const((1, D_KV)),
        const((NORM_BLK, NORM_BLK)), const((D_KV, D_KV)),
        const((N_KV_HEADS, Q_PER_KV * WINDOW, 2 * WINDOW)),
        pl.BlockSpec(memory_space=pltpu.SMEM),
        const((D, ROUTER_COLS)), const((D, 2 * ROUTER_COLS)), const((1, ROUTER_COLS)),
        const((TM, TM)),
    ]
    out_specs = [
        pl.BlockSpec((None, TM, D), lambda b, t: (b, t, 0)),
        pl.BlockSpec((None, TM, D // 2), lambda b, t: (b, t, 0)),
        pl.BlockSpec((8, TM), lambda b, t: (0, b * nt + t)),
        pl.BlockSpec((TM, LANES), lambda b, t: (b * nt + t, 0)),
        pl.BlockSpec((N_EXPERTS, LANES), lambda b, t: (0, 0)),
    ]
    out_shape = [
        jax.ShapeDtypeStruct((B, T, D), f32),
        jax.ShapeDtypeStruct((B, T, D // 2), i32),
        jax.ShapeDtypeStruct((8, N), i32),
        jax.ShapeDtypeStruct((N, LANES), f32),
        jax.ShapeDtypeStruct((N_EXPERTS, LANES), f32),
    ]
    scratch = [
        pltpu.VMEM((TM + 32, D_CONV), f32),
        pltpu.VMEM((7, TM + CONV_SPAN, D_CONV), f32),
        pltpu.VMEM((TM, D_CONV), f32),
        pltpu.VMEM(((p["w_in"].shape[1] - 2 * D_CONV) // PROJ_CHUNK, TM, PROJ_CHUNK), f32),
        pltpu.VMEM((N_KV_HEADS, TM + WINDOW, LANES), bf16),
        pltpu.VMEM((N_KV_HEADS, TM + WINDOW, LANES), bf16),
        pltpu.VMEM((TM, D_Q), bf16),
    ]
    tri = jnp.asarray(np.triu(np.ones((TM, TM), np.float32), 1), bf16)
    return pl.pallas_call(
        _mixer_kernel,
        grid=(B, nt),
        in_specs=in_specs, out_specs=out_specs, out_shape=out_shape, scratch_shapes=scratch,
        compiler_params=pltpu.CompilerParams(
            dimension_semantics=("arbitrary", "arbitrary"), vmem_limit_bytes=VMEM_LIMIT),
        name="mixer_router",
    )(x, mod6, p["gmix"], p["gffn"], p["w_in"], p["dw"], p["dwb"], p["lng"], p["lnb"],
      p["wco"], p["wao"], p["wout"], p["qg"], p["kg"], p["bq"], p["bk"], p["bias"], p["sinks"],
      p["wrh"], p["wr2"], p["br"], tri)


def _dest_kernel(pstart_ref, route_ref, dest_ref):
    e = route_ref[0:2, :]
    base = jnp.zeros(e.shape, i32)
    for x in range(N_EXPERTS):
        base = jnp.where(e == x, pstart_ref[x], base)
    dest_ref[...] = base + route_ref[2:4, :]


def _dest_slots(pstart, route):
    N = route.shape[1]
    TS = min(DEST_TILE, N)
    return pl.pallas_call(
        _dest_kernel,
        grid=(N // TS,),
        in_specs=[pl.BlockSpec(memory_space=pltpu.SMEM),
                  pl.BlockSpec((8, TS), lambda i: (0, i))],
        out_specs=pl.BlockSpec((2, TS), lambda i: (0, i)),
        out_shape=jax.ShapeDtypeStruct((2, N), i32),
        name="dest_slots",
    )(pstart, route)


def _sc_mesh():
    return plsc.VectorSubcoreMesh(core_axis_name="c", subcore_axis_name="s")


def _sc_scatter(h2, dest, n_slots):
    N, D = h2.shape
    per_w = N // SC_WORKERS
    C = SC_CHUNK
    n_chunks = per_w // C
    dest4 = dest.reshape(TOP_K, SC_WORKERS, n_chunks, C)

    @functools.partial(
        pl.kernel, mesh=_sc_mesh(),
        out_type=jax.ShapeDtypeStruct((n_slots, D), h2.dtype),
        scratch_types=[pltpu.VMEM((n_chunks, C), i32), pltpu.VMEM((n_chunks, C), i32),
                       pltpu.VMEM((2, C, D), h2.dtype),
                       pltpu.SemaphoreType.DMA((2,)), pltpu.SemaphoreType.DMA((2,))],
        name="sc_dispatch_scatter",
    )
    def run(h2_hbm, dest_hbm, xs_hbm, idx0, idx1, buf, sem_in, sem_out):
        wid = lax.axis_index("s") * SC_CORES + lax.axis_index("c")
        base = wid * per_w
        pltpu.sync_copy(dest_hbm.at[0, wid], idx0)
        pltpu.sync_copy(dest_hbm.at[1, wid], idx1)

        def load(j, b):
            return pltpu.make_async_copy(h2_hbm.at[pl.ds(base + j * C, C)], buf.at[b], sem_in.at[b])

        def put(j, b, idx):
            return pltpu.make_async_copy(buf.at[b], xs_hbm.at[idx.at[j]], sem_out.at[b])

        load(0, 0).start()

        @pl.loop(0, n_chunks, step=2)
        def _(j0):
            for b in range(2):
                j = j0 + b
                load(j, b).wait()

                @pl.when(j >= 1)
                def _():
                    put(j - 1, 1 - b, idx0).wait()
                    put(j - 1, 1 - b, idx1).wait()

                @pl.when(j + 1 < n_chunks)
                def _():
                    load(j + 1, 1 - b).start()

                put(j, b, idx0).start()
                put(j, b, idx1).start()

        put(n_chunks - 1, 1, idx0).wait()
        put(n_chunks - 1, 1, idx1).wait()

    return run(h2, dest4)


def _sc_gather(y, dest, N):
    D = y.shape[1]
    per_w = N // SC_WORKERS
    C = SC_CHUNK
    n_chunks = per_w // C
    dest4 = dest.reshape(TOP_K, SC_WORKERS, n_chunks, C)

    @functools.partial(
        pl.kernel, mesh=_sc_mesh(),
        out_type=jax.ShapeDtypeStruct((TOP_K, N, D), y.dtype),
        scratch_types=[pltpu.VMEM((n_chunks, C), i32), pltpu.VMEM((n_chunks, C), i32),
                       pltpu.VMEM((2, C, D), y.dtype),
                       pltpu.SemaphoreType.DMA((2,)), pltpu.SemaphoreType.DMA((2,))],
        name="sc_combine_gather",
    )
    def run(y_hbm, dest_hbm, yg_hbm, idx0, idx1, buf, sem_in, sem_out):
        wid = lax.axis_index("s") * SC_CORES + lax.axis_index("c")
        base = wid * per_w
        pltpu.sync_copy(dest_hbm.at[0, wid], idx0)
        pltpu.sync_copy(dest_hbm.at[1, wid], idx1)
        idx = (idx0, idx1)

        def get(j, k):
            return pltpu.make_async_copy(y_hbm.at[idx[k].at[j]], buf.at[k], sem_in.at[k])

        def put(j, k):
            return pltpu.make_async_copy(buf.at[k], yg_hbm.at[k, pl.ds(base + j * C, C)], sem_out.at[k])

        get(0, 0).start()
        get(0, 1).start()

        @pl.loop(0, n_chunks)
        def _(j):
            for k in range(TOP_K):
                get(j, k).wait()
                put(j, k).start()
            for k in range(TOP_K):
                put(j, k).wait()

                @pl.when(j + 1 < n_chunks)
                def _():
                    get(j + 1, k).start()

    return run(y, dest4)


def _expert_kernel(be_ref, nv_ref, nu_ref, first_ref, slot_ref, nxt_ref,
                   xs_ref, wg_hbm, wu_hbm, wdn_hbm, y_ref,
                   stg_g, stg_u, stg_d, wgu_ref, wd_ref, sems):
    i = pl.program_id(0)
    in_use = i < nu_ref[0]

    def fetch(e, s):
        return (pltpu.make_async_copy(wg_hbm.at[e], stg_g.at[s], sems.at[s, 0]),
                pltpu.make_async_copy(wu_hbm.at[e], stg_u.at[s], sems.at[s, 1]),
                pltpu.make_async_copy(wdn_hbm.at[e], stg_d.at[s], sems.at[s, 2]))

    @pl.when(in_use & (first_ref[i] == 1))
    def _():
        s = slot_ref[i]

        @pl.when(i == 0)
        def _():
            for cp in fetch(be_ref[0], 0):
                cp.start()

        for cp in fetch(be_ref[i], s):
            cp.wait()
        wgu_ref[:, 0:D_EXPERT] = stg_g[s].astype(bf16)
        wgu_ref[:, D_EXPERT:2 * D_EXPERT] = stg_u[s].astype(bf16)
        wd_ref[...] = stg_d[s].astype(bf16)

        @pl.when(nxt_ref[i] >= 0)
        def _():
            for cp in fetch(nxt_ref[i], 1 - s):
                cp.start()

    @pl.when(in_use)
    def _():
        live = lax.broadcasted_iota(i32, (MOE_BLOCK, 1), 0) < nv_ref[i]
        x_hi, x_lo = _unpack_halves(jnp.where(live, xs_ref[...], 0))
        half = x_hi.shape[1]
        gu = (_dot(x_hi.astype(bf16), wgu_ref[0:half, :]) +
              _dot(x_lo.astype(bf16), wgu_ref[half:2 * half, :]))
        gate = gu[:, 0:D_EXPERT]
        hid = (gate * jax.nn.sigmoid(gate)) * gu[:, D_EXPERT:2 * D_EXPERT]
        y_ref[...] = _pack_halves(_dot(hid.astype(bf16), wd_ref[...]))


def _experts(block_e, n_valid, n_used, counts, xs, w_gate, w_up, w_down):
    n_slots, DH = xs.shape
    D = 2 * DH
    n_blocks = n_slots // MOE_BLOCK
    idx = jnp.arange(n_blocks, dtype=i32)
    prev_e = jnp.concatenate([jnp.full((1,), -1, i32), block_e[:-1]])
    first = ((block_e != prev_e) & (idx < n_used[0])).astype(i32)
    slot = ((jnp.cumsum(first) - 1) % 2).astype(i32)
    e_ids = jnp.arange(N_EXPERTS, dtype=i32)
    later = (e_ids[None, :] > e_ids[:, None]) & (counts[None, :] > 0)
    nxt_of_e = jnp.min(jnp.where(later, e_ids[None, :], N_EXPERTS), axis=1)
    nxt = jnp.where(nxt_of_e[block_e] < N_EXPERTS, nxt_of_e[block_e], -1).astype(i32)
    last = lambda i, nu: jnp.minimum(i, nu[0] - 1)
    slot_map = lambda i, be, nv, nu, fi, sl, nx: (last(i, nu), 0)
    grid_spec = pltpu.PrefetchScalarGridSpec(
        num_scalar_prefetch=6,
        grid=(n_blocks,),
        in_specs=[pl.BlockSpec((MOE_BLOCK, DH), slot_map),
                  pl.BlockSpec(memory_space=pl.ANY),
                  pl.BlockSpec(memory_space=pl.ANY),
                  pl.BlockSpec(memory_space=pl.ANY)],
        out_specs=pl.BlockSpec((MOE_BLOCK, DH), slot_map),
        scratch_shapes=[pltpu.VMEM((2, D, D_EXPERT), f32), pltpu.VMEM((2, D, D_EXPERT), f32),
                        pltpu.VMEM((2, D_EXPERT, D), f32),
                        pltpu.VMEM((D, 2 * D_EXPERT), bf16), pltpu.VMEM((D_EXPERT, D), bf16),
                        pltpu.SemaphoreType.DMA((2, 3))],
    )
    return pl.pallas_call(
        _expert_kernel,
        grid_spec=grid_spec,
        out_shape=jax.ShapeDtypeStruct((n_slots, DH), i32),
        compiler_params=pltpu.CompilerParams(dimension_semantics=("arbitrary",),
                                             vmem_limit_bytes=VMEM_LIMIT),
        name="experts",
    )(block_e, n_valid, n_used, first, slot, nxt, xs, w_gate, w_up, w_down)


def _combine_kernel(yg_ref, x1_ref, wtok_ref, mod_ref, *rest):
    o_ref = rest[-1]
    w = wtok_ref[...]
    g2 = mod_ref[5:6, :]
    a_hi, a_lo = _unpack_halves(yg_ref[0])
    b_hi, b_lo = _unpack_halves(yg_ref[1])
    moe = jnp.concatenate([w[:, 0:1] * a_hi + w[:, 1:2] * b_hi,
                           w[:, 0:1] * a_lo + w[:, 1:2] * b_lo], axis=1)
    o_ref[...] = x1_ref[...] + g2 * moe


def _combine(yg, x1, wtok, mod6, T, b0, n_total, out_prev):
    Nc, D = x1.shape
    TS = min(COMBINE_TILE, T)
    per_seq = T // TS
    blk0 = b0 * per_seq
    in_specs = [pl.BlockSpec((TOP_K, TS, D // 2), lambda i: (0, i, 0)),
                pl.BlockSpec((TS, D), lambda i: (i, 0)),
                pl.BlockSpec((TS, LANES), lambda i: (i, 0)),
                pl.BlockSpec((None, 6, D), lambda i: (b0 + i // per_seq, 0, 0))]
    args = [yg, x1, wtok, mod6]
    aliases = {}
    if out_prev is not None:
        in_specs.append(pl.BlockSpec(memory_space=pl.ANY))
        args.append(out_prev)
        aliases = {len(args) - 1: 0}
    return pl.pallas_call(
        _combine_kernel,
        grid=(Nc // TS,),
        in_specs=in_specs,
        out_specs=pl.BlockSpec((TS, D), lambda i: (blk0 + i, 0)),
        out_shape=jax.ShapeDtypeStruct((n_total, D), f32),
        input_output_aliases=aliases,
        compiler_params=pltpu.CompilerParams(dimension_semantics=("arbitrary",),
                                             vmem_limit_bytes=VMEM_LIMIT),
        name="combine",
    )(*args)


def _block_diag_mean(n, blk):
    m = np.kron(np.eye(n // blk, dtype=np.float32), np.full((blk, blk), 1.0 / blk, np.float32))
    return jnp.asarray(m, bf16)


def _layer(x, mod6, bias, l, w):
    B, T, D = x.shape
    N = B * T
    w_rg, w_re = w["w_router_group"][l], w["w_router_expert"][l]
    wr = jnp.zeros((D, ROUTER_COLS), f32)
    wr = wr.at[:, 0:N_GROUPS].set(w_rg).at[:, EXPERT_COL0:EXPERT_COL0 + N_EXPERTS].set(w_re)
    br = jnp.zeros((1, ROUTER_COLS), f32)
    br = br.at[0, 0:N_GROUPS].set(w["b_router_group"][l])
    br = br.at[0, EXPERT_COL0:EXPERT_COL0 + N_EXPERTS].set(w["b_router_expert"][l])
    wrh = wr.astype(bf16)
    wrl = (wr - wrh.astype(f32)).astype(bf16)
    p = dict(
        gmix=w["norm_mix_g"][l].reshape(1, D), gffn=w["norm_ffn_g"][l].reshape(1, D),
        w_in=w["w_in"][l].astype(bf16),
        dw=jnp.broadcast_to(w["dw_kernel"][l][:, None, :], (CONV_WIDTH, 8, D_CONV)), dwb=jnp.broadcast_to(w["dw_bias"][l][None, :], (8, D_CONV)),
        lng=w["conv_ln_g"][l].reshape(1, D_CONV), lnb=w["conv_ln_b"][l].reshape(1, D_CONV),
        wco=w["w_conv_out"][l].astype(bf16), wao=w["w_attn_out"][l].astype(bf16),
        wout=w["w_out"][l].astype(bf16),
        qg=(jnp.tile(w["q_norm_g"][l], N_Q_HEADS) * (HEAD_DIM ** -0.5 * LOG2E)).reshape(1, D_Q),
        kg=jnp.tile(w["k_norm_g"][l], N_KV_HEADS).reshape(1, D_KV),
        bq=_block_diag_mean(NORM_BLK, HEAD_DIM), bk=_block_diag_mean(D_KV, HEAD_DIM),
        bias=bias, sinks=w["sinks"][l], wrh=wrh, wr2=jnp.concatenate([wrh, wrl], axis=1), br=br,
    )
    w_gate, w_up, w_down = w["w_exp_gate"][l], w["w_exp_up"][l], w["w_exp_down"][l]

    n_chunks = MOE_CHUNKS if B % MOE_CHUNKS == 0 else 1
    Bc = B // n_chunks
    Nc = Bc * T
    n_blocks = -(-(Nc * TOP_K) // MOE_BLOCK) + N_EXPERTS
    blk0 = jnp.arange(n_blocks, dtype=i32) * MOE_BLOCK
    stage = []
    for ch in range(n_chunks):
        x1, h2, route, wtok, cnt = _mixer(x, mod6, p, ch * Bc, Bc)
        counts = cnt[:, 0].astype(i32)
        pcounts = (counts + MOE_BLOCK - 1) // MOE_BLOCK * MOE_BLOCK
        pend = jnp.cumsum(pcounts)
        pstart = pend - pcounts
        block_e = jnp.minimum(jnp.sum((pend[None, :] <= blk0[:, None]).astype(i32), axis=1), N_EXPERTS - 1)
        n_valid = jnp.clip((pstart + counts)[block_e] - blk0, 0, MOE_BLOCK).astype(i32)
        n_used = (pend[-1:] // MOE_BLOCK).astype(i32)
        dest = _dest_slots(pstart, route)
        xs = _sc_scatter(h2.reshape(Nc, D // 2), dest, n_blocks * MOE_BLOCK)
        stage.append((x1, wtok, dest, xs, block_e, n_valid, n_used, counts))
    ys = [_experts(be, nv, nu, cn, xs, w_gate, w_up, w_down) for (_, _, _, xs, be, nv, nu, cn) in stage]
    ygs = [_sc_gather(y, st[2], Nc) for y, st in zip(ys, stage)]
    out = None
    for ch in range(n_chunks):
        x1, wtok = stage[ch][0], stage[ch][1]
        out = _combine(ygs[ch], x1.reshape(Nc, D), wtok, mod6, T, ch * Bc, B * T, out)
    return out.reshape(B, T, D)


def kernel(x, c, w_ada, b_ada, norm_mix_g, w_in, dw_kernel, dw_bias, conv_ln_g, conv_ln_b,
           w_conv_out, q_norm_g, k_norm_g, sinks, w_attn_out, w_out, rel_bias_table, norm_ffn_g,
           w_router_group, b_router_group, w_router_expert, b_router_expert,
           w_exp_gate, w_exp_up, w_exp_down):
    w = dict(norm_mix_g=norm_mix_g, w_in=w_in, dw_kernel=dw_kernel, dw_bias=dw_bias,
             conv_ln_g=conv_ln_g, conv_ln_b=conv_ln_b, w_conv_out=w_conv_out, q_norm_g=q_norm_g,
             k_norm_g=k_norm_g, sinks=sinks, w_attn_out=w_attn_out, w_out=w_out,
             norm_ffn_g=norm_ffn_g, w_router_group=w_router_group, b_router_group=b_router_group,
             w_router_expert=w_router_expert, b_router_expert=b_router_expert,
             w_exp_gate=w_exp_gate, w_exp_up=w_exp_up, w_exp_down=w_exp_down)
    B = x.shape[0]
    bias = _bias_band(rel_bias_table).reshape(N_KV_HEADS, Q_PER_KV * WINDOW, 2 * WINDOW)
    for l in range(w_ada.shape[0]):
        mod6 = _modulation(c, w_ada[l], b_ada[l]).reshape(B, 6, D_MODEL)
        x = _layer(x, mod6, bias, l, w)
    return x
```

```python
import functools
import math

import jax
import jax.numpy as jnp
import numpy as np
from jax import lax
from jax.experimental import pallas as pl
from jax.experimental.pallas import tpu as pltpu
from jax.experimental.pallas import tpu_sc as plsc

D_MODEL = 1024
D_CONV = 512
CONV_WIDTH = 31
N_Q_HEADS = 8
N_KV_HEADS = 2
HEAD_DIM = 64
Q_PER_KV = N_Q_HEADS // N_KV_HEADS
WINDOW = 128
N_BUCKETS = 32
MAX_DISTANCE = 128
N_GROUPS = 4
EXPERTS_PER_GROUP = 8
N_EXPERTS = N_GROUPS * EXPERTS_PER_GROUP
TOP_K = 2
D_EXPERT = 256
D_Q = N_Q_HEADS * HEAD_DIM
D_KV = N_KV_HEADS * HEAD_DIM
EPS = 1e-6
NEG_INF = -1e30
LOG2E = math.log2(math.e)

LANES = 128
SEQ_TILE = 512
CONV_ROWS = 64
NORM_BLK = 256
PROJ_CHUNK = 256
CONV_SPAN = 24
MOE_BLOCK = 512
MOE_CHUNKS = 2
DEST_TILE = 8192
COMBINE_TILE = 1024
SC_CORES = 2
SC_SUBCORES = 16
SC_WORKERS = SC_CORES * SC_SUBCORES
SC_CHUNK = 32
ROUTER_COLS = LANES
EXPERT_COL0 = 8
VMEM_LIMIT = 56 * 1024 * 1024

f32 = jnp.float32
bf16 = jnp.bfloat16
i32 = jnp.int32


def _dot(a, b):
    return jnp.dot(a, b, preferred_element_type=f32)


def _split(a):
    hi = a.astype(bf16)
    lo = (a - hi.astype(f32)).astype(bf16)
    return hi, lo


def _pack_halves(x):
    c = x.shape[1] // 2
    hi = lax.bitcast_convert_type(x[:, 0:c].astype(bf16).astype(f32), jnp.uint32)
    lo = lax.bitcast_convert_type(x[:, c:2 * c].astype(bf16).astype(f32), jnp.uint32)
    word = (hi & jnp.uint32(0xFFFF0000)) | (lo >> jnp.uint32(16))
    return lax.bitcast_convert_type(word, i32)


def _unpack_halves(word):
    u = lax.bitcast_convert_type(word, jnp.uint32)
    hi = lax.bitcast_convert_type(u & jnp.uint32(0xFFFF0000), f32)
    lo = lax.bitcast_convert_type(u << jnp.uint32(16), f32)
    return hi, lo


def _dot3(a, b):
    ah, al = _split(a)
    bh, bl = _split(b)
    return _dot(ah, bh) + _dot(al, bh) + _dot(ah, bl)


def _mod_kernel(c_ref, w_ref, b_ref, o_ref):
    c = c_ref[...]
    s = c * jax.nn.sigmoid(c)
    o_ref[...] = _dot3(s, w_ref[...]) + b_ref[...]


def _modulation(c, w_ada, b_ada):
    B, D = c.shape
    n_out = w_ada.shape[1]
    return pl.pallas_call(
        _mod_kernel,
        grid=(n_out // D,),
        in_specs=[pl.BlockSpec((B, D), lambda j: (0, 0)),
                  pl.BlockSpec((D, D), lambda j: (0, j)),
                  pl.BlockSpec((1, D), lambda j: (0, j))],
        out_specs=pl.BlockSpec((B, D), lambda j: (0, j)),
        out_shape=jax.ShapeDtypeStruct((B, n_out), f32),
        name="modulation",
    )(c, w_ada, b_ada.reshape(1, n_out))


def _band_buckets():
    qi = np.arange(WINDOW)[:, None]
    kj = np.arange(2 * WINDOW)[None, :]
    dist = np.clip(qi + WINDOW - kj, 0, MAX_DISTANCE)
    max_exact = N_BUCKETS // 2
    d = np.maximum(dist, 1).astype(np.float32)
    large = max_exact + (np.log(d / np.float32(max_exact)) / np.float32(math.log(MAX_DISTANCE / max_exact))
                         * np.float32(N_BUCKETS - max_exact)).astype(np.int32)
    large = np.minimum(large, N_BUCKETS - 1)
    bucket = np.where(dist < max_exact, dist, large).astype(np.int32)
    raw = qi + WINDOW - kj
    return np.where((raw >= 0) & (raw < WINDOW), bucket, -1).astype(np.int32)


def _bias_kernel(tab_ref, bucket_ref, o_ref):
    bk = bucket_ref[...]
    for h in range(N_Q_HEADS):
        acc = jnp.full(bk.shape, NEG_INF, f32)
        for b in range(N_BUCKETS):
            acc = jnp.where(bk == b, tab_ref[b, h] * LOG2E, acc)
        o_ref[h] = acc


def _bias_band(rel_bias_table):
    return pl.pallas_call(
        _bias_kernel,
        in_specs=[pl.BlockSpec(memory_space=pltpu.SMEM),
                  pl.BlockSpec(memory_space=pltpu.VMEM)],
        out_specs=pl.BlockSpec(memory_space=pltpu.VMEM),
        out_shape=jax.ShapeDtypeStruct((N_Q_HEADS, WINDOW, 2 * WINDOW), f32),
        name="bias_band",
    )(rel_bias_table, jnp.asarray(_band_buckets()))


def _mixer_kernel(x_ref, mod_ref, gmix_ref, gffn_ref, win_ref, dw_ref, dwb_ref, lng_ref, lnb_ref,
                  wco_ref, wao_ref, wout_ref, qg_ref, kg_ref, bq_ref, bk_ref, bias_ref, sink_ref,
                  wr_ref, br_ref, tri_ref,
                  x1_ref, h2_ref, route_ref, wtok_ref, cnt_ref,
                  uext, ush, conv_sc, proj_sc, k2, v2, osc):
    TM = x_ref.shape[0]
    HALO = 32
    b = pl.program_id(0)
    t = pl.program_id(1)
    first = t == 0

    @pl.when(first)
    def _():
        uext[0:HALO, :] = jnp.zeros((HALO, D_CONV), f32)
        k2[:, 0:WINDOW, :] = jnp.zeros((N_KV_HEADS, WINDOW, LANES), bf16)
        v2[:, 0:WINDOW, :] = jnp.zeros((N_KV_HEADS, WINDOW, LANES), bf16)

    @pl.when(first & (b == 0))
    def _():
        cnt_ref[...] = jnp.zeros(cnt_ref.shape, f32)

    x = x_ref[...]
    mod = mod_ref[...]
    sh1, sc1, g1 = mod[0:1, :], mod[1:2, :], mod[2:3, :]
    sh2, sc2, g2 = mod[3:4, :], mod[4:5, :], mod[5:6, :]
    del g2

    ms = jnp.mean(x * x, axis=-1, keepdims=True)
    h = (x * lax.rsqrt(ms + EPS)) * (gmix_ref[...] * (1.0 + sc1)) + sh1
    hb = h.astype(bf16)

    ab = _dot(hb, win_ref[:, 0:2 * D_CONV])
    u = ab[:, 0:D_CONV] * jax.nn.sigmoid(ab[:, D_CONV:2 * D_CONV])
    uext[HALO:HALO + TM, :] = u
    for r in range(1, 8):
        ush[r - 1] = uext[r:r + TM + CONV_SPAN, :]
    n_q = D_Q // PROJ_CHUNK
    n_g = D_MODEL // PROJ_CHUNK
    n_jobs = (win_ref.shape[1] - 2 * D_CONV) // PROJ_CHUNK
    n_rb = TM // CONV_ROWS
    n_t8 = CONV_ROWS // 8
    units = [(cb, rg) for cb in range(D_CONV // LANES) for rg in range(n_rb)]
    job_at = {(i * len(units)) // n_jobs: i for i in range(n_jobs)}
    assert len(job_at) == n_jobs
    for ui, (cb, rg) in enumerate(units):
        if ui in job_at:
            c_lo = 2 * D_CONV + job_at[ui] * PROJ_CHUNK
            proj_sc[job_at[ui]] = _dot(hb, win_ref[:, c_lo:c_lo + PROJ_CHUNK])
        cs = slice(cb * LANES, (cb + 1) * LANES)
        base = rg * CONV_ROWS
        accs = [None] * n_t8
        for r in range(8):
            taps = [(j, (j + 2) // 8) for j in range(CONV_WIDTH) if (j + 2) % 8 == r]
            t_lo = min(a for _, a in taps)
            t_hi = max(a for _, a in taps) + n_t8
            tiles = {}
            for t8 in range(t_lo, t_hi):
                rows = slice(base + 8 * t8, base + 8 * t8 + 8)
                tiles[t8] = uext[rows, cs] if r == 0 else ush[r - 1, rows, cs]
            for j, a in taps:
                tap = dw_ref[j, :, cs]
                for s8 in range(n_t8):
                    term = tap * tiles[a + s8]
                    accs[s8] = term if accs[s8] is None else accs[s8] + term
        for s8 in range(n_t8):
            conv_sc[base + 8 * s8:base + 8 * s8 + 8, cs] = accs[s8] + dwb_ref[:, cs]
    conv = conv_sc[...]
    uext[0:HALO, :] = uext[TM:TM + HALO, :]
    mu = jnp.mean(conv, axis=-1, keepdims=True)
    dc = conv - mu
    var = jnp.mean(dc * dc, axis=-1, keepdims=True)
    yn = dc * lax.rsqrt(var + EPS) * lng_ref[...] + lnb_ref[...]
    act = yn * jax.nn.sigmoid(yn)
    y_conv = _dot(act.astype(bf16), wco_ref[...])

    q = jnp.concatenate([proj_sc[i] for i in range(n_q)], axis=1)
    k = proj_sc[n_q, :, 0:D_KV]
    v = proj_sc[n_q, :, D_KV:2 * D_KV]

    def head_norm(z, blk_ref, g):
        sq = (z * z).astype(bf16)
        blk = blk_ref.shape[0]
        msq = jnp.concatenate([_dot(sq[:, c:c + blk], blk_ref[...]) for c in range(0, z.shape[1], blk)], axis=1)
        return z * lax.rsqrt(msq + EPS) * g

    qn = head_norm(q, bq_ref, qg_ref[...]).astype(bf16)
    kn = head_norm(k, bk_ref, kg_ref[...])
    lo_half = lax.broadcasted_iota(i32, (TM, LANES), 1) < HEAD_DIM
    kr = pltpu.roll(kn, HEAD_DIM, 1)
    vr = pltpu.roll(v, HEAD_DIM, 1)
    k2[0, WINDOW:WINDOW + TM, :] = jnp.where(lo_half, kn, kr).astype(bf16)
    k2[1, WINDOW:WINDOW + TM, :] = jnp.where(lo_half, kr, kn).astype(bf16)
    v2[0, WINDOW:WINDOW + TM, :] = jnp.where(lo_half, v, vr).astype(bf16)
    v2[1, WINDOW:WINDOW + TM, :] = jnp.where(lo_half, vr, v).astype(bf16)

    QROWS = Q_PER_KV * WINDOW
    col = lax.broadcasted_iota(i32, (QROWS, 2 * WINDOW), 1)
    has_prev = col >= jnp.where(first, WINDOW, 0)
    hrow = lax.broadcasted_iota(i32, (QROWS, 1), 0) // WINDOW
    lo128 = lax.broadcasted_iota(i32, (WINDOW, LANES), 1) < HEAD_DIM
    zero_q = jnp.zeros((WINDOW, LANES), bf16)
    for g in range(N_KV_HEADS):
        sink = LOG2E * jnp.where(hrow == 0, sink_ref[4 * g],
                                 jnp.where(hrow == 1, sink_ref[4 * g + 1],
                                           jnp.where(hrow == 2, sink_ref[4 * g + 2], sink_ref[4 * g + 3])))
        bias_g = bias_ref[g]
        for j in range(TM // WINDOW):
            rs = slice(j * WINDOW, (j + 1) * WINDOW)
            qa = qn[rs, 2 * LANES * g:2 * LANES * g + LANES]
            qb = qn[rs, 2 * LANES * g + LANES:2 * LANES * (g + 1)]
            qs = jnp.concatenate([jnp.where(lo128, qa, zero_q), jnp.where(lo128, zero_q, qa),
                                  jnp.where(lo128, qb, zero_q), jnp.where(lo128, zero_q, qb)], axis=0)
            kk = k2[g, j * WINDOW:(j + 2) * WINDOW, :]
            s = lax.dot_general(qs, kk, (((1,), (1,)), ((), ())), preferred_element_type=f32)
            logits = jnp.where(has_prev, s + bias_g, NEG_INF) if j == 0 else s + bias_g
            m = jnp.maximum(jnp.max(logits, axis=-1, keepdims=True), sink)
            p = jnp.exp2(logits - m)
            den = jnp.sum(p, axis=-1, keepdims=True) + jnp.exp2(sink - m)
            o2 = _dot(p.astype(bf16), v2[g, j * WINDOW:(j + 2) * WINDOW, :]) * (1.0 / den)
            osc[rs, 2 * LANES * g:2 * LANES * g + LANES] = jnp.where(
                lo128, o2[0:WINDOW], o2[WINDOW:2 * WINDOW]).astype(bf16)
            osc[rs, 2 * LANES * g + LANES:2 * LANES * (g + 1)] = jnp.where(
                lo128, o2[2 * WINDOW:3 * WINDOW], o2[3 * WINDOW:4 * WINDOW]).astype(bf16)
    k2[:, 0:WINDOW, :] = k2[:, TM:TM + WINDOW, :]
    v2[:, 0:WINDOW, :] = v2[:, TM:TM + WINDOW, :]
    y_attn = _dot(osc[...], wao_ref[...])

    merged = []
    for i in range(n_g):
        cs = slice(i * PROJ_CHUNK, (i + 1) * PROJ_CHUNK)
        g_conv = jax.nn.sigmoid(proj_sc[n_q + 1 + i])
        g_attn = jax.nn.sigmoid(proj_sc[n_q + 1 + n_g + i])
        merged.append((g_conv * y_conv[:, cs] + g_attn * y_attn[:, cs]).astype(bf16))
    merged = jnp.concatenate(merged, axis=1)
    x1 = x + g1 * _dot(merged, wout_ref[...])
    x1_ref[...] = x1

    ms2 = jnp.mean(x1 * x1, axis=-1, keepdims=True)
    h2 = (x1 * lax.rsqrt(ms2 + EPS)) * (gffn_ref[...] * (1.0 + sc2)) + sh2
    h2_ref[...] = _pack_halves(h2)
    lg = _dot(h2.astype(bf16), wr_ref[...]) + br_ref[...]
    lt = lg.T
    gl = lt[0:N_GROUPS, :]
    grow = lax.broadcasted_iota(i32, (N_GROUPS, TM), 0)
    gmax = jnp.max(gl, axis=0, keepdims=True)
    gi = jnp.min(jnp.where(gl == gmax, grow, N_GROUPS), axis=0, keepdims=True)
    p_top = 1.0 / jnp.sum(jnp.exp(gl - gmax), axis=0, keepdims=True)
    sel = lt[EXPERT_COL0:EXPERT_COL0 + EXPERTS_PER_GROUP, :]
    for gg in range(1, N_GROUPS):
        lo_r = EXPERT_COL0 + gg * EXPERTS_PER_GROUP
        sel = jnp.where(gi == gg, lt[lo_r:lo_r + EXPERTS_PER_GROUP, :], sel)
    erow = lax.broadcasted_iota(i32, (EXPERTS_PER_GROUP, TM), 0)
    m1 = jnp.max(sel, axis=0, keepdims=True)
    i1 = jnp.min(jnp.where(sel == m1, erow, EXPERTS_PER_GROUP), axis=0, keepdims=True)
    rest = jnp.where(erow == i1, -jnp.inf, sel)
    m2 = jnp.max(rest, axis=0, keepdims=True)
    i2 = jnp.min(jnp.where(rest == m2, erow, EXPERTS_PER_GROUP), axis=0, keepdims=True)
    z = jnp.sum(jnp.exp(sel - m1), axis=0, keepdims=True)
    v1 = 1.0 / z
    v2nd = jnp.exp(m2 - m1) / z
    w1 = v1 / (v1 + v2nd) * p_top
    w2 = v2nd / (v1 + v2nd) * p_top
    e1 = gi * EXPERTS_PER_GROUP + i1
    e2 = gi * EXPERTS_PER_GROUP + i2

    xrow = lax.broadcasted_iota(i32, (N_EXPERTS, TM), 0)
    oh1 = xrow == e1
    oh2 = xrow == e2
    both = jnp.where(oh1 | oh2, 1.0, 0.0)
    prefix = _dot(both.astype(bf16), tri_ref[...]) + cnt_ref[:, 0:1]
    r1 = jnp.sum(jnp.where(oh1, prefix, 0.0), axis=0, keepdims=True)
    r2 = jnp.sum(jnp.where(oh2, prefix, 0.0), axis=0, keepdims=True)
    cnt_ref[...] = cnt_ref[...] + jnp.sum(both, axis=1, keepdims=True)

    route_ref[...] = jnp.concatenate(
        [e1, e2, r1.astype(i32), r2.astype(i32), jnp.zeros((4, TM), i32)], axis=0)
    wpad = jnp.concatenate([w1, w2, jnp.zeros((LANES - 2, TM), f32)], axis=0)
    wtok_ref[...] = wpad.T


def _mixer(x, mod6, p, b0, B):
    _, T, D = x.shape
    TM = min(SEQ_TILE, T)
    nt = T // TM
    N = B * T
    const = lambda shape: pl.BlockSpec(shape, lambda b, t: (0,) * len(shape))
    in_specs = [
        pl.BlockSpec((None, TM, D), lambda b, t: (b0 + b, t, 0)),
        pl.BlockSpec((None, 6, D), lambda b, t: (b0 + b, 0, 0)),
        const((1, D)), const((1, D)),
        const(p["w_in"].shape),
        const((CONV_WIDTH, 8, D_CONV)), const((8, D_CONV)), const((1, D_CONV)), const((1, D_CONV)),
        const((D_CONV, D)), const((D_Q, D)), const((D, D)),
        const((1, D_Q)), const((1, D_KV)),
        const((NORM_BLK, NORM_BLK)), const((D_KV, D_KV)),
        const((N_KV_HEADS, Q_PER_KV * WINDOW, 2 * WINDOW)),
        pl.BlockSpec(memory_space=pltpu.SMEM),
        const((D, ROUTER_COLS)), const((1, ROUTER_COLS)),
        const((TM, TM)),
    ]
    out_specs = [
        pl.BlockSpec((None, TM, D), lambda b, t: (b, t, 0)),
        pl.BlockSpec((None, TM, D // 2), lambda b, t: (b, t, 0)),
        pl.BlockSpec((8, TM), lambda b, t: (0, b * nt + t)),
        pl.BlockSpec((TM, LANES), lambda b, t: (b * nt + t, 0)),
        pl.BlockSpec((N_EXPERTS, LANES), lambda b, t: (0, 0)),
    ]
    out_shape = [
        jax.ShapeDtypeStruct((B, T, D), f32),
        jax.ShapeDtypeStruct((B, T, D // 2), i32),
        jax.ShapeDtypeStruct((8, N), i32),
        jax.ShapeDtypeStruct((N, LANES), f32),
        jax.ShapeDtypeStruct((N_EXPERTS, LANES), f32),
    ]
    scratch = [
        pltpu.VMEM((TM + 32, D_CONV), f32),
        pltpu.VMEM((7, TM + CONV_SPAN, D_CONV), f32),
        pltpu.VMEM((TM, D_CONV), f32),
        pltpu.VMEM(((p["w_in"].shape[1] - 2 * D_CONV) // PROJ_CHUNK, TM, PROJ_CHUNK), f32),
        pltpu.VMEM((N_KV_HEADS, TM + WINDOW, LANES), bf16),
        pltpu.VMEM((N_KV_HEADS, TM + WINDOW, LANES), bf16),
        pltpu.VMEM((TM, D_Q), bf16),
    ]
    tri = jnp.asarray(np.triu(np.ones((TM, TM), np.float32), 1), bf16)
    return pl.pallas_call(
        _mixer_kernel,
        grid=(B, nt),
        in_specs=in_specs, out_specs=out_specs, out_shape=out_shape, scratch_shapes=scratch,
        compiler_params=pltpu.CompilerParams(
            dimension_semantics=("arbitrary", "arbitrary"), vmem_limit_bytes=VMEM_LIMIT),
        name="mixer_router",
    )(x, mod6, p["gmix"], p["gffn"], p["w_in"], p["dw"], p["dwb"], p["lng"], p["lnb"],
      p["wco"], p["wao"], p["wout"], p["qg"], p["kg"], p["bq"], p["bk"], p["bias"], p["sinks"],
      p["wr"], p["br"], tri)


def _dest_kernel(pstart_ref, route_ref, dest_ref):
    e = route_ref[0:2, :]
    base = jnp.zeros(e.shape, i32)
    for x in range(N_EXPERTS):
        base = jnp.where(e == x, pstart_ref[x], base)
    dest_ref[...] = base + route_ref[2:4, :]


def _dest_slots(pstart, route):
    N = route.shape[1]
    TS = min(DEST_TILE, N)
    return pl.pallas_call(
        _dest_kernel,
        grid=(N // TS,),
        in_specs=[pl.BlockSpec(memory_space=pltpu.SMEM),
                  pl.BlockSpec((8, TS), lambda i: (0, i))],
        out_specs=pl.BlockSpec((2, TS), lambda i: (0, i)),
        out_shape=jax.ShapeDtypeStruct((2, N), i32),
        name="dest_slots",
    )(pstart, route)


def _sc_mesh():
    return plsc.VectorSubcoreMesh(core_axis_name="c", subcore_axis_name="s")


def _sc_scatter(h2, dest, n_slots):
    N, D = h2.shape
    per_w = N // SC_WORKERS
    C = SC_CHUNK
    n_chunks = per_w // C
    dest4 = dest.reshape(TOP_K, SC_WORKERS, n_chunks, C)

    @functools.partial(
        pl.kernel, mesh=_sc_mesh(),
        out_type=jax.ShapeDtypeStruct((n_slots, D), h2.dtype),
        scratch_types=[pltpu.VMEM((n_chunks, C), i32), pltpu.VMEM((n_chunks, C), i32),
                       pltpu.VMEM((2, C, D), h2.dtype),
                       pltpu.SemaphoreType.DMA((2,)), pltpu.SemaphoreType.DMA((2,))],
        name="sc_dispatch_scatter",
    )
    def run(h2_hbm, dest_hbm, xs_hbm, idx0, idx1, buf, sem_in, sem_out):
        wid = lax.axis_index("s") * SC_CORES + lax.axis_index("c")
        base = wid * per_w
        pltpu.sync_copy(dest_hbm.at[0, wid], idx0)
        pltpu.sync_copy(dest_hbm.at[1, wid], idx1)

        def load(j, b):
            return pltpu.make_async_copy(h2_hbm.at[pl.ds(base + j * C, C)], buf.at[b], sem_in.at[b])

        def put(j, b, idx):
            return pltpu.make_async_copy(buf.at[b], xs_hbm.at[idx.at[j]], sem_out.at[b])

        load(0, 0).start()

        @pl.loop(0, n_chunks, step=2)
        def _(j0):
            for b in range(2):
                j = j0 + b
                load(j, b).wait()

                @pl.when(j >= 1)
                def _():
                    put(j - 1, 1 - b, idx0).wait()
                    put(j - 1, 1 - b, idx1).wait()

                @pl.when(j + 1 < n_chunks)
                def _():
                    load(j + 1, 1 - b).start()

                put(j, b, idx0).start()
                put(j, b, idx1).start()

        put(n_chunks - 1, 1, idx0).wait()
        put(n_chunks - 1, 1, idx1).wait()

    return run(h2, dest4)


def _sc_gather(y, dest, N):
    D = y.shape[1]
    per_w = N // SC_WORKERS
    C = SC_CHUNK
    n_chunks = per_w // C
    dest4 = dest.reshape(TOP_K, SC_WORKERS, n_chunks, C)

    @functools.partial(
        pl.kernel, mesh=_sc_mesh(),
        out_type=jax.ShapeDtypeStruct((TOP_K, N, D), y.dtype),
        scratch_types=[pltpu.VMEM((n_chunks, C), i32), pltpu.VMEM((n_chunks, C), i32),
                       pltpu.VMEM((2, C, D), y.dtype),
                       pltpu.SemaphoreType.DMA((2,)), pltpu.SemaphoreType.DMA((2,))],
        name="sc_combine_gather",
    )
    def run(y_hbm, dest_hbm, yg_hbm, idx0, idx1, buf, sem_in, sem_out):
        wid = lax.axis_index("s") * SC_CORES + lax.axis_index("c")
        base = wid * per_w
        pltpu.sync_copy(dest_hbm.at[0, wid], idx0)
        pltpu.sync_copy(dest_hbm.at[1, wid], idx1)
        idx = (idx0, idx1)

        def get(j, k):
            return pltpu.make_async_copy(y_hbm.at[idx[k].at[j]], buf.at[k], sem_in.at[k])

        def put(j, k):
            return pltpu.make_async_copy(buf.at[k], yg_hbm.at[k, pl.ds(base + j * C, C)], sem_out.at[k])

        get(0, 0).start()
        get(0, 1).start()

        @pl.loop(0, n_chunks)
        def _(j):
            for k in range(TOP_K):
                get(j, k).wait()
                put(j, k).start()
            for k in range(TOP_K):
                put(j, k).wait()

                @pl.when(j + 1 < n_chunks)
                def _():
                    get(j + 1, k).start()

    return run(y, dest4)


def _expert_kernel(be_ref, nv_ref, nu_ref, first_ref, slot_ref, nxt_ref,
                   xs_ref, wg_hbm, wu_hbm, wdn_hbm, y_ref,
                   stg_g, stg_u, stg_d, wgu_ref, wd_ref, sems):
    i = pl.program_id(0)
    in_use = i < nu_ref[0]

    def fetch(e, s):
        return (pltpu.make_async_copy(wg_hbm.at[e], stg_g.at[s], sems.at[s, 0]),
                pltpu.make_async_copy(wu_hbm.at[e], stg_u.at[s], sems.at[s, 1]),
                pltpu.make_async_copy(wdn_hbm.at[e], stg_d.at[s], sems.at[s, 2]))

    @pl.when(in_use & (first_ref[i] == 1))
    def _():
        s = slot_ref[i]

        @pl.when(i == 0)
        def _():
            for cp in fetch(be_ref[0], 0):
                cp.start()

        for cp in fetch(be_ref[i], s):
            cp.wait()
        wgu_ref[:, 0:D_EXPERT] = stg_g[s].astype(bf16)
        wgu_ref[:, D_EXPERT:2 * D_EXPERT] = stg_u[s].astype(bf16)
        wd_ref[...] = stg_d[s].astype(bf16)

        @pl.when(nxt_ref[i] >= 0)
        def _():
            for cp in fetch(nxt_ref[i], 1 - s):
                cp.start()

    @pl.when(in_use)
    def _():
        live = lax.broadcasted_iota(i32, (MOE_BLOCK, 1), 0) < nv_ref[i]
        x_hi, x_lo = _unpack_halves(jnp.where(live, xs_ref[...], 0))
        half = x_hi.shape[1]
        gu = (_dot(x_hi.astype(bf16), wgu_ref[0:half, :]) +
              _dot(x_lo.astype(bf16), wgu_ref[half:2 * half, :]))
        gate = gu[:, 0:D_EXPERT]
        hid = (gate * jax.nn.sigmoid(gate)) * gu[:, D_EXPERT:2 * D_EXPERT]
        y_ref[...] = _pack_halves(_dot(hid.astype(bf16), wd_ref[...]))


def _experts(block_e, n_valid, n_used, counts, xs, w_gate, w_up, w_down):
    n_slots, DH = xs.shape
    D = 2 * DH
    n_blocks = n_slots // MOE_BLOCK
    idx = jnp.arange(n_blocks, dtype=i32)
    prev_e = jnp.concatenate([jnp.full((1,), -1, i32), block_e[:-1]])
    first = ((block_e != prev_e) & (idx < n_used[0])).astype(i32)
    slot = ((jnp.cumsum(first) - 1) % 2).astype(i32)
    e_ids = jnp.arange(N_EXPERTS, dtype=i32)
    later = (e_ids[None, :] > e_ids[:, None]) & (counts[None, :] > 0)
    nxt_of_e = jnp.min(jnp.where(later, e_ids[None, :], N_EXPERTS), axis=1)
    nxt = jnp.where(nxt_of_e[block_e] < N_EXPERTS, nxt_of_e[block_e], -1).astype(i32)
    last = lambda i, nu: jnp.minimum(i, nu[0] - 1)
    slot_map = lambda i, be, nv, nu, fi, sl, nx: (last(i, nu), 0)
    grid_spec = pltpu.PrefetchScalarGridSpec(
        num_scalar_prefetch=6,
        grid=(n_blocks,),
        in_specs=[pl.BlockSpec((MOE_BLOCK, DH), slot_map),
                  pl.BlockSpec(memory_space=pl.ANY),
                  pl.BlockSpec(memory_space=pl.ANY),
                  pl.BlockSpec(memory_space=pl.ANY)],
        out_specs=pl.BlockSpec((MOE_BLOCK, DH), slot_map),
        scratch_shapes=[pltpu.VMEM((2, D, D_EXPERT), f32), pltpu.VMEM((2, D, D_EXPERT), f32),
                        pltpu.VMEM((2, D_EXPERT, D), f32),
                        pltpu.VMEM((D, 2 * D_EXPERT), bf16), pltpu.VMEM((D_EXPERT, D), bf16),
                        pltpu.SemaphoreType.DMA((2, 3))],
    )
    return pl.pallas_call(
        _expert_kernel,
        grid_spec=grid_spec,
        out_shape=jax.ShapeDtypeStruct((n_slots, DH), i32),
        compiler_params=pltpu.CompilerParams(dimension_semantics=("arbitrary",),
                                             vmem_limit_bytes=VMEM_LIMIT),
        name="experts",
    )(block_e, n_valid, n_used, first, slot, nxt, xs, w_gate, w_up, w_down)


def _combine_kernel(yg_ref, x1_ref, wtok_ref, mod_ref, *rest):
    o_ref = rest[-1]
    w = wtok_ref[...]
    g2 = mod_ref[5:6, :]
    a_hi, a_lo = _unpack_halves(yg_ref[0])
    b_hi, b_lo = _unpack_halves(yg_ref[1])
    moe = jnp.concatenate([w[:, 0:1] * a_hi + w[:, 1:2] * b_hi,
                           w[:, 0:1] * a_lo + w[:, 1:2] * b_lo], axis=1)
    o_ref[...] = x1_ref[...] + g2 * moe


def _combine(yg, x1, wtok, mod6, T, b0, n_total, out_prev):
    Nc, D = x1.shape
    TS = min(COMBINE_TILE, T)
    per_seq = T // TS
    blk0 = b0 * per_seq
    in_specs = [pl.BlockSpec((TOP_K, TS, D // 2), lambda i: (0, i, 0)),
                pl.BlockSpec((TS, D), lambda i: (i, 0)),
                pl.BlockSpec((TS, LANES), lambda i: (i, 0)),
                pl.BlockSpec((None, 6, D), lambda i: (b0 + i // per_seq, 0, 0))]
    args = [yg, x1, wtok, mod6]
    aliases = {}
    if out_prev is not None:
        in_specs.append(pl.BlockSpec(memory_space=pl.ANY))
        args.append(out_prev)
        aliases = {len(args) - 1: 0}
    return pl.pallas_call(
        _combine_kernel,
        grid=(Nc // TS,),
        in_specs=in_specs,
        out_specs=pl.BlockSpec((TS, D), lambda i: (blk0 + i, 0)),
        out_shape=jax.ShapeDtypeStruct((n_total, D), f32),
        input_output_aliases=aliases,
        compiler_params=pltpu.CompilerParams(dimension_semantics=("arbitrary",),
                                             vmem_limit_bytes=VMEM_LIMIT),
        name="combine",
    )(*args)


def _block_diag_mean(n, blk):
    m = np.kron(np.eye(n // blk, dtype=np.float32), np.full((blk, blk), 1.0 / blk, np.float32))
    return jnp.asarray(m, bf16)


def _layer(x, mod6, bias, l, w):
    B, T, D = x.shape
    N = B * T
    w_rg, w_re = w["w_router_group"][l], w["w_router_expert"][l]
    wr = jnp.zeros((D, ROUTER_COLS), f32)
    wr = wr.at[:, 0:N_GROUPS].set(w_rg).at[:, EXPERT_COL0:EXPERT_COL0 + N_EXPERTS].set(w_re)
    br = jnp.zeros((1, ROUTER_COLS), f32)
    br = br.at[0, 0:N_GROUPS].set(w["b_router_group"][l])
    br = br.at[0, EXPERT_COL0:EXPERT_COL0 + N_EXPERTS].set(w["b_router_expert"][l])
    p = dict(
        gmix=w["norm_mix_g"][l].reshape(1, D), gffn=w["norm_ffn_g"][l].reshape(1, D),
        w_in=w["w_in"][l].astype(bf16),
        dw=jnp.broadcast_to(w["dw_kernel"][l][:, None, :], (CONV_WIDTH, 8, D_CONV)), dwb=jnp.broadcast_to(w["dw_bias"][l][None, :], (8, D_CONV)),
        lng=w["conv_ln_g"][l].reshape(1, D_CONV), lnb=w["conv_ln_b"][l].reshape(1, D_CONV),
        wco=w["w_conv_out"][l].astype(bf16), wao=w["w_attn_out"][l].astype(bf16),
        wout=w["w_out"][l].astype(bf16),
        qg=(jnp.tile(w["q_norm_g"][l], N_Q_HEADS) * (HEAD_DIM ** -0.5 * LOG2E)).reshape(1, D_Q),
        kg=jnp.tile(w["k_norm_g"][l], N_KV_HEADS).reshape(1, D_KV),
        bq=_block_diag_mean(NORM_BLK, HEAD_DIM), bk=_block_diag_mean(D_KV, HEAD_DIM),
        bias=bias, sinks=w["sinks"][l], wr=wr.astype(bf16), br=br,
    )
    w_gate, w_up, w_down = w["w_exp_gate"][l], w["w_exp_up"][l], w["w_exp_down"][l]

    n_chunks = MOE_CHUNKS if B % MOE_CHUNKS == 0 else 1
    Bc = B // n_chunks
    Nc = Bc * T
    n_blocks = -(-(Nc * TOP_K) // MOE_BLOCK) + N_EXPERTS
    blk0 = jnp.arange(n_blocks, dtype=i32) * MOE_BLOCK
    stage = []
    for ch in range(n_chunks):
        x1, h2, route, wtok, cnt = _mixer(x, mod6, p, ch * Bc, Bc)
        counts = cnt[:, 0].astype(i32)
        pcounts = (counts + MOE_BLOCK - 1) // MOE_BLOCK * MOE_BLOCK
        pend = jnp.cumsum(pcounts)
        pstart = pend - pcounts
        block_e = jnp.minimum(jnp.sum((pend[None, :] <= blk0[:, None]).astype(i32), axis=1), N_EXPERTS - 1)
        n_valid = jnp.clip((pstart + counts)[block_e] - blk0, 0, MOE_BLOCK).astype(i32)
        n_used = (pend[-1:] // MOE_BLOCK).astype(i32)
        dest = _dest_slots(pstart, route)
        xs = _sc_scatter(h2.reshape(Nc, D // 2), dest, n_blocks * MOE_BLOCK)
        stage.append((x1, wtok, dest, xs, block_e, n_valid, n_used, counts))
    ys = [_experts(be, nv, nu, cn, xs, w_gate, w_up, w_down) for (_, _, _, xs, be, nv, nu, cn) in stage]
    ygs = [_sc_gather(y, st[2], Nc) for y, st in zip(ys, stage)]
    out = None
    for ch in range(n_chunks):
        x1, wtok = stage[ch][0], stage[ch][1]
        out = _combine(ygs[ch], x1.reshape(Nc, D), wtok, mod6, T, ch * Bc, B * T, out)
    return out.reshape(B, T, D)


def kernel(x, c, w_ada, b_ada, norm_mix_g, w_in, dw_kernel, dw_bias, conv_ln_g, conv_ln_b,
           w_conv_out, q_norm_g, k_norm_g, sinks, w_attn_out, w_out, rel_bias_table, norm_ffn_g,
           w_router_group, b_router_group, w_router_expert, b_router_expert,
           w_exp_gate, w_exp_up, w_exp_down):
    w = dict(norm_mix_g=norm_mix_g, w_in=w_in, dw_kernel=dw_kernel, dw_bias=dw_bias,
             conv_ln_g=conv_ln_g, conv_ln_b=conv_ln_b, w_conv_out=w_conv_out, q_norm_g=q_norm_g,
             k_norm_g=k_norm_g, sinks=sinks, w_attn_out=w_attn_out, w_out=w_out,
             norm_ffn_g=norm_ffn_g, w_router_group=w_router_group, b_router_group=b_router_group,
             w_router_expert=w_router_expert, b_router_expert=b_router_expert,
             w_exp_gate=w_exp_gate, w_exp_up=w_exp_up, w_exp_down=w_exp_down)
    B = x.shape[0]
    bias = _bias_band(rel_bias_table).reshape(N_KV_HEADS, Q_PER_KV * WINDOW, 2 * WINDOW)
    for l in range(w_ada.shape[0]):
        mod6 = _modulation(c, w_ada[l], b_ada[l]).reshape(B, 6, D_MODEL)
        x = _layer(x, mod6, bias, l, w)
    return x
```

```python
import functools
import math

import jax
import jax.numpy as jnp
import numpy as np
from jax import lax
from jax.experimental import pallas as pl
from jax.experimental.pallas import tpu as pltpu
from jax.experimental.pallas import tpu_sc as plsc

D_MODEL = 1024
D_CONV = 512
CONV_WIDTH = 31
N_Q_HEADS = 8
N_KV_HEADS = 2
HEAD_DIM = 64
Q_PER_KV = N_Q_HEADS // N_KV_HEADS
WINDOW = 128
N_BUCKETS = 32
MAX_DISTANCE = 128
N_GROUPS = 4
EXPERTS_PER_GROUP = 8
N_EXPERTS = N_GROUPS * EXPERTS_PER_GROUP
TOP_K = 2
D_EXPERT = 256
D_Q = N_Q_HEADS * HEAD_DIM
D_KV = N_KV_HEADS * HEAD_DIM
EPS = 1e-6
NEG_INF = -1e30
LOG2E = math.log2(math.e)

LANES = 128
SEQ_TILE = 512
CONV_ROWS = 64
NORM_BLK = 256
PROJ_CHUNK = 256
CONV_SPAN = 24
MOE_BLOCK = 512
MOE_CHUNKS = 2
DEST_TILE = 8192
COMBINE_TILE = 1024
SC_CORES = 2
SC_SUBCORES = 16
SC_WORKERS = SC_CORES * SC_SUBCORES
SC_CHUNK = 32
ROUTER_COLS = LANES
EXPERT_COL0 = 8
VMEM_LIMIT = 56 * 1024 * 1024

f32 = jnp.float32
bf16 = jnp.bfloat16
i32 = jnp.int32


def _dot(a, b):
    return jnp.dot(a, b, preferred_element_type=f32)


def _split(a):
    hi = a.astype(bf16)
    lo = (a - hi.astype(f32)).astype(bf16)
    return hi, lo


def _pack_halves(x):
    c = x.shape[1] // 2
    word = pltpu.pack_elementwise([x[:, 0:c], x[:, c:2 * c]], packed_dtype=bf16)
    return lax.bitcast_convert_type(word, i32)


def _unpack_halves(word):
    u = lax.bitcast_convert_type(word, jnp.uint32)
    first = pltpu.unpack_elementwise(u, index=0, packed_dtype=bf16, unpacked_dtype=f32)
    second = pltpu.unpack_elementwise(u, index=1, packed_dtype=bf16, unpacked_dtype=f32)
    return first, second


def _dot3(a, b):
    ah, al = _split(a)
    bh, bl = _split(b)
    return _dot(ah, bh) + _dot(al, bh) + _dot(ah, bl)


def _mod_kernel(c_ref, w_ref, b_ref, o_ref):
    c = c_ref[...]
    s = c * jax.nn.sigmoid(c)
    o_ref[...] = _dot3(s, w_ref[...]) + b_ref[...]


def _modulation(c, w_ada, b_ada):
    B, D = c.shape
    n_out = w_ada.shape[1]
    return pl.pallas_call(
        _mod_kernel,
        grid=(n_out // D,),
        in_specs=[pl.BlockSpec((B, D), lambda j: (0, 0)),
                  pl.BlockSpec((D, D), lambda j: (0, j)),
                  pl.BlockSpec((1, D), lambda j: (0, j))],
        out_specs=pl.BlockSpec((B, D), lambda j: (0, j)),
        out_shape=jax.ShapeDtypeStruct((B, n_out), f32),
        name="modulation",
    )(c, w_ada, b_ada.reshape(1, n_out))


def _band_buckets():
    qi = np.arange(WINDOW)[:, None]
    kj = np.arange(2 * WINDOW)[None, :]
    dist = np.clip(qi + WINDOW - kj, 0, MAX_DISTANCE)
    max_exact = N_BUCKETS // 2
    d = np.maximum(dist, 1).astype(np.float32)
    large = max_exact + (np.log(d / np.float32(max_exact)) / np.float32(math.log(MAX_DISTANCE / max_exact))
                         * np.float32(N_BUCKETS - max_exact)).astype(np.int32)
    large = np.minimum(large, N_BUCKETS - 1)
    bucket = np.where(dist < max_exact, dist, large).astype(np.int32)
    raw = qi + WINDOW - kj
    return np.where((raw >= 0) & (raw < WINDOW), bucket, -1).astype(np.int32)


def _bias_kernel(tab_ref, bucket_ref, o_ref):
    bk = bucket_ref[...]
    for h in range(N_Q_HEADS):
        acc = jnp.full(bk.shape, NEG_INF, f32)
        for b in range(N_BUCKETS):
            acc = jnp.where(bk == b, tab_ref[b, h] * LOG2E, acc)
        o_ref[h] = acc


def _bias_band(rel_bias_table):
    return pl.pallas_call(
        _bias_kernel,
        in_specs=[pl.BlockSpec(memory_space=pltpu.SMEM),
                  pl.BlockSpec(memory_space=pltpu.VMEM)],
        out_specs=pl.BlockSpec(memory_space=pltpu.VMEM),
        out_shape=jax.ShapeDtypeStruct((N_Q_HEADS, WINDOW, 2 * WINDOW), f32),
        name="bias_band",
    )(rel_bias_table, jnp.asarray(_band_buckets()))


def _mixer_kernel(x_ref, mod_ref, gmix_ref, gffn_ref, win_ref, dw_ref, dwb_ref, lng_ref, lnb_ref,
                  wco_ref, wao_ref, wout_ref, qg_ref, kg_ref, bq_ref, bk_ref, bias_ref, sink_ref,
                  wrh_ref, wr2_ref, br_ref, tri_ref,
                  x1_ref, h2_ref, route_ref, wtok_ref, cnt_ref,
                  uext, ush, conv_sc, proj_sc, k2, v2, osc):
    TM = x_ref.shape[0]
    HALO = 32
    b = pl.program_id(0)
    t = pl.program_id(1)
    first = t == 0

    @pl.when(first)
    def _():
        uext[0:HALO, :] = jnp.zeros((HALO, D_CONV), f32)
        k2[:, 0:WINDOW, :] = jnp.zeros((N_KV_HEADS, WINDOW, LANES), bf16)
        v2[:, 0:WINDOW, :] = jnp.zeros((N_KV_HEADS, WINDOW, LANES), bf16)

    @pl.when(first & (b == 0))
    def _():
        cnt_ref[...] = jnp.zeros(cnt_ref.shape, f32)

    x = x_ref[...]
    mod = mod_ref[...]
    sh1, sc1, g1 = mod[0:1, :], mod[1:2, :], mod[2:3, :]
    sh2, sc2, g2 = mod[3:4, :], mod[4:5, :], mod[5:6, :]
    del g2

    ms = jnp.mean(x * x, axis=-1, keepdims=True)
    h = (x * lax.rsqrt(ms + EPS)) * (gmix_ref[...] * (1.0 + sc1)) + sh1
    hb = h.astype(bf16)

    ab = _dot(hb, win_ref[:, 0:2 * D_CONV])
    u = ab[:, 0:D_CONV] * jax.nn.sigmoid(ab[:, D_CONV:2 * D_CONV])
    uext[HALO:HALO + TM, :] = u
    for r in range(1, 8):
        ush[r - 1] = uext[r:r + TM + CONV_SPAN, :]
    n_q = D_Q // PROJ_CHUNK
    n_g = D_MODEL // PROJ_CHUNK
    n_jobs = (win_ref.shape[1] - 2 * D_CONV) // PROJ_CHUNK
    n_rb = TM // CONV_ROWS
    n_t8 = CONV_ROWS // 8
    units = [(cb, rg) for cb in range(D_CONV // LANES) for rg in range(n_rb)]
    job_at = {(i * len(units)) // n_jobs: i for i in range(n_jobs)}
    assert len(job_at) == n_jobs
    for ui, (cb, rg) in enumerate(units):
        if ui in job_at:
            c_lo = 2 * D_CONV + job_at[ui] * PROJ_CHUNK
            proj_sc[job_at[ui]] = _dot(hb, win_ref[:, c_lo:c_lo + PROJ_CHUNK])
        cs = slice(cb * LANES, (cb + 1) * LANES)
        base = rg * CONV_ROWS
        accs = [None] * n_t8
        for r in range(8):
            taps = [(j, (j + 2) // 8) for j in range(CONV_WIDTH) if (j + 2) % 8 == r]
            t_lo = min(a for _, a in taps)
            t_hi = max(a for _, a in taps) + n_t8
            tiles = {}
            for t8 in range(t_lo, t_hi):
                rows = slice(base + 8 * t8, base + 8 * t8 + 8)
                tiles[t8] = uext[rows, cs] if r == 0 else ush[r - 1, rows, cs]
            for j, a in taps:
                tap = dw_ref[j, :, cs]
                for s8 in range(n_t8):
                    term = tap * tiles[a + s8]
                    accs[s8] = term if accs[s8] is None else accs[s8] + term
        for s8 in range(n_t8):
            conv_sc[base + 8 * s8:base + 8 * s8 + 8, cs] = accs[s8] + dwb_ref[:, cs]
    conv = conv_sc[...]
    uext[0:HALO, :] = uext[TM:TM + HALO, :]
    mu = jnp.mean(conv, axis=-1, keepdims=True)
    dc = conv - mu
    var = jnp.mean(dc * dc, axis=-1, keepdims=True)
    yn = dc * lax.rsqrt(var + EPS) * lng_ref[...] + lnb_ref[...]
    act = yn * jax.nn.sigmoid(yn)
    y_conv = _dot(act.astype(bf16), wco_ref[...])

    q = jnp.concatenate([proj_sc[i] for i in range(n_q)], axis=1)
    k = proj_sc[n_q, :, 0:D_KV]
    v = proj_sc[n_q, :, D_KV:2 * D_KV]

    def head_norm(z, blk_ref, g):
        sq = (z * z).astype(bf16)
        blk = blk_ref.shape[0]
        msq = jnp.concatenate([_dot(sq[:, c:c + blk], blk_ref[...]) for c in range(0, z.shape[1], blk)], axis=1)
        return z * lax.rsqrt(msq + EPS) * g

    qn = head_norm(q, bq_ref, qg_ref[...]).astype(bf16)
    kn = head_norm(k, bk_ref, kg_ref[...])
    lo_half = lax.broadcasted_iota(i32, (TM, LANES), 1) < HEAD_DIM
    kr = pltpu.roll(kn, HEAD_DIM, 1)
    vr = pltpu.roll(v, HEAD_DIM, 1)
    k2[0, WINDOW:WINDOW + TM, :] = jnp.where(lo_half, kn, kr).astype(bf16)
    k2[1, WINDOW:WINDOW + TM, :] = jnp.where(lo_half, kr, kn).astype(bf16)
    v2[0, WINDOW:WINDOW + TM, :] = jnp.where(lo_half, v, vr).astype(bf16)
    v2[1, WINDOW:WINDOW + TM, :] = jnp.where(lo_half, vr, v).astype(bf16)

    QROWS = Q_PER_KV * WINDOW
    col = lax.broadcasted_iota(i32, (QROWS, 2 * WINDOW), 1)
    has_prev = col >= jnp.where(first, WINDOW, 0)
    hrow = lax.broadcasted_iota(i32, (QROWS, 1), 0) // WINDOW
    lo128 = lax.broadcasted_iota(i32, (WINDOW, LANES), 1) < HEAD_DIM
    zero_q = jnp.zeros((WINDOW, LANES), bf16)
    for g in range(N_KV_HEADS):
        sink = LOG2E * jnp.where(hrow == 0, sink_ref[4 * g],
                                 jnp.where(hrow == 1, sink_ref[4 * g + 1],
                                           jnp.where(hrow == 2, sink_ref[4 * g + 2], sink_ref[4 * g + 3])))
        bias_g = bias_ref[g]
        for j in range(TM // WINDOW):
            rs = slice(j * WINDOW, (j + 1) * WINDOW)
            qa = qn[rs, 2 * LANES * g:2 * LANES * g + LANES]
            qb = qn[rs, 2 * LANES * g + LANES:2 * LANES * (g + 1)]
            qs = jnp.concatenate([jnp.where(lo128, qa, zero_q), jnp.where(lo128, zero_q, qa),
                                  jnp.where(lo128, qb, zero_q), jnp.where(lo128, zero_q, qb)], axis=0)
            kk = k2[g, j * WINDOW:(j + 2) * WINDOW, :]
            s = lax.dot_general(qs, kk, (((1,), (1,)), ((), ())), preferred_element_type=f32)
            logits = jnp.where(has_prev, s + bias_g, NEG_INF) if j == 0 else s + bias_g
            m = jnp.maximum(jnp.max(logits, axis=-1, keepdims=True), sink)
            p = jnp.exp2(logits - m)
            den = jnp.sum(p, axis=-1, keepdims=True) + jnp.exp2(sink - m)
            o2 = _dot(p.astype(bf16), v2[g, j * WINDOW:(j + 2) * WINDOW, :]) * (1.0 / den)
            osc[rs, 2 * LANES * g:2 * LANES * g + LANES] = jnp.where(
                lo128, o2[0:WINDOW], o2[WINDOW:2 * WINDOW]).astype(bf16)
            osc[rs, 2 * LANES * g + LANES:2 * LANES * (g + 1)] = jnp.where(
                lo128, o2[2 * WINDOW:3 * WINDOW], o2[3 * WINDOW:4 * WINDOW]).astype(bf16)
    k2[:, 0:WINDOW, :] = k2[:, TM:TM + WINDOW, :]
    v2[:, 0:WINDOW, :] = v2[:, TM:TM + WINDOW, :]
    y_attn = _dot(osc[...], wao_ref[...])

    merged = []
    for i in range(n_g):
        cs = slice(i * PROJ_CHUNK, (i + 1) * PROJ_CHUNK)
        g_conv = jax.nn.sigmoid(proj_sc[n_q + 1 + i])
        g_attn = jax.nn.sigmoid(proj_sc[n_q + 1 + n_g + i])
        merged.append((g_conv * y_conv[:, cs] + g_attn * y_attn[:, cs]).astype(bf16))
    merged = jnp.concatenate(merged, axis=1)
    x1 = x + g1 * _dot(merged, wout_ref[...])
    x1_ref[...] = x1

    ms2 = jnp.mean(x1 * x1, axis=-1, keepdims=True)
    h2 = (x1 * lax.rsqrt(ms2 + EPS)) * (gffn_ref[...] * (1.0 + sc2)) + sh2
    h2_ref[...] = _pack_halves(h2)
    hh, hl = _split(h2)
    hw = _dot(hh, wr2_ref[...])
    lg = hw[:, 0:ROUTER_COLS] + hw[:, ROUTER_COLS:2 * ROUTER_COLS] + _dot(hl, wrh_ref[...]) + br_ref[...]
    lt = lg.T
    gl = lt[0:N_GROUPS, :]
    grow = lax.broadcasted_iota(i32, (N_GROUPS, TM), 0)
    gmax = jnp.max(gl, axis=0, keepdims=True)
    gi = jnp.min(jnp.where(gl == gmax, grow, N_GROUPS), axis=0, keepdims=True)
    p_top = 1.0 / jnp.sum(jnp.exp(gl - gmax), axis=0, keepdims=True)
    sel = lt[EXPERT_COL0:EXPERT_COL0 + EXPERTS_PER_GROUP, :]
    for gg in range(1, N_GROUPS):
        lo_r = EXPERT_COL0 + gg * EXPERTS_PER_GROUP
        sel = jnp.where(gi == gg, lt[lo_r:lo_r + EXPERTS_PER_GROUP, :], sel)
    erow = lax.broadcasted_iota(i32, (EXPERTS_PER_GROUP, TM), 0)
    m1 = jnp.max(sel, axis=0, keepdims=True)
    i1 = jnp.min(jnp.where(sel == m1, erow, EXPERTS_PER_GROUP), axis=0, keepdims=True)
    rest = jnp.where(erow == i1, -jnp.inf, sel)
    m2 = jnp.max(rest, axis=0, keepdims=True)
    i2 = jnp.min(jnp.where(rest == m2, erow, EXPERTS_PER_GROUP), axis=0, keepdims=True)
    z = jnp.sum(jnp.exp(sel - m1), axis=0, keepdims=True)
    v1 = 1.0 / z
    v2nd = jnp.exp(m2 - m1) / z
    w1 = v1 / (v1 + v2nd) * p_top
    w2 = v2nd / (v1 + v2nd) * p_top
    e1 = gi * EXPERTS_PER_GROUP + i1
    e2 = gi * EXPERTS_PER_GROUP + i2

    xrow = lax.broadcasted_iota(i32, (N_EXPERTS, TM), 0)
    oh1 = xrow == e1
    oh2 = xrow == e2
    both = jnp.where(oh1 | oh2, 1.0, 0.0)
    prefix = _dot(both.astype(bf16), tri_ref[...]) + cnt_ref[:, 0:1]
    r1 = jnp.sum(jnp.where(oh1, prefix, 0.0), axis=0, keepdims=True)
    r2 = jnp.sum(jnp.where(oh2, prefix, 0.0), axis=0, keepdims=True)
    cnt_ref[...] = cnt_ref[...] + jnp.sum(both, axis=1, keepdims=True)

    route_ref[...] = jnp.concatenate(
        [e1, e2, r1.astype(i32), r2.astype(i32), jnp.zeros((4, TM), i32)], axis=0)
    wpad = jnp.concatenate([w1, w2, jnp.zeros((LANES - 2, TM), f32)], axis=0)
    wtok_ref[...] = wpad.T


def _mixer(x, mod6, p, b0, B):
    _, T, D = x.shape
    TM = min(SEQ_TILE, T)
    nt = T // TM
    N = B * T
    const = lambda shape: pl.BlockSpec(shape, lambda b, t: (0,) * len(shape))
    in_specs = [
        pl.BlockSpec((None, TM, D), lambda b, t: (b0 + b, t, 0)),
        pl.BlockSpec((None, 6, D), lambda b, t: (b0 + b, 0, 0)),
        const((1, D)), const((1, D)),
        const(p["w_in"].shape),
        const((CONV_WIDTH, 8, D_CONV)), const((8, D_CONV)), const((1, D_CONV)), const((1, D_CONV)),
        const((D_CONV, D)), const((D_Q, D)), const((D, D)),
        const((1, D_Q)), const((1, D_KV)),
        const((NORM_BLK, NORM_BLK)), const((D_KV, D_KV)),
        const((N_KV_HEADS, Q_PER_KV * WINDOW, 2 * WINDOW)),
        pl.BlockSpec(memory_space=pltpu.SMEM),
        const((D, ROUTER_COLS)), const((D, 2 * ROUTER_COLS)), const((1, ROUTER_COLS)),
        const((TM, TM)),
    ]
    out_specs = [
        pl.BlockSpec((None, TM, D), lambda b, t: (b, t, 0)),
        pl.BlockSpec((None, TM, D // 2), lambda b, t: (b, t, 0)),
        pl.BlockSpec((8, TM), lambda b, t: (0, b * nt + t)),
        pl.BlockSpec((TM, LANES), lambda b, t: (b * nt + t, 0)),
        pl.BlockSpec((N_EXPERTS, LANES), lambda b, t: (0, 0)),
    ]
    out_shape = [
        jax.ShapeDtypeStruct((B, T, D), f32),
        jax.ShapeDtypeStruct((B, T, D // 2), i32),
        jax.ShapeDtypeStruct((8, N), i32),
        jax.ShapeDtypeStruct((N, LANES), f32),
        jax.ShapeDtypeStruct((N_EXPERTS, LANES), f32),
    ]
    scratch = [
        pltpu.VMEM((TM + 32, D_CONV), f32),
        pltpu.VMEM((7, TM + CONV_SPAN, D_CONV), f32),
        pltpu.VMEM((TM, D_CONV), f32),
        pltpu.VMEM(((p["w_in"].shape[1] - 2 * D_CONV) // PROJ_CHUNK, TM, PROJ_CHUNK), f32),
        pltpu.VMEM((N_KV_HEADS, TM + WINDOW, LANES), bf16),
        pltpu.VMEM((N_KV_HEADS, TM + WINDOW, LANES), bf16),
        pltpu.VMEM((TM, D_Q), bf16),
    ]
    tri = jnp.asarray(np.triu(np.ones((TM, TM), np.float32), 1), bf16)
    return pl.pallas_call(
        _mixer_kernel,
        grid=(B, nt),
        in_specs=in_specs, out_specs=out_specs, out_shape=out_shape, scratch_shapes=scratch,
        compiler_params=pltpu.CompilerParams(
            dimension_semantics=("arbitrary", "arbitrary"), vmem_limit_bytes=VMEM_LIMIT),
        name="mixer_router",
    )(x, mod6, p["gmix"], p["gffn"], p["w_in"], p["dw"], p["dwb"], p["lng"], p["lnb"],
      p["wco"], p["wao"], p["wout"], p["qg"], p["kg"], p["bq"], p["bk"], p["bias"], p["sinks"],
      p["wrh"], p["wr2"], p["br"], tri)


def _dest_kernel(pstart_ref, route_ref, dest_ref):
    e = route_ref[0:2, :]
    base = jnp.zeros(e.shape, i32)
    for x in range(N_EXPERTS):
        base = jnp.where(e == x, pstart_ref[x], base)
    dest_ref[...] = base + route_ref[2:4, :]


def _dest_slots(pstart, route):
    N = route.shape[1]
    TS = min(DEST_TILE, N)
    return pl.pallas_call(
        _dest_kernel,
        grid=(N // TS,),
        in_specs=[pl.BlockSpec(memory_space=pltpu.SMEM),
                  pl.BlockSpec((8, TS), lambda i: (0, i))],
        out_specs=pl.BlockSpec((2, TS), lambda i: (0, i)),
        out_shape=jax.ShapeDtypeStruct((2, N), i32),
        name="dest_slots",
    )(pstart, route)


def _sc_mesh():
    return plsc.VectorSubcoreMesh(core_axis_name="c", subcore_axis_name="s")


def _sc_scatter(h2, dest, n_slots):
    N, D = h2.shape
    per_w = N // SC_WORKERS
    C = SC_CHUNK
    n_chunks = per_w // C
    dest4 = dest.reshape(TOP_K, SC_WORKERS, n_chunks, C)

    @functools.partial(
        pl.kernel, mesh=_sc_mesh(),
        out_type=jax.ShapeDtypeStruct((n_slots, D), h2.dtype),
        scratch_types=[pltpu.VMEM((n_chunks, C), i32), pltpu.VMEM((n_chunks, C), i32),
                       pltpu.VMEM((2, C, D), h2.dtype),
                       pltpu.SemaphoreType.DMA((2,)), pltpu.SemaphoreType.DMA((2,))],
        name="sc_dispatch_scatter",
    )
    def run(h2_hbm, dest_hbm, xs_hbm, idx0, idx1, buf, sem_in, sem_out):
        wid = lax.axis_index("s") * SC_CORES + lax.axis_index("c")
        base = wid * per_w
        pltpu.sync_copy(dest_hbm.at[0, wid], idx0)
        pltpu.sync_copy(dest_hbm.at[1, wid], idx1)

        def load(j, b):
            return pltpu.make_async_copy(h2_hbm.at[pl.ds(base + j * C, C)], buf.at[b], sem_in.at[b])

        def put(j, b, idx):
            return pltpu.make_async_copy(buf.at[b], xs_hbm.at[idx.at[j]], sem_out.at[b])

        load(0, 0).start()

        @pl.loop(0, n_chunks, step=2)
        def _(j0):
            for b in range(2):
                j = j0 + b
                load(j, b).wait()

                @pl.when(j >= 1)
                def _():
                    put(j - 1, 1 - b, idx0).wait()
                    put(j - 1, 1 - b, idx1).wait()

                @pl.when(j + 1 < n_chunks)
                def _():
                    load(j + 1, 1 - b).start()

                put(j, b, idx0).start()
                put(j, b, idx1).start()

        put(n_chunks - 1, 1, idx0).wait()
        put(n_chunks - 1, 1, idx1).wait()

    return run(h2, dest4)


def _sc_gather(y, dest, N):
    D = y.shape[1]
    per_w = N // SC_WORKERS
    C = SC_CHUNK
    n_chunks = per_w // C
    dest4 = dest.reshape(TOP_K, SC_WORKERS, n_chunks, C)

    @functools.partial(
        pl.kernel, mesh=_sc_mesh(),
        out_type=jax.ShapeDtypeStruct((TOP_K, N, D), y.dtype),
        scratch_types=[pltpu.VMEM((n_chunks, C), i32), pltpu.VMEM((n_chunks, C), i32),
                       pltpu.VMEM((2, C, D), y.dtype),
                       pltpu.SemaphoreType.DMA((2,)), pltpu.SemaphoreType.DMA((2,))],
        name="sc_combine_gather",
    )
    def run(y_hbm, dest_hbm, yg_hbm, idx0, idx1, buf, sem_in, sem_out):
        wid = lax.axis_index("s") * SC_CORES + lax.axis_index("c")
        base = wid * per_w
        pltpu.sync_copy(dest_hbm.at[0, wid], idx0)
        pltpu.sync_copy(dest_hbm.at[1, wid], idx1)
        idx = (idx0, idx1)

        def get(j, k):
            return pltpu.make_async_copy(y_hbm.at[idx[k].at[j]], buf.at[k], sem_in.at[k])

        def put(j, k):
            return pltpu.make_async_copy(buf.at[k], yg_hbm.at[k, pl.ds(base + j * C, C)], sem_out.at[k])

        get(0, 0).start()
        get(0, 1).start()

        @pl.loop(0, n_chunks)
        def _(j):
            for k in range(TOP_K):
                get(j, k).wait()
                put(j, k).start()
            for k in range(TOP_K):
                put(j, k).wait()

                @pl.when(j + 1 < n_chunks)
                def _():
                    get(j + 1, k).start()

    return run(y, dest4)


def _expert_kernel(be_ref, nv_ref, nu_ref, first_ref, slot_ref, nxt_ref,
                   xs_ref, wg_hbm, wu_hbm, wdn_hbm, y_ref,
                   stg_g, stg_u, stg_d, wgu_ref, wd_ref, sems):
    i = pl.program_id(0)
    in_use = i < nu_ref[0]

    def fetch(e, s):
        return (pltpu.make_async_copy(wg_hbm.at[e], stg_g.at[s], sems.at[s, 0]),
                pltpu.make_async_copy(wu_hbm.at[e], stg_u.at[s], sems.at[s, 1]),
                pltpu.make_async_copy(wdn_hbm.at[e], stg_d.at[s], sems.at[s, 2]))

    @pl.when(in_use & (first_ref[i] == 1))
    def _():
        s = slot_ref[i]

        @pl.when(i == 0)
        def _():
            for cp in fetch(be_ref[0], 0):
                cp.start()

        for cp in fetch(be_ref[i], s):
            cp.wait()
        wgu_ref[:, 0:D_EXPERT] = stg_g[s].astype(bf16)
        wgu_ref[:, D_EXPERT:2 * D_EXPERT] = stg_u[s].astype(bf16)
        wd_ref[...] = stg_d[s].astype(bf16)

        @pl.when(nxt_ref[i] >= 0)
        def _():
            for cp in fetch(nxt_ref[i], 1 - s):
                cp.start()

    @pl.when(in_use)
    def _():
        live = lax.broadcasted_iota(i32, (MOE_BLOCK, 1), 0) < nv_ref[i]
        x_hi, x_lo = _unpack_halves(jnp.where(live, xs_ref[...], 0))
        half = x_hi.shape[1]
        gu = (_dot(x_hi.astype(bf16), wgu_ref[0:half, :]) +
              _dot(x_lo.astype(bf16), wgu_ref[half:2 * half, :]))
        gate = gu[:, 0:D_EXPERT]
        hid = (gate * jax.nn.sigmoid(gate)) * gu[:, D_EXPERT:2 * D_EXPERT]
        y_ref[...] = _pack_halves(_dot(hid.astype(bf16), wd_ref[...]))


def _experts(block_e, n_valid, n_used, counts, xs, w_gate, w_up, w_down):
    n_slots, DH = xs.shape
    D = 2 * DH
    n_blocks = n_slots // MOE_BLOCK
    idx = jnp.arange(n_blocks, dtype=i32)
    prev_e = jnp.concatenate([jnp.full((1,), -1, i32), block_e[:-1]])
    first = ((block_e != prev_e) & (idx < n_used[0])).astype(i32)
    slot = ((jnp.cumsum(first) - 1) % 2).astype(i32)
    e_ids = jnp.arange(N_EXPERTS, dtype=i32)
    later = (e_ids[None, :] > e_ids[:, None]) & (counts[None, :] > 0)
    nxt_of_e = jnp.min(jnp.where(later, e_ids[None, :], N_EXPERTS), axis=1)
    nxt = jnp.where(nxt_of_e[block_e] < N_EXPERTS, nxt_of_e[block_e], -1).astype(i32)
    last = lambda i, nu: jnp.minimum(i, nu[0] - 1)
    slot_map = lambda i, be, nv, nu, fi, sl, nx: (last(i, nu), 0)
    grid_spec = pltpu.PrefetchScalarGridSpec(
        num_scalar_prefetch=6,
        grid=(n_blocks,),
        in_specs=[pl.BlockSpec((MOE_BLOCK, DH), slot_map),
                  pl.BlockSpec(memory_space=pl.ANY),
                  pl.BlockSpec(memory_space=pl.ANY),
                  pl.BlockSpec(memory_space=pl.ANY)],
        out_specs=pl.BlockSpec((MOE_BLOCK, DH), slot_map),
        scratch_shapes=[pltpu.VMEM((2, D, D_EXPERT), f32), pltpu.VMEM((2, D, D_EXPERT), f32),
                        pltpu.VMEM((2, D_EXPERT, D), f32),
                        pltpu.VMEM((D, 2 * D_EXPERT), bf16), pltpu.VMEM((D_EXPERT, D), bf16),
                        pltpu.SemaphoreType.DMA((2, 3))],
    )
    return pl.pallas_call(
        _expert_kernel,
        grid_spec=grid_spec,
        out_shape=jax.ShapeDtypeStruct((n_slots, DH), i32),
        compiler_params=pltpu.CompilerParams(dimension_semantics=("arbitrary",),
                                             vmem_limit_bytes=VMEM_LIMIT),
        name="experts",
    )(block_e, n_valid, n_used, first, slot, nxt, xs, w_gate, w_up, w_down)


def _combine_kernel(yg_ref, x1_ref, wtok_ref, mod_ref, *rest):
    o_ref = rest[-1]
    w = wtok_ref[...]
    g2 = mod_ref[5:6, :]
    a_hi, a_lo = _unpack_halves(yg_ref[0])
    b_hi, b_lo = _unpack_halves(yg_ref[1])
    moe = jnp.concatenate([w[:, 0:1] * a_hi + w[:, 1:2] * b_hi,
                           w[:, 0:1] * a_lo + w[:, 1:2] * b_lo], axis=1)
    o_ref[...] = x1_ref[...] + g2 * moe


def _combine(yg, x1, wtok, mod6, T, b0, n_total, out_prev):
    Nc, D = x1.shape
    TS = min(COMBINE_TILE, T)
    per_seq = T // TS
    blk0 = b0 * per_seq
    in_specs = [pl.BlockSpec((TOP_K, TS, D // 2), lambda i: (0, i, 0)),
                pl.BlockSpec((TS, D), lambda i: (i, 0)),
                pl.BlockSpec((TS, LANES), lambda i: (i, 0)),
                pl.BlockSpec((None, 6, D), lambda i: (b0 + i // per_seq, 0, 0))]
    args = [yg, x1, wtok, mod6]
    aliases = {}
    if out_prev is not None:
        in_specs.append(pl.BlockSpec(memory_space=pl.ANY))
        args.append(out_prev)
        aliases = {len(args) - 1: 0}
    return pl.pallas_call(
        _combine_kernel,
        grid=(Nc // TS,),
        in_specs=in_specs,
        out_specs=pl.BlockSpec((TS, D), lambda i: (blk0 + i, 0)),
        out_shape=jax.ShapeDtypeStruct((n_total, D), f32),
        input_output_aliases=aliases,
        compiler_params=pltpu.CompilerParams(dimension_semantics=("arbitrary",),
                                             vmem_limit_bytes=VMEM_LIMIT),
        name="combine",
    )(*args)


def _block_diag_mean(n, blk):
    m = np.kron(np.eye(n // blk, dtype=np.float32), np.full((blk, blk), 1.0 / blk, np.float32))
    return jnp.asarray(m, bf16)


def _layer(x, mod6, bias, l, w):
    B, T, D = x.shape
    N = B * T
    w_rg, w_re = w["w_router_group"][l], w["w_router_expert"][l]
    wr = jnp.zeros((D, ROUTER_COLS), f32)
    wr = wr.at[:, 0:N_GROUPS].set(w_rg).at[:, EXPERT_COL0:EXPERT_COL0 + N_EXPERTS].set(w_re)
    br = jnp.zeros((1, ROUTER_COLS), f32)
    br = br.at[0, 0:N_GROUPS].set(w["b_router_group"][l])
    br = br.at[0, EXPERT_COL0:EXPERT_COL0 + N_EXPERTS].set(w["b_router_expert"][l])
    wrh = wr.astype(bf16)
    wrl = (wr - wrh.astype(f32)).astype(bf16)
    p = dict(
        gmix=w["norm_mix_g"][l].reshape(1, D), gffn=w["norm_ffn_g"][l].reshape(1, D),
        w_in=w["w_in"][l].astype(bf16),
        dw=jnp.broadcast_to(w["dw_kernel"][l][:, None, :], (CONV_WIDTH, 8, D_CONV)), dwb=jnp.broadcast_to(w["dw_bias"][l][None, :], (8, D_CONV)),
        lng=w["conv_ln_g"][l].reshape(1, D_CONV), lnb=w["conv_ln_b"][l].reshape(1, D_CONV),
        wco=w["w_conv_out"][l].astype(bf16), wao=w["w_attn_out"][l].astype(bf16),
        wout=w["w_out"][l].astype(bf16),
        qg=(jnp.tile(w["q_norm_g"][l], N_Q_HEADS) * (HEAD_DIM ** -0.5 * LOG2E)).reshape(1, D_Q),
        kg=jnp.tile(w["k_norm_g"][l], N_KV_HEADS).reshape(1, D_KV),
        bq=_block_diag_mean(NORM_BLK, HEAD_DIM), bk=_block_diag_mean(D_KV, HEAD_DIM),
        bias=bias, sinks=w["sinks"][l], wrh=wrh, wr2=jnp.concatenate([wrh, wrl], axis=1), br=br,
    )
    w_gate, w_up, w_down = w["w_exp_gate"][l], w["w_exp_up"][l], w["w_exp_down"][l]

    n_chunks = MOE_CHUNKS if B % MOE_CHUNKS == 0 else 1
    Bc = B // n_chunks
    Nc = Bc * T
    n_blocks = -(-(Nc * TOP_K) // MOE_BLOCK) + N_EXPERTS
    blk0 = jnp.arange(n_blocks, dtype=i32) * MOE_BLOCK
    stage = []
    for ch in range(n_chunks):
        x1, h2, route, wtok, cnt = _mixer(x, mod6, p, ch * Bc, Bc)
        counts = cnt[:, 0].astype(i32)
        pcounts = (counts + MOE_BLOCK - 1) // MOE_BLOCK * MOE_BLOCK
        pend = jnp.cumsum(pcounts)
        pstart = pend - pcounts
        block_e = jnp.minimum(jnp.sum((pend[None, :] <= blk0[:, None]).astype(i32), axis=1), N_EXPERTS - 1)
        n_valid = jnp.clip((pstart + counts)[block_e] - blk0, 0, MOE_BLOCK).astype(i32)
        n_used = (pend[-1:] // MOE_BLOCK).astype(i32)
        dest = _dest_slots(pstart, route)
        xs = _sc_scatter(h2.reshape(Nc, D // 2), dest, n_blocks * MOE_BLOCK)
        stage.append((x1, wtok, dest, xs, block_e, n_valid, n_used, counts))
    ys = [_experts(be, nv, nu, cn, xs, w_gate, w_up, w_down) for (_, _, _, xs, be, nv, nu, cn) in stage]
    ygs = [_sc_gather(y, st[2], Nc) for y, st in zip(ys, stage)]
    out = None
    for ch in range(n_chunks):
        x1, wtok = stage[ch][0], stage[ch][1]
        out = _combine(ygs[ch], x1.reshape(Nc, D), wtok, mod6, T, ch * Bc, B * T, out)
    return out.reshape(B, T, D)


def kernel(x, c, w_ada, b_ada, norm_mix_g, w_in, dw_kernel, dw_bias, conv_ln_g, conv_ln_b,
           w_conv_out, q_norm_g, k_norm_g, sinks, w_attn_out, w_out, rel_bias_table, norm_ffn_g,
           w_router_group, b_router_group, w_router_expert, b_router_expert,
           w_exp_gate, w_exp_up, w_exp_down):
    w = dict(norm_mix_g=norm_mix_g, w_in=w_in, dw_kernel=dw_kernel, dw_bias=dw_bias,
             conv_ln_g=conv_ln_g, conv_ln_b=conv_ln_b, w_conv_out=w_conv_out, q_norm_g=q_norm_g,
             k_norm_g=k_norm_g, sinks=sinks, w_attn_out=w_attn_out, w_out=w_out,
             norm_ffn_g=norm_ffn_g, w_router_group=w_router_group, b_router_group=b_router_group,
             w_router_expert=w_router_expert, b_router_expert=b_router_expert,
             w_exp_gate=w_exp_gate, w_exp_up=w_exp_up, w_exp_down=w_exp_down)
    B = x.shape[0]
    bias = _bias_band(rel_bias_table).reshape(N_KV_HEADS, Q_PER_KV * WINDOW, 2 * WINDOW)
    for l in range(w_ada.shape[0]):
        mod6 = _modulation(c, w_ada[l], b_ada[l]).reshape(B, 6, D_MODEL)
        x = _layer(x, mod6, bias, l, w)
    return x
```

```python
import functools
import math

import jax
import jax.numpy as jnp
import numpy as np
from jax import lax
from jax.experimental import pallas as pl
from jax.experimental.pallas import tpu as pltpu
from jax.experimental.pallas import tpu_sc as plsc

D_MODEL = 1024
D_CONV = 512
CONV_WIDTH = 31
N_Q_HEADS = 8
N_KV_HEADS = 2
HEAD_DIM = 64
Q_PER_KV = N_Q_HEADS // N_KV_HEADS
WINDOW = 128
N_BUCKETS = 32
MAX_DISTANCE = 128
N_GROUPS = 4
EXPERTS_PER_GROUP = 8
N_EXPERTS = N_GROUPS * EXPERTS_PER_GROUP
TOP_K = 2
D_EXPERT = 256
D_Q = N_Q_HEADS * HEAD_DIM
D_KV = N_KV_HEADS * HEAD_DIM
EPS = 1e-6
NEG_INF = -1e30
LOG2E = math.log2(math.e)

LANES = 128
SEQ_TILE = 512
CONV_ROWS = 64
NORM_BLK = 256
PROJ_CHUNK = 256
CONV_SPAN = 24
MOE_BLOCK = 1024
MOE_CHUNKS = 2
DEST_TILE = 8192
COMBINE_TILE = 1024
SC_CORES = 2
SC_SUBCORES = 16
SC_WORKERS = SC_CORES * SC_SUBCORES
SC_CHUNK = 32
ROUTER_COLS = LANES
EXPERT_COL0 = 8
VMEM_LIMIT = 56 * 1024 * 1024

f32 = jnp.float32
bf16 = jnp.bfloat16
i32 = jnp.int32


def _dot(a, b):
    return jnp.dot(a, b, preferred_element_type=f32)


def _split(a):
    hi = a.astype(bf16)
    lo = (a - hi.astype(f32)).astype(bf16)
    return hi, lo


def _pack_halves(x):
    c = x.shape[1] // 2
    hi = lax.bitcast_convert_type(x[:, 0:c].astype(bf16).astype(f32), jnp.uint32)
    lo = lax.bitcast_convert_type(x[:, c:2 * c].astype(bf16).astype(f32), jnp.uint32)
    word = (hi & jnp.uint32(0xFFFF0000)) | (lo >> jnp.uint32(16))
    return lax.bitcast_convert_type(word, i32)


def _unpack_halves(word):
    u = lax.bitcast_convert_type(word, jnp.uint32)
    hi = lax.bitcast_convert_type(u & jnp.uint32(0xFFFF0000), f32)
    lo = lax.bitcast_convert_type(u << jnp.uint32(16), f32)
    return hi, lo


def _dot3(a, b):
    ah, al = _split(a)
    bh, bl = _split(b)
    return _dot(ah, bh) + _dot(al, bh) + _dot(ah, bl)


def _mod_kernel(c_ref, w_ref, b_ref, o_ref):
    c = c_ref[...]
    s = c * jax.nn.sigmoid(c)
    o_ref[...] = _dot3(s, w_ref[...]) + b_ref[...]


def _modulation(c, w_ada, b_ada):
    B, D = c.shape
    n_out = w_ada.shape[1]
    return pl.pallas_call(
        _mod_kernel,
        grid=(n_out // D,),
        in_specs=[pl.BlockSpec((B, D), lambda j: (0, 0)),
                  pl.BlockSpec((D, D), lambda j: (0, j)),
                  pl.BlockSpec((1, D), lambda j: (0, j))],
        out_specs=pl.BlockSpec((B, D), lambda j: (0, j)),
        out_shape=jax.ShapeDtypeStruct((B, n_out), f32),
        name="modulation",
    )(c, w_ada, b_ada.reshape(1, n_out))


def _band_buckets():
    qi = np.arange(WINDOW)[:, None]
    kj = np.arange(2 * WINDOW)[None, :]
    dist = np.clip(qi + WINDOW - kj, 0, MAX_DISTANCE)
    max_exact = N_BUCKETS // 2
    d = np.maximum(dist, 1).astype(np.float32)
    large = max_exact + (np.log(d / np.float32(max_exact)) / np.float32(math.log(MAX_DISTANCE / max_exact))
                         * np.float32(N_BUCKETS - max_exact)).astype(np.int32)
    large = np.minimum(large, N_BUCKETS - 1)
    bucket = np.where(dist < max_exact, dist, large).astype(np.int32)
    raw = qi + WINDOW - kj
    return np.where((raw >= 0) & (raw < WINDOW), bucket, -1).astype(np.int32)


def _bias_kernel(tab_ref, bucket_ref, o_ref):
    bk = bucket_ref[...]
    for h in range(N_Q_HEADS):
        acc = jnp.full(bk.shape, NEG_INF, f32)
        for b in range(N_BUCKETS):
            acc = jnp.where(bk == b, tab_ref[b, h] * LOG2E, acc)
        o_ref[h] = acc


def _bias_band(rel_bias_table):
    return pl.pallas_call(
        _bias_kernel,
        in_specs=[pl.BlockSpec(memory_space=pltpu.SMEM),
                  pl.BlockSpec(memory_space=pltpu.VMEM)],
        out_specs=pl.BlockSpec(memory_space=pltpu.VMEM),
        out_shape=jax.ShapeDtypeStruct((N_Q_HEADS, WINDOW, 2 * WINDOW), f32),
        name="bias_band",
    )(rel_bias_table, jnp.asarray(_band_buckets()))


def _mixer_kernel(x_ref, mod_ref, gmix_ref, gffn_ref, win_ref, dw_ref, dwb_ref, lng_ref, lnb_ref,
                  wco_ref, wao_ref, wout_ref, qg_ref, kg_ref, bq_ref, bk_ref, bias_ref, sink_ref,
                  wrh_ref, wr2_ref, br_ref, tri_ref,
                  x1_ref, h2_ref, route_ref, wtok_ref, cnt_ref,
                  uext, ush, conv_sc, proj_sc, k2, v2, osc):
    TM = x_ref.shape[0]
    HALO = 32
    b = pl.program_id(0)
    t = pl.program_id(1)
    first = t == 0

    @pl.when(first)
    def _():
        uext[0:HALO, :] = jnp.zeros((HALO, D_CONV), f32)
        k2[:, 0:WINDOW, :] = jnp.zeros((N_KV_HEADS, WINDOW, LANES), bf16)
        v2[:, 0:WINDOW, :] = jnp.zeros((N_KV_HEADS, WINDOW, LANES), bf16)

    @pl.when(first & (b == 0))
    def _():
        cnt_ref[...] = jnp.zeros(cnt_ref.shape, f32)

    x = x_ref[...]
    mod = mod_ref[...]
    sh1, sc1, g1 = mod[0:1, :], mod[1:2, :], mod[2:3, :]
    sh2, sc2, g2 = mod[3:4, :], mod[4:5, :], mod[5:6, :]
    del g2

    ms = jnp.mean(x * x, axis=-1, keepdims=True)
    h = (x * lax.rsqrt(ms + EPS)) * (gmix_ref[...] * (1.0 + sc1)) + sh1
    hb = h.astype(bf16)

    ab = _dot(hb, win_ref[:, 0:2 * D_CONV])
    u = ab[:, 0:D_CONV] * jax.nn.sigmoid(ab[:, D_CONV:2 * D_CONV])
    uext[HALO:HALO + TM, :] = u
    for r in range(1, 8):
        ush[r - 1] = uext[r:r + TM + CONV_SPAN, :]
    n_q = D_Q // PROJ_CHUNK
    n_g = D_MODEL // PROJ_CHUNK
    n_jobs = (win_ref.shape[1] - 2 * D_CONV) // PROJ_CHUNK
    n_rb = TM // CONV_ROWS
    n_t8 = CONV_ROWS // 8
    units = [(cb, rg) for cb in range(D_CONV // LANES) for rg in range(n_rb)]
    job_at = {(i * len(units)) // n_jobs: i for i in range(n_jobs)}
    assert len(job_at) == n_jobs
    for ui, (cb, rg) in enumerate(units):
        if ui in job_at:
            c_lo = 2 * D_CONV + job_at[ui] * PROJ_CHUNK
            proj_sc[job_at[ui]] = _dot(hb, win_ref[:, c_lo:c_lo + PROJ_CHUNK])
        cs = slice(cb * LANES, (cb + 1) * LANES)
        base = rg * CONV_ROWS
        accs = [None] * n_t8
        for r in range(8):
            taps = [(j, (j + 2) // 8) for j in range(CONV_WIDTH) if (j + 2) % 8 == r]
            t_lo = min(a for _, a in taps)
            t_hi = max(a for _, a in taps) + n_t8
            tiles = {}
            for t8 in range(t_lo, t_hi):
                rows = slice(base + 8 * t8, base + 8 * t8 + 8)
                tiles[t8] = uext[rows, cs] if r == 0 else ush[r - 1, rows, cs]
            for j, a in taps:
                tap = dw_ref[j, :, cs]
                for s8 in range(n_t8):
                    term = tap * tiles[a + s8]
                    accs[s8] = term if accs[s8] is None else accs[s8] + term
        for s8 in range(n_t8):
            conv_sc[base + 8 * s8:base + 8 * s8 + 8, cs] = accs[s8] + dwb_ref[:, cs]
    conv = conv_sc[...]
    uext[0:HALO, :] = uext[TM:TM + HALO, :]
    mu = jnp.mean(conv, axis=-1, keepdims=True)
    dc = conv - mu
    var = jnp.mean(dc * dc, axis=-1, keepdims=True)
    yn = dc * lax.rsqrt(var + EPS) * lng_ref[...] + lnb_ref[...]
    act = yn * jax.nn.sigmoid(yn)
    y_conv = _dot(act.astype(bf16), wco_ref[...])

    q = jnp.concatenate([proj_sc[i] for i in range(n_q)], axis=1)
    k = proj_sc[n_q, :, 0:D_KV]
    v = proj_sc[n_q, :, D_KV:2 * D_KV]

    def head_norm(z, blk_ref, g):
        sq = (z * z).astype(bf16)
        blk = blk_ref.shape[0]
        msq = jnp.concatenate([_dot(sq[:, c:c + blk], blk_ref[...]) for c in range(0, z.shape[1], blk)], axis=1)
        return z * lax.rsqrt(msq + EPS) * g

    qn = head_norm(q, bq_ref, qg_ref[...]).astype(bf16)
    kn = head_norm(k, bk_ref, kg_ref[...])
    lo_half = lax.broadcasted_iota(i32, (TM, LANES), 1) < HEAD_DIM
    kr = pltpu.roll(kn, HEAD_DIM, 1)
    vr = pltpu.roll(v, HEAD_DIM, 1)
    k2[0, WINDOW:WINDOW + TM, :] = jnp.where(lo_half, kn, kr).astype(bf16)
    k2[1, WINDOW:WINDOW + TM, :] = jnp.where(lo_half, kr, kn).astype(bf16)
    v2[0, WINDOW:WINDOW + TM, :] = jnp.where(lo_half, v, vr).astype(bf16)
    v2[1, WINDOW:WINDOW + TM, :] = jnp.where(lo_half, vr, v).astype(bf16)

    QROWS = Q_PER_KV * WINDOW
    col = lax.broadcasted_iota(i32, (QROWS, 2 * WINDOW), 1)
    has_prev = col >= jnp.where(first, WINDOW, 0)
    hrow = lax.broadcasted_iota(i32, (QROWS, 1), 0) // WINDOW
    lo128 = lax.broadcasted_iota(i32, (WINDOW, LANES), 1) < HEAD_DIM
    zero_q = jnp.zeros((WINDOW, LANES), bf16)
    for g in range(N_KV_HEADS):
        sink = LOG2E * jnp.where(hrow == 0, sink_ref[4 * g],
                                 jnp.where(hrow == 1, sink_ref[4 * g + 1],
                                           jnp.where(hrow == 2, sink_ref[4 * g + 2], sink_ref[4 * g + 3])))
        bias_g = bias_ref[g]
        for j in range(TM // WINDOW):
            rs = slice(j * WINDOW, (j + 1) * WINDOW)
            qa = qn[rs, 2 * LANES * g:2 * LANES * g + LANES]
            qb = qn[rs, 2 * LANES * g + LANES:2 * LANES * (g + 1)]
            qs = jnp.concatenate([jnp.where(lo128, qa, zero_q), jnp.where(lo128, zero_q, qa),
                                  jnp.where(lo128, qb, zero_q), jnp.where(lo128, zero_q, qb)], axis=0)
            kk = k2[g, j * WINDOW:(j + 2) * WINDOW, :]
            s = lax.dot_general(qs, kk, (((1,), (1,)), ((), ())), preferred_element_type=f32)
            logits = jnp.where(has_prev, s + bias_g, NEG_INF) if j == 0 else s + bias_g
            m = jnp.maximum(jnp.max(logits, axis=-1, keepdims=True), sink)
            p = jnp.exp2(logits - m)
            den = jnp.sum(p, axis=-1, keepdims=True) + jnp.exp2(sink - m)
            o2 = _dot(p.astype(bf16), v2[g, j * WINDOW:(j + 2) * WINDOW, :]) * (1.0 / den)
            osc[rs, 2 * LANES * g:2 * LANES * g + LANES] = jnp.where(
                lo128, o2[0:WINDOW], o2[WINDOW:2 * WINDOW]).astype(bf16)
            osc[rs, 2 * LANES * g + LANES:2 * LANES * (g + 1)] = jnp.where(
                lo128, o2[2 * WINDOW:3 * WINDOW], o2[3 * WINDOW:4 * WINDOW]).astype(bf16)
    k2[:, 0:WINDOW, :] = k2[:, TM:TM + WINDOW, :]
    v2[:, 0:WINDOW, :] = v2[:, TM:TM + WINDOW, :]
    y_attn = _dot(osc[...], wao_ref[...])

    merged = []
    for i in range(n_g):
        cs = slice(i * PROJ_CHUNK, (i + 1) * PROJ_CHUNK)
        g_conv = jax.nn.sigmoid(proj_sc[n_q + 1 + i])
        g_attn = jax.nn.sigmoid(proj_sc[n_q + 1 + n_g + i])
        merged.append((g_conv * y_conv[:, cs] + g_attn * y_attn[:, cs]).astype(bf16))
    merged = jnp.concatenate(merged, axis=1)
    x1 = x + g1 * _dot(merged, wout_ref[...])
    x1_ref[...] = x1

    ms2 = jnp.mean(x1 * x1, axis=-1, keepdims=True)
    h2 = (x1 * lax.rsqrt(ms2 + EPS)) * (gffn_ref[...] * (1.0 + sc2)) + sh2
    h2_ref[...] = _pack_halves(h2)
    hh, hl = _split(h2)
    hw = _dot(hh, wr2_ref[...])
    lg = hw[:, 0:ROUTER_COLS] + hw[:, ROUTER_COLS:2 * ROUTER_COLS] + _dot(hl, wrh_ref[...]) + br_ref[...]
    lt = lg.T
    gl = lt[0:N_GROUPS, :]
    grow = lax.broadcasted_iota(i32, (N_GROUPS, TM), 0)
    gmax = jnp.max(gl, axis=0, keepdims=True)
    gi = jnp.min(jnp.where(gl == gmax, grow, N_GROUPS), axis=0, keepdims=True)
    p_top = 1.0 / jnp.sum(jnp.exp(gl - gmax), axis=0, keepdims=True)
    sel = lt[EXPERT_COL0:EXPERT_COL0 + EXPERTS_PER_GROUP, :]
    for gg in range(1, N_GROUPS):
        lo_r = EXPERT_COL0 + gg * EXPERTS_PER_GROUP
        sel = jnp.where(gi == gg, lt[lo_r:lo_r + EXPERTS_PER_GROUP, :], sel)
    erow = lax.broadcasted_iota(i32, (EXPERTS_PER_GROUP, TM), 0)
    m1 = jnp.max(sel, axis=0, keepdims=True)
    i1 = jnp.min(jnp.where(sel == m1, erow, EXPERTS_PER_GROUP), axis=0, keepdims=True)
    rest = jnp.where(erow == i1, -jnp.inf, sel)
    m2 = jnp.max(rest, axis=0, keepdims=True)
    i2 = jnp.min(jnp.where(rest == m2, erow, EXPERTS_PER_GROUP), axis=0, keepdims=True)
    z = jnp.sum(jnp.exp(sel - m1), axis=0, keepdims=True)
    v1 = 1.0 / z
    v2nd = jnp.exp(m2 - m1) / z
    w1 = v1 / (v1 + v2nd) * p_top
    w2 = v2nd / (v1 + v2nd) * p_top
    e1 = gi * EXPERTS_PER_GROUP + i1
    e2 = gi * EXPERTS_PER_GROUP + i2

    xrow = lax.broadcasted_iota(i32, (N_EXPERTS, TM), 0)
    oh1 = xrow == e1
    oh2 = xrow == e2
    both = jnp.where(oh1 | oh2, 1.0, 0.0)
    prefix = _dot(both.astype(bf16), tri_ref[...]) + cnt_ref[:, 0:1]
    r1 = jnp.sum(jnp.where(oh1, prefix, 0.0), axis=0, keepdims=True)
    r2 = jnp.sum(jnp.where(oh2, prefix, 0.0), axis=0, keepdims=True)
    cnt_ref[...] = cnt_ref[...] + jnp.sum(both, axis=1, keepdims=True)

    route_ref[...] = jnp.concatenate(
        [e1, e2, r1.astype(i32), r2.astype(i32), jnp.zeros((4, TM), i32)], axis=0)
    wpad = jnp.concatenate([w1, w2, jnp.zeros((LANES - 2, TM), f32)], axis=0)
    wtok_ref[...] = wpad.T


def _mixer(x, mod6, p, b0, B):
    _, T, D = x.shape
    TM = min(SEQ_TILE, T)
    nt = T // TM
    N = B * T
    const = lambda shape: pl.BlockSpec(shape, lambda b, t: (0,) * len(shape))
    in_specs = [
        pl.BlockSpec((None, TM, D), lambda b, t: (b0 + b, t, 0)),
        pl.BlockSpec((None, 6, D), lambda b, t: (b0 + b, 0, 0)),
        const((1, D)), const((1, D)),
        const(p["w_in"].shape),
        const((CONV_WIDTH, 8, D_CONV)), const((8, D_CONV)), const((1, D_CONV)), const((1, D_CONV)),
        const((D_CONV, D)), const((D_Q, D)), const((D, D)),
        const((1, D_Q)), const((1, D_KV)),
        const((NORM_BLK, NORM_BLK)), const((D_KV, D_KV)),
        const((N_KV_HEADS, Q_PER_KV * WINDOW, 2 * WINDOW)),
        pl.BlockSpec(memory_space=pltpu.SMEM),
        const((D, ROUTER_COLS)), const((D, 2 * ROUTER_COLS)), const((1, ROUTER_COLS)),
        const((TM, TM)),
    ]
    out_specs = [
        pl.BlockSpec((None, TM, D), lambda b, t: (b, t, 0)),
        pl.BlockSpec((None, TM, D // 2), lambda b, t: (b, t, 0)),
        pl.BlockSpec((8, TM), lambda b, t: (0, b * nt + t)),
        pl.BlockSpec((TM, LANES), lambda b, t: (b * nt + t, 0)),
        pl.BlockSpec((N_EXPERTS, LANES), lambda b, t: (0, 0)),
    ]
    out_shape = [
        jax.ShapeDtypeStruct((B, T, D), f32),
        jax.ShapeDtypeStruct((B, T, D // 2), i32),
        jax.ShapeDtypeStruct((8, N), i32),
        jax.ShapeDtypeStruct((N, LANES), f32),
        jax.ShapeDtypeStruct((N_EXPERTS, LANES), f32),
    ]
    scratch = [
        pltpu.VMEM((TM + 32, D_CONV), f32),
        pltpu.VMEM((7, TM + CONV_SPAN, D_CONV), f32),
        pltpu.VMEM((TM, D_CONV), f32),
        pltpu.VMEM(((p["w_in"].shape[1] - 2 * D_CONV) // PROJ_CHUNK, TM, PROJ_CHUNK), f32),
        pltpu.VMEM((N_KV_HEADS, TM + WINDOW, LANES), bf16),
        pltpu.VMEM((N_KV_HEADS, TM + WINDOW, LANES), bf16),
        pltpu.VMEM((TM, D_Q), bf16),
    ]
    tri = jnp.asarray(np.triu(np.ones((TM, TM), np.float32), 1), bf16)
    return pl.pallas_call(
        _mixer_kernel,
        grid=(B, nt),
        in_specs=in_specs, out_specs=out_specs, out_shape=out_shape, scratch_shapes=scratch,
        compiler_params=pltpu.CompilerParams(
            dimension_semantics=("arbitrary", "arbitrary"), vmem_limit_bytes=VMEM_LIMIT),
        name="mixer_router",
    )(x, mod6, p["gmix"], p["gffn"], p["w_in"], p["dw"], p["dwb"], p["lng"], p["lnb"],
      p["wco"], p["wao"], p["wout"], p["qg"], p["kg"], p["bq"], p["bk"], p["bias"], p["sinks"],
      p["wrh"], p["wr2"], p["br"], tri)


def _dest_kernel(pstart_ref, route_ref, dest_ref):
    e = route_ref[0:2, :]
    base = jnp.zeros(e.shape, i32)
    for x in range(N_EXPERTS):
        base = jnp.where(e == x, pstart_ref[x], base)
    dest_ref[...] = base + route_ref[2:4, :]


def _dest_slots(pstart, route):
    N = route.shape[1]
    TS = min(DEST_TILE, N)
    return pl.pallas_call(
        _dest_kernel,
        grid=(N // TS,),
        in_specs=[pl.BlockSpec(memory_space=pltpu.SMEM),
                  pl.BlockSpec((8, TS), lambda i: (0, i))],
        out_specs=pl.BlockSpec((2, TS), lambda i: (0, i)),
        out_shape=jax.ShapeDtypeStruct((2, N), i32),
        name="dest_slots",
    )(pstart, route)


def _sc_mesh():
    return plsc.VectorSubcoreMesh(core_axis_name="c", subcore_axis_name="s")


def _sc_scatter(h2, dest, n_slots):
    N, D = h2.shape
    per_w = N // SC_WORKERS
    C = SC_CHUNK
    n_chunks = per_w // C
    dest4 = dest.reshape(TOP_K, SC_WORKERS, n_chunks, C)

    @functools.partial(
        pl.kernel, mesh=_sc_mesh(),
        out_type=jax.ShapeDtypeStruct((n_slots, D), h2.dtype),
        scratch_types=[pltpu.VMEM((n_chunks, C), i32), pltpu.VMEM((n_chunks, C), i32),
                       pltpu.VMEM((2, C, D), h2.dtype),
                       pltpu.SemaphoreType.DMA((2,)), pltpu.SemaphoreType.DMA((2,))],
        name="sc_dispatch_scatter",
    )
    def run(h2_hbm, dest_hbm, xs_hbm, idx0, idx1, buf, sem_in, sem_out):
        wid = lax.axis_index("s") * SC_CORES + lax.axis_index("c")
        base = wid * per_w
        pltpu.sync_copy(dest_hbm.at[0, wid], idx0)
        pltpu.sync_copy(dest_hbm.at[1, wid], idx1)

        def load(j, b):
            return pltpu.make_async_copy(h2_hbm.at[pl.ds(base + j * C, C)], buf.at[b], sem_in.at[b])

        def put(j, b, idx):
            return pltpu.make_async_copy(buf.at[b], xs_hbm.at[idx.at[j]], sem_out.at[b])

        load(0, 0).start()

        @pl.loop(0, n_chunks, step=2)
        def _(j0):
            for b in range(2):
                j = j0 + b
                load(j, b).wait()

                @pl.when(j >= 1)
                def _():
                    put(j - 1, 1 - b, idx0).wait()
                    put(j - 1, 1 - b, idx1).wait()

                @pl.when(j + 1 < n_chunks)
                def _():
                    load(j + 1, 1 - b).start()

                put(j, b, idx0).start()
                put(j, b, idx1).start()

        put(n_chunks - 1, 1, idx0).wait()
        put(n_chunks - 1, 1, idx1).wait()

    return run(h2, dest4)


def _sc_gather(y, dest, N):
    D = y.shape[1]
    per_w = N // SC_WORKERS
    C = SC_CHUNK
    n_chunks = per_w // C
    dest4 = dest.reshape(TOP_K, SC_WORKERS, n_chunks, C)

    @functools.partial(
        pl.kernel, mesh=_sc_mesh(),
        out_type=jax.ShapeDtypeStruct((TOP_K, N, D), y.dtype),
        scratch_types=[pltpu.VMEM((n_chunks, C), i32), pltpu.VMEM((n_chunks, C), i32),
                       pltpu.VMEM((2, C, D), y.dtype),
                       pltpu.SemaphoreType.DMA((2,)), pltpu.SemaphoreType.DMA((2,))],
        name="sc_combine_gather",
    )
    def run(y_hbm, dest_hbm, yg_hbm, idx0, idx1, buf, sem_in, sem_out):
        wid = lax.axis_index("s") * SC_CORES + lax.axis_index("c")
        base = wid * per_w
        pltpu.sync_copy(dest_hbm.at[0, wid], idx0)
        pltpu.sync_copy(dest_hbm.at[1, wid], idx1)
        idx = (idx0, idx1)

        def get(j, k):
            return pltpu.make_async_copy(y_hbm.at[idx[k].at[j]], buf.at[k], sem_in.at[k])

        def put(j, k):
            return pltpu.make_async_copy(buf.at[k], yg_hbm.at[k, pl.ds(base + j * C, C)], sem_out.at[k])

        get(0, 0).start()
        get(0, 1).start()

        @pl.loop(0, n_chunks)
        def _(j):
            for k in range(TOP_K):
                get(j, k).wait()
                put(j, k).start()
            for k in range(TOP_K):
                put(j, k).wait()

                @pl.when(j + 1 < n_chunks)
                def _():
                    get(j + 1, k).start()

    return run(y, dest4)


def _expert_kernel(be_ref, nv_ref, nu_ref, first_ref, slot_ref, nxt_ref,
                   xs_ref, wg_hbm, wu_hbm, wdn_hbm, y_ref,
                   stg_g, stg_u, stg_d, wgu_ref, wd_ref, sems):
    i = pl.program_id(0)
    in_use = i < nu_ref[0]

    def fetch(e, s):
        return (pltpu.make_async_copy(wg_hbm.at[e], stg_g.at[s], sems.at[s, 0]),
                pltpu.make_async_copy(wu_hbm.at[e], stg_u.at[s], sems.at[s, 1]),
                pltpu.make_async_copy(wdn_hbm.at[e], stg_d.at[s], sems.at[s, 2]))

    @pl.when(in_use & (first_ref[i] == 1))
    def _():
        s = slot_ref[i]

        @pl.when(i == 0)
        def _():
            for cp in fetch(be_ref[0], 0):
                cp.start()

        for cp in fetch(be_ref[i], s):
            cp.wait()
        wgu_ref[:, 0:D_EXPERT] = stg_g[s].astype(bf16)
        wgu_ref[:, D_EXPERT:2 * D_EXPERT] = stg_u[s].astype(bf16)
        wd_ref[...] = stg_d[s].astype(bf16)

        @pl.when(nxt_ref[i] >= 0)
        def _():
            for cp in fetch(nxt_ref[i], 1 - s):
                cp.start()

    @pl.when(in_use)
    def _():
        live = lax.broadcasted_iota(i32, (MOE_BLOCK, 1), 0) < nv_ref[i]
        x_hi, x_lo = _unpack_halves(jnp.where(live, xs_ref[...], 0))
        half = x_hi.shape[1]
        gu = (_dot(x_hi.astype(bf16), wgu_ref[0:half, :]) +
              _dot(x_lo.astype(bf16), wgu_ref[half:2 * half, :]))
        gate = gu[:, 0:D_EXPERT]
        hid = (gate * jax.nn.sigmoid(gate)) * gu[:, D_EXPERT:2 * D_EXPERT]
        y_ref[...] = _pack_halves(_dot(hid.astype(bf16), wd_ref[...]))


def _experts(block_e, n_valid, n_used, counts, xs, w_gate, w_up, w_down):
    n_slots, DH = xs.shape
    D = 2 * DH
    n_blocks = n_slots // MOE_BLOCK
    idx = jnp.arange(n_blocks, dtype=i32)
    prev_e = jnp.concatenate([jnp.full((1,), -1, i32), block_e[:-1]])
    first = ((block_e != prev_e) & (idx < n_used[0])).astype(i32)
    slot = ((jnp.cumsum(first) - 1) % 2).astype(i32)
    e_ids = jnp.arange(N_EXPERTS, dtype=i32)
    later = (e_ids[None, :] > e_ids[:, None]) & (counts[None, :] > 0)
    nxt_of_e = jnp.min(jnp.where(later, e_ids[None, :], N_EXPERTS), axis=1)
    nxt = jnp.where(nxt_of_e[block_e] < N_EXPERTS, nxt_of_e[block_e], -1).astype(i32)
    last = lambda i, nu: jnp.minimum(i, nu[0] - 1)
    slot_map = lambda i, be, nv, nu, fi, sl, nx: (last(i, nu), 0)
    grid_spec = pltpu.PrefetchScalarGridSpec(
        num_scalar_prefetch=6,
        grid=(n_blocks,),
        in_specs=[pl.BlockSpec((MOE_BLOCK, DH), slot_map),
                  pl.BlockSpec(memory_space=pl.ANY),
                  pl.BlockSpec(memory_space=pl.ANY),
                  pl.BlockSpec(memory_space=pl.ANY)],
        out_specs=pl.BlockSpec((MOE_BLOCK, DH), slot_map),
        scratch_shapes=[pltpu.VMEM((2, D, D_EXPERT), f32), pltpu.VMEM((2, D, D_EXPERT), f32),
                        pltpu.VMEM((2, D_EXPERT, D), f32),
                        pltpu.VMEM((D, 2 * D_EXPERT), bf16), pltpu.VMEM((D_EXPERT, D), bf16),
                        pltpu.SemaphoreType.DMA((2, 3))],
    )
    return pl.pallas_call(
        _expert_kernel,
        grid_spec=grid_spec,
        out_shape=jax.ShapeDtypeStruct((n_slots, DH), i32),
        compiler_params=pltpu.CompilerParams(dimension_semantics=("arbitrary",),
                                             vmem_limit_bytes=VMEM_LIMIT),
        name="experts",
    )(block_e, n_valid, n_used, first, slot, nxt, xs, w_gate, w_up, w_down)


def _combine_kernel(yg_ref, x1_ref, wtok_ref, mod_ref, *rest):
    o_ref = rest[-1]
    w = wtok_ref[...]
    g2 = mod_ref[5:6, :]
    a_hi, a_lo = _unpack_halves(yg_ref[0])
    b_hi, b_lo = _unpack_halves(yg_ref[1])
    moe = jnp.concatenate([w[:, 0:1] * a_hi + w[:, 1:2] * b_hi,
                           w[:, 0:1] * a_lo + w[:, 1:2] * b_lo], axis=1)
    o_ref[...] = x1_ref[...] + g2 * moe


def _combine(yg, x1, wtok, mod6, T, b0, n_total, out_prev):
    Nc, D = x1.shape
    TS = min(COMBINE_TILE, T)
    per_seq = T // TS
    blk0 = b0 * per_seq
    in_specs = [pl.BlockSpec((TOP_K, TS, D // 2), lambda i: (0, i, 0)),
                pl.BlockSpec((TS, D), lambda i: (i, 0)),
                pl.BlockSpec((TS, LANES), lambda i: (i, 0)),
                pl.BlockSpec((None, 6, D), lambda i: (b0 + i // per_seq, 0, 0))]
    args = [yg, x1, wtok, mod6]
    aliases = {}
    if out_prev is not None:
        in_specs.append(pl.BlockSpec(memory_space=pl.ANY))
        args.append(out_prev)
        aliases = {len(args) - 1: 0}
    return pl.pallas_call(
        _combine_kernel,
        grid=(Nc // TS,),
        in_specs=in_specs,
        out_specs=pl.BlockSpec((TS, D), lambda i: (blk0 + i, 0)),
        out_shape=jax.ShapeDtypeStruct((n_total, D), f32),
        input_output_aliases=aliases,
        compiler_params=pltpu.CompilerParams(dimension_semantics=("arbitrary",),
                                             vmem_limit_bytes=VMEM_LIMIT),
        name="combine",
    )(*args)


def _block_diag_mean(n, blk):
    m = np.kron(np.eye(n // blk, dtype=np.float32), np.full((blk, blk), 1.0 / blk, np.float32))
    return jnp.asarray(m, bf16)


def _layer(x, mod6, bias, l, w):
    B, T, D = x.shape
    N = B * T
    w_rg, w_re = w["w_router_group"][l], w["w_router_expert"][l]
    wr = jnp.zeros((D, ROUTER_COLS), f32)
    wr = wr.at[:, 0:N_GROUPS].set(w_rg).at[:, EXPERT_COL0:EXPERT_COL0 + N_EXPERTS].set(w_re)
    br = jnp.zeros((1, ROUTER_COLS), f32)
    br = br.at[0, 0:N_GROUPS].set(w["b_router_group"][l])
    br = br.at[0, EXPERT_COL0:EXPERT_COL0 + N_EXPERTS].set(w["b_router_expert"][l])
    wrh = wr.astype(bf16)
    wrl = (wr - wrh.astype(f32)).astype(bf16)
    p = dict(
        gmix=w["norm_mix_g"][l].reshape(1, D), gffn=w["norm_ffn_g"][l].reshape(1, D),
        w_in=w["w_in"][l].astype(bf16),
        dw=jnp.broadcast_to(w["dw_kernel"][l][:, None, :], (CONV_WIDTH, 8, D_CONV)), dwb=jnp.broadcast_to(w["dw_bias"][l][None, :], (8, D_CONV)),
        lng=w["conv_ln_g"][l].reshape(1, D_CONV), lnb=w["conv_ln_b"][l].reshape(1, D_CONV),
        wco=w["w_conv_out"][l].astype(bf16), wao=w["w_attn_out"][l].astype(bf16),
        wout=w["w_out"][l].astype(bf16),
        qg=(jnp.tile(w["q_norm_g"][l], N_Q_HEADS) * (HEAD_DIM ** -0.5 * LOG2E)).reshape(1, D_Q),
        kg=jnp.tile(w["k_norm_g"][l], N_KV_HEADS).reshape(1, D_KV),
        bq=_block_diag_mean(NORM_BLK, HEAD_DIM), bk=_block_diag_mean(D_KV, HEAD_DIM),
        bias=bias, sinks=w["sinks"][l], wrh=wrh, wr2=jnp.concatenate([wrh, wrl], axis=1), br=br,
    )
    w_gate, w_up, w_down = w["w_exp_gate"][l], w["w_exp_up"][l], w["w_exp_down"][l]

    n_chunks = MOE_CHUNKS if B % MOE_CHUNKS == 0 else 1
    Bc = B // n_chunks
    Nc = Bc * T
    n_blocks = -(-(Nc * TOP_K) // MOE_BLOCK) + N_EXPERTS
    blk0 = jnp.arange(n_blocks, dtype=i32) * MOE_BLOCK
    stage = []
    for ch in range(n_chunks):
        x1, h2, route, wtok, cnt = _mixer(x, mod6, p, ch * Bc, Bc)
        counts = cnt[:, 0].astype(i32)
        pcounts = (counts + MOE_BLOCK - 1) // MOE_BLOCK * MOE_BLOCK
        pend = jnp.cumsum(pcounts)
        pstart = pend - pcounts
        block_e = jnp.minimum(jnp.sum((pend[None, :] <= blk0[:, None]).astype(i32), axis=1), N_EXPERTS - 1)
        n_valid = jnp.clip((pstart + counts)[block_e] - blk0, 0, MOE_BLOCK).astype(i32)
        n_used = (pend[-1:] // MOE_BLOCK).astype(i32)
        dest = _dest_slots(pstart, route)
        xs = _sc_scatter(h2.reshape(Nc, D // 2), dest, n_blocks * MOE_BLOCK)
        stage.append((x1, wtok, dest, xs, block_e, n_valid, n_used, counts))
    ys = [_experts(be, nv, nu, cn, xs, w_gate, w_up, w_down) for (_, _, _, xs, be, nv, nu, cn) in stage]
    ygs = [_sc_gather(y, st[2], Nc) for y, st in zip(ys, stage)]
    out = None
    for ch in range(n_chunks):
        x1, wtok = stage[ch][0], stage[ch][1]
        out = _combine(ygs[ch], x1.reshape(Nc, D), wtok, mod6, T, ch * Bc, B * T, out)
    return out.reshape(B, T, D)


def kernel(x, c, w_ada, b_ada, norm_mix_g, w_in, dw_kernel, dw_bias, conv_ln_g, conv_ln_b,
           w_conv_out, q_norm_g, k_norm_g, sinks, w_attn_out, w_out, rel_bias_table, norm_ffn_g,
           w_router_group, b_router_group, w_router_expert, b_router_expert,
           w_exp_gate, w_exp_up, w_exp_down):
    w = dict(norm_mix_g=norm_mix_g, w_in=w_in, dw_kernel=dw_kernel, dw_bias=dw_bias,
             conv_ln_g=conv_ln_g, conv_ln_b=conv_ln_b, w_conv_out=w_conv_out, q_norm_g=q_norm_g,
             k_norm_g=k_norm_g, sinks=sinks, w_attn_out=w_attn_out, w_out=w_out,
             norm_ffn_g=norm_ffn_g, w_router_group=w_router_group, b_router_group=b_router_group,
             w_router_expert=w_router_expert, b_router_expert=b_router_expert,
             w_exp_gate=w_exp_gate, w_exp_up=w_exp_up, w_exp_down=w_exp_down)
    B = x.shape[0]
    bias = _bias_band(rel_bias_table).reshape(N_KV_HEADS, Q_PER_KV * WINDOW, 2 * WINDOW)
    for l in range(w_ada.shape[0]):
        mod6 = _modulation(c, w_ada[l], b_ada[l]).reshape(B, 6, D_MODEL)
        x = _layer(x, mod6, bias, l, w)
    return x
```

```python
import functools
import math

import jax
import jax.numpy as jnp
import numpy as np
from jax import lax
from jax.experimental import pallas as pl
from jax.experimental.pallas import tpu as pltpu
from jax.experimental.pallas import tpu_sc as plsc

D_MODEL = 1024
D_CONV = 512
CONV_WIDTH = 31
N_Q_HEADS = 8
N_KV_HEADS = 2
HEAD_DIM = 64
Q_PER_KV = N_Q_HEADS // N_KV_HEADS
WINDOW = 128
N_BUCKETS = 32
MAX_DISTANCE = 128
N_GROUPS = 4
EXPERTS_PER_GROUP = 8
N_EXPERTS = N_GROUPS * EXPERTS_PER_GROUP
TOP_K = 2
D_EXPERT = 256
D_Q = N_Q_HEADS * HEAD_DIM
D_KV = N_KV_HEADS * HEAD_DIM
EPS = 1e-6
NEG_INF = -1e30
LOG2E = math.log2(math.e)

LANES = 128
SEQ_TILE = 512
CONV_ROWS = 64
NORM_BLK = 256
PROJ_CHUNK = 256
CONV_SPAN = 24
MOE_BLOCK = 1024
MOE_CHUNKS = 2
DEST_TILE = 8192
COMBINE_TILE = 1024
SC_CORES = 2
SC_SUBCORES = 16
SC_WORKERS = SC_CORES * SC_SUBCORES
SC_CHUNK = 64
ROUTER_COLS = LANES
EXPERT_COL0 = 8
VMEM_LIMIT = 56 * 1024 * 1024

f32 = jnp.float32
bf16 = jnp.bfloat16
i32 = jnp.int32


def _dot(a, b):
    return jnp.dot(a, b, preferred_element_type=f32)


def _split(a):
    hi = a.astype(bf16)
    lo = (a - hi.astype(f32)).astype(bf16)
    return hi, lo


def _pack_halves(x):
    c = x.shape[1] // 2
    hi = lax.bitcast_convert_type(x[:, 0:c].astype(bf16).astype(f32), jnp.uint32)
    lo = lax.bitcast_convert_type(x[:, c:2 * c].astype(bf16).astype(f32), jnp.uint32)
    word = (hi & jnp.uint32(0xFFFF0000)) | (lo >> jnp.uint32(16))
    return lax.bitcast_convert_type(word, i32)


def _unpack_halves(word):
    u = lax.bitcast_convert_type(word, jnp.uint32)
    hi = lax.bitcast_convert_type(u & jnp.uint32(0xFFFF0000), f32)
    lo = lax.bitcast_convert_type(u << jnp.uint32(16), f32)
    return hi, lo


def _dot3(a, b):
    ah, al = _split(a)
    bh, bl = _split(b)
    return _dot(ah, bh) + _dot(al, bh) + _dot(ah, bl)


def _mod_kernel(c_ref, w_ref, b_ref, o_ref):
    c = c_ref[...]
    s = c * jax.nn.sigmoid(c)
    o_ref[...] = _dot3(s, w_ref[...]) + b_ref[...]


def _modulation(c, w_ada, b_ada):
    B, D = c.shape
    n_out = w_ada.shape[1]
    return pl.pallas_call(
        _mod_kernel,
        grid=(n_out // D,),
        in_specs=[pl.BlockSpec((B, D), lambda j: (0, 0)),
                  pl.BlockSpec((D, D), lambda j: (0, j)),
                  pl.BlockSpec((1, D), lambda j: (0, j))],
        out_specs=pl.BlockSpec((B, D), lambda j: (0, j)),
        out_shape=jax.ShapeDtypeStruct((B, n_out), f32),
        name="modulation",
    )(c, w_ada, b_ada.reshape(1, n_out))


def _band_buckets():
    qi = np.arange(WINDOW)[:, None]
    kj = np.arange(2 * WINDOW)[None, :]
    dist = np.clip(qi + WINDOW - kj, 0, MAX_DISTANCE)
    max_exact = N_BUCKETS // 2
    d = np.maximum(dist, 1).astype(np.float32)
    large = max_exact + (np.log(d / np.float32(max_exact)) / np.float32(math.log(MAX_DISTANCE / max_exact))
                         * np.float32(N_BUCKETS - max_exact)).astype(np.int32)
    large = np.minimum(large, N_BUCKETS - 1)
    bucket = np.where(dist < max_exact, dist, large).astype(np.int32)
    raw = qi + WINDOW - kj
    return np.where((raw >= 0) & (raw < WINDOW), bucket, -1).astype(np.int32)


def _bias_kernel(tab_ref, bucket_ref, o_ref):
    bk = bucket_ref[...]
    for h in range(N_Q_HEADS):
        acc = jnp.full(bk.shape, NEG_INF, f32)
        for b in range(N_BUCKETS):
            acc = jnp.where(bk == b, tab_ref[b, h] * LOG2E, acc)
        o_ref[h] = acc


def _bias_band(rel_bias_table):
    return pl.pallas_call(
        _bias_kernel,
        in_specs=[pl.BlockSpec(memory_space=pltpu.SMEM),
                  pl.BlockSpec(memory_space=pltpu.VMEM)],
        out_specs=pl.BlockSpec(memory_space=pltpu.VMEM),
        out_shape=jax.ShapeDtypeStruct((N_Q_HEADS, WINDOW, 2 * WINDOW), f32),
        name="bias_band",
    )(rel_bias_table, jnp.asarray(_band_buckets()))


def _mixer_kernel(x_ref, mod_ref, gmix_ref, gffn_ref, win_ref, dw_ref, dwb_ref, lng_ref, lnb_ref,
                  wco_ref, wao_ref, wout_ref, qg_ref, kg_ref, bq_ref, bk_ref, bias_ref, sink_ref,
                  wrh_ref, wr2_ref, br_ref, tri_ref,
                  x1_ref, h2_ref, route_ref, wtok_ref, cnt_ref,
                  uext, ush, conv_sc, proj_sc, k2, v2, osc):
    TM = x_ref.shape[0]
    HALO = 32
    b = pl.program_id(0)
    t = pl.program_id(1)
    first = t == 0

    @pl.when(first)
    def _():
        uext[0:HALO, :] = jnp.zeros((HALO, D_CONV), f32)
        k2[:, 0:WINDOW, :] = jnp.zeros((N_KV_HEADS, WINDOW, LANES), bf16)
        v2[:, 0:WINDOW, :] = jnp.zeros((N_KV_HEADS, WINDOW, LANES), bf16)

    @pl.when(first & (b == 0))
    def _():
        cnt_ref[...] = jnp.zeros(cnt_ref.shape, f32)

    x = x_ref[...]
    mod = mod_ref[...]
    sh1, sc1, g1 = mod[0:1, :], mod[1:2, :], mod[2:3, :]
    sh2, sc2, g2 = mod[3:4, :], mod[4:5, :], mod[5:6, :]
    del g2

    ms = jnp.mean(x * x, axis=-1, keepdims=True)
    h = (x * lax.rsqrt(ms + EPS)) * (gmix_ref[...] * (1.0 + sc1)) + sh1
    hb = h.astype(bf16)

    ab = _dot(hb, win_ref[:, 0:2 * D_CONV])
    u = ab[:, 0:D_CONV] * jax.nn.sigmoid(ab[:, D_CONV:2 * D_CONV])
    uext[HALO:HALO + TM, :] = u
    for r in range(1, 8):
        ush[r - 1] = uext[r:r + TM + CONV_SPAN, :]
    n_q = D_Q // PROJ_CHUNK
    n_g = D_MODEL // PROJ_CHUNK
    n_jobs = (win_ref.shape[1] - 2 * D_CONV) // PROJ_CHUNK
    n_rb = TM // CONV_ROWS
    n_t8 = CONV_ROWS // 8
    units = [(cb, rg) for cb in range(D_CONV // LANES) for rg in range(n_rb)]
    job_at = {(i * len(units)) // n_jobs: i for i in range(n_jobs)}
    assert len(job_at) == n_jobs
    for ui, (cb, rg) in enumerate(units):
        if ui in job_at:
            c_lo = 2 * D_CONV + job_at[ui] * PROJ_CHUNK
            proj_sc[job_at[ui]] = _dot(hb, win_ref[:, c_lo:c_lo + PROJ_CHUNK])
        cs = slice(cb * LANES, (cb + 1) * LANES)
        base = rg * CONV_ROWS
        accs = [None] * n_t8
        for r in range(8):
            taps = [(j, (j + 2) // 8) for j in range(CONV_WIDTH) if (j + 2) % 8 == r]
            t_lo = min(a for _, a in taps)
            t_hi = max(a for _, a in taps) + n_t8
            tiles = {}
            for t8 in range(t_lo, t_hi):
                rows = slice(base + 8 * t8, base + 8 * t8 + 8)
                tiles[t8] = uext[rows, cs] if r == 0 else ush[r - 1, rows, cs]
            for j, a in taps:
                tap = dw_ref[j, :, cs]
                for s8 in range(n_t8):
                    term = tap * tiles[a + s8]
                    accs[s8] = term if accs[s8] is None else accs[s8] + term
        for s8 in range(n_t8):
            conv_sc[base + 8 * s8:base + 8 * s8 + 8, cs] = accs[s8] + dwb_ref[:, cs]
    conv = conv_sc[...]
    uext[0:HALO, :] = uext[TM:TM + HALO, :]
    mu = jnp.mean(conv, axis=-1, keepdims=True)
    dc = conv - mu
    var = jnp.mean(dc * dc, axis=-1, keepdims=True)
    yn = dc * lax.rsqrt(var + EPS) * lng_ref[...] + lnb_ref[...]
    act = yn * jax.nn.sigmoid(yn)
    y_conv = _dot(act.astype(bf16), wco_ref[...])

    q = jnp.concatenate([proj_sc[i] for i in range(n_q)], axis=1)
    k = proj_sc[n_q, :, 0:D_KV]
    v = proj_sc[n_q, :, D_KV:2 * D_KV]

    def head_norm(z, blk_ref, g):
        sq = (z * z).astype(bf16)
        blk = blk_ref.shape[0]
        msq = jnp.concatenate([_dot(sq[:, c:c + blk], blk_ref[...]) for c in range(0, z.shape[1], blk)], axis=1)
        return z * lax.rsqrt(msq + EPS) * g

    qn = head_norm(q, bq_ref, qg_ref[...]).astype(bf16)
    kn = head_norm(k, bk_ref, kg_ref[...])
    lo_half = lax.broadcasted_iota(i32, (TM, LANES), 1) < HEAD_DIM
    kr = pltpu.roll(kn, HEAD_DIM, 1)
    vr = pltpu.roll(v, HEAD_DIM, 1)
    k2[0, WINDOW:WINDOW + TM, :] = jnp.where(lo_half, kn, kr).astype(bf16)
    k2[1, WINDOW:WINDOW + TM, :] = jnp.where(lo_half, kr, kn).astype(bf16)
    v2[0, WINDOW:WINDOW + TM, :] = jnp.where(lo_half, v, vr).astype(bf16)
    v2[1, WINDOW:WINDOW + TM, :] = jnp.where(lo_half, vr, v).astype(bf16)

    QROWS = Q_PER_KV * WINDOW
    col = lax.broadcasted_iota(i32, (QROWS, 2 * WINDOW), 1)
    has_prev = col >= jnp.where(first, WINDOW, 0)
    hrow = lax.broadcasted_iota(i32, (QROWS, 1), 0) // WINDOW
    lo128 = lax.broadcasted_iota(i32, (WINDOW, LANES), 1) < HEAD_DIM
    zero_q = jnp.zeros((WINDOW, LANES), bf16)
    for g in range(N_KV_HEADS):
        sink = LOG2E * jnp.where(hrow == 0, sink_ref[4 * g],
                                 jnp.where(hrow == 1, sink_ref[4 * g + 1],
                                           jnp.where(hrow == 2, sink_ref[4 * g + 2], sink_ref[4 * g + 3])))
        bias_g = bias_ref[g]
        for j in range(TM // WINDOW):
            rs = slice(j * WINDOW, (j + 1) * WINDOW)
            qa = qn[rs, 2 * LANES * g:2 * LANES * g + LANES]
            qb = qn[rs, 2 * LANES * g + LANES:2 * LANES * (g + 1)]
            qs = jnp.concatenate([jnp.where(lo128, qa, zero_q), jnp.where(lo128, zero_q, qa),
                                  jnp.where(lo128, qb, zero_q), jnp.where(lo128, zero_q, qb)], axis=0)
            kk = k2[g, j * WINDOW:(j + 2) * WINDOW, :]
            s = lax.dot_general(qs, kk, (((1,), (1,)), ((), ())), preferred_element_type=f32)
            logits = jnp.where(has_prev, s + bias_g, NEG_INF) if j == 0 else s + bias_g
            m = jnp.maximum(jnp.max(logits, axis=-1, keepdims=True), sink)
            p = jnp.exp2(logits - m)
            den = jnp.sum(p, axis=-1, keepdims=True) + jnp.exp2(sink - m)
            o2 = _dot(p.astype(bf16), v2[g, j * WINDOW:(j + 2) * WINDOW, :]) * (1.0 / den)
            osc[rs, 2 * LANES * g:2 * LANES * g + LANES] = jnp.where(
                lo128, o2[0:WINDOW], o2[WINDOW:2 * WINDOW]).astype(bf16)
            osc[rs, 2 * LANES * g + LANES:2 * LANES * (g + 1)] = jnp.where(
                lo128, o2[2 * WINDOW:3 * WINDOW], o2[3 * WINDOW:4 * WINDOW]).astype(bf16)
    k2[:, 0:WINDOW, :] = k2[:, TM:TM + WINDOW, :]
    v2[:, 0:WINDOW, :] = v2[:, TM:TM + WINDOW, :]
    y_attn = _dot(osc[...], wao_ref[...])

    merged = []
    for i in range(n_g):
        cs = slice(i * PROJ_CHUNK, (i + 1) * PROJ_CHUNK)
        g_conv = jax.nn.sigmoid(proj_sc[n_q + 1 + i])
        g_attn = jax.nn.sigmoid(proj_sc[n_q + 1 + n_g + i])
        merged.append((g_conv * y_conv[:, cs] + g_attn * y_attn[:, cs]).astype(bf16))
    merged = jnp.concatenate(merged, axis=1)
    x1 = x + g1 * _dot(merged, wout_ref[...])
    x1_ref[...] = x1

    ms2 = jnp.mean(x1 * x1, axis=-1, keepdims=True)
    h2 = (x1 * lax.rsqrt(ms2 + EPS)) * (gffn_ref[...] * (1.0 + sc2)) + sh2
    h2_ref[...] = _pack_halves(h2)
    hh, hl = _split(h2)
    hw = _dot(hh, wr2_ref[...])
    lg = hw[:, 0:ROUTER_COLS] + hw[:, ROUTER_COLS:2 * ROUTER_COLS] + _dot(hl, wrh_ref[...]) + br_ref[...]
    lt = lg.T
    gl = lt[0:N_GROUPS, :]
    grow = lax.broadcasted_iota(i32, (N_GROUPS, TM), 0)
    gmax = jnp.max(gl, axis=0, keepdims=True)
    gi = jnp.min(jnp.where(gl == gmax, grow, N_GROUPS), axis=0, keepdims=True)
    p_top = 1.0 / jnp.sum(jnp.exp(gl - gmax), axis=0, keepdims=True)
    sel = lt[EXPERT_COL0:EXPERT_COL0 + EXPERTS_PER_GROUP, :]
    for gg in range(1, N_GROUPS):
        lo_r = EXPERT_COL0 + gg * EXPERTS_PER_GROUP
        sel = jnp.where(gi == gg, lt[lo_r:lo_r + EXPERTS_PER_GROUP, :], sel)
    erow = lax.broadcasted_iota(i32, (EXPERTS_PER_GROUP, TM), 0)
    m1 = jnp.max(sel, axis=0, keepdims=True)
    i1 = jnp.min(jnp.where(sel == m1, erow, EXPERTS_PER_GROUP), axis=0, keepdims=True)
    rest = jnp.where(erow == i1, -jnp.inf, sel)
    m2 = jnp.max(rest, axis=0, keepdims=True)
    i2 = jnp.min(jnp.where(rest == m2, erow, EXPERTS_PER_GROUP), axis=0, keepdims=True)
    z = jnp.sum(jnp.exp(sel - m1), axis=0, keepdims=True)
    v1 = 1.0 / z
    v2nd = jnp.exp(m2 - m1) / z
    w1 = v1 / (v1 + v2nd) * p_top
    w2 = v2nd / (v1 + v2nd) * p_top
    e1 = gi * EXPERTS_PER_GROUP + i1
    e2 = gi * EXPERTS_PER_GROUP + i2

    xrow = lax.broadcasted_iota(i32, (N_EXPERTS, TM), 0)
    oh1 = xrow == e1
    oh2 = xrow == e2
    both = jnp.where(oh1 | oh2, 1.0, 0.0)
    prefix = _dot(both.astype(bf16), tri_ref[...]) + cnt_ref[:, 0:1]
    r1 = jnp.sum(jnp.where(oh1, prefix, 0.0), axis=0, keepdims=True)
    r2 = jnp.sum(jnp.where(oh2, prefix, 0.0), axis=0, keepdims=True)
    cnt_ref[...] = cnt_ref[...] + jnp.sum(both, axis=1, keepdims=True)

    route_ref[...] = jnp.concatenate(
        [e1, e2, r1.astype(i32), r2.astype(i32), jnp.zeros((4, TM), i32)], axis=0)
    wpad = jnp.concatenate([w1, w2, jnp.zeros((LANES - 2, TM), f32)], axis=0)
    wtok_ref[...] = wpad.T


def _mixer(x, mod6, p, b0, B):
    _, T, D = x.shape
    TM = min(SEQ_TILE, T)
    nt = T // TM
    N = B * T
    const = lambda shape: pl.BlockSpec(shape, lambda b, t: (0,) * len(shape))
    in_specs = [
        pl.BlockSpec((None, TM, D), lambda b, t: (b0 + b, t, 0)),
        pl.BlockSpec((None, 6, D), lambda b, t: (b0 + b, 0, 0)),
        const((1, D)), const((1, D)),
        const(p["w_in"].shape),
        const((CONV_WIDTH, 8, D_CONV)), const((8, D_CONV)), const((1, D_CONV)), const((1, D_CONV)),
        const((D_CONV, D)), const((D_Q, D)), const((D, D)),
        const((1, D_Q)), const((1, D_KV)),
        const((NORM_BLK, NORM_BLK)), const((D_KV, D_KV)),
        const((N_KV_HEADS, Q_PER_KV * WINDOW, 2 * WINDOW)),
        pl.BlockSpec(memory_space=pltpu.SMEM),
        const((D, ROUTER_COLS)), const((D, 2 * ROUTER_COLS)), const((1, ROUTER_COLS)),
        const((TM, TM)),
    ]
    out_specs = [
        pl.BlockSpec((None, TM, D), lambda b, t: (b, t, 0)),
        pl.BlockSpec((None, TM, D // 2), lambda b, t: (b, t, 0)),
        pl.BlockSpec((8, TM), lambda b, t: (0, b * nt + t)),
        pl.BlockSpec((TM, LANES), lambda b, t: (b * nt + t, 0)),
        pl.BlockSpec((N_EXPERTS, LANES), lambda b, t: (0, 0)),
    ]
    out_shape = [
        jax.ShapeDtypeStruct((B, T, D), f32),
        jax.ShapeDtypeStruct((B, T, D // 2), i32),
        jax.ShapeDtypeStruct((8, N), i32),
        jax.ShapeDtypeStruct((N, LANES), f32),
        jax.ShapeDtypeStruct((N_EXPERTS, LANES), f32),
    ]
    scratch = [
        pltpu.VMEM((TM + 32, D_CONV), f32),
        pltpu.VMEM((7, TM + CONV_SPAN, D_CONV), f32),
        pltpu.VMEM((TM, D_CONV), f32),
        pltpu.VMEM(((p["w_in"].shape[1] - 2 * D_CONV) // PROJ_CHUNK, TM, PROJ_CHUNK), f32),
        pltpu.VMEM((N_KV_HEADS, TM + WINDOW, LANES), bf16),
        pltpu.VMEM((N_KV_HEADS, TM + WINDOW, LANES), bf16),
        pltpu.VMEM((TM, D_Q), bf16),
    ]
    tri = jnp.asarray(np.triu(np.ones((TM, TM), np.float32), 1), bf16)
    return pl.pallas_call(
        _mixer_kernel,
        grid=(B, nt),
        in_specs=in_specs, out_specs=out_specs, out_shape=out_shape, scratch_shapes=scratch,
        compiler_params=pltpu.CompilerParams(
            dimension_semantics=("arbitrary", "arbitrary"), vmem_limit_bytes=VMEM_LIMIT),
        name="mixer_router",
    )(x, mod6, p["gmix"], p["gffn"], p["w_in"], p["dw"], p["dwb"], p["lng"], p["lnb"],
      p["wco"], p["wao"], p["wout"], p["qg"], p["kg"], p["bq"], p["bk"], p["bias"], p["sinks"],
      p["wrh"], p["wr2"], p["br"], tri)


def _dest_kernel(pstart_ref, route_ref, dest_ref):
    e = route_ref[0:2, :]
    base = jnp.zeros(e.shape, i32)
    for x in range(N_EXPERTS):
        base = jnp.where(e == x, pstart_ref[x], base)
    dest_ref[...] = base + route_ref[2:4, :]


def _dest_slots(pstart, route):
    N = route.shape[1]
    TS = min(DEST_TILE, N)
    return pl.pallas_call(
        _dest_kernel,
        grid=(N // TS,),
        in_specs=[pl.BlockSpec(memory_space=pltpu.SMEM),
                  pl.BlockSpec((8, TS), lambda i: (0, i))],
        out_specs=pl.BlockSpec((2, TS), lambda i: (0, i)),
        out_shape=jax.ShapeDtypeStruct((2, N), i32),
        name="dest_slots",
    )(pstart, route)


def _sc_mesh():
    return plsc.VectorSubcoreMesh(core_axis_name="c", subcore_axis_name="s")


def _sc_scatter(h2, dest, n_slots):
    N, D = h2.shape
    per_w = N // SC_WORKERS
    C = SC_CHUNK
    n_chunks = per_w // C
    dest4 = dest.reshape(TOP_K, SC_WORKERS, n_chunks, C)

    @functools.partial(
        pl.kernel, mesh=_sc_mesh(),
        out_type=jax.ShapeDtypeStruct((n_slots, D), h2.dtype),
        scratch_types=[pltpu.VMEM((n_chunks, C), i32), pltpu.VMEM((n_chunks, C), i32),
                       pltpu.VMEM((2, C, D), h2.dtype),
                       pltpu.SemaphoreType.DMA((2,)), pltpu.SemaphoreType.DMA((2,))],
        name="sc_dispatch_scatter",
    )
    def run(h2_hbm, dest_hbm, xs_hbm, idx0, idx1, buf, sem_in, sem_out):
        wid = lax.axis_index("s") * SC_CORES + lax.axis_index("c")
        base = wid * per_w
        pltpu.sync_copy(dest_hbm.at[0, wid], idx0)
        pltpu.sync_copy(dest_hbm.at[1, wid], idx1)

        def load(j, b):
            return pltpu.make_async_copy(h2_hbm.at[pl.ds(base + j * C, C)], buf.at[b], sem_in.at[b])

        def put(j, b, idx):
            return pltpu.make_async_copy(buf.at[b], xs_hbm.at[idx.at[j]], sem_out.at[b])

        load(0, 0).start()

        @pl.loop(0, n_chunks, step=2)
        def _(j0):
            for b in range(2):
                j = j0 + b
                load(j, b).wait()

                @pl.when(j >= 1)
                def _():
                    put(j - 1, 1 - b, idx0).wait()
                    put(j - 1, 1 - b, idx1).wait()

                @pl.when(j + 1 < n_chunks)
                def _():
                    load(j + 1, 1 - b).start()

                put(j, b, idx0).start()
                put(j, b, idx1).start()

        put(n_chunks - 1, 1, idx0).wait()
        put(n_chunks - 1, 1, idx1).wait()

    return run(h2, dest4)


def _sc_gather(y, dest, N):
    D = y.shape[1]
    per_w = N // SC_WORKERS
    C = SC_CHUNK
    n_chunks = per_w // C
    dest4 = dest.reshape(TOP_K, SC_WORKERS, n_chunks, C)

    @functools.partial(
        pl.kernel, mesh=_sc_mesh(),
        out_type=jax.ShapeDtypeStruct((TOP_K, N, D), y.dtype),
        scratch_types=[pltpu.VMEM((n_chunks, C), i32), pltpu.VMEM((n_chunks, C), i32),
                       pltpu.VMEM((2, C, D), y.dtype),
                       pltpu.SemaphoreType.DMA((2,)), pltpu.SemaphoreType.DMA((2,))],
        name="sc_combine_gather",
    )
    def run(y_hbm, dest_hbm, yg_hbm, idx0, idx1, buf, sem_in, sem_out):
        wid = lax.axis_index("s") * SC_CORES + lax.axis_index("c")
        base = wid * per_w
        pltpu.sync_copy(dest_hbm.at[0, wid], idx0)
        pltpu.sync_copy(dest_hbm.at[1, wid], idx1)
        idx = (idx0, idx1)

        def get(j, k):
            return pltpu.make_async_copy(y_hbm.at[idx[k].at[j]], buf.at[k], sem_in.at[k])

        def put(j, k):
            return pltpu.make_async_copy(buf.at[k], yg_hbm.at[k, pl.ds(base + j * C, C)], sem_out.at[k])

        get(0, 0).start()
        get(0, 1).start()

        @pl.loop(0, n_chunks)
        def _(j):
            for k in range(TOP_K):
                get(j, k).wait()
                put(j, k).start()
            for k in range(TOP_K):
                put(j, k).wait()

                @pl.when(j + 1 < n_chunks)
                def _():
                    get(j + 1, k).start()

    return run(y, dest4)


def _expert_kernel(be_ref, nv_ref, nu_ref, first_ref, slot_ref, nxt_ref,
                   xs_ref, wg_hbm, wu_hbm, wdn_hbm, y_ref,
                   stg_g, stg_u, stg_d, wgu_ref, wd_ref, sems):
    i = pl.program_id(0)
    in_use = i < nu_ref[0]

    def fetch(e, s):
        return (pltpu.make_async_copy(wg_hbm.at[e], stg_g.at[s], sems.at[s, 0]),
                pltpu.make_async_copy(wu_hbm.at[e], stg_u.at[s], sems.at[s, 1]),
                pltpu.make_async_copy(wdn_hbm.at[e], stg_d.at[s], sems.at[s, 2]))

    @pl.when(in_use & (first_ref[i] == 1))
    def _():
        s = slot_ref[i]

        @pl.when(i == 0)
        def _():
            for cp in fetch(be_ref[0], 0):
                cp.start()

        for cp in fetch(be_ref[i], s):
            cp.wait()
        wgu_ref[:, 0:D_EXPERT] = stg_g[s].astype(bf16)
        wgu_ref[:, D_EXPERT:2 * D_EXPERT] = stg_u[s].astype(bf16)
        wd_ref[...] = stg_d[s].astype(bf16)

        @pl.when(nxt_ref[i] >= 0)
        def _():
            for cp in fetch(nxt_ref[i], 1 - s):
                cp.start()

    @pl.when(in_use)
    def _():
        live = lax.broadcasted_iota(i32, (MOE_BLOCK, 1), 0) < nv_ref[i]
        x_hi, x_lo = _unpack_halves(jnp.where(live, xs_ref[...], 0))
        half = x_hi.shape[1]
        gu = (_dot(x_hi.astype(bf16), wgu_ref[0:half, :]) +
              _dot(x_lo.astype(bf16), wgu_ref[half:2 * half, :]))
        gate = gu[:, 0:D_EXPERT]
        hid = (gate * jax.nn.sigmoid(gate)) * gu[:, D_EXPERT:2 * D_EXPERT]
        y_ref[...] = _pack_halves(_dot(hid.astype(bf16), wd_ref[...]))


def _experts(block_e, n_valid, n_used, counts, xs, w_gate, w_up, w_down):
    n_slots, DH = xs.shape
    D = 2 * DH
    n_blocks = n_slots // MOE_BLOCK
    idx = jnp.arange(n_blocks, dtype=i32)
    prev_e = jnp.concatenate([jnp.full((1,), -1, i32), block_e[:-1]])
    first = ((block_e != prev_e) & (idx < n_used[0])).astype(i32)
    slot = ((jnp.cumsum(first) - 1) % 2).astype(i32)
    e_ids = jnp.arange(N_EXPERTS, dtype=i32)
    later = (e_ids[None, :] > e_ids[:, None]) & (counts[None, :] > 0)
    nxt_of_e = jnp.min(jnp.where(later, e_ids[None, :], N_EXPERTS), axis=1)
    nxt = jnp.where(nxt_of_e[block_e] < N_EXPERTS, nxt_of_e[block_e], -1).astype(i32)
    last = lambda i, nu: jnp.minimum(i, nu[0] - 1)
    slot_map = lambda i, be, nv, nu, fi, sl, nx: (last(i, nu), 0)
    grid_spec = pltpu.PrefetchScalarGridSpec(
        num_scalar_prefetch=6,
        grid=(n_blocks,),
        in_specs=[pl.BlockSpec((MOE_BLOCK, DH), slot_map),
                  pl.BlockSpec(memory_space=pl.ANY),
                  pl.BlockSpec(memory_space=pl.ANY),
                  pl.BlockSpec(memory_space=pl.ANY)],
        out_specs=pl.BlockSpec((MOE_BLOCK, DH), slot_map),
        scratch_shapes=[pltpu.VMEM((2, D, D_EXPERT), f32), pltpu.VMEM((2, D, D_EXPERT), f32),
                        pltpu.VMEM((2, D_EXPERT, D), f32),
                        pltpu.VMEM((D, 2 * D_EXPERT), bf16), pltpu.VMEM((D_EXPERT, D), bf16),
                        pltpu.SemaphoreType.DMA((2, 3))],
    )
    return pl.pallas_call(
        _expert_kernel,
        grid_spec=grid_spec,
        out_shape=jax.ShapeDtypeStruct((n_slots, DH), i32),
        compiler_params=pltpu.CompilerParams(dimension_semantics=("arbitrary",),
                                             vmem_limit_bytes=VMEM_LIMIT),
        name="experts",
    )(block_e, n_valid, n_used, first, slot, nxt, xs, w_gate, w_up, w_down)


def _combine_kernel(yg_ref, x1_ref, wtok_ref, mod_ref, *rest):
    o_ref = rest[-1]
    w = wtok_ref[...]
    g2 = mod_ref[5:6, :]
    a_hi, a_lo = _unpack_halves(yg_ref[0])
    b_hi, b_lo = _unpack_halves(yg_ref[1])
    moe = jnp.concatenate([w[:, 0:1] * a_hi + w[:, 1:2] * b_hi,
                           w[:, 0:1] * a_lo + w[:, 1:2] * b_lo], axis=1)
    o_ref[...] = x1_ref[...] + g2 * moe


def _combine(yg, x1, wtok, mod6, T, b0, n_total, out_prev):
    Nc, D = x1.shape
    TS = min(COMBINE_TILE, T)
    per_seq = T // TS
    blk0 = b0 * per_seq
    in_specs = [pl.BlockSpec((TOP_K, TS, D // 2), lambda i: (0, i, 0)),
                pl.BlockSpec((TS, D), lambda i: (i, 0)),
                pl.BlockSpec((TS, LANES), lambda i: (i, 0)),
                pl.BlockSpec((None, 6, D), lambda i: (b0 + i // per_seq, 0, 0))]
    args = [yg, x1, wtok, mod6]
    aliases = {}
    if out_prev is not None:
        in_specs.append(pl.BlockSpec(memory_space=pl.ANY))
        args.append(out_prev)
        aliases = {len(args) - 1: 0}
    return pl.pallas_call(
        _combine_kernel,
        grid=(Nc // TS,),
        in_specs=in_specs,
        out_specs=pl.BlockSpec((TS, D), lambda i: (blk0 + i, 0)),
        out_shape=jax.ShapeDtypeStruct((n_total, D), f32),
        input_output_aliases=aliases,
        compiler_params=pltpu.CompilerParams(dimension_semantics=("arbitrary",),
                                             vmem_limit_bytes=VMEM_LIMIT),
        name="combine",
    )(*args)


def _block_diag_mean(n, blk):
    m = np.kron(np.eye(n // blk, dtype=np.float32), np.full((blk, blk), 1.0 / blk, np.float32))
    return jnp.asarray(m, bf16)


def _layer(x, mod6, bias, l, w):
    B, T, D = x.shape
    N = B * T
    w_rg, w_re = w["w_router_group"][l], w["w_router_expert"][l]
    wr = jnp.zeros((D, ROUTER_COLS), f32)
    wr = wr.at[:, 0:N_GROUPS].set(w_rg).at[:, EXPERT_COL0:EXPERT_COL0 + N_EXPERTS].set(w_re)
    br = jnp.zeros((1, ROUTER_COLS), f32)
    br = br.at[0, 0:N_GROUPS].set(w["b_router_group"][l])
    br = br.at[0, EXPERT_COL0:EXPERT_COL0 + N_EXPERTS].set(w["b_router_expert"][l])
    wrh = wr.astype(bf16)
    wrl = (wr - wrh.astype(f32)).astype(bf16)
    p = dict(
        gmix=w["norm_mix_g"][l].reshape(1, D), gffn=w["norm_ffn_g"][l].reshape(1, D),
        w_in=w["w_in"][l].astype(bf16),
        dw=jnp.broadcast_to(w["dw_kernel"][l][:, None, :], (CONV_WIDTH, 8, D_CONV)), dwb=jnp.broadcast_to(w["dw_bias"][l][None, :], (8, D_CONV)),
        lng=w["conv_ln_g"][l].reshape(1, D_CONV), lnb=w["conv_ln_b"][l].reshape(1, D_CONV),
        wco=w["w_conv_out"][l].astype(bf16), wao=w["w_attn_out"][l].astype(bf16),
        wout=w["w_out"][l].astype(bf16),
        qg=(jnp.tile(w["q_norm_g"][l], N_Q_HEADS) * (HEAD_DIM ** -0.5 * LOG2E)).reshape(1, D_Q),
        kg=jnp.tile(w["k_norm_g"][l], N_KV_HEADS).reshape(1, D_KV),
        bq=_block_diag_mean(NORM_BLK, HEAD_DIM), bk=_block_diag_mean(D_KV, HEAD_DIM),
        bias=bias, sinks=w["sinks"][l], wrh=wrh, wr2=jnp.concatenate([wrh, wrl], axis=1), br=br,
    )
    w_gate, w_up, w_down = w["w_exp_gate"][l], w["w_exp_up"][l], w["w_exp_down"][l]

    n_chunks = MOE_CHUNKS if B % MOE_CHUNKS == 0 else 1
    Bc = B // n_chunks
    Nc = Bc * T
    n_blocks = -(-(Nc * TOP_K) // MOE_BLOCK) + N_EXPERTS
    blk0 = jnp.arange(n_blocks, dtype=i32) * MOE_BLOCK
    stage = []
    for ch in range(n_chunks):
        x1, h2, route, wtok, cnt = _mixer(x, mod6, p, ch * Bc, Bc)
        counts = cnt[:, 0].astype(i32)
        pcounts = (counts + MOE_BLOCK - 1) // MOE_BLOCK * MOE_BLOCK
        pend = jnp.cumsum(pcounts)
        pstart = pend - pcounts
        block_e = jnp.minimum(jnp.sum((pend[None, :] <= blk0[:, None]).astype(i32), axis=1), N_EXPERTS - 1)
        n_valid = jnp.clip((pstart + counts)[block_e] - blk0, 0, MOE_BLOCK).astype(i32)
        n_used = (pend[-1:] // MOE_BLOCK).astype(i32)
        dest = _dest_slots(pstart, route)
        xs = _sc_scatter(h2.reshape(Nc, D // 2), dest, n_blocks * MOE_BLOCK)
        stage.append((x1, wtok, dest, xs, block_e, n_valid, n_used, counts))
    ys = [_experts(be, nv, nu, cn, xs, w_gate, w_up, w_down) for (_, _, _, xs, be, nv, nu, cn) in stage]
    ygs = [_sc_gather(y, st[2], Nc) for y, st in zip(ys, stage)]
    out = None
    for ch in range(n_chunks):
        x1, wtok = stage[ch][0], stage[ch][1]
        out = _combine(ygs[ch], x1.reshape(Nc, D), wtok, mod6, T, ch * Bc, B * T, out)
    return out.reshape(B, T, D)


def kernel(x, c, w_ada, b_ada, norm_mix_g, w_in, dw_kernel, dw_bias, conv_ln_g, conv_ln_b,
           w_conv_out, q_norm_g, k_norm_g, sinks, w_attn_out, w_out, rel_bias_table, norm_ffn_g,
           w_router_group, b_router_group, w_router_expert, b_router_expert,
           w_exp_gate, w_exp_up, w_exp_down):
    w = dict(norm_mix_g=norm_mix_g, w_in=w_in, dw_kernel=dw_kernel, dw_bias=dw_bias,
             conv_ln_g=conv_ln_g, conv_ln_b=conv_ln_b, w_conv_out=w_conv_out, q_norm_g=q_norm_g,
             k_norm_g=k_norm_g, sinks=sinks, w_attn_out=w_attn_out, w_out=w_out,
             norm_ffn_g=norm_ffn_g, w_router_group=w_router_group, b_router_group=b_router_group,
             w_router_expert=w_router_expert, b_router_expert=b_router_expert,
             w_exp_gate=w_exp_gate, w_exp_up=w_exp_up, w_exp_down=w_exp_down)
    B = x.shape[0]
    bias = _bias_band(rel_bias_table).reshape(N_KV_HEADS, Q_PER_KV * WINDOW, 2 * WINDOW)
    for l in range(w_ada.shape[0]):
        mod6 = _modulation(c, w_ada[l], b_ada[l]).reshape(B, 6, D_MODEL)
        x = _layer(x, mod6, bias, l, w)
    return x
```

```python
import functools
import math

import jax
import jax.numpy as jnp
import numpy as np
from jax import lax
from jax.experimental import pallas as pl
from jax.experimental.pallas import tpu as pltpu
from jax.experimental.pallas import tpu_sc as plsc

D_MODEL = 1024
D_CONV = 512
CONV_WIDTH = 31
N_Q_HEADS = 8
N_KV_HEADS = 2
HEAD_DIM = 64
Q_PER_KV = N_Q_HEADS // N_KV_HEADS
WINDOW = 128
N_BUCKETS = 32
MAX_DISTANCE = 128
N_GROUPS = 4
EXPERTS_PER_GROUP = 8
N_EXPERTS = N_GROUPS * EXPERTS_PER_GROUP
TOP_K = 2
D_EXPERT = 256
D_Q = N_Q_HEADS * HEAD_DIM
D_KV = N_KV_HEADS * HEAD_DIM
EPS = 1e-6
NEG_INF = -1e30
LOG2E = math.log2(math.e)

LANES = 128
SEQ_TILE = 512
CONV_ROWS = 64
NORM_BLK = 256
PROJ_CHUNK = 256
CONV_SPAN = 24
MOE_BLOCK = 1024
MOE_CHUNKS = 2
DEST_TILE = 8192
COMBINE_TILE = 1024
SC_CORES = 2
SC_SUBCORES = 16
SC_WORKERS = SC_CORES * SC_SUBCORES
SC_CHUNK = 64
ROUTER_COLS = LANES
EXPERT_COL0 = 8
VMEM_LIMIT = 56 * 1024 * 1024

f32 = jnp.float32
bf16 = jnp.bfloat16
i32 = jnp.int32


def _dot(a, b):
    return jnp.dot(a, b, preferred_element_type=f32)


def _split(a):
    hi = a.astype(bf16)
    lo = (a - hi.astype(f32)).astype(bf16)
    return hi, lo


def _pack_halves(x):
    c = x.shape[1] // 2
    hi = lax.bitcast_convert_type(x[:, 0:c].astype(bf16).astype(f32), jnp.uint32)
    lo = lax.bitcast_convert_type(x[:, c:2 * c].astype(bf16).astype(f32), jnp.uint32)
    word = (hi & jnp.uint32(0xFFFF0000)) | (lo >> jnp.uint32(16))
    return lax.bitcast_convert_type(word, i32)


def _unpack_halves(word):
    u = lax.bitcast_convert_type(word, jnp.uint32)
    hi = lax.bitcast_convert_type(u & jnp.uint32(0xFFFF0000), f32)
    lo = lax.bitcast_convert_type(u << jnp.uint32(16), f32)
    return hi, lo


def _dot3(a, b):
    ah, al = _split(a)
    bh, bl = _split(b)
    return _dot(ah, bh) + _dot(al, bh) + _dot(ah, bl)


def _mod_kernel(c_ref, w_ref, b_ref, o_ref):
    c = c_ref[...]
    s = c * jax.nn.sigmoid(c)
    o_ref[...] = _dot3(s, w_ref[...]) + b_ref[...]


def _modulation(c, w_ada, b_ada):
    B, D = c.shape
    n_out = w_ada.shape[1]
    return pl.pallas_call(
        _mod_kernel,
        grid=(n_out // D,),
        in_specs=[pl.BlockSpec((B, D), lambda j: (0, 0)),
                  pl.BlockSpec((D, D), lambda j: (0, j)),
                  pl.BlockSpec((1, D), lambda j: (0, j))],
        out_specs=pl.BlockSpec((B, D), lambda j: (0, j)),
        out_shape=jax.ShapeDtypeStruct((B, n_out), f32),
        name="modulation",
    )(c, w_ada, b_ada.reshape(1, n_out))


def _band_buckets():
    qi = np.arange(WINDOW)[:, None]
    kj = np.arange(2 * WINDOW)[None, :]
    dist = np.clip(qi + WINDOW - kj, 0, MAX_DISTANCE)
    max_exact = N_BUCKETS // 2
    d = np.maximum(dist, 1).astype(np.float32)
    large = max_exact + (np.log(d / np.float32(max_exact)) / np.float32(math.log(MAX_DISTANCE / max_exact))
                         * np.float32(N_BUCKETS - max_exact)).astype(np.int32)
    large = np.minimum(large, N_BUCKETS - 1)
    bucket = np.where(dist < max_exact, dist, large).astype(np.int32)
    raw = qi + WINDOW - kj
    return np.where((raw >= 0) & (raw < WINDOW), bucket, -1).astype(np.int32)


def _bias_kernel(tab_ref, bucket_ref, o_ref):
    bk = bucket_ref[...]
    for h in range(N_Q_HEADS):
        acc = jnp.full(bk.shape, NEG_INF, f32)
        for b in range(N_BUCKETS):
            acc = jnp.where(bk == b, tab_ref[b, h] * LOG2E, acc)
        o_ref[h] = acc


def _bias_band(rel_bias_table):
    return pl.pallas_call(
        _bias_kernel,
        in_specs=[pl.BlockSpec(memory_space=pltpu.SMEM),
                  pl.BlockSpec(memory_space=pltpu.VMEM)],
        out_specs=pl.BlockSpec(memory_space=pltpu.VMEM),
        out_shape=jax.ShapeDtypeStruct((N_Q_HEADS, WINDOW, 2 * WINDOW), f32),
        name="bias_band",
    )(rel_bias_table, jnp.asarray(_band_buckets()))


def _mixer_kernel(x_ref, mod_ref, gmix_ref, gffn_ref, win_ref, dw_ref, dwb_ref, lng_ref, lnb_ref,
                  wco_ref, wao_ref, wout_ref, qg_ref, kg_ref, bq_ref, bk_ref, bias_ref, sink_ref,
                  wrh_ref, wr2_ref, br_ref, tri_ref,
                  x1_ref, h2_ref, route_ref, wtok_ref, cnt_ref,
                  uext, ush, conv_sc, proj_sc, k2, v2, osc):
    TM = x_ref.shape[0]
    HALO = 32
    b = pl.program_id(0)
    t = pl.program_id(1)
    first = t == 0

    @pl.when(first)
    def _():
        uext[0:HALO, :] = jnp.zeros((HALO, D_CONV), f32)
        k2[:, 0:WINDOW, :] = jnp.zeros((N_KV_HEADS, WINDOW, LANES), bf16)
        v2[:, 0:WINDOW, :] = jnp.zeros((N_KV_HEADS, WINDOW, LANES), bf16)

    @pl.when(first & (b == 0))
    def _():
        cnt_ref[...] = jnp.zeros(cnt_ref.shape, f32)

    x = x_ref[...]
    mod = mod_ref[...]
    sh1, sc1, g1 = mod[0:1, :], mod[1:2, :], mod[2:3, :]
    sh2, sc2, g2 = mod[3:4, :], mod[4:5, :], mod[5:6, :]
    del g2

    ms = jnp.mean(x * x, axis=-1, keepdims=True)
    h = (x * lax.rsqrt(ms + EPS)) * (gmix_ref[...] * (1.0 + sc1)) + sh1
    hb = h.astype(bf16)

    ab = _dot(hb, win_ref[:, 0:2 * D_CONV])
    u = ab[:, 0:D_CONV] * jax.nn.sigmoid(ab[:, D_CONV:2 * D_CONV])
    uext[HALO:HALO + TM, :] = u
    u_all = uext[...]
    for r in range(1, 8):
        ush[r - 1] = pltpu.roll(u_all, HALO + TM - r, 0)[0:TM + CONV_SPAN, :]
    n_q = D_Q // PROJ_CHUNK
    n_g = D_MODEL // PROJ_CHUNK
    n_jobs = (win_ref.shape[1] - 2 * D_CONV) // PROJ_CHUNK
    n_rb = TM // CONV_ROWS
    n_t8 = CONV_ROWS // 8
    units = [(cb, rg) for cb in range(D_CONV // LANES) for rg in range(n_rb)]
    job_at = {(i * len(units)) // n_jobs: i for i in range(n_jobs)}
    assert len(job_at) == n_jobs
    for ui, (cb, rg) in enumerate(units):
        if ui in job_at:
            c_lo = 2 * D_CONV + job_at[ui] * PROJ_CHUNK
            proj_sc[job_at[ui]] = _dot(hb, win_ref[:, c_lo:c_lo + PROJ_CHUNK])
        cs = slice(cb * LANES, (cb + 1) * LANES)
        base = rg * CONV_ROWS
        accs = [None] * n_t8
        for r in range(8):
            taps = [(j, (j + 2) // 8) for j in range(CONV_WIDTH) if (j + 2) % 8 == r]
            t_lo = min(a for _, a in taps)
            t_hi = max(a for _, a in taps) + n_t8
            tiles = {}
            for t8 in range(t_lo, t_hi):
                rows = slice(base + 8 * t8, base + 8 * t8 + 8)
                tiles[t8] = uext[rows, cs] if r == 0 else ush[r - 1, rows, cs]
            for j, a in taps:
                tap = dw_ref[j, :, cs]
                for s8 in range(n_t8):
                    term = tap * tiles[a + s8]
                    accs[s8] = term if accs[s8] is None else accs[s8] + term
        for s8 in range(n_t8):
            conv_sc[base + 8 * s8:base + 8 * s8 + 8, cs] = accs[s8] + dwb_ref[:, cs]
    conv = conv_sc[...]
    uext[0:HALO, :] = uext[TM:TM + HALO, :]
    mu = jnp.mean(conv, axis=-1, keepdims=True)
    dc = conv - mu
    var = jnp.mean(dc * dc, axis=-1, keepdims=True)
    yn = dc * lax.rsqrt(var + EPS) * lng_ref[...] + lnb_ref[...]
    act = yn * jax.nn.sigmoid(yn)
    y_conv = _dot(act.astype(bf16), wco_ref[...])

    q = jnp.concatenate([proj_sc[i] for i in range(n_q)], axis=1)
    k = proj_sc[n_q, :, 0:D_KV]
    v = proj_sc[n_q, :, D_KV:2 * D_KV]

    def head_norm(z, blk_ref, g):
        sq = (z * z).astype(bf16)
        blk = blk_ref.shape[0]
        msq = jnp.concatenate([_dot(sq[:, c:c + blk], blk_ref[...]) for c in range(0, z.shape[1], blk)], axis=1)
        return z * lax.rsqrt(msq + EPS) * g

    qn = head_norm(q, bq_ref, qg_ref[...]).astype(bf16)
    kn = head_norm(k, bk_ref, kg_ref[...])
    lo_half = lax.broadcasted_iota(i32, (TM, LANES), 1) < HEAD_DIM
    kr = pltpu.roll(kn, HEAD_DIM, 1)
    vr = pltpu.roll(v, HEAD_DIM, 1)
    k2[0, WINDOW:WINDOW + TM, :] = jnp.where(lo_half, kn, kr).astype(bf16)
    k2[1, WINDOW:WINDOW + TM, :] = jnp.where(lo_half, kr, kn).astype(bf16)
    v2[0, WINDOW:WINDOW + TM, :] = jnp.where(lo_half, v, vr).astype(bf16)
    v2[1, WINDOW:WINDOW + TM, :] = jnp.where(lo_half, vr, v).astype(bf16)

    QROWS = Q_PER_KV * WINDOW
    col = lax.broadcasted_iota(i32, (QROWS, 2 * WINDOW), 1)
    has_prev = col >= jnp.where(first, WINDOW, 0)
    hrow = lax.broadcasted_iota(i32, (QROWS, 1), 0) // WINDOW
    lo128 = lax.broadcasted_iota(i32, (WINDOW, LANES), 1) < HEAD_DIM
    zero_q = jnp.zeros((WINDOW, LANES), bf16)
    for g in range(N_KV_HEADS):
        sink = LOG2E * jnp.where(hrow == 0, sink_ref[4 * g],
                                 jnp.where(hrow == 1, sink_ref[4 * g + 1],
                                           jnp.where(hrow == 2, sink_ref[4 * g + 2], sink_ref[4 * g + 3])))
        bias_g = bias_ref[g]
        for j in range(TM // WINDOW):
            rs = slice(j * WINDOW, (j + 1) * WINDOW)
            qa = qn[rs, 2 * LANES * g:2 * LANES * g + LANES]
            qb = qn[rs, 2 * LANES * g + LANES:2 * LANES * (g + 1)]
            qs = jnp.concatenate([jnp.where(lo128, qa, zero_q), jnp.where(lo128, zero_q, qa),
                                  jnp.where(lo128, qb, zero_q), jnp.where(lo128, zero_q, qb)], axis=0)
            kk = k2[g, j * WINDOW:(j + 2) * WINDOW, :]
            s = lax.dot_general(qs, kk, (((1,), (1,)), ((), ())), preferred_element_type=f32)
            logits = jnp.where(has_prev, s + bias_g, NEG_INF) if j == 0 else s + bias_g
            m = jnp.maximum(jnp.max(logits, axis=-1, keepdims=True), sink)
            p = jnp.exp2(logits - m)
            den = jnp.sum(p, axis=-1, keepdims=True) + jnp.exp2(sink - m)
            o2 = _dot(p.astype(bf16), v2[g, j * WINDOW:(j + 2) * WINDOW, :]) * (1.0 / den)
            osc[rs, 2 * LANES * g:2 * LANES * g + LANES] = jnp.where(
                lo128, o2[0:WINDOW], o2[WINDOW:2 * WINDOW]).astype(bf16)
            osc[rs, 2 * LANES * g + LANES:2 * LANES * (g + 1)] = jnp.where(
                lo128, o2[2 * WINDOW:3 * WINDOW], o2[3 * WINDOW:4 * WINDOW]).astype(bf16)
    k2[:, 0:WINDOW, :] = k2[:, TM:TM + WINDOW, :]
    v2[:, 0:WINDOW, :] = v2[:, TM:TM + WINDOW, :]
    y_attn = _dot(osc[...], wao_ref[...])

    merged = []
    for i in range(n_g):
        cs = slice(i * PROJ_CHUNK, (i + 1) * PROJ_CHUNK)
        g_conv = jax.nn.sigmoid(proj_sc[n_q + 1 + i])
        g_attn = jax.nn.sigmoid(proj_sc[n_q + 1 + n_g + i])
        merged.append((g_conv * y_conv[:, cs] + g_attn * y_attn[:, cs]).astype(bf16))
    merged = jnp.concatenate(merged, axis=1)
    x1 = x + g1 * _dot(merged, wout_ref[...])
    x1_ref[...] = x1

    ms2 = jnp.mean(x1 * x1, axis=-1, keepdims=True)
    h2 = (x1 * lax.rsqrt(ms2 + EPS)) * (gffn_ref[...] * (1.0 + sc2)) + sh2
    h2_ref[...] = _pack_halves(h2)
    hh, hl = _split(h2)
    hw = _dot(hh, wr2_ref[...])
    lg = hw[:, 0:ROUTER_COLS] + hw[:, ROUTER_COLS:2 * ROUTER_COLS] + _dot(hl, wrh_ref[...]) + br_ref[...]
    lt = lg.T
    gl = lt[0:N_GROUPS, :]
    grow = lax.broadcasted_iota(i32, (N_GROUPS, TM), 0)
    gmax = jnp.max(gl, axis=0, keepdims=True)
    gi = jnp.min(jnp.where(gl == gmax, grow, N_GROUPS), axis=0, keepdims=True)
    p_top = 1.0 / jnp.sum(jnp.exp(gl - gmax), axis=0, keepdims=True)
    sel = lt[EXPERT_COL0:EXPERT_COL0 + EXPERTS_PER_GROUP, :]
    for gg in range(1, N_GROUPS):
        lo_r = EXPERT_COL0 + gg * EXPERTS_PER_GROUP
        sel = jnp.where(gi == gg, lt[lo_r:lo_r + EXPERTS_PER_GROUP, :], sel)
    erow = lax.broadcasted_iota(i32, (EXPERTS_PER_GROUP, TM), 0)
    m1 = jnp.max(sel, axis=0, keepdims=True)
    i1 = jnp.min(jnp.where(sel == m1, erow, EXPERTS_PER_GROUP), axis=0, keepdims=True)
    rest = jnp.where(erow == i1, -jnp.inf, sel)
    m2 = jnp.max(rest, axis=0, keepdims=True)
    i2 = jnp.min(jnp.where(rest == m2, erow, EXPERTS_PER_GROUP), axis=0, keepdims=True)
    z = jnp.sum(jnp.exp(sel - m1), axis=0, keepdims=True)
    v1 = 1.0 / z
    v2nd = jnp.exp(m2 - m1) / z
    w1 = v1 / (v1 + v2nd) * p_top
    w2 = v2nd / (v1 + v2nd) * p_top
    e1 = gi * EXPERTS_PER_GROUP + i1
    e2 = gi * EXPERTS_PER_GROUP + i2

    xrow = lax.broadcasted_iota(i32, (N_EXPERTS, TM), 0)
    oh1 = xrow == e1
    oh2 = xrow == e2
    both = jnp.where(oh1 | oh2, 1.0, 0.0)
    prefix = _dot(both.astype(bf16), tri_ref[...]) + cnt_ref[:, 0:1]
    r1 = jnp.sum(jnp.where(oh1, prefix, 0.0), axis=0, keepdims=True)
    r2 = jnp.sum(jnp.where(oh2, prefix, 0.0), axis=0, keepdims=True)
    cnt_ref[...] = cnt_ref[...] + jnp.sum(both, axis=1, keepdims=True)

    route_ref[...] = jnp.concatenate(
        [e1, e2, r1.astype(i32), r2.astype(i32), jnp.zeros((4, TM), i32)], axis=0)
    wpad = jnp.concatenate([w1, w2, jnp.zeros((LANES - 2, TM), f32)], axis=0)
    wtok_ref[...] = wpad.T


def _mixer(x, mod6, p, b0, B):
    _, T, D = x.shape
    TM = min(SEQ_TILE, T)
    nt = T // TM
    N = B * T
    const = lambda shape: pl.BlockSpec(shape, lambda b, t: (0,) * len(shape))
    in_specs = [
        pl.BlockSpec((None, TM, D), lambda b, t: (b0 + b, t, 0)),
        pl.BlockSpec((None, 6, D), lambda b, t: (b0 + b, 0, 0)),
        const((1, D)), const((1, D)),
        const(p["w_in"].shape),
        const((CONV_WIDTH, 8, D_CONV)), const((8, D_CONV)), const((1, D_CONV)), const((1, D_CONV)),
        const((D_CONV, D)), const((D_Q, D)), const((D, D)),
        const((1, D_Q)), const((1, D_KV)),
        const((NORM_BLK, NORM_BLK)), const((D_KV, D_KV)),
        const((N_KV_HEADS, Q_PER_KV * WINDOW, 2 * WINDOW)),
        pl.BlockSpec(memory_space=pltpu.SMEM),
        const((D, ROUTER_COLS)), const((D, 2 * ROUTER_COLS)), const((1, ROUTER_COLS)),
        const((TM, TM)),
    ]
    out_specs = [
        pl.BlockSpec((None, TM, D), lambda b, t: (b, t, 0)),
        pl.BlockSpec((None, TM, D // 2), lambda b, t: (b, t, 0)),
        pl.BlockSpec((8, TM), lambda b, t: (0, b * nt + t)),
        pl.BlockSpec((TM, LANES), lambda b, t: (b * nt + t, 0)),
        pl.BlockSpec((N_EXPERTS, LANES), lambda b, t: (0, 0)),
    ]
    out_shape = [
        jax.ShapeDtypeStruct((B, T, D), f32),
        jax.ShapeDtypeStruct((B, T, D // 2), i32),
        jax.ShapeDtypeStruct((8, N), i32),
        jax.ShapeDtypeStruct((N, LANES), f32),
        jax.ShapeDtypeStruct((N_EXPERTS, LANES), f32),
    ]
    scratch = [
        pltpu.VMEM((TM + 32, D_CONV), f32),
        pltpu.VMEM((7, TM + CONV_SPAN, D_CONV), f32),
        pltpu.VMEM((TM, D_CONV), f32),
        pltpu.VMEM(((p["w_in"].shape[1] - 2 * D_CONV) // PROJ_CHUNK, TM, PROJ_CHUNK), f32),
        pltpu.VMEM((N_KV_HEADS, TM + WINDOW, LANES), bf16),
        pltpu.VMEM((N_KV_HEADS, TM + WINDOW, LANES), bf16),
        pltpu.VMEM((TM, D_Q), bf16),
    ]
    tri = jnp.asarray(np.triu(np.ones((TM, TM), np.float32), 1), bf16)
    return pl.pallas_call(
        _mixer_kernel,
        grid=(B, nt),
        in_specs=in_specs, out_specs=out_specs, out_shape=out_shape, scratch_shapes=scratch,
        compiler_params=pltpu.CompilerParams(
            dimension_semantics=("arbitrary", "arbitrary"), vmem_limit_bytes=VMEM_LIMIT),
        name="mixer_router",
    )(x, mod6, p["gmix"], p["gffn"], p["w_in"], p["dw"], p["dwb"], p["lng"], p["lnb"],
      p["wco"], p["wao"], p["wout"], p["qg"], p["kg"], p["bq"], p["bk"], p["bias"], p["sinks"],
      p["wrh"], p["wr2"], p["br"], tri)


def _dest_kernel(pstart_ref, route_ref, dest_ref):
    e = route_ref[0:2, :]
    base = jnp.zeros(e.shape, i32)
    for x in range(N_EXPERTS):
        base = jnp.where(e == x, pstart_ref[x], base)
    dest_ref[...] = base + route_ref[2:4, :]


def _dest_slots(pstart, route):
    N = route.shape[1]
    TS = min(DEST_TILE, N)
    return pl.pallas_call(
        _dest_kernel,
        grid=(N // TS,),
        in_specs=[pl.BlockSpec(memory_space=pltpu.SMEM),
                  pl.BlockSpec((8, TS), lambda i: (0, i))],
        out_specs=pl.BlockSpec((2, TS), lambda i: (0, i)),
        out_shape=jax.ShapeDtypeStruct((2, N), i32),
        name="dest_slots",
    )(pstart, route)


def _sc_mesh():
    return plsc.VectorSubcoreMesh(core_axis_name="c", subcore_axis_name="s")


def _sc_scatter(h2, dest, n_slots):
    N, D = h2.shape
    per_w = N // SC_WORKERS
    C = SC_CHUNK
    n_chunks = per_w // C
    dest4 = dest.reshape(TOP_K, SC_WORKERS, n_chunks, C)

    @functools.partial(
        pl.kernel, mesh=_sc_mesh(),
        out_type=jax.ShapeDtypeStruct((n_slots, D), h2.dtype),
        scratch_types=[pltpu.VMEM((n_chunks, C), i32), pltpu.VMEM((n_chunks, C), i32),
                       pltpu.VMEM((2, C, D), h2.dtype),
                       pltpu.SemaphoreType.DMA((2,)), pltpu.SemaphoreType.DMA((2,))],
        name="sc_dispatch_scatter",
    )
    def run(h2_hbm, dest_hbm, xs_hbm, idx0, idx1, buf, sem_in, sem_out):
        wid = lax.axis_index("s") * SC_CORES + lax.axis_index("c")
        base = wid * per_w
        pltpu.sync_copy(dest_hbm.at[0, wid], idx0)
        pltpu.sync_copy(dest_hbm.at[1, wid], idx1)

        def load(j, b):
            return pltpu.make_async_copy(h2_hbm.at[pl.ds(base + j * C, C)], buf.at[b], sem_in.at[b])

        def put(j, b, idx):
            return pltpu.make_async_copy(buf.at[b], xs_hbm.at[idx.at[j]], sem_out.at[b])

        load(0, 0).start()

        @pl.loop(0, n_chunks, step=2)
        def _(j0):
            for b in range(2):
                j = j0 + b
                load(j, b).wait()

                @pl.when(j >= 1)
                def _():
                    put(j - 1, 1 - b, idx0).wait()
                    put(j - 1, 1 - b, idx1).wait()

                @pl.when(j + 1 < n_chunks)
                def _():
                    load(j + 1, 1 - b).start()

                put(j, b, idx0).start()
                put(j, b, idx1).start()

        put(n_chunks - 1, 1, idx0).wait()
        put(n_chunks - 1, 1, idx1).wait()

    return run(h2, dest4)


def _sc_gather(y, dest, N):
    D = y.shape[1]
    per_w = N // SC_WORKERS
    C = SC_CHUNK
    n_chunks = per_w // C
    dest4 = dest.reshape(TOP_K, SC_WORKERS, n_chunks, C)

    @functools.partial(
        pl.kernel, mesh=_sc_mesh(),
        out_type=jax.ShapeDtypeStruct((TOP_K, N, D), y.dtype),
        scratch_types=[pltpu.VMEM((n_chunks, C), i32), pltpu.VMEM((n_chunks, C), i32),
                       pltpu.VMEM((2, C, D), y.dtype),
                       pltpu.SemaphoreType.DMA((2,)), pltpu.SemaphoreType.DMA((2,))],
        name="sc_combine_gather",
    )
    def run(y_hbm, dest_hbm, yg_hbm, idx0, idx1, buf, sem_in, sem_out):
        wid = lax.axis_index("s") * SC_CORES + lax.axis_index("c")
        base = wid * per_w
        pltpu.sync_copy(dest_hbm.at[0, wid], idx0)
        pltpu.sync_copy(dest_hbm.at[1, wid], idx1)
        idx = (idx0, idx1)

        def get(j, k):
            return pltpu.make_async_copy(y_hbm.at[idx[k].at[j]], buf.at[k], sem_in.at[k])

        def put(j, k):
            return pltpu.make_async_copy(buf.at[k], yg_hbm.at[k, pl.ds(base + j * C, C)], sem_out.at[k])

        get(0, 0).start()
        get(0, 1).start()

        @pl.loop(0, n_chunks)
        def _(j):
            for k in range(TOP_K):
                get(j, k).wait()
                put(j, k).start()
            for k in range(TOP_K):
                put(j, k).wait()

                @pl.when(j + 1 < n_chunks)
                def _():
                    get(j + 1, k).start()

    return run(y, dest4)


def _expert_kernel(be_ref, nv_ref, nu_ref, first_ref, slot_ref, nxt_ref,
                   xs_ref, wg_hbm, wu_hbm, wdn_hbm, y_ref,
                   stg_g, stg_u, stg_d, wgu_ref, wd_ref, sems):
    i = pl.program_id(0)
    in_use = i < nu_ref[0]

    def fetch(e, s):
        return (pltpu.make_async_copy(wg_hbm.at[e], stg_g.at[s], sems.at[s, 0]),
                pltpu.make_async_copy(wu_hbm.at[e], stg_u.at[s], sems.at[s, 1]),
                pltpu.make_async_copy(wdn_hbm.at[e], stg_d.at[s], sems.at[s, 2]))

    @pl.when(in_use & (first_ref[i] == 1))
    def _():
        s = slot_ref[i]

        @pl.when(i == 0)
        def _():
            for cp in fetch(be_ref[0], 0):
                cp.start()

        for cp in fetch(be_ref[i], s):
            cp.wait()
        wgu_ref[:, 0:D_EXPERT] = stg_g[s].astype(bf16)
        wgu_ref[:, D_EXPERT:2 * D_EXPERT] = stg_u[s].astype(bf16)
        wd_ref[...] = stg_d[s].astype(bf16)

        @pl.when(nxt_ref[i] >= 0)
        def _():
            for cp in fetch(nxt_ref[i], 1 - s):
                cp.start()

    @pl.when(in_use)
    def _():
        live = lax.broadcasted_iota(i32, (MOE_BLOCK, 1), 0) < nv_ref[i]
        x_hi, x_lo = _unpack_halves(jnp.where(live, xs_ref[...], 0))
        half = x_hi.shape[1]
        gu = (_dot(x_hi.astype(bf16), wgu_ref[0:half, :]) +
              _dot(x_lo.astype(bf16), wgu_ref[half:2 * half, :]))
        gate = gu[:, 0:D_EXPERT]
        hid = (gate * jax.nn.sigmoid(gate)) * gu[:, D_EXPERT:2 * D_EXPERT]
        y_ref[...] = _pack_halves(_dot(hid.astype(bf16), wd_ref[...]))


def _experts(block_e, n_valid, n_used, counts, xs, w_gate, w_up, w_down):
    n_slots, DH = xs.shape
    D = 2 * DH
    n_blocks = n_slots // MOE_BLOCK
    idx = jnp.arange(n_blocks, dtype=i32)
    prev_e = jnp.concatenate([jnp.full((1,), -1, i32), block_e[:-1]])
    first = ((block_e != prev_e) & (idx < n_used[0])).astype(i32)
    slot = ((jnp.cumsum(first) - 1) % 2).astype(i32)
    e_ids = jnp.arange(N_EXPERTS, dtype=i32)
    later = (e_ids[None, :] > e_ids[:, None]) & (counts[None, :] > 0)
    nxt_of_e = jnp.min(jnp.where(later, e_ids[None, :], N_EXPERTS), axis=1)
    nxt = jnp.where(nxt_of_e[block_e] < N_EXPERTS, nxt_of_e[block_e], -1).astype(i32)
    last = lambda i, nu: jnp.minimum(i, nu[0] - 1)
    slot_map = lambda i, be, nv, nu, fi, sl, nx: (last(i, nu), 0)
    grid_spec = pltpu.PrefetchScalarGridSpec(
        num_scalar_prefetch=6,
        grid=(n_blocks,),
        in_specs=[pl.BlockSpec((MOE_BLOCK, DH), slot_map),
                  pl.BlockSpec(memory_space=pl.ANY),
                  pl.BlockSpec(memory_space=pl.ANY),
                  pl.BlockSpec(memory_space=pl.ANY)],
        out_specs=pl.BlockSpec((MOE_BLOCK, DH), slot_map),
        scratch_shapes=[pltpu.VMEM((2, D, D_EXPERT), f32), pltpu.VMEM((2, D, D_EXPERT), f32),
                        pltpu.VMEM((2, D_EXPERT, D), f32),
                        pltpu.VMEM((D, 2 * D_EXPERT), bf16), pltpu.VMEM((D_EXPERT, D), bf16),
                        pltpu.SemaphoreType.DMA((2, 3))],
    )
    return pl.pallas_call(
        _expert_kernel,
        grid_spec=grid_spec,
        out_shape=jax.ShapeDtypeStruct((n_slots, DH), i32),
        compiler_params=pltpu.CompilerParams(dimension_semantics=("arbitrary",),
                                             vmem_limit_bytes=VMEM_LIMIT),
        name="experts",
    )(block_e, n_valid, n_used, first, slot, nxt, xs, w_gate, w_up, w_down)


def _combine_kernel(yg_ref, x1_ref, wtok_ref, mod_ref, *rest):
    o_ref = rest[-1]
    w = wtok_ref[...]
    g2 = mod_ref[5:6, :]
    a_hi, a_lo = _unpack_halves(yg_ref[0])
    b_hi, b_lo = _unpack_halves(yg_ref[1])
    moe = jnp.concatenate([w[:, 0:1] * a_hi + w[:, 1:2] * b_hi,
                           w[:, 0:1] * a_lo + w[:, 1:2] * b_lo], axis=1)
    o_ref[...] = x1_ref[...] + g2 * moe


def _combine(yg, x1, wtok, mod6, T, b0, n_total, out_prev):
    Nc, D = x1.shape
    TS = min(COMBINE_TILE, T)
    per_seq = T // TS
    blk0 = b0 * per_seq
    in_specs = [pl.BlockSpec((TOP_K, TS, D // 2), lambda i: (0, i, 0)),
                pl.BlockSpec((TS, D), lambda i: (i, 0)),
                pl.BlockSpec((TS, LANES), lambda i: (i, 0)),
                pl.BlockSpec((None, 6, D), lambda i: (b0 + i // per_seq, 0, 0))]
    args = [yg, x1, wtok, mod6]
    aliases = {}
    if out_prev is not None:
        in_specs.append(pl.BlockSpec(memory_space=pl.ANY))
        args.append(out_prev)
        aliases = {len(args) - 1: 0}
    return pl.pallas_call(
        _combine_kernel,
        grid=(Nc // TS,),
        in_specs=in_specs,
        out_specs=pl.BlockSpec((TS, D), lambda i: (blk0 + i, 0)),
        out_shape=jax.ShapeDtypeStruct((n_total, D), f32),
        input_output_aliases=aliases,
        compiler_params=pltpu.CompilerParams(dimension_semantics=("arbitrary",),
                                             vmem_limit_bytes=VMEM_LIMIT),
        name="combine",
    )(*args)


def _block_diag_mean(n, blk):
    m = np.kron(np.eye(n // blk, dtype=np.float32), np.full((blk, blk), 1.0 / blk, np.float32))
    return jnp.asarray(m, bf16)


def _layer(x, mod6, bias, l, w):
    B, T, D = x.shape
    N = B * T
    w_rg, w_re = w["w_router_group"][l], w["w_router_expert"][l]
    wr = jnp.zeros((D, ROUTER_COLS), f32)
    wr = wr.at[:, 0:N_GROUPS].set(w_rg).at[:, EXPERT_COL0:EXPERT_COL0 + N_EXPERTS].set(w_re)
    br = jnp.zeros((1, ROUTER_COLS), f32)
    br = br.at[0, 0:N_GROUPS].set(w["b_router_group"][l])
    br = br.at[0, EXPERT_COL0:EXPERT_COL0 + N_EXPERTS].set(w["b_router_expert"][l])
    wrh = wr.astype(bf16)
    wrl = (wr - wrh.astype(f32)).astype(bf16)
    p = dict(
        gmix=w["norm_mix_g"][l].reshape(1, D), gffn=w["norm_ffn_g"][l].reshape(1, D),
        w_in=w["w_in"][l].astype(bf16),
        dw=jnp.broadcast_to(w["dw_kernel"][l][:, None, :], (CONV_WIDTH, 8, D_CONV)), dwb=jnp.broadcast_to(w["dw_bias"][l][None, :], (8, D_CONV)),
        lng=w["conv_ln_g"][l].reshape(1, D_CONV), lnb=w["conv_ln_b"][l].reshape(1, D_CONV),
        wco=w["w_conv_out"][l].astype(bf16), wao=w["w_attn_out"][l].astype(bf16),
        wout=w["w_out"][l].astype(bf16),
        qg=(jnp.tile(w["q_norm_g"][l], N_Q_HEADS) * (HEAD_DIM ** -0.5 * LOG2E)).reshape(1, D_Q),
        kg=jnp.tile(w["k_norm_g"][l], N_KV_HEADS).reshape(1, D_KV),
        bq=_block_diag_mean(NORM_BLK, HEAD_DIM), bk=_block_diag_mean(D_KV, HEAD_DIM),
        bias=bias, sinks=w["sinks"][l], wrh=wrh, wr2=jnp.concatenate([wrh, wrl], axis=1), br=br,
    )
    w_gate, w_up, w_down = w["w_exp_gate"][l], w["w_exp_up"][l], w["w_exp_down"][l]

    n_chunks = MOE_CHUNKS if B % MOE_CHUNKS == 0 else 1
    Bc = B // n_chunks
    Nc = Bc * T
    n_blocks = -(-(Nc * TOP_K) // MOE_BLOCK) + N_EXPERTS
    blk0 = jnp.arange(n_blocks, dtype=i32) * MOE_BLOCK
    stage = []
    for ch in range(n_chunks):
        x1, h2, route, wtok, cnt = _mixer(x, mod6, p, ch * Bc, Bc)
        counts = cnt[:, 0].astype(i32)
        pcounts = (counts + MOE_BLOCK - 1) // MOE_BLOCK * MOE_BLOCK
        pend = jnp.cumsum(pcounts)
        pstart = pend - pcounts
        block_e = jnp.minimum(jnp.sum((pend[None, :] <= blk0[:, None]).astype(i32), axis=1), N_EXPERTS - 1)
        n_valid = jnp.clip((pstart + counts)[block_e] - blk0, 0, MOE_BLOCK).astype(i32)
        n_used = (pend[-1:] // MOE_BLOCK).astype(i32)
        dest = _dest_slots(pstart, route)
        xs = _sc_scatter(h2.reshape(Nc, D // 2), dest, n_blocks * MOE_BLOCK)
        stage.append((x1, wtok, dest, xs, block_e, n_valid, n_used, counts))
    ys = [_experts(be, nv, nu, cn, xs, w_gate, w_up, w_down) for (_, _, _, xs, be, nv, nu, cn) in stage]
    ygs = [_sc_gather(y, st[2], Nc) for y, st in zip(ys, stage)]
    out = None
    for ch in range(n_chunks):
        x1, wtok = stage[ch][0], stage[ch][1]
        out = _combine(ygs[ch], x1.reshape(Nc, D), wtok, mod6, T, ch * Bc, B * T, out)
    return out.reshape(B, T, D)


def kernel(x, c, w_ada, b_ada, norm_mix_g, w_in, dw_kernel, dw_bias, conv_ln_g, conv_ln_b,
           w_conv_out, q_norm_g, k_norm_g, sinks, w_attn_out, w_out, rel_bias_table, norm_ffn_g,
           w_router_group, b_router_group, w_router_expert, b_router_expert,
           w_exp_gate, w_exp_up, w_exp_down):
    w = dict(norm_mix_g=norm_mix_g, w_in=w_in, dw_kernel=dw_kernel, dw_bias=dw_bias,
             conv_ln_g=conv_ln_g, conv_ln_b=conv_ln_b, w_conv_out=w_conv_out, q_norm_g=q_norm_g,
             k_norm_g=k_norm_g, sinks=sinks, w_attn_out=w_attn_out, w_out=w_out,
             norm_ffn_g=norm_ffn_g, w_router_group=w_router_group, b_router_group=b_router_group,
             w_router_expert=w_router_expert, b_router_expert=b_router_expert,
             w_exp_gate=w_exp_gate, w_exp_up=w_exp_up, w_exp_down=w_exp_down)
    B = x.shape[0]
    bias = _bias_band(rel_bias_table).reshape(N_KV_HEADS, Q_PER_KV * WINDOW, 2 * WINDOW)
    for l in range(w_ada.shape[0]):
        mod6 = _modulation(c, w_ada[l], b_ada[l]).reshape(B, 6, D_MODEL)
        x = _layer(x, mod6, bias, l, w)
    return x
```

```python
import functools
import math

import jax
import jax.numpy as jnp
import numpy as np
from jax import lax
from jax.experimental import pallas as pl
from jax.experimental.pallas import tpu as pltpu
from jax.experimental.pallas import tpu_sc as plsc

D_MODEL = 1024
D_CONV = 512
CONV_WIDTH = 31
N_Q_HEADS = 8
N_KV_HEADS = 2
HEAD_DIM = 64
Q_PER_KV = N_Q_HEADS // N_KV_HEADS
WINDOW = 128
N_BUCKETS = 32
MAX_DISTANCE = 128
N_GROUPS = 4
EXPERTS_PER_GROUP = 8
N_EXPERTS = N_GROUPS * EXPERTS_PER_GROUP
TOP_K = 2
D_EXPERT = 256
D_Q = N_Q_HEADS * HEAD_DIM
D_KV = N_KV_HEADS * HEAD_DIM
EPS = 1e-6
NEG_INF = -1e30
LOG2E = math.log2(math.e)

LANES = 128
SEQ_TILE = 256
CONV_ROWS = 64
NORM_BLK = 256
PROJ_CHUNK = 256
CONV_SPAN = 24
MOE_BLOCK = 1024
MOE_CHUNKS = 2
DEST_TILE = 8192
COMBINE_TILE = 1024
SC_CORES = 2
SC_SUBCORES = 16
SC_WORKERS = SC_CORES * SC_SUBCORES
SC_CHUNK = 64
ROUTER_COLS = LANES
EXPERT_COL0 = 8
VMEM_LIMIT = 56 * 1024 * 1024

f32 = jnp.float32
bf16 = jnp.bfloat16
i32 = jnp.int32


def _dot(a, b):
    return jnp.dot(a, b, preferred_element_type=f32)


def _split(a):
    hi = a.astype(bf16)
    lo = (a - hi.astype(f32)).astype(bf16)
    return hi, lo


def _pack_halves(x):
    c = x.shape[1] // 2
    hi = lax.bitcast_convert_type(x[:, 0:c].astype(bf16).astype(f32), jnp.uint32)
    lo = lax.bitcast_convert_type(x[:, c:2 * c].astype(bf16).astype(f32), jnp.uint32)
    word = (hi & jnp.uint32(0xFFFF0000)) | (lo >> jnp.uint32(16))
    return lax.bitcast_convert_type(word, i32)


def _unpack_halves(word):
    u = lax.bitcast_convert_type(word, jnp.uint32)
    hi = lax.bitcast_convert_type(u & jnp.uint32(0xFFFF0000), f32)
    lo = lax.bitcast_convert_type(u << jnp.uint32(16), f32)
    return hi, lo


def _dot3(a, b):
    ah, al = _split(a)
    bh, bl = _split(b)
    return _dot(ah, bh) + _dot(al, bh) + _dot(ah, bl)


def _mod_kernel(c_ref, w_ref, b_ref, o_ref):
    c = c_ref[...]
    s = c * jax.nn.sigmoid(c)
    o_ref[...] = _dot3(s, w_ref[...]) + b_ref[...]


def _modulation(c, w_ada, b_ada):
    B, D = c.shape
    n_out = w_ada.shape[1]
    return pl.pallas_call(
        _mod_kernel,
        grid=(n_out // D,),
        in_specs=[pl.BlockSpec((B, D), lambda j: (0, 0)),
                  pl.BlockSpec((D, D), lambda j: (0, j)),
                  pl.BlockSpec((1, D), lambda j: (0, j))],
        out_specs=pl.BlockSpec((B, D), lambda j: (0, j)),
        out_shape=jax.ShapeDtypeStruct((B, n_out), f32),
        name="modulation",
    )(c, w_ada, b_ada.reshape(1, n_out))


def _band_buckets():
    qi = np.arange(WINDOW)[:, None]
    kj = np.arange(2 * WINDOW)[None, :]
    dist = np.clip(qi + WINDOW - kj, 0, MAX_DISTANCE)
    max_exact = N_BUCKETS // 2
    d = np.maximum(dist, 1).astype(np.float32)
    large = max_exact + (np.log(d / np.float32(max_exact)) / np.float32(math.log(MAX_DISTANCE / max_exact))
                         * np.float32(N_BUCKETS - max_exact)).astype(np.int32)
    large = np.minimum(large, N_BUCKETS - 1)
    bucket = np.where(dist < max_exact, dist, large).astype(np.int32)
    raw = qi + WINDOW - kj
    return np.where((raw >= 0) & (raw < WINDOW), bucket, -1).astype(np.int32)


def _bias_kernel(tab_ref, bucket_ref, o_ref):
    bk = bucket_ref[...]
    for h in range(N_Q_HEADS):
        acc = jnp.full(bk.shape, NEG_INF, f32)
        for b in range(N_BUCKETS):
            acc = jnp.where(bk == b, tab_ref[b, h] * LOG2E, acc)
        o_ref[h] = acc


def _bias_band(rel_bias_table):
    return pl.pallas_call(
        _bias_kernel,
        in_specs=[pl.BlockSpec(memory_space=pltpu.SMEM),
                  pl.BlockSpec(memory_space=pltpu.VMEM)],
        out_specs=pl.BlockSpec(memory_space=pltpu.VMEM),
        out_shape=jax.ShapeDtypeStruct((N_Q_HEADS, WINDOW, 2 * WINDOW), f32),
        name="bias_band",
    )(rel_bias_table, jnp.asarray(_band_buckets()))


def _mixer_kernel(x_ref, mod_ref, gmix_ref, gffn_ref, win_ref, dw_ref, dwb_ref, lng_ref, lnb_ref,
                  wco_ref, wao_ref, wout_ref, qg_ref, kg_ref, bq_ref, bk_ref, bias_ref, sink_ref,
                  wrh_ref, wr2_ref, br_ref, tri_ref,
                  x1_ref, h2_ref, route_ref, wtok_ref, cnt_ref,
                  uext, ush, conv_sc, proj_sc, k2, v2, osc):
    TM = x_ref.shape[0]
    HALO = 32
    b = pl.program_id(0)
    t = pl.program_id(1)
    first = t == 0

    @pl.when(first)
    def _():
        uext[0:HALO, :] = jnp.zeros((HALO, D_CONV), f32)
        k2[:, 0:WINDOW, :] = jnp.zeros((N_KV_HEADS, WINDOW, LANES), bf16)
        v2[:, 0:WINDOW, :] = jnp.zeros((N_KV_HEADS, WINDOW, LANES), bf16)

    @pl.when(first & (b == 0))
    def _():
        cnt_ref[...] = jnp.zeros(cnt_ref.shape, f32)

    x = x_ref[...]
    mod = mod_ref[...]
    sh1, sc1, g1 = mod[0:1, :], mod[1:2, :], mod[2:3, :]
    sh2, sc2, g2 = mod[3:4, :], mod[4:5, :], mod[5:6, :]
    del g2

    ms = jnp.mean(x * x, axis=-1, keepdims=True)
    h = (x * lax.rsqrt(ms + EPS)) * (gmix_ref[...] * (1.0 + sc1)) + sh1
    hb = h.astype(bf16)

    ab = _dot(hb, win_ref[:, 0:2 * D_CONV])
    u = ab[:, 0:D_CONV] * jax.nn.sigmoid(ab[:, D_CONV:2 * D_CONV])
    uext[HALO:HALO + TM, :] = u
    for r in range(1, 8):
        ush[r - 1] = uext[r:r + TM + CONV_SPAN, :]
    n_q = D_Q // PROJ_CHUNK
    n_g = D_MODEL // PROJ_CHUNK
    n_jobs = (win_ref.shape[1] - 2 * D_CONV) // PROJ_CHUNK
    n_rb = TM // CONV_ROWS
    n_t8 = CONV_ROWS // 8
    units = [(cb, rg) for cb in range(D_CONV // LANES) for rg in range(n_rb)]
    job_at = {(i * len(units)) // n_jobs: i for i in range(n_jobs)}
    assert len(job_at) == n_jobs
    for ui, (cb, rg) in enumerate(units):
        if ui in job_at:
            c_lo = 2 * D_CONV + job_at[ui] * PROJ_CHUNK
            proj_sc[job_at[ui]] = _dot(hb, win_ref[:, c_lo:c_lo + PROJ_CHUNK])
        cs = slice(cb * LANES, (cb + 1) * LANES)
        base = rg * CONV_ROWS
        accs = [None] * n_t8
        for r in range(8):
            taps = [(j, (j + 2) // 8) for j in range(CONV_WIDTH) if (j + 2) % 8 == r]
            t_lo = min(a for _, a in taps)
            t_hi = max(a for _, a in taps) + n_t8
            tiles = {}
            for t8 in range(t_lo, t_hi):
                rows = slice(base + 8 * t8, base + 8 * t8 + 8)
                tiles[t8] = uext[rows, cs] if r == 0 else ush[r - 1, rows, cs]
            for j, a in taps:
                tap = dw_ref[j, :, cs]
                for s8 in range(n_t8):
                    term = tap * tiles[a + s8]
                    accs[s8] = term if accs[s8] is None else accs[s8] + term
        for s8 in range(n_t8):
            conv_sc[base + 8 * s8:base + 8 * s8 + 8, cs] = accs[s8] + dwb_ref[:, cs]
    conv = conv_sc[...]
    uext[0:HALO, :] = uext[TM:TM + HALO, :]
    mu = jnp.mean(conv, axis=-1, keepdims=True)
    dc = conv - mu
    var = jnp.mean(dc * dc, axis=-1, keepdims=True)
    yn = dc * lax.rsqrt(var + EPS) * lng_ref[...] + lnb_ref[...]
    act = yn * jax.nn.sigmoid(yn)
    y_conv = _dot(act.astype(bf16), wco_ref[...])

    q = jnp.concatenate([proj_sc[i] for i in range(n_q)], axis=1)
    k = proj_sc[n_q, :, 0:D_KV]
    v = proj_sc[n_q, :, D_KV:2 * D_KV]

    def head_norm(z, blk_ref, g):
        sq = (z * z).astype(bf16)
        blk = blk_ref.shape[0]
        msq = jnp.concatenate([_dot(sq[:, c:c + blk], blk_ref[...]) for c in range(0, z.shape[1], blk)], axis=1)
        return z * lax.rsqrt(msq + EPS) * g

    qn = head_norm(q, bq_ref, qg_ref[...]).astype(bf16)
    kn = head_norm(k, bk_ref, kg_ref[...])
    lo_half = lax.broadcasted_iota(i32, (TM, LANES), 1) < HEAD_DIM
    kr = pltpu.roll(kn, HEAD_DIM, 1)
    vr = pltpu.roll(v, HEAD_DIM, 1)
    k2[0, WINDOW:WINDOW + TM, :] = jnp.where(lo_half, kn, kr).astype(bf16)
    k2[1, WINDOW:WINDOW + TM, :] = jnp.where(lo_half, kr, kn).astype(bf16)
    v2[0, WINDOW:WINDOW + TM, :] = jnp.where(lo_half, v, vr).astype(bf16)
    v2[1, WINDOW:WINDOW + TM, :] = jnp.where(lo_half, vr, v).astype(bf16)

    QROWS = Q_PER_KV * WINDOW
    col = lax.broadcasted_iota(i32, (QROWS, 2 * WINDOW), 1)
    has_prev = col >= jnp.where(first, WINDOW, 0)
    hrow = lax.broadcasted_iota(i32, (QROWS, 1), 0) // WINDOW
    lo128 = lax.broadcasted_iota(i32, (WINDOW, LANES), 1) < HEAD_DIM
    zero_q = jnp.zeros((WINDOW, LANES), bf16)
    for g in range(N_KV_HEADS):
        sink = LOG2E * jnp.where(hrow == 0, sink_ref[4 * g],
                                 jnp.where(hrow == 1, sink_ref[4 * g + 1],
                                           jnp.where(hrow == 2, sink_ref[4 * g + 2], sink_ref[4 * g + 3])))
        bias_g = bias_ref[g]
        for j in range(TM // WINDOW):
            rs = slice(j * WINDOW, (j + 1) * WINDOW)
            qa = qn[rs, 2 * LANES * g:2 * LANES * g + LANES]
            qb = qn[rs, 2 * LANES * g + LANES:2 * LANES * (g + 1)]
            qs = jnp.concatenate([jnp.where(lo128, qa, zero_q), jnp.where(lo128, zero_q, qa),
                                  jnp.where(lo128, qb, zero_q), jnp.where(lo128, zero_q, qb)], axis=0)
            kk = k2[g, j * WINDOW:(j + 2) * WINDOW, :]
            s = lax.dot_general(qs, kk, (((1,), (1,)), ((), ())), preferred_element_type=f32)
            logits = jnp.where(has_prev, s + bias_g, NEG_INF) if j == 0 else s + bias_g
            m = jnp.maximum(jnp.max(logits, axis=-1, keepdims=True), sink)
            p = jnp.exp2(logits - m)
            den = jnp.sum(p, axis=-1, keepdims=True) + jnp.exp2(sink - m)
            o2 = _dot(p.astype(bf16), v2[g, j * WINDOW:(j + 2) * WINDOW, :]) * (1.0 / den)
            osc[rs, 2 * LANES * g:2 * LANES * g + LANES] = jnp.where(
                lo128, o2[0:WINDOW], o2[WINDOW:2 * WINDOW]).astype(bf16)
            osc[rs, 2 * LANES * g + LANES:2 * LANES * (g + 1)] = jnp.where(
                lo128, o2[2 * WINDOW:3 * WINDOW], o2[3 * WINDOW:4 * WINDOW]).astype(bf16)
    k2[:, 0:WINDOW, :] = k2[:, TM:TM + WINDOW, :]
    v2[:, 0:WINDOW, :] = v2[:, TM:TM + WINDOW, :]
    y_attn = _dot(osc[...], wao_ref[...])

    merged = []
    for i in range(n_g):
        cs = slice(i * PROJ_CHUNK, (i + 1) * PROJ_CHUNK)
        g_conv = jax.nn.sigmoid(proj_sc[n_q + 1 + i])
        g_attn = jax.nn.sigmoid(proj_sc[n_q + 1 + n_g + i])
        merged.append((g_conv * y_conv[:, cs] + g_attn * y_attn[:, cs]).astype(bf16))
    merged = jnp.concatenate(merged, axis=1)
    x1 = x + g1 * _dot(merged, wout_ref[...])
    x1_ref[...] = x1

    ms2 = jnp.mean(x1 * x1, axis=-1, keepdims=True)
    h2 = (x1 * lax.rsqrt(ms2 + EPS)) * (gffn_ref[...] * (1.0 + sc2)) + sh2
    h2_ref[...] = _pack_halves(h2)
    hh, hl = _split(h2)
    hw = _dot(hh, wr2_ref[...])
    lg = hw[:, 0:ROUTER_COLS] + hw[:, ROUTER_COLS:2 * ROUTER_COLS] + _dot(hl, wrh_ref[...]) + br_ref[...]
    lt = lg.T
    gl = lt[0:N_GROUPS, :]
    grow = lax.broadcasted_iota(i32, (N_GROUPS, TM), 0)
    gmax = jnp.max(gl, axis=0, keepdims=True)
    gi = jnp.min(jnp.where(gl == gmax, grow, N_GROUPS), axis=0, keepdims=True)
    p_top = 1.0 / jnp.sum(jnp.exp(gl - gmax), axis=0, keepdims=True)
    sel = lt[EXPERT_COL0:EXPERT_COL0 + EXPERTS_PER_GROUP, :]
    for gg in range(1, N_GROUPS):
        lo_r = EXPERT_COL0 + gg * EXPERTS_PER_GROUP
        sel = jnp.where(gi == gg, lt[lo_r:lo_r + EXPERTS_PER_GROUP, :], sel)
    erow = lax.broadcasted_iota(i32, (EXPERTS_PER_GROUP, TM), 0)
    m1 = jnp.max(sel, axis=0, keepdims=True)
    i1 = jnp.min(jnp.where(sel == m1, erow, EXPERTS_PER_GROUP), axis=0, keepdims=True)
    rest = jnp.where(erow == i1, -jnp.inf, sel)
    m2 = jnp.max(rest, axis=0, keepdims=True)
    i2 = jnp.min(jnp.where(rest == m2, erow, EXPERTS_PER_GROUP), axis=0, keepdims=True)
    z = jnp.sum(jnp.exp(sel - m1), axis=0, keepdims=True)
    v1 = 1.0 / z
    v2nd = jnp.exp(m2 - m1) / z
    w1 = v1 / (v1 + v2nd) * p_top
    w2 = v2nd / (v1 + v2nd) * p_top
    e1 = gi * EXPERTS_PER_GROUP + i1
    e2 = gi * EXPERTS_PER_GROUP + i2

    xrow = lax.broadcasted_iota(i32, (N_EXPERTS, TM), 0)
    oh1 = xrow == e1
    oh2 = xrow == e2
    both = jnp.where(oh1 | oh2, 1.0, 0.0)
    prefix = _dot(both.astype(bf16), tri_ref[...]) + cnt_ref[:, 0:1]
    r1 = jnp.sum(jnp.where(oh1, prefix, 0.0), axis=0, keepdims=True)
    r2 = jnp.sum(jnp.where(oh2, prefix, 0.0), axis=0, keepdims=True)
    cnt_ref[...] = cnt_ref[...] + jnp.sum(both, axis=1, keepdims=True)

    route_ref[...] = jnp.concatenate(
        [e1, e2, r1.astype(i32), r2.astype(i32), jnp.zeros((4, TM), i32)], axis=0)
    wpad = jnp.concatenate([w1, w2, jnp.zeros((LANES - 2, TM), f32)], axis=0)
    wtok_ref[...] = wpad.T


def _mixer(x, mod6, p, b0, B):
    _, T, D = x.shape
    TM = min(SEQ_TILE, T)
    nt = T // TM
    N = B * T
    const = lambda shape: pl.BlockSpec(shape, lambda b, t: (0,) * len(shape))
    in_specs = [
        pl.BlockSpec((None, TM, D), lambda b, t: (b0 + b, t, 0)),
        pl.BlockSpec((None, 6, D), lambda b, t: (b0 + b, 0, 0)),
        const((1, D)), const((1, D)),
        const(p["w_in"].shape),
        const((CONV_WIDTH, 8, D_CONV)), const((8, D_CONV)), const((1, D_CONV)), const((1, D_CONV)),
        const((D_CONV, D)), const((D_Q, D)), const((D, D)),
        const((1, D_Q)), const((1, D_KV)),
        const((NORM_BLK, NORM_BLK)), const((D_KV, D_KV)),
        const((N_KV_HEADS, Q_PER_KV * WINDOW, 2 * WINDOW)),
        pl.BlockSpec(memory_space=pltpu.SMEM),
        const((D, ROUTER_COLS)), const((D, 2 * ROUTER_COLS)), const((1, ROUTER_COLS)),
        const((TM, TM)),
    ]
    out_specs = [
        pl.BlockSpec((None, TM, D), lambda b, t: (b, t, 0)),
        pl.BlockSpec((None, TM, D // 2), lambda b, t: (b, t, 0)),
        pl.BlockSpec((8, TM), lambda b, t: (0, b * nt + t)),
        pl.BlockSpec((TM, LANES), lambda b, t: (b * nt + t, 0)),
        pl.BlockSpec((N_EXPERTS, LANES), lambda b, t: (0, 0)),
    ]
    out_shape = [
        jax.ShapeDtypeStruct((B, T, D), f32),
        jax.ShapeDtypeStruct((B, T, D // 2), i32),
        jax.ShapeDtypeStruct((8, N), i32),
        jax.ShapeDtypeStruct((N, LANES), f32),
        jax.ShapeDtypeStruct((N_EXPERTS, LANES), f32),
    ]
    scratch = [
        pltpu.VMEM((TM + 32, D_CONV), f32),
        pltpu.VMEM((7, TM + CONV_SPAN, D_CONV), f32),
        pltpu.VMEM((TM, D_CONV), f32),
        pltpu.VMEM(((p["w_in"].shape[1] - 2 * D_CONV) // PROJ_CHUNK, TM, PROJ_CHUNK), f32),
        pltpu.VMEM((N_KV_HEADS, TM + WINDOW, LANES), bf16),
        pltpu.VMEM((N_KV_HEADS, TM + WINDOW, LANES), bf16),
        pltpu.VMEM((TM, D_Q), bf16),
    ]
    tri = jnp.asarray(np.triu(np.ones((TM, TM), np.float32), 1), bf16)
    return pl.pallas_call(
        _mixer_kernel,
        grid=(B, nt),
        in_specs=in_specs, out_specs=out_specs, out_shape=out_shape, scratch_shapes=scratch,
        compiler_params=pltpu.CompilerParams(
            dimension_semantics=("arbitrary", "arbitrary"), vmem_limit_bytes=VMEM_LIMIT),
        name="mixer_router",
    )(x, mod6, p["gmix"], p["gffn"], p["w_in"], p["dw"], p["dwb"], p["lng"], p["lnb"],
      p["wco"], p["wao"], p["wout"], p["qg"], p["kg"], p["bq"], p["bk"], p["bias"], p["sinks"],
      p["wrh"], p["wr2"], p["br"], tri)


def _dest_kernel(pstart_ref, route_ref, dest_ref):
    e = route_ref[0:2, :]
    base = jnp.zeros(e.shape, i32)
    for x in range(N_EXPERTS):
        base = jnp.where(e == x, pstart_ref[x], base)
    dest_ref[...] = base + route_ref[2:4, :]


def _dest_slots(pstart, route):
    N = route.shape[1]
    TS = min(DEST_TILE, N)
    return pl.pallas_call(
        _dest_kernel,
        grid=(N // TS,),
        in_specs=[pl.BlockSpec(memory_space=pltpu.SMEM),
                  pl.BlockSpec((8, TS), lambda i: (0, i))],
        out_specs=pl.BlockSpec((2, TS), lambda i: (0, i)),
        out_shape=jax.ShapeDtypeStruct((2, N), i32),
        name="dest_slots",
    )(pstart, route)


def _sc_mesh():
    return plsc.VectorSubcoreMesh(core_axis_name="c", subcore_axis_name="s")


def _sc_scatter(h2, dest, n_slots):
    N, D = h2.shape
    per_w = N // SC_WORKERS
    C = SC_CHUNK
    n_chunks = per_w // C
    dest4 = dest.reshape(TOP_K, SC_WORKERS, n_chunks, C)

    @functools.partial(
        pl.kernel, mesh=_sc_mesh(),
        out_type=jax.ShapeDtypeStruct((n_slots, D), h2.dtype),
        scratch_types=[pltpu.VMEM((n_chunks, C), i32), pltpu.VMEM((n_chunks, C), i32),
                       pltpu.VMEM((2, C, D), h2.dtype),
                       pltpu.SemaphoreType.DMA((2,)), pltpu.SemaphoreType.DMA((2,))],
        name="sc_dispatch_scatter",
    )
    def run(h2_hbm, dest_hbm, xs_hbm, idx0, idx1, buf, sem_in, sem_out):
        wid = lax.axis_index("s") * SC_CORES + lax.axis_index("c")
        base = wid * per_w
        pltpu.sync_copy(dest_hbm.at[0, wid], idx0)
        pltpu.sync_copy(dest_hbm.at[1, wid], idx1)

        def load(j, b):
            return pltpu.make_async_copy(h2_hbm.at[pl.ds(base + j * C, C)], buf.at[b], sem_in.at[b])

        def put(j, b, idx):
            return pltpu.make_async_copy(buf.at[b], xs_hbm.at[idx.at[j]], sem_out.at[b])

        load(0, 0).start()

        @pl.loop(0, n_chunks, step=2)
        def _(j0):
            for b in range(2):
                j = j0 + b
                load(j, b).wait()

                @pl.when(j >= 1)
                def _():
                    put(j - 1, 1 - b, idx0).wait()
                    put(j - 1, 1 - b, idx1).wait()

                @pl.when(j + 1 < n_chunks)
                def _():
                    load(j + 1, 1 - b).start()

                put(j, b, idx0).start()
                put(j, b, idx1).start()

        put(n_chunks - 1, 1, idx0).wait()
        put(n_chunks - 1, 1, idx1).wait()

    return run(h2, dest4)


def _sc_gather(y, dest, N):
    D = y.shape[1]
    per_w = N // SC_WORKERS
    C = SC_CHUNK
    n_chunks = per_w // C
    dest4 = dest.reshape(TOP_K, SC_WORKERS, n_chunks, C)

    @functools.partial(
        pl.kernel, mesh=_sc_mesh(),
        out_type=jax.ShapeDtypeStruct((TOP_K, N, D), y.dtype),
        scratch_types=[pltpu.VMEM((n_chunks, C), i32), pltpu.VMEM((n_chunks, C), i32),
                       pltpu.VMEM((2, C, D), y.dtype),
                       pltpu.SemaphoreType.DMA((2,)), pltpu.SemaphoreType.DMA((2,))],
        name="sc_combine_gather",
    )
    def run(y_hbm, dest_hbm, yg_hbm, idx0, idx1, buf, sem_in, sem_out):
        wid = lax.axis_index("s") * SC_CORES + lax.axis_index("c")
        base = wid * per_w
        pltpu.sync_copy(dest_hbm.at[0, wid], idx0)
        pltpu.sync_copy(dest_hbm.at[1, wid], idx1)
        idx = (idx0, idx1)

        def get(j, k):
            return pltpu.make_async_copy(y_hbm.at[idx[k].at[j]], buf.at[k], sem_in.at[k])

        def put(j, k):
            return pltpu.make_async_copy(buf.at[k], yg_hbm.at[k, pl.ds(base + j * C, C)], sem_out.at[k])

        get(0, 0).start()
        get(0, 1).start()

        @pl.loop(0, n_chunks)
        def _(j):
            for k in range(TOP_K):
                get(j, k).wait()
                put(j, k).start()
            for k in range(TOP_K):
                put(j, k).wait()

                @pl.when(j + 1 < n_chunks)
                def _():
                    get(j + 1, k).start()

    return run(y, dest4)


def _expert_kernel(be_ref, nv_ref, nu_ref, first_ref, slot_ref, nxt_ref,
                   xs_ref, wg_hbm, wu_hbm, wdn_hbm, y_ref,
                   stg_g, stg_u, stg_d, wgu_ref, wd_ref, sems):
    i = pl.program_id(0)
    in_use = i < nu_ref[0]

    def fetch(e, s):
        return (pltpu.make_async_copy(wg_hbm.at[e], stg_g.at[s], sems.at[s, 0]),
                pltpu.make_async_copy(wu_hbm.at[e], stg_u.at[s], sems.at[s, 1]),
                pltpu.make_async_copy(wdn_hbm.at[e], stg_d.at[s], sems.at[s, 2]))

    @pl.when(in_use & (first_ref[i] == 1))
    def _():
        s = slot_ref[i]

        @pl.when(i == 0)
        def _():
            for cp in fetch(be_ref[0], 0):
                cp.start()

        for cp in fetch(be_ref[i], s):
            cp.wait()
        wgu_ref[:, 0:D_EXPERT] = stg_g[s].astype(bf16)
        wgu_ref[:, D_EXPERT:2 * D_EXPERT] = stg_u[s].astype(bf16)
        wd_ref[...] = stg_d[s].astype(bf16)

        @pl.when(nxt_ref[i] >= 0)
        def _():
            for cp in fetch(nxt_ref[i], 1 - s):
                cp.start()

    @pl.when(in_use)
    def _():
        live = lax.broadcasted_iota(i32, (MOE_BLOCK, 1), 0) < nv_ref[i]
        x_hi, x_lo = _unpack_halves(jnp.where(live, xs_ref[...], 0))
        half = x_hi.shape[1]
        gu = (_dot(x_hi.astype(bf16), wgu_ref[0:half, :]) +
              _dot(x_lo.astype(bf16), wgu_ref[half:2 * half, :]))
        gate = gu[:, 0:D_EXPERT]
        hid = (gate * jax.nn.sigmoid(gate)) * gu[:, D_EXPERT:2 * D_EXPERT]
        y_ref[...] = _pack_halves(_dot(hid.astype(bf16), wd_ref[...]))


def _experts(block_e, n_valid, n_used, counts, xs, w_gate, w_up, w_down):
    n_slots, DH = xs.shape
    D = 2 * DH
    n_blocks = n_slots // MOE_BLOCK
    idx = jnp.arange(n_blocks, dtype=i32)
    prev_e = jnp.concatenate([jnp.full((1,), -1, i32), block_e[:-1]])
    first = ((block_e != prev_e) & (idx < n_used[0])).astype(i32)
    slot = ((jnp.cumsum(first) - 1) % 2).astype(i32)
    e_ids = jnp.arange(N_EXPERTS, dtype=i32)
    later = (e_ids[None, :] > e_ids[:, None]) & (counts[None, :] > 0)
    nxt_of_e = jnp.min(jnp.where(later, e_ids[None, :], N_EXPERTS), axis=1)
    nxt = jnp.where(nxt_of_e[block_e] < N_EXPERTS, nxt_of_e[block_e], -1).astype(i32)
    last = lambda i, nu: jnp.minimum(i, nu[0] - 1)
    slot_map = lambda i, be, nv, nu, fi, sl, nx: (last(i, nu), 0)
    grid_spec = pltpu.PrefetchScalarGridSpec(
        num_scalar_prefetch=6,
        grid=(n_blocks,),
        in_specs=[pl.BlockSpec((MOE_BLOCK, DH), slot_map),
                  pl.BlockSpec(memory_space=pl.ANY),
                  pl.BlockSpec(memory_space=pl.ANY),
                  pl.BlockSpec(memory_space=pl.ANY)],
        out_specs=pl.BlockSpec((MOE_BLOCK, DH), slot_map),
        scratch_shapes=[pltpu.VMEM((2, D, D_EXPERT), f32), pltpu.VMEM((2, D, D_EXPERT), f32),
                        pltpu.VMEM((2, D_EXPERT, D), f32),
                        pltpu.VMEM((D, 2 * D_EXPERT), bf16), pltpu.VMEM((D_EXPERT, D), bf16),
                        pltpu.SemaphoreType.DMA((2, 3))],
    )
    return pl.pallas_call(
        _expert_kernel,
        grid_spec=grid_spec,
        out_shape=jax.ShapeDtypeStruct((n_slots, DH), i32),
        compiler_params=pltpu.CompilerParams(dimension_semantics=("arbitrary",),
                                             vmem_limit_bytes=VMEM_LIMIT),
        name="experts",
    )(block_e, n_valid, n_used, first, slot, nxt, xs, w_gate, w_up, w_down)


def _combine_kernel(yg_ref, x1_ref, wtok_ref, mod_ref, *rest):
    o_ref = rest[-1]
    w = wtok_ref[...]
    g2 = mod_ref[5:6, :]
    a_hi, a_lo = _unpack_halves(yg_ref[0])
    b_hi, b_lo = _unpack_halves(yg_ref[1])
    moe = jnp.concatenate([w[:, 0:1] * a_hi + w[:, 1:2] * b_hi,
                           w[:, 0:1] * a_lo + w[:, 1:2] * b_lo], axis=1)
    o_ref[...] = x1_ref[...] + g2 * moe


def _combine(yg, x1, wtok, mod6, T, b0, n_total, out_prev):
    Nc, D = x1.shape
    TS = min(COMBINE_TILE, T)
    per_seq = T // TS
    blk0 = b0 * per_seq
    in_specs = [pl.BlockSpec((TOP_K, TS, D // 2), lambda i: (0, i, 0)),
                pl.BlockSpec((TS, D), lambda i: (i, 0)),
                pl.BlockSpec((TS, LANES), lambda i: (i, 0)),
                pl.BlockSpec((None, 6, D), lambda i: (b0 + i // per_seq, 0, 0))]
    args = [yg, x1, wtok, mod6]
    aliases = {}
    if out_prev is not None:
        in_specs.append(pl.BlockSpec(memory_space=pl.ANY))
        args.append(out_prev)
        aliases = {len(args) - 1: 0}
    return pl.pallas_call(
        _combine_kernel,
        grid=(Nc // TS,),
        in_specs=in_specs,
        out_specs=pl.BlockSpec((TS, D), lambda i: (blk0 + i, 0)),
        out_shape=jax.ShapeDtypeStruct((n_total, D), f32),
        input_output_aliases=aliases,
        compiler_params=pltpu.CompilerParams(dimension_semantics=("arbitrary",),
                                             vmem_limit_bytes=VMEM_LIMIT),
        name="combine",
    )(*args)


def _block_diag_mean(n, blk):
    m = np.kron(np.eye(n // blk, dtype=np.float32), np.full((blk, blk), 1.0 / blk, np.float32))
    return jnp.asarray(m, bf16)


def _layer(x, mod6, bias, l, w):
    B, T, D = x.shape
    N = B * T
    w_rg, w_re = w["w_router_group"][l], w["w_router_expert"][l]
    wr = jnp.zeros((D, ROUTER_COLS), f32)
    wr = wr.at[:, 0:N_GROUPS].set(w_rg).at[:, EXPERT_COL0:EXPERT_COL0 + N_EXPERTS].set(w_re)
    br = jnp.zeros((1, ROUTER_COLS), f32)
    br = br.at[0, 0:N_GROUPS].set(w["b_router_group"][l])
    br = br.at[0, EXPERT_COL0:EXPERT_COL0 + N_EXPERTS].set(w["b_router_expert"][l])
    wrh = wr.astype(bf16)
    wrl = (wr - wrh.astype(f32)).astype(bf16)
    p = dict(
        gmix=w["norm_mix_g"][l].reshape(1, D), gffn=w["norm_ffn_g"][l].reshape(1, D),
        w_in=w["w_in"][l].astype(bf16),
        dw=jnp.broadcast_to(w["dw_kernel"][l][:, None, :], (CONV_WIDTH, 8, D_CONV)), dwb=jnp.broadcast_to(w["dw_bias"][l][None, :], (8, D_CONV)),
        lng=w["conv_ln_g"][l].reshape(1, D_CONV), lnb=w["conv_ln_b"][l].reshape(1, D_CONV),
        wco=w["w_conv_out"][l].astype(bf16), wao=w["w_attn_out"][l].astype(bf16),
        wout=w["w_out"][l].astype(bf16),
        qg=(jnp.tile(w["q_norm_g"][l], N_Q_HEADS) * (HEAD_DIM ** -0.5 * LOG2E)).reshape(1, D_Q),
        kg=jnp.tile(w["k_norm_g"][l], N_KV_HEADS).reshape(1, D_KV),
        bq=_block_diag_mean(NORM_BLK, HEAD_DIM), bk=_block_diag_mean(D_KV, HEAD_DIM),
        bias=bias, sinks=w["sinks"][l], wrh=wrh, wr2=jnp.concatenate([wrh, wrl], axis=1), br=br,
    )
    w_gate, w_up, w_down = w["w_exp_gate"][l], w["w_exp_up"][l], w["w_exp_down"][l]

    n_chunks = MOE_CHUNKS if B % MOE_CHUNKS == 0 else 1
    Bc = B // n_chunks
    Nc = Bc * T
    n_blocks = -(-(Nc * TOP_K) // MOE_BLOCK) + N_EXPERTS
    blk0 = jnp.arange(n_blocks, dtype=i32) * MOE_BLOCK
    stage = []
    for ch in range(n_chunks):
        x1, h2, route, wtok, cnt = _mixer(x, mod6, p, ch * Bc, Bc)
        counts = cnt[:, 0].astype(i32)
        pcounts = (counts + MOE_BLOCK - 1) // MOE_BLOCK * MOE_BLOCK
        pend = jnp.cumsum(pcounts)
        pstart = pend - pcounts
        block_e = jnp.minimum(jnp.sum((pend[None, :] <= blk0[:, None]).astype(i32), axis=1), N_EXPERTS - 1)
        n_valid = jnp.clip((pstart + counts)[block_e] - blk0, 0, MOE_BLOCK).astype(i32)
        n_used = (pend[-1:] // MOE_BLOCK).astype(i32)
        dest = _dest_slots(pstart, route)
        xs = _sc_scatter(h2.reshape(Nc, D // 2), dest, n_blocks * MOE_BLOCK)
        stage.append((x1, wtok, dest, xs, block_e, n_valid, n_used, counts))
    ys = [_experts(be, nv, nu, cn, xs, w_gate, w_up, w_down) for (_, _, _, xs, be, nv, nu, cn) in stage]
    ygs = [_sc_gather(y, st[2], Nc) for y, st in zip(ys, stage)]
    out = None
    for ch in range(n_chunks):
        x1, wtok = stage[ch][0], stage[ch][1]
        out = _combine(ygs[ch], x1.reshape(Nc, D), wtok, mod6, T, ch * Bc, B * T, out)
    return out.reshape(B, T, D)


def kernel(x, c, w_ada, b_ada, norm_mix_g, w_in, dw_kernel, dw_bias, conv_ln_g, conv_ln_b,
           w_conv_out, q_norm_g, k_norm_g, sinks, w_attn_out, w_out, rel_bias_table, norm_ffn_g,
           w_router_group, b_router_group, w_router_expert, b_router_expert,
           w_exp_gate, w_exp_up, w_exp_down):
    w = dict(norm_mix_g=norm_mix_g, w_in=w_in, dw_kernel=dw_kernel, dw_bias=dw_bias,
             conv_ln_g=conv_ln_g, conv_ln_b=conv_ln_b, w_conv_out=w_conv_out, q_norm_g=q_norm_g,
             k_norm_g=k_norm_g, sinks=sinks, w_attn_out=w_attn_out, w_out=w_out,
             norm_ffn_g=norm_ffn_g, w_router_group=w_router_group, b_router_group=b_router_group,
             w_router_expert=w_router_expert, b_router_expert=b_router_expert,
             w_exp_gate=w_exp_gate, w_exp_up=w_exp_up, w_exp_down=w_exp_down)
    B = x.shape[0]
    bias = _bias_band(rel_bias_table).reshape(N_KV_HEADS, Q_PER_KV * WINDOW, 2 * WINDOW)
    for l in range(w_ada.shape[0]):
        mod6 = _modulation(c, w_ada[l], b_ada[l]).reshape(B, 6, D_MODEL)
        x = _layer(x, mod6, bias, l, w)
    return x
```

```python
import functools
import math

import jax
import jax.numpy as jnp
import numpy as np
from jax import lax
from jax.experimental import pallas as pl
from jax.experimental.pallas import tpu as pltpu
from jax.experimental.pallas import tpu_sc as plsc

D_MODEL = 1024
D_CONV = 512
CONV_WIDTH = 31
N_Q_HEADS = 8
N_KV_HEADS = 2
HEAD_DIM = 64
Q_PER_KV = N_Q_HEADS // N_KV_HEADS
WINDOW = 128
N_BUCKETS = 32
MAX_DISTANCE = 128
N_GROUPS = 4
EXPERTS_PER_GROUP = 8
N_EXPERTS = N_GROUPS * EXPERTS_PER_GROUP
TOP_K = 2
D_EXPERT = 256
D_Q = N_Q_HEADS * HEAD_DIM
D_KV = N_KV_HEADS * HEAD_DIM
EPS = 1e-6
NEG_INF = -1e30
LOG2E = math.log2(math.e)

LANES = 128
SEQ_TILE = 512
CONV_ROWS = 64
NORM_BLK = 256
PROJ_CHUNK = 256
CONV_SPAN = 24
MOE_BLOCK = 1024
MOE_CHUNKS = 2
DEST_TILE = 8192
COMBINE_TILE = 1024
SC_CORES = 2
SC_SUBCORES = 16
SC_WORKERS = SC_CORES * SC_SUBCORES
SC_CHUNK = 64
ROUTER_COLS = LANES
EXPERT_COL0 = 8
VMEM_LIMIT = 56 * 1024 * 1024

f32 = jnp.float32
bf16 = jnp.bfloat16
i32 = jnp.int32


def _dot(a, b):
    return jnp.dot(a, b, preferred_element_type=f32)


def _split(a):
    hi = a.astype(bf16)
    lo = (a - hi.astype(f32)).astype(bf16)
    return hi, lo


def _pack_halves(x):
    c = x.shape[1] // 2
    hi = lax.bitcast_convert_type(x[:, 0:c].astype(bf16).astype(f32), jnp.uint32)
    lo = lax.bitcast_convert_type(x[:, c:2 * c].astype(bf16).astype(f32), jnp.uint32)
    word = (hi & jnp.uint32(0xFFFF0000)) | (lo >> jnp.uint32(16))
    return lax.bitcast_convert_type(word, i32)


def _unpack_halves(word):
    u = lax.bitcast_convert_type(word, jnp.uint32)
    hi = lax.bitcast_convert_type(u & jnp.uint32(0xFFFF0000), f32)
    lo = lax.bitcast_convert_type(u << jnp.uint32(16), f32)
    return hi, lo


def _dot3(a, b):
    ah, al = _split(a)
    bh, bl = _split(b)
    return _dot(ah, bh) + _dot(al, bh) + _dot(ah, bl)


def _mod_kernel(c_ref, w_ref, b_ref, o_ref):
    c = c_ref[...]
    s = c * jax.nn.sigmoid(c)
    o_ref[...] = _dot3(s, w_ref[...]) + b_ref[...]


def _modulation(c, w_ada, b_ada):
    B, D = c.shape
    n_out = w_ada.shape[1]
    return pl.pallas_call(
        _mod_kernel,
        grid=(n_out // D,),
        in_specs=[pl.BlockSpec((B, D), lambda j: (0, 0)),
                  pl.BlockSpec((D, D), lambda j: (0, j)),
                  pl.BlockSpec((1, D), lambda j: (0, j))],
        out_specs=pl.BlockSpec((B, D), lambda j: (0, j)),
        out_shape=jax.ShapeDtypeStruct((B, n_out), f32),
        name="modulation",
    )(c, w_ada, b_ada.reshape(1, n_out))


def _band_buckets():
    qi = np.arange(WINDOW)[:, None]
    kj = np.arange(2 * WINDOW)[None, :]
    dist = np.clip(qi + WINDOW - kj, 0, MAX_DISTANCE)
    max_exact = N_BUCKETS // 2
    d = np.maximum(dist, 1).astype(np.float32)
    large = max_exact + (np.log(d / np.float32(max_exact)) / np.float32(math.log(MAX_DISTANCE / max_exact))
                         * np.float32(N_BUCKETS - max_exact)).astype(np.int32)
    large = np.minimum(large, N_BUCKETS - 1)
    bucket = np.where(dist < max_exact, dist, large).astype(np.int32)
    raw = qi + WINDOW - kj
    return np.where((raw >= 0) & (raw < WINDOW), bucket, -1).astype(np.int32)


def _bias_kernel(tab_ref, bucket_ref, o_ref):
    bk = bucket_ref[...]
    for h in range(N_Q_HEADS):
        acc = jnp.full(bk.shape, NEG_INF, f32)
        for b in range(N_BUCKETS):
            acc = jnp.where(bk == b, tab_ref[b, h] * LOG2E, acc)
        o_ref[h] = acc


def _bias_band(rel_bias_table):
    return pl.pallas_call(
        _bias_kernel,
        in_specs=[pl.BlockSpec(memory_space=pltpu.SMEM),
                  pl.BlockSpec(memory_space=pltpu.VMEM)],
        out_specs=pl.BlockSpec(memory_space=pltpu.VMEM),
        out_shape=jax.ShapeDtypeStruct((N_Q_HEADS, WINDOW, 2 * WINDOW), f32),
        name="bias_band",
    )(rel_bias_table, jnp.asarray(_band_buckets()))


def _mixer_kernel(x_ref, mod_ref, gmix_ref, gffn_ref, win_ref, dw_ref, dwb_ref, lng_ref, lnb_ref,
                  wco_ref, wao_ref, wout_ref, qg_ref, kg_ref, bq_ref, bk_ref, bias_ref, sink_ref,
                  wrh_ref, wr2_ref, br_ref, tri_ref,
                  x1_ref, h2_ref, route_ref, wtok_ref, cnt_ref,
                  uext, ush, conv_sc, proj_sc, k2, v2, osc):
    TM = x_ref.shape[0]
    HALO = 32
    b = pl.program_id(0)
    t = pl.program_id(1)
    first = t == 0

    @pl.when(first)
    def _():
        uext[0:HALO, :] = jnp.zeros((HALO, D_CONV), f32)
        k2[:, 0:WINDOW, :] = jnp.zeros((N_KV_HEADS, WINDOW, LANES), bf16)
        v2[:, 0:WINDOW, :] = jnp.zeros((N_KV_HEADS, WINDOW, LANES), bf16)

    @pl.when(first & (b == 0))
    def _():
        cnt_ref[...] = jnp.zeros(cnt_ref.shape, f32)

    x = x_ref[...]
    mod = mod_ref[...]
    sh1, sc1, g1 = mod[0:1, :], mod[1:2, :], mod[2:3, :]
    sh2, sc2, g2 = mod[3:4, :], mod[4:5, :], mod[5:6, :]
    del g2

    ms = jnp.mean(x * x, axis=-1, keepdims=True)
    h = (x * lax.rsqrt(ms + EPS)) * (gmix_ref[...] * (1.0 + sc1)) + sh1
    hb = h.astype(bf16)

    ab = _dot(hb, win_ref[:, 0:2 * D_CONV])
    u = ab[:, 0:D_CONV] * jax.nn.sigmoid(ab[:, D_CONV:2 * D_CONV])
    uext[HALO:HALO + TM, :] = u
    for r in range(1, 8):
        ush[r - 1] = uext[r:r + TM + CONV_SPAN, :]
    n_q = D_Q // PROJ_CHUNK
    n_g = D_MODEL // PROJ_CHUNK
    n_jobs = (win_ref.shape[1] - 2 * D_CONV) // PROJ_CHUNK
    n_rb = TM // CONV_ROWS
    n_t8 = CONV_ROWS // 8
    units = [(cb, rg) for cb in range(D_CONV // LANES) for rg in range(n_rb)]
    job_at = {(i * len(units)) // n_jobs: i for i in range(n_jobs)}
    assert len(job_at) == n_jobs
    for ui, (cb, rg) in enumerate(units):
        if ui in job_at:
            c_lo = 2 * D_CONV + job_at[ui] * PROJ_CHUNK
            proj_sc[job_at[ui]] = _dot(hb, win_ref[:, c_lo:c_lo + PROJ_CHUNK])
        cs = slice(cb * LANES, (cb + 1) * LANES)
        base = rg * CONV_ROWS
        accs = [None] * n_t8
        for r in range(8):
            taps = [(j, (j + 2) // 8) for j in range(CONV_WIDTH) if (j + 2) % 8 == r]
            t_lo = min(a for _, a in taps)
            t_hi = max(a for _, a in taps) + n_t8
            tiles = {}
            for t8 in range(t_lo, t_hi):
                rows = slice(base + 8 * t8, base + 8 * t8 + 8)
                tiles[t8] = uext[rows, cs] if r == 0 else ush[r - 1, rows, cs]
            for j, a in taps:
                tap = dw_ref[j, :, cs]
                for s8 in range(n_t8):
                    term = tap * tiles[a + s8]
                    accs[s8] = term if accs[s8] is None else accs[s8] + term
        for s8 in range(n_t8):
            conv_sc[base + 8 * s8:base + 8 * s8 + 8, cs] = accs[s8] + dwb_ref[:, cs]
    conv = conv_sc[...]
    uext[0:HALO, :] = uext[TM:TM + HALO, :]
    mu = jnp.mean(conv, axis=-1, keepdims=True)
    dc = conv - mu
    var = jnp.mean(dc * dc, axis=-1, keepdims=True)
    yn = dc * lax.rsqrt(var + EPS) * lng_ref[...] + lnb_ref[...]
    act = yn * jax.nn.sigmoid(yn)
    y_conv = _dot(act.astype(bf16), wco_ref[...])

    q = jnp.concatenate([proj_sc[i] for i in range(n_q)], axis=1)
    k = proj_sc[n_q, :, 0:D_KV]
    v = proj_sc[n_q, :, D_KV:2 * D_KV]

    def head_norm(z, blk_ref, g):
        sq = (z * z).astype(bf16)
        blk = blk_ref.shape[0]
        msq = jnp.concatenate([_dot(sq[:, c:c + blk], blk_ref[...]) for c in range(0, z.shape[1], blk)], axis=1)
        return z * lax.rsqrt(msq + EPS) * g

    qn = head_norm(q, bq_ref, qg_ref[...]).astype(bf16)
    kn = head_norm(k, bk_ref, kg_ref[...])
    lo_half = lax.broadcasted_iota(i32, (TM, LANES), 1) < HEAD_DIM
    kr = pltpu.roll(kn, HEAD_DIM, 1)
    vr = pltpu.roll(v, HEAD_DIM, 1)
    k2[0, WINDOW:WINDOW + TM, :] = jnp.where(lo_half, kn, kr).astype(bf16)
    k2[1, WINDOW:WINDOW + TM, :] = jnp.where(lo_half, kr, kn).astype(bf16)
    v2[0, WINDOW:WINDOW + TM, :] = jnp.where(lo_half, v, vr).astype(bf16)
    v2[1, WINDOW:WINDOW + TM, :] = jnp.where(lo_half, vr, v).astype(bf16)

    QROWS = Q_PER_KV * WINDOW
    col = lax.broadcasted_iota(i32, (QROWS, 2 * WINDOW), 1)
    has_prev = col >= jnp.where(first, WINDOW, 0)
    hrow = lax.broadcasted_iota(i32, (QROWS, 1), 0) // WINDOW
    lo128 = lax.broadcasted_iota(i32, (WINDOW, LANES), 1) < HEAD_DIM
    zero_q = jnp.zeros((WINDOW, LANES), bf16)
    for g in range(N_KV_HEADS):
        sink = LOG2E * jnp.where(hrow == 0, sink_ref[4 * g],
                                 jnp.where(hrow == 1, sink_ref[4 * g + 1],
                                           jnp.where(hrow == 2, sink_ref[4 * g + 2], sink_ref[4 * g + 3])))
        bias_g = bias_ref[g]
        for j in range(TM // WINDOW):
            rs = slice(j * WINDOW, (j + 1) * WINDOW)
            qa = qn[rs, 2 * LANES * g:2 * LANES * g + LANES]
            qb = qn[rs, 2 * LANES * g + LANES:2 * LANES * (g + 1)]
            qs = jnp.concatenate([jnp.where(lo128, qa, zero_q), jnp.where(lo128, zero_q, qa),
                                  jnp.where(lo128, qb, zero_q), jnp.where(lo128, zero_q, qb)], axis=0)
            kk = k2[g, j * WINDOW:(j + 2) * WINDOW, :]
            s = lax.dot_general(qs, kk, (((1,), (1,)), ((), ())), preferred_element_type=f32)
            logits = jnp.where(has_prev, s + bias_g, NEG_INF) if j == 0 else s + bias_g
            m = jnp.maximum(jnp.max(logits, axis=-1, keepdims=True), sink)
            p = jnp.exp2(logits - m)
            den = jnp.sum(p, axis=-1, keepdims=True) + jnp.exp2(sink - m)
            o2 = _dot(p.astype(bf16), v2[g, j * WINDOW:(j + 2) * WINDOW, :]) * (1.0 / den)
            osc[rs, 2 * LANES * g:2 * LANES * g + LANES] = jnp.where(
                lo128, o2[0:WINDOW], o2[WINDOW:2 * WINDOW]).astype(bf16)
            osc[rs, 2 * LANES * g + LANES:2 * LANES * (g + 1)] = jnp.where(
                lo128, o2[2 * WINDOW:3 * WINDOW], o2[3 * WINDOW:4 * WINDOW]).astype(bf16)
    k2[:, 0:WINDOW, :] = k2[:, TM:TM + WINDOW, :]
    v2[:, 0:WINDOW, :] = v2[:, TM:TM + WINDOW, :]
    y_attn = _dot(osc[...], wao_ref[...])

    merged = []
    for i in range(n_g):
        cs = slice(i * PROJ_CHUNK, (i + 1) * PROJ_CHUNK)
        g_conv = jax.nn.sigmoid(proj_sc[n_q + 1 + i])
        g_attn = jax.nn.sigmoid(proj_sc[n_q + 1 + n_g + i])
        merged.append((g_conv * y_conv[:, cs] + g_attn * y_attn[:, cs]).astype(bf16))
    merged = jnp.concatenate(merged, axis=1)
    x1 = x_ref[...] + g1 * _dot(merged, wout_ref[...])
    x1_ref[...] = x1

    ms2 = jnp.mean(x1 * x1, axis=-1, keepdims=True)
    h2 = (x1 * lax.rsqrt(ms2 + EPS)) * (gffn_ref[...] * (1.0 + sc2)) + sh2
    h2_ref[...] = _pack_halves(h2)
    hh, hl = _split(h2)
    hw = _dot(hh, wr2_ref[...])
    lg = hw[:, 0:ROUTER_COLS] + hw[:, ROUTER_COLS:2 * ROUTER_COLS] + _dot(hl, wrh_ref[...]) + br_ref[...]
    lt = lg.T
    gl = lt[0:N_GROUPS, :]
    grow = lax.broadcasted_iota(i32, (N_GROUPS, TM), 0)
    gmax = jnp.max(gl, axis=0, keepdims=True)
    gi = jnp.min(jnp.where(gl == gmax, grow, N_GROUPS), axis=0, keepdims=True)
    p_top = 1.0 / jnp.sum(jnp.exp(gl - gmax), axis=0, keepdims=True)
    sel = lt[EXPERT_COL0:EXPERT_COL0 + EXPERTS_PER_GROUP, :]
    for gg in range(1, N_GROUPS):
        lo_r = EXPERT_COL0 + gg * EXPERTS_PER_GROUP
        sel = jnp.where(gi == gg, lt[lo_r:lo_r + EXPERTS_PER_GROUP, :], sel)
    erow = lax.broadcasted_iota(i32, (EXPERTS_PER_GROUP, TM), 0)
    m1 = jnp.max(sel, axis=0, keepdims=True)
    i1 = jnp.min(jnp.where(sel == m1, erow, EXPERTS_PER_GROUP), axis=0, keepdims=True)
    rest = jnp.where(erow == i1, -jnp.inf, sel)
    m2 = jnp.max(rest, axis=0, keepdims=True)
    i2 = jnp.min(jnp.where(rest == m2, erow, EXPERTS_PER_GROUP), axis=0, keepdims=True)
    z = jnp.sum(jnp.exp(sel - m1), axis=0, keepdims=True)
    v1 = 1.0 / z
    v2nd = jnp.exp(m2 - m1) / z
    w1 = v1 / (v1 + v2nd) * p_top
    w2 = v2nd / (v1 + v2nd) * p_top
    e1 = gi * EXPERTS_PER_GROUP + i1
    e2 = gi * EXPERTS_PER_GROUP + i2

    xrow = lax.broadcasted_iota(i32, (N_EXPERTS, TM), 0)
    oh1 = xrow == e1
    oh2 = xrow == e2
    both = jnp.where(oh1 | oh2, 1.0, 0.0)
    prefix = _dot(both.astype(bf16), tri_ref[...]) + cnt_ref[:, 0:1]
    r1 = jnp.sum(jnp.where(oh1, prefix, 0.0), axis=0, keepdims=True)
    r2 = jnp.sum(jnp.where(oh2, prefix, 0.0), axis=0, keepdims=True)
    cnt_ref[...] = cnt_ref[...] + jnp.sum(both, axis=1, keepdims=True)

    route_ref[...] = jnp.concatenate(
        [e1, e2, r1.astype(i32), r2.astype(i32), jnp.zeros((4, TM), i32)], axis=0)
    wpad = jnp.concatenate([w1, w2, jnp.zeros((LANES - 2, TM), f32)], axis=0)
    wtok_ref[...] = wpad.T


def _mixer(x, mod6, p, b0, B):
    _, T, D = x.shape
    TM = min(SEQ_TILE, T)
    nt = T // TM
    N = B * T
    const = lambda shape: pl.BlockSpec(shape, lambda b, t: (0,) * len(shape))
    in_specs = [
        pl.BlockSpec((None, TM, D), lambda b, t: (b0 + b, t, 0)),
        pl.BlockSpec((None, 6, D), lambda b, t: (b0 + b, 0, 0)),
        const((1, D)), const((1, D)),
        const(p["w_in"].shape),
        const((CONV_WIDTH, 8, D_CONV)), const((8, D_CONV)), const((1, D_CONV)), const((1, D_CONV)),
        const((D_CONV, D)), const((D_Q, D)), const((D, D)),
        const((1, D_Q)), const((1, D_KV)),
        const((NORM_BLK, NORM_BLK)), const((D_KV, D_KV)),
        const((N_KV_HEADS, Q_PER_KV * WINDOW, 2 * WINDOW)),
        pl.BlockSpec(memory_space=pltpu.SMEM),
        const((D, ROUTER_COLS)), const((D, 2 * ROUTER_COLS)), const((1, ROUTER_COLS)),
        const((TM, TM)),
    ]
    out_specs = [
        pl.BlockSpec((None, TM, D), lambda b, t: (b, t, 0)),
        pl.BlockSpec((None, TM, D // 2), lambda b, t: (b, t, 0)),
        pl.BlockSpec((8, TM), lambda b, t: (0, b * nt + t)),
        pl.BlockSpec((TM, LANES), lambda b, t: (b * nt + t, 0)),
        pl.BlockSpec((N_EXPERTS, LANES), lambda b, t: (0, 0)),
    ]
    out_shape = [
        jax.ShapeDtypeStruct((B, T, D), f32),
        jax.ShapeDtypeStruct((B, T, D // 2), i32),
        jax.ShapeDtypeStruct((8, N), i32),
        jax.ShapeDtypeStruct((N, LANES), f32),
        jax.ShapeDtypeStruct((N_EXPERTS, LANES), f32),
    ]
    scratch = [
        pltpu.VMEM((TM + 32, D_CONV), f32),
        pltpu.VMEM((7, TM + CONV_SPAN, D_CONV), f32),
        pltpu.VMEM((TM, D_CONV), f32),
        pltpu.VMEM(((p["w_in"].shape[1] - 2 * D_CONV) // PROJ_CHUNK, TM, PROJ_CHUNK), f32),
        pltpu.VMEM((N_KV_HEADS, TM + WINDOW, LANES), bf16),
        pltpu.VMEM((N_KV_HEADS, TM + WINDOW, LANES), bf16),
        pltpu.VMEM((TM, D_Q), bf16),
    ]
    tri = jnp.asarray(np.triu(np.ones((TM, TM), np.float32), 1), bf16)
    return pl.pallas_call(
        _mixer_kernel,
        grid=(B, nt),
        in_specs=in_specs, out_specs=out_specs, out_shape=out_shape, scratch_shapes=scratch,
        compiler_params=pltpu.CompilerParams(
            dimension_semantics=("arbitrary", "arbitrary"), vmem_limit_bytes=VMEM_LIMIT),
        name="mixer_router",
    )(x, mod6, p["gmix"], p["gffn"], p["w_in"], p["dw"], p["dwb"], p["lng"], p["lnb"],
      p["wco"], p["wao"], p["wout"], p["qg"], p["kg"], p["bq"], p["bk"], p["bias"], p["sinks"],
      p["wrh"], p["wr2"], p["br"], tri)


def _dest_kernel(pstart_ref, route_ref, dest_ref):
    e = route_ref[0:2, :]
    base = jnp.zeros(e.shape, i32)
    for x in range(N_EXPERTS):
        base = jnp.where(e == x, pstart_ref[x], base)
    dest_ref[...] = base + route_ref[2:4, :]


def _dest_slots(pstart, route):
    N = route.shape[1]
    TS = min(DEST_TILE, N)
    return pl.pallas_call(
        _dest_kernel,
        grid=(N // TS,),
        in_specs=[pl.BlockSpec(memory_space=pltpu.SMEM),
                  pl.BlockSpec((8, TS), lambda i: (0, i))],
        out_specs=pl.BlockSpec((2, TS), lambda i: (0, i)),
        out_shape=jax.ShapeDtypeStruct((2, N), i32),
        name="dest_slots",
    )(pstart, route)


def _sc_mesh():
    return plsc.VectorSubcoreMesh(core_axis_name="c", subcore_axis_name="s")


def _sc_scatter(h2, dest, n_slots):
    N, D = h2.shape
    per_w = N // SC_WORKERS
    C = SC_CHUNK
    n_chunks = per_w // C
    dest4 = dest.reshape(TOP_K, SC_WORKERS, n_chunks, C)

    @functools.partial(
        pl.kernel, mesh=_sc_mesh(),
        out_type=jax.ShapeDtypeStruct((n_slots, D), h2.dtype),
        scratch_types=[pltpu.VMEM((n_chunks, C), i32), pltpu.VMEM((n_chunks, C), i32),
                       pltpu.VMEM((2, C, D), h2.dtype),
                       pltpu.SemaphoreType.DMA((2,)), pltpu.SemaphoreType.DMA((2,))],
        name="sc_dispatch_scatter",
    )
    def run(h2_hbm, dest_hbm, xs_hbm, idx0, idx1, buf, sem_in, sem_out):
        wid = lax.axis_index("s") * SC_CORES + lax.axis_index("c")
        base = wid * per_w
        pltpu.sync_copy(dest_hbm.at[0, wid], idx0)
        pltpu.sync_copy(dest_hbm.at[1, wid], idx1)

        def load(j, b):
            return pltpu.make_async_copy(h2_hbm.at[pl.ds(base + j * C, C)], buf.at[b], sem_in.at[b])

        def put(j, b, idx):
            return pltpu.make_async_copy(buf.at[b], xs_hbm.at[idx.at[j]], sem_out.at[b])

        load(0, 0).start()

        @pl.loop(0, n_chunks, step=2)
        def _(j0):
            for b in range(2):
                j = j0 + b
                load(j, b).wait()

                @pl.when(j >= 1)
                def _():
                    put(j - 1, 1 - b, idx0).wait()
                    put(j - 1, 1 - b, idx1).wait()

                @pl.when(j + 1 < n_chunks)
                def _():
                    load(j + 1, 1 - b).start()

                put(j, b, idx0).start()
                put(j, b, idx1).start()

        put(n_chunks - 1, 1, idx0).wait()
        put(n_chunks - 1, 1, idx1).wait()

    return run(h2, dest4)


def _sc_gather(y, dest, N):
    D = y.shape[1]
    per_w = N // SC_WORKERS
    C = SC_CHUNK
    n_chunks = per_w // C
    dest4 = dest.reshape(TOP_K, SC_WORKERS, n_chunks, C)

    @functools.partial(
        pl.kernel, mesh=_sc_mesh(),
        out_type=jax.ShapeDtypeStruct((TOP_K, N, D), y.dtype),
        scratch_types=[pltpu.VMEM((n_chunks, C), i32), pltpu.VMEM((n_chunks, C), i32),
                       pltpu.VMEM((2, C, D), y.dtype),
                       pltpu.SemaphoreType.DMA((2,)), pltpu.SemaphoreType.DMA((2,))],
        name="sc_combine_gather",
    )
    def run(y_hbm, dest_hbm, yg_hbm, idx0, idx1, buf, sem_in, sem_out):
        wid = lax.axis_index("s") * SC_CORES + lax.axis_index("c")
        base = wid * per_w
        pltpu.sync_copy(dest_hbm.at[0, wid], idx0)
        pltpu.sync_copy(dest_hbm.at[1, wid], idx1)
        idx = (idx0, idx1)

        def get(j, k):
            return pltpu.make_async_copy(y_hbm.at[idx[k].at[j]], buf.at[k], sem_in.at[k])

        def put(j, k):
            return pltpu.make_async_copy(buf.at[k], yg_hbm.at[k, pl.ds(base + j * C, C)], sem_out.at[k])

        get(0, 0).start()
        get(0, 1).start()

        @pl.loop(0, n_chunks)
        def _(j):
            for k in range(TOP_K):
                get(j, k).wait()
                put(j, k).start()
            for k in range(TOP_K):
                put(j, k).wait()

                @pl.when(j + 1 < n_chunks)
                def _():
                    get(j + 1, k).start()

    return run(y, dest4)


def _expert_kernel(be_ref, nv_ref, nu_ref, first_ref, slot_ref, nxt_ref,
                   xs_ref, wg_hbm, wu_hbm, wdn_hbm, y_ref,
                   stg_g, stg_u, stg_d, wgu_ref, wd_ref, sems):
    i = pl.program_id(0)
    in_use = i < nu_ref[0]

    def fetch(e, s):
        return (pltpu.make_async_copy(wg_hbm.at[e], stg_g.at[s], sems.at[s, 0]),
                pltpu.make_async_copy(wu_hbm.at[e], stg_u.at[s], sems.at[s, 1]),
                pltpu.make_async_copy(wdn_hbm.at[e], stg_d.at[s], sems.at[s, 2]))

    @pl.when(in_use & (first_ref[i] == 1))
    def _():
        s = slot_ref[i]

        @pl.when(i == 0)
        def _():
            for cp in fetch(be_ref[0], 0):
                cp.start()

        for cp in fetch(be_ref[i], s):
            cp.wait()
        wgu_ref[:, 0:D_EXPERT] = stg_g[s].astype(bf16)
        wgu_ref[:, D_EXPERT:2 * D_EXPERT] = stg_u[s].astype(bf16)
        wd_ref[...] = stg_d[s].astype(bf16)

        @pl.when(nxt_ref[i] >= 0)
        def _():
            for cp in fetch(nxt_ref[i], 1 - s):
                cp.start()

    @pl.when(in_use)
    def _():
        live = lax.broadcasted_iota(i32, (MOE_BLOCK, 1), 0) < nv_ref[i]
        x_hi, x_lo = _unpack_halves(jnp.where(live, xs_ref[...], 0))
        half = x_hi.shape[1]
        gu = (_dot(x_hi.astype(bf16), wgu_ref[0:half, :]) +
              _dot(x_lo.astype(bf16), wgu_ref[half:2 * half, :]))
        gate = gu[:, 0:D_EXPERT]
        hid = (gate * jax.nn.sigmoid(gate)) * gu[:, D_EXPERT:2 * D_EXPERT]
        y_ref[...] = _pack_halves(_dot(hid.astype(bf16), wd_ref[...]))


def _experts(block_e, n_valid, n_used, counts, xs, w_gate, w_up, w_down):
    n_slots, DH = xs.shape
    D = 2 * DH
    n_blocks = n_slots // MOE_BLOCK
    idx = jnp.arange(n_blocks, dtype=i32)
    prev_e = jnp.concatenate([jnp.full((1,), -1, i32), block_e[:-1]])
    first = ((block_e != prev_e) & (idx < n_used[0])).astype(i32)
    slot = ((jnp.cumsum(first) - 1) % 2).astype(i32)
    e_ids = jnp.arange(N_EXPERTS, dtype=i32)
    later = (e_ids[None, :] > e_ids[:, None]) & (counts[None, :] > 0)
    nxt_of_e = jnp.min(jnp.where(later, e_ids[None, :], N_EXPERTS), axis=1)
    nxt = jnp.where(nxt_of_e[block_e] < N_EXPERTS, nxt_of_e[block_e], -1).astype(i32)
    last = lambda i, nu: jnp.minimum(i, nu[0] - 1)
    slot_map = lambda i, be, nv, nu, fi, sl, nx: (last(i, nu), 0)
    grid_spec = pltpu.PrefetchScalarGridSpec(
        num_scalar_prefetch=6,
        grid=(n_blocks,),
        in_specs=[pl.BlockSpec((MOE_BLOCK, DH), slot_map),
                  pl.BlockSpec(memory_space=pl.ANY),
                  pl.BlockSpec(memory_space=pl.ANY),
                  pl.BlockSpec(memory_space=pl.ANY)],
        out_specs=pl.BlockSpec((MOE_BLOCK, DH), slot_map),
        scratch_shapes=[pltpu.VMEM((2, D, D_EXPERT), f32), pltpu.VMEM((2, D, D_EXPERT), f32),
                        pltpu.VMEM((2, D_EXPERT, D), f32),
                        pltpu.VMEM((D, 2 * D_EXPERT), bf16), pltpu.VMEM((D_EXPERT, D), bf16),
                        pltpu.SemaphoreType.DMA((2, 3))],
    )
    return pl.pallas_call(
        _expert_kernel,
        grid_spec=grid_spec,
        out_shape=jax.ShapeDtypeStruct((n_slots, DH), i32),
        compiler_params=pltpu.CompilerParams(dimension_semantics=("arbitrary",),
                                             vmem_limit_bytes=VMEM_LIMIT),
        name="experts",
    )(block_e, n_valid, n_used, first, slot, nxt, xs, w_gate, w_up, w_down)


def _combine_kernel(yg_ref, x1_ref, wtok_ref, mod_ref, *rest):
    o_ref = rest[-1]
    w = wtok_ref[...]
    g2 = mod_ref[5:6, :]
    a_hi, a_lo = _unpack_halves(yg_ref[0])
    b_hi, b_lo = _unpack_halves(yg_ref[1])
    moe = jnp.concatenate([w[:, 0:1] * a_hi + w[:, 1:2] * b_hi,
                           w[:, 0:1] * a_lo + w[:, 1:2] * b_lo], axis=1)
    o_ref[...] = x1_ref[...] + g2 * moe


def _combine(yg, x1, wtok, mod6, T, b0, n_total, out_prev):
    Nc, D = x1.shape
    TS = min(COMBINE_TILE, T)
    per_seq = T // TS
    blk0 = b0 * per_seq
    in_specs = [pl.BlockSpec((TOP_K, TS, D // 2), lambda i: (0, i, 0)),
                pl.BlockSpec((TS, D), lambda i: (i, 0)),
                pl.BlockSpec((TS, LANES), lambda i: (i, 0)),
                pl.BlockSpec((None, 6, D), lambda i: (b0 + i // per_seq, 0, 0))]
    args = [yg, x1, wtok, mod6]
    aliases = {}
    if out_prev is not None:
        in_specs.append(pl.BlockSpec(memory_space=pl.ANY))
        args.append(out_prev)
        aliases = {len(args) - 1: 0}
    return pl.pallas_call(
        _combine_kernel,
        grid=(Nc // TS,),
        in_specs=in_specs,
        out_specs=pl.BlockSpec((TS, D), lambda i: (blk0 + i, 0)),
        out_shape=jax.ShapeDtypeStruct((n_total, D), f32),
        input_output_aliases=aliases,
        compiler_params=pltpu.CompilerParams(dimension_semantics=("arbitrary",),
                                             vmem_limit_bytes=VMEM_LIMIT),
        name="combine",
    )(*args)


def _block_diag_mean(n, blk):
    m = np.kron(np.eye(n // blk, dtype=np.float32), np.full((blk, blk), 1.0 / blk, np.float32))
    return jnp.asarray(m, bf16)


def _layer(x, mod6, bias, l, w):
    B, T, D = x.shape
    N = B * T
    w_rg, w_re = w["w_router_group"][l], w["w_router_expert"][l]
    wr = jnp.zeros((D, ROUTER_COLS), f32)
    wr = wr.at[:, 0:N_GROUPS].set(w_rg).at[:, EXPERT_COL0:EXPERT_COL0 + N_EXPERTS].set(w_re)
    br = jnp.zeros((1, ROUTER_COLS), f32)
    br = br.at[0, 0:N_GROUPS].set(w["b_router_group"][l])
    br = br.at[0, EXPERT_COL0:EXPERT_COL0 + N_EXPERTS].set(w["b_router_expert"][l])
    wrh = wr.astype(bf16)
    wrl = (wr - wrh.astype(f32)).astype(bf16)
    p = dict(
        gmix=w["norm_mix_g"][l].reshape(1, D), gffn=w["norm_ffn_g"][l].reshape(1, D),
        w_in=w["w_in"][l].astype(bf16),
        dw=jnp.broadcast_to(w["dw_kernel"][l][:, None, :], (CONV_WIDTH, 8, D_CONV)), dwb=jnp.broadcast_to(w["dw_bias"][l][None, :], (8, D_CONV)),
        lng=w["conv_ln_g"][l].reshape(1, D_CONV), lnb=w["conv_ln_b"][l].reshape(1, D_CONV),
        wco=w["w_conv_out"][l].astype(bf16), wao=w["w_attn_out"][l].astype(bf16),
        wout=w["w_out"][l].astype(bf16),
        qg=(jnp.tile(w["q_norm_g"][l], N_Q_HEADS) * (HEAD_DIM ** -0.5 * LOG2E)).reshape(1, D_Q),
        kg=jnp.tile(w["k_norm_g"][l], N_KV_HEADS).reshape(1, D_KV),
        bq=_block_diag_mean(NORM_BLK, HEAD_DIM), bk=_block_diag_mean(D_KV, HEAD_DIM),
        bias=bias, sinks=w["sinks"][l], wrh=wrh, wr2=jnp.concatenate([wrh, wrl], axis=1), br=br,
    )
    w_gate, w_up, w_down = w["w_exp_gate"][l], w["w_exp_up"][l], w["w_exp_down"][l]

    n_chunks = MOE_CHUNKS if B % MOE_CHUNKS == 0 else 1
    Bc = B // n_chunks
    Nc = Bc * T
    n_blocks = -(-(Nc * TOP_K) // MOE_BLOCK) + N_EXPERTS
    blk0 = jnp.arange(n_blocks, dtype=i32) * MOE_BLOCK
    stage = []
    for ch in range(n_chunks):
        x1, h2, route, wtok, cnt = _mixer(x, mod6, p, ch * Bc, Bc)
        counts = cnt[:, 0].astype(i32)
        pcounts = (counts + MOE_BLOCK - 1) // MOE_BLOCK * MOE_BLOCK
        pend = jnp.cumsum(pcounts)
        pstart = pend - pcounts
        block_e = jnp.minimum(jnp.sum((pend[None, :] <= blk0[:, None]).astype(i32), axis=1), N_EXPERTS - 1)
        n_valid = jnp.clip((pstart + counts)[block_e] - blk0, 0, MOE_BLOCK).astype(i32)
        n_used = (pend[-1:] // MOE_BLOCK).astype(i32)
        dest = _dest_slots(pstart, route)
        xs = _sc_scatter(h2.reshape(Nc, D // 2), dest, n_blocks * MOE_BLOCK)
        stage.append((x1, wtok, dest, xs, block_e, n_valid, n_used, counts))
    ys = [_experts(be, nv, nu, cn, xs, w_gate, w_up, w_down) for (_, _, _, xs, be, nv, nu, cn) in stage]
    ygs = [_sc_gather(y, st[2], Nc) for y, st in zip(ys, stage)]
    out = None
    for ch in range(n_chunks):
        x1, wtok = stage[ch][0], stage[ch][1]
        out = _combine(ygs[ch], x1.reshape(Nc, D), wtok, mod6, T, ch * Bc, B * T, out)
    return out.reshape(B, T, D)


def kernel(x, c, w_ada, b_ada, norm_mix_g, w_in, dw_kernel, dw_bias, conv_ln_g, conv_ln_b,
           w_conv_out, q_norm_g, k_norm_g, sinks, w_attn_out, w_out, rel_bias_table, norm_ffn_g,
           w_router_group, b_router_group, w_router_expert, b_router_expert,
           w_exp_gate, w_exp_up, w_exp_down):
    w = dict(norm_mix_g=norm_mix_g, w_in=w_in, dw_kernel=dw_kernel, dw_bias=dw_bias,
             conv_ln_g=conv_ln_g, conv_ln_b=conv_ln_b, w_conv_out=w_conv_out, q_norm_g=q_norm_g,
             k_norm_g=k_norm_g, sinks=sinks, w_attn_out=w_attn_out, w_out=w_out,
             norm_ffn_g=norm_ffn_g, w_router_group=w_router_group, b_router_group=b_router_group,
             w_router_expert=w_router_expert, b_router_expert=b_router_expert,
             w_exp_gate=w_exp_gate, w_exp_up=w_exp_up, w_exp_down=w_exp_down)
    B = x.shape[0]
    bias = _bias_band(rel_bias_table).reshape(N_KV_HEADS, Q_PER_KV * WINDOW, 2 * WINDOW)
    for l in range(w_ada.shape[0]):
        mod6 = _modulation(c, w_ada[l], b_ada[l]).reshape(B, 6, D_MODEL)
        x = _layer(x, mod6, bias, l, w)
    return x
```

```python
import functools
import math

import jax
import jax.numpy as jnp
import numpy as np
from jax import lax
from jax.experimental import pallas as pl
from jax.experimental.pallas import tpu as pltpu
from jax.experimental.pallas import tpu_sc as plsc

D_MODEL = 1024
D_CONV = 512
CONV_WIDTH = 31
N_Q_HEADS = 8
N_KV_HEADS = 2
HEAD_DIM = 64
Q_PER_KV = N_Q_HEADS // N_KV_HEADS
WINDOW = 128
N_BUCKETS = 32
MAX_DISTANCE = 128
N_GROUPS = 4
EXPERTS_PER_GROUP = 8
N_EXPERTS = N_GROUPS * EXPERTS_PER_GROUP
TOP_K = 2
D_EXPERT = 256
D_Q = N_Q_HEADS * HEAD_DIM
D_KV = N_KV_HEADS * HEAD_DIM
EPS = 1e-6
NEG_INF = -1e30
LOG2E = math.log2(math.e)

LANES = 128
SEQ_TILE = 512
CONV_ROWS = 64
NORM_BLK = 256
PROJ_CHUNK = 256
CONV_SPAN = 24
MOE_BLOCK = 1024
MOE_CHUNKS = 2
DEST_TILE = 8192
COMBINE_TILE = 1024
SC_CORES = 2
SC_SUBCORES = 16
SC_WORKERS = SC_CORES * SC_SUBCORES
SC_CHUNK = 64
ROUTER_COLS = LANES
EXPERT_COL0 = 8
VMEM_LIMIT = 56 * 1024 * 1024

f32 = jnp.float32
bf16 = jnp.bfloat16
i32 = jnp.int32


def _dot(a, b):
    return jnp.dot(a, b, preferred_element_type=f32)


def _split(a):
    hi = a.astype(bf16)
    lo = (a - hi.astype(f32)).astype(bf16)
    return hi, lo


def _pack_halves(x):
    c = x.shape[1] // 2
    hi = lax.bitcast_convert_type(x[:, 0:c].astype(bf16).astype(f32), jnp.uint32)
    lo = lax.bitcast_convert_type(x[:, c:2 * c].astype(bf16).astype(f32), jnp.uint32)
    word = (hi & jnp.uint32(0xFFFF0000)) | (lo >> jnp.uint32(16))
    return lax.bitcast_convert_type(word, i32)


def _unpack_halves(word):
    u = lax.bitcast_convert_type(word, jnp.uint32)
    hi = lax.bitcast_convert_type(u & jnp.uint32(0xFFFF0000), f32)
    lo = lax.bitcast_convert_type(u << jnp.uint32(16), f32)
    return hi, lo


def _dot3(a, b):
    ah, al = _split(a)
    bh, bl = _split(b)
    return _dot(ah, bh) + _dot(al, bh) + _dot(ah, bl)


def _mod_kernel(c_ref, w_ref, b_ref, o_ref):
    c = c_ref[...]
    s = c * jax.nn.sigmoid(c)
    o_ref[...] = _dot3(s, w_ref[...]) + b_ref[...]


def _modulation(c, w_ada, b_ada):
    B, D = c.shape
    n_out = w_ada.shape[1]
    return pl.pallas_call(
        _mod_kernel,
        grid=(n_out // D,),
        in_specs=[pl.BlockSpec((B, D), lambda j: (0, 0)),
                  pl.BlockSpec((D, D), lambda j: (0, j)),
                  pl.BlockSpec((1, D), lambda j: (0, j))],
        out_specs=pl.BlockSpec((B, D), lambda j: (0, j)),
        out_shape=jax.ShapeDtypeStruct((B, n_out), f32),
        name="modulation",
    )(c, w_ada, b_ada.reshape(1, n_out))


def _band_buckets():
    qi = np.arange(WINDOW)[:, None]
    kj = np.arange(2 * WINDOW)[None, :]
    dist = np.clip(qi + WINDOW - kj, 0, MAX_DISTANCE)
    max_exact = N_BUCKETS // 2
    d = np.maximum(dist, 1).astype(np.float32)
    large = max_exact + (np.log(d / np.float32(max_exact)) / np.float32(math.log(MAX_DISTANCE / max_exact))
                         * np.float32(N_BUCKETS - max_exact)).astype(np.int32)
    large = np.minimum(large, N_BUCKETS - 1)
    bucket = np.where(dist < max_exact, dist, large).astype(np.int32)
    raw = qi + WINDOW - kj
    return np.where((raw >= 0) & (raw < WINDOW), bucket, -1).astype(np.int32)


def _bias_kernel(tab_ref, bucket_ref, o_ref):
    bk = bucket_ref[...]
    for h in range(N_Q_HEADS):
        acc = jnp.full(bk.shape, NEG_INF, f32)
        for b in range(N_BUCKETS):
            acc = jnp.where(bk == b, tab_ref[b, h] * LOG2E, acc)
        o_ref[h] = acc


def _bias_band(rel_bias_table):
    return pl.pallas_call(
        _bias_kernel,
        in_specs=[pl.BlockSpec(memory_space=pltpu.SMEM),
                  pl.BlockSpec(memory_space=pltpu.VMEM)],
        out_specs=pl.BlockSpec(memory_space=pltpu.VMEM),
        out_shape=jax.ShapeDtypeStruct((N_Q_HEADS, WINDOW, 2 * WINDOW), f32),
        name="bias_band",
    )(rel_bias_table, jnp.asarray(_band_buckets()))


def _mixer_kernel(x_ref, mod_ref, gmix_ref, gffn_ref, win_ref, dw_ref, dwb_ref, lng_ref, lnb_ref,
                  wco_ref, wao_ref, wout_ref, qg_ref, kg_ref, bq_ref, bk_ref, bias_ref, sink_ref,
                  wrh_ref, wr2_ref, br_ref, tri_ref,
                  x1_ref, h2_ref, route_ref, wtok_ref, cnt_ref,
                  uext, ush, conv_sc, proj_sc, k2, v2, osc):
    TM = x_ref.shape[0]
    HALO = 32
    b = pl.program_id(0)
    t = pl.program_id(1)
    first = t == 0

    @pl.when(first)
    def _():
        uext[0:HALO, :] = jnp.zeros((HALO, D_CONV), f32)
        k2[:, 0:WINDOW, :] = jnp.zeros((N_KV_HEADS, WINDOW, LANES), bf16)
        v2[:, 0:WINDOW, :] = jnp.zeros((N_KV_HEADS, WINDOW, LANES), bf16)

    @pl.when(first & (b == 0))
    def _():
        cnt_ref[...] = jnp.zeros(cnt_ref.shape, f32)

    x = x_ref[...]
    mod = mod_ref[...]
    sh1, sc1, g1 = mod[0:1, :], mod[1:2, :], mod[2:3, :]
    sh2, sc2, g2 = mod[3:4, :], mod[4:5, :], mod[5:6, :]
    del g2

    ms = jnp.mean(x * x, axis=-1, keepdims=True)
    h = (x * lax.rsqrt(ms + EPS)) * (gmix_ref[...] * (1.0 + sc1)) + sh1
    hb = h.astype(bf16)

    ab = _dot(hb, win_ref[:, 0:2 * D_CONV])
    u = ab[:, 0:D_CONV] * jax.nn.sigmoid(ab[:, D_CONV:2 * D_CONV])
    uext[HALO:HALO + TM, :] = u
    for r in range(1, 8):
        ush[r - 1] = uext[r:r + TM + CONV_SPAN, :]
    n_q = D_Q // PROJ_CHUNK
    n_g = D_MODEL // PROJ_CHUNK
    n_jobs = (win_ref.shape[1] - 2 * D_CONV) // PROJ_CHUNK
    n_rb = TM // CONV_ROWS
    n_t8 = CONV_ROWS // 8
    units = [(cb, rg) for cb in range(D_CONV // LANES) for rg in range(n_rb)]
    job_at = {(i * len(units)) // n_jobs: i for i in range(n_jobs)}
    assert len(job_at) == n_jobs
    for ui, (cb, rg) in enumerate(units):
        if ui in job_at:
            c_lo = 2 * D_CONV + job_at[ui] * PROJ_CHUNK
            proj_sc[job_at[ui]] = _dot(hb, win_ref[:, c_lo:c_lo + PROJ_CHUNK])
        cs = slice(cb * LANES, (cb + 1) * LANES)
        base = rg * CONV_ROWS
        accs = [None] * n_t8
        for r in range(8):
            taps = [(j, (j + 2) // 8) for j in range(CONV_WIDTH) if (j + 2) % 8 == r]
            t_lo = min(a for _, a in taps)
            t_hi = max(a for _, a in taps) + n_t8
            tiles = {}
            for t8 in range(t_lo, t_hi):
                rows = slice(base + 8 * t8, base + 8 * t8 + 8)
                tiles[t8] = uext[rows, cs] if r == 0 else ush[r - 1, rows, cs]
            for j, a in taps:
                tap = dw_ref[j, :, cs]
                for s8 in range(n_t8):
                    term = tap * tiles[a + s8]
                    accs[s8] = term if accs[s8] is None else accs[s8] + term
        for s8 in range(n_t8):
            conv_sc[base + 8 * s8:base + 8 * s8 + 8, cs] = accs[s8] + dwb_ref[:, cs]
    conv = conv_sc[...]
    uext[0:HALO, :] = uext[TM:TM + HALO, :]
    mu = jnp.mean(conv, axis=-1, keepdims=True)
    dc = conv - mu
    var = jnp.mean(dc * dc, axis=-1, keepdims=True)
    yn = dc * lax.rsqrt(var + EPS) * lng_ref[...] + lnb_ref[...]
    act = yn * jax.nn.sigmoid(yn)
    y_conv = _dot(act.astype(bf16), wco_ref[...])

    q = jnp.concatenate([proj_sc[i] for i in range(n_q)], axis=1)
    k = proj_sc[n_q, :, 0:D_KV]
    v = proj_sc[n_q, :, D_KV:2 * D_KV]

    def head_norm(z, blk_ref, g):
        sq = (z * z).astype(bf16)
        blk = blk_ref.shape[0]
        msq = jnp.concatenate([_dot(sq[:, c:c + blk], blk_ref[...]) for c in range(0, z.shape[1], blk)], axis=1)
        return z * lax.rsqrt(msq + EPS) * g

    qn = head_norm(q, bq_ref, qg_ref[...]).astype(bf16)
    kn = head_norm(k, bk_ref, kg_ref[...])
    lo_half = lax.broadcasted_iota(i32, (TM, LANES), 1) < HEAD_DIM
    kr = pltpu.roll(kn, HEAD_DIM, 1)
    vr = pltpu.roll(v, HEAD_DIM, 1)
    k2[0, WINDOW:WINDOW + TM, :] = jnp.where(lo_half, kn, kr).astype(bf16)
    k2[1, WINDOW:WINDOW + TM, :] = jnp.where(lo_half, kr, kn).astype(bf16)
    v2[0, WINDOW:WINDOW + TM, :] = jnp.where(lo_half, v, vr).astype(bf16)
    v2[1, WINDOW:WINDOW + TM, :] = jnp.where(lo_half, vr, v).astype(bf16)

    QROWS = Q_PER_KV * WINDOW
    col = lax.broadcasted_iota(i32, (QROWS, 2 * WINDOW), 1)
    has_prev = col >= jnp.where(first, WINDOW, 0)
    hrow = lax.broadcasted_iota(i32, (QROWS, 1), 0) // WINDOW
    lo128 = lax.broadcasted_iota(i32, (WINDOW, LANES), 1) < HEAD_DIM
    zero_q = jnp.zeros((WINDOW, LANES), bf16)
    for g in range(N_KV_HEADS):
        sink = LOG2E * jnp.where(hrow == 0, sink_ref[4 * g],
                                 jnp.where(hrow == 1, sink_ref[4 * g + 1],
                                           jnp.where(hrow == 2, sink_ref[4 * g + 2], sink_ref[4 * g + 3])))
        bias_g = bias_ref[g]
        for j in range(TM // WINDOW):
            rs = slice(j * WINDOW, (j + 1) * WINDOW)
            qa = qn[rs, 2 * LANES * g:2 * LANES * g + LANES]
            qb = qn[rs, 2 * LANES * g + LANES:2 * LANES * (g + 1)]
            qs = jnp.concatenate([jnp.where(lo128, qa, zero_q), jnp.where(lo128, zero_q, qa),
                                  jnp.where(lo128, qb, zero_q), jnp.where(lo128, zero_q, qb)], axis=0)
            kk = k2[g, j * WINDOW:(j + 2) * WINDOW, :]
            s = lax.dot_general(qs, kk, (((1,), (1,)), ((), ())), preferred_element_type=f32)
            logits = jnp.where(has_prev, s + bias_g, NEG_INF) if j == 0 else s + bias_g
            m = jnp.maximum(jnp.max(logits, axis=-1, keepdims=True), sink)
            p = jnp.exp2(logits - m)
            den = jnp.sum(p, axis=-1, keepdims=True) + jnp.exp2(sink - m)
            o2 = _dot(p.astype(bf16), v2[g, j * WINDOW:(j + 2) * WINDOW, :]) * (1.0 / den)
            osc[rs, 2 * LANES * g:2 * LANES * g + LANES] = jnp.where(
                lo128, o2[0:WINDOW], o2[WINDOW:2 * WINDOW]).astype(bf16)
            osc[rs, 2 * LANES * g + LANES:2 * LANES * (g + 1)] = jnp.where(
                lo128, o2[2 * WINDOW:3 * WINDOW], o2[3 * WINDOW:4 * WINDOW]).astype(bf16)
    k2[:, 0:WINDOW, :] = k2[:, TM:TM + WINDOW, :]
    v2[:, 0:WINDOW, :] = v2[:, TM:TM + WINDOW, :]
    y_attn = _dot(osc[...], wao_ref[...])

    merged = []
    for i in range(n_g):
        cs = slice(i * PROJ_CHUNK, (i + 1) * PROJ_CHUNK)
        g_conv = jax.nn.sigmoid(proj_sc[n_q + 1 + i])
        g_attn = jax.nn.sigmoid(proj_sc[n_q + 1 + n_g + i])
        merged.append((g_conv * y_conv[:, cs] + g_attn * y_attn[:, cs]).astype(bf16))
    merged = jnp.concatenate(merged, axis=1)
    x1 = x + g1 * _dot(merged, wout_ref[...])
    x1_ref[...] = x1

    ms2 = jnp.mean(x1 * x1, axis=-1, keepdims=True)
    h2 = (x1 * lax.rsqrt(ms2 + EPS)) * (gffn_ref[...] * (1.0 + sc2)) + sh2
    h2_ref[...] = _pack_halves(h2)
    hh, hl = _split(h2)
    hw = _dot(hh, wr2_ref[...])
    lg = hw[:, 0:ROUTER_COLS] + hw[:, ROUTER_COLS:2 * ROUTER_COLS] + _dot(hl, wrh_ref[...]) + br_ref[...]
    lt = lg.T
    gl = lt[0:N_GROUPS, :]
    grow = lax.broadcasted_iota(i32, (N_GROUPS, TM), 0)
    gmax = jnp.max(gl, axis=0, keepdims=True)
    gi = jnp.min(jnp.where(gl == gmax, grow, N_GROUPS), axis=0, keepdims=True)
    p_top = 1.0 / jnp.sum(jnp.exp(gl - gmax), axis=0, keepdims=True)
    sel = lt[EXPERT_COL0:EXPERT_COL0 + EXPERTS_PER_GROUP, :]
    for gg in range(1, N_GROUPS):
        lo_r = EXPERT_COL0 + gg * EXPERTS_PER_GROUP
        sel = jnp.where(gi == gg, lt[lo_r:lo_r + EXPERTS_PER_GROUP, :], sel)
    erow = lax.broadcasted_iota(i32, (EXPERTS_PER_GROUP, TM), 0)
    m1 = jnp.max(sel, axis=0, keepdims=True)
    i1 = jnp.min(jnp.where(sel == m1, erow, EXPERTS_PER_GROUP), axis=0, keepdims=True)
    rest = jnp.where(erow == i1, -jnp.inf, sel)
    m2 = jnp.max(rest, axis=0, keepdims=True)
    i2 = jnp.min(jnp.where(rest == m2, erow, EXPERTS_PER_GROUP), axis=0, keepdims=True)
    z = jnp.sum(jnp.exp(sel - m1), axis=0, keepdims=True)
    v1 = 1.0 / z
    v2nd = jnp.exp(m2 - m1) / z
    w1 = v1 / (v1 + v2nd) * p_top
    w2 = v2nd / (v1 + v2nd) * p_top
    e1 = gi * EXPERTS_PER_GROUP + i1
    e2 = gi * EXPERTS_PER_GROUP + i2

    xrow = lax.broadcasted_iota(i32, (N_EXPERTS, TM), 0)
    oh1 = xrow == e1
    oh2 = xrow == e2
    both = jnp.where(oh1 | oh2, 1.0, 0.0)
    prefix = _dot(both.astype(bf16), tri_ref[...]) + cnt_ref[:, 0:1]
    r1 = jnp.sum(jnp.where(oh1, prefix, 0.0), axis=0, keepdims=True)
    r2 = jnp.sum(jnp.where(oh2, prefix, 0.0), axis=0, keepdims=True)
    cnt_ref[...] = cnt_ref[...] + jnp.sum(both, axis=1, keepdims=True)

    route_ref[...] = jnp.concatenate(
        [e1, e2, r1.astype(i32), r2.astype(i32), jnp.zeros((4, TM), i32)], axis=0)
    wpad = jnp.concatenate([w1, w2, jnp.zeros((LANES - 2, TM), f32)], axis=0)
    wtok_ref[...] = wpad.T


def _mixer(x, mod6, p, b0, B):
    _, T, D = x.shape
    TM = min(SEQ_TILE, T)
    nt = T // TM
    N = B * T
    const = lambda shape: pl.BlockSpec(shape, lambda b, t: (0,) * len(shape))
    in_specs = [
        pl.BlockSpec((None, TM, D), lambda b, t: (b0 + b, t, 0)),
        pl.BlockSpec((None, 6, D), lambda b, t: (b0 + b, 0, 0)),
        const((1, D)), const((1, D)),
        const(p["w_in"].shape),
        const((CONV_WIDTH, 8, D_CONV)), const((8, D_CONV)), const((1, D_CONV)), const((1, D_CONV)),
        const((D_CONV, D)), const((D_Q, D)), const((D, D)),
        const((1, D_Q)), const((1, D_KV)),
        const((NORM_BLK, NORM_BLK)), const((D_KV, D_KV)),
        const((N_KV_HEADS, Q_PER_KV * WINDOW, 2 * WINDOW)),
        pl.BlockSpec(memory_space=pltpu.SMEM),
        const((D, ROUTER_COLS)), const((D, 2 * ROUTER_COLS)), const((1, ROUTER_COLS)),
        const((TM, TM)),
    ]
    out_specs = [
        pl.BlockSpec((None, TM, D), lambda b, t: (b, t, 0)),
        pl.BlockSpec((None, TM, D // 2), lambda b, t: (b, t, 0)),
        pl.BlockSpec((8, TM), lambda b, t: (0, b * nt + t)),
        pl.BlockSpec((TM, LANES), lambda b, t: (b * nt + t, 0)),
        pl.BlockSpec((N_EXPERTS, LANES), lambda b, t: (0, 0)),
    ]
    out_shape = [
        jax.ShapeDtypeStruct((B, T, D), f32),
        jax.ShapeDtypeStruct((B, T, D // 2), i32),
        jax.ShapeDtypeStruct((8, N), i32),
        jax.ShapeDtypeStruct((N, LANES), f32),
        jax.ShapeDtypeStruct((N_EXPERTS, LANES), f32),
    ]
    scratch = [
        pltpu.VMEM((TM + 32, D_CONV), f32),
        pltpu.VMEM((7, TM + CONV_SPAN, D_CONV), f32),
        pltpu.VMEM((TM, D_CONV), f32),
        pltpu.VMEM(((p["w_in"].shape[1] - 2 * D_CONV) // PROJ_CHUNK, TM, PROJ_CHUNK), f32),
        pltpu.VMEM((N_KV_HEADS, TM + WINDOW, LANES), bf16),
        pltpu.VMEM((N_KV_HEADS, TM + WINDOW, LANES), bf16),
        pltpu.VMEM((TM, D_Q), bf16),
    ]
    tri = jnp.asarray(np.triu(np.ones((TM, TM), np.float32), 1), bf16)
    return pl.pallas_call(
        _mixer_kernel,
        grid=(B, nt),
        in_specs=in_specs, out_specs=out_specs, out_shape=out_shape, scratch_shapes=scratch,
        compiler_params=pltpu.CompilerParams(
            dimension_semantics=("arbitrary", "arbitrary"), vmem_limit_bytes=VMEM_LIMIT),
        name="mixer_router",
    )(x, mod6, p["gmix"], p["gffn"], p["w_in"], p["dw"], p["dwb"], p["lng"], p["lnb"],
      p["wco"], p["wao"], p["wout"], p["qg"], p["kg"], p["bq"], p["bk"], p["bias"], p["sinks"],
      p["wrh"], p["wr2"], p["br"], tri)


def _dest_kernel(pstart_ref, route_ref, dest_ref):
    e = route_ref[0:2, :]
    base = jnp.zeros(e.shape, i32)
    for x in range(N_EXPERTS):
        base = jnp.where(e == x, pstart_ref[x], base)
    dest_ref[...] = base + route_ref[2:4, :]


def _dest_slots(pstart, route):
    N = route.shape[1]
    TS = min(DEST_TILE, N)
    return pl.pallas_call(
        _dest_kernel,
        grid=(N // TS,),
        in_specs=[pl.BlockSpec(memory_space=pltpu.SMEM),
                  pl.BlockSpec((8, TS), lambda i: (0, i))],
        out_specs=pl.BlockSpec((2, TS), lambda i: (0, i)),
        out_shape=jax.ShapeDtypeStruct((2, N), i32),
        name="dest_slots",
    )(pstart, route)


def _sc_mesh():
    return plsc.VectorSubcoreMesh(core_axis_name="c", subcore_axis_name="s")


def _sc_scatter(h2, dest, n_slots):
    N, D = h2.shape
    per_w = N // SC_WORKERS
    C = SC_CHUNK
    n_chunks = per_w // C
    dest4 = dest.reshape(TOP_K, SC_WORKERS, n_chunks, C)

    @functools.partial(
        pl.kernel, mesh=_sc_mesh(),
        out_type=jax.ShapeDtypeStruct((n_slots, D), h2.dtype),
        scratch_types=[pltpu.VMEM((n_chunks, C), i32), pltpu.VMEM((n_chunks, C), i32),
                       pltpu.VMEM((2, C, D), h2.dtype),
                       pltpu.SemaphoreType.DMA((2,)), pltpu.SemaphoreType.DMA((2,))],
        name="sc_dispatch_scatter",
    )
    def run(h2_hbm, dest_hbm, xs_hbm, idx0, idx1, buf, sem_in, sem_out):
        wid = lax.axis_index("s") * SC_CORES + lax.axis_index("c")
        base = wid * per_w
        pltpu.sync_copy(dest_hbm.at[0, wid], idx0)
        pltpu.sync_copy(dest_hbm.at[1, wid], idx1)

        def load(j, b):
            return pltpu.make_async_copy(h2_hbm.at[pl.ds(base + j * C, C)], buf.at[b], sem_in.at[b])

        def put(j, b, idx):
            return pltpu.make_async_copy(buf.at[b], xs_hbm.at[idx.at[j]], sem_out.at[b])

        load(0, 0).start()

        @pl.loop(0, n_chunks, step=2)
        def _(j0):
            for b in range(2):
                j = j0 + b
                load(j, b).wait()

                @pl.when(j >= 1)
                def _():
                    put(j - 1, 1 - b, idx0).wait()
                    put(j - 1, 1 - b, idx1).wait()

                @pl.when(j + 1 < n_chunks)
                def _():
                    load(j + 1, 1 - b).start()

                put(j, b, idx0).start()
                put(j, b, idx1).start()

        put(n_chunks - 1, 1, idx0).wait()
        put(n_chunks - 1, 1, idx1).wait()

    return run(h2, dest4)


def _sc_gather(y, dest, N):
    D = y.shape[1]
    per_w = N // SC_WORKERS
    C = SC_CHUNK
    n_chunks = per_w // C
    dest4 = dest.reshape(TOP_K, SC_WORKERS, n_chunks, C)

    @functools.partial(
        pl.kernel, mesh=_sc_mesh(),
        out_type=jax.ShapeDtypeStruct((TOP_K, N, D), y.dtype),
        scratch_types=[pltpu.VMEM((n_chunks, C), i32), pltpu.VMEM((n_chunks, C), i32),
                       pltpu.VMEM((2, C, D), y.dtype),
                       pltpu.SemaphoreType.DMA((2,)), pltpu.SemaphoreType.DMA((2,))],
        name="sc_combine_gather",
    )
    def run(y_hbm, dest_hbm, yg_hbm, idx0, idx1, buf, sem_in, sem_out):
        wid = lax.axis_index("s") * SC_CORES + lax.axis_index("c")
        base = wid * per_w
        pltpu.sync_copy(dest_hbm.at[0, wid], idx0)
        pltpu.sync_copy(dest_hbm.at[1, wid], idx1)
        idx = (idx0, idx1)

        def get(j, k):
            return pltpu.make_async_copy(y_hbm.at[idx[k].at[j]], buf.at[k], sem_in.at[k])

        def put(j, k):
            return pltpu.make_async_copy(buf.at[k], yg_hbm.at[k, pl.ds(base + j * C, C)], sem_out.at[k])

        get(0, 0).start()
        get(0, 1).start()

        @pl.loop(0, n_chunks)
        def _(j):
            for k in range(TOP_K):
                get(j, k).wait()
                put(j, k).start()
            for k in range(TOP_K):
                put(j, k).wait()

                @pl.when(j + 1 < n_chunks)
                def _():
                    get(j + 1, k).start()

    return run(y, dest4)


def _expert_kernel(be_ref, nv_ref, nu_ref, first_ref, slot_ref, nxt_ref,
                   xs_ref, wg_hbm, wu_hbm, wdn_hbm, y_ref,
                   stg_g, stg_u, stg_d, wgu_ref, wd_ref, sems):
    i = pl.program_id(0)
    in_use = i < nu_ref[0]

    def fetch(e, s):
        return (pltpu.make_async_copy(wg_hbm.at[e], stg_g.at[s], sems.at[s, 0]),
                pltpu.make_async_copy(wu_hbm.at[e], stg_u.at[s], sems.at[s, 1]),
                pltpu.make_async_copy(wdn_hbm.at[e], stg_d.at[s], sems.at[s, 2]))

    @pl.when(in_use & (first_ref[i] == 1))
    def _():
        s = slot_ref[i]

        @pl.when(i == 0)
        def _():
            for cp in fetch(be_ref[0], 0):
                cp.start()

        for cp in fetch(be_ref[i], s):
            cp.wait()
        wgu_ref[:, 0:D_EXPERT] = stg_g[s].astype(bf16)
        wgu_ref[:, D_EXPERT:2 * D_EXPERT] = stg_u[s].astype(bf16)
        wd_ref[...] = stg_d[s].astype(bf16)

        @pl.when(nxt_ref[i] >= 0)
        def _():
            for cp in fetch(nxt_ref[i], 1 - s):
                cp.start(priority=1)

    @pl.when(in_use)
    def _():
        live = lax.broadcasted_iota(i32, (MOE_BLOCK, 1), 0) < nv_ref[i]
        x_hi, x_lo = _unpack_halves(jnp.where(live, xs_ref[...], 0))
        half = x_hi.shape[1]
        gu = (_dot(x_hi.astype(bf16), wgu_ref[0:half, :]) +
              _dot(x_lo.astype(bf16), wgu_ref[half:2 * half, :]))
        gate = gu[:, 0:D_EXPERT]
        hid = (gate * jax.nn.sigmoid(gate)) * gu[:, D_EXPERT:2 * D_EXPERT]
        y_ref[...] = _pack_halves(_dot(hid.astype(bf16), wd_ref[...]))


def _experts(block_e, n_valid, n_used, counts, xs, w_gate, w_up, w_down):
    n_slots, DH = xs.shape
    D = 2 * DH
    n_blocks = n_slots // MOE_BLOCK
    idx = jnp.arange(n_blocks, dtype=i32)
    prev_e = jnp.concatenate([jnp.full((1,), -1, i32), block_e[:-1]])
    first = ((block_e != prev_e) & (idx < n_used[0])).astype(i32)
    slot = ((jnp.cumsum(first) - 1) % 2).astype(i32)
    e_ids = jnp.arange(N_EXPERTS, dtype=i32)
    later = (e_ids[None, :] > e_ids[:, None]) & (counts[None, :] > 0)
    nxt_of_e = jnp.min(jnp.where(later, e_ids[None, :], N_EXPERTS), axis=1)
    nxt = jnp.where(nxt_of_e[block_e] < N_EXPERTS, nxt_of_e[block_e], -1).astype(i32)
    last = lambda i, nu: jnp.minimum(i, nu[0] - 1)
    slot_map = lambda i, be, nv, nu, fi, sl, nx: (last(i, nu), 0)
    grid_spec = pltpu.PrefetchScalarGridSpec(
        num_scalar_prefetch=6,
        grid=(n_blocks,),
        in_specs=[pl.BlockSpec((MOE_BLOCK, DH), slot_map),
                  pl.BlockSpec(memory_space=pl.ANY),
                  pl.BlockSpec(memory_space=pl.ANY),
                  pl.BlockSpec(memory_space=pl.ANY)],
        out_specs=pl.BlockSpec((MOE_BLOCK, DH), slot_map),
        scratch_shapes=[pltpu.VMEM((2, D, D_EXPERT), f32), pltpu.VMEM((2, D, D_EXPERT), f32),
                        pltpu.VMEM((2, D_EXPERT, D), f32),
                        pltpu.VMEM((D, 2 * D_EXPERT), bf16), pltpu.VMEM((D_EXPERT, D), bf16),
                        pltpu.SemaphoreType.DMA((2, 3))],
    )
    return pl.pallas_call(
        _expert_kernel,
        grid_spec=grid_spec,
        out_shape=jax.ShapeDtypeStruct((n_slots, DH), i32),
        compiler_params=pltpu.CompilerParams(dimension_semantics=("arbitrary",),
                                             vmem_limit_bytes=VMEM_LIMIT),
        name="experts",
    )(block_e, n_valid, n_used, first, slot, nxt, xs, w_gate, w_up, w_down)


def _combine_kernel(yg_ref, x1_ref, wtok_ref, mod_ref, *rest):
    o_ref = rest[-1]
    w = wtok_ref[...]
    g2 = mod_ref[5:6, :]
    a_hi, a_lo = _unpack_halves(yg_ref[0])
    b_hi, b_lo = _unpack_halves(yg_ref[1])
    moe = jnp.concatenate([w[:, 0:1] * a_hi + w[:, 1:2] * b_hi,
                           w[:, 0:1] * a_lo + w[:, 1:2] * b_lo], axis=1)
    o_ref[...] = x1_ref[...] + g2 * moe


def _combine(yg, x1, wtok, mod6, T, b0, n_total, out_prev):
    Nc, D = x1.shape
    TS = min(COMBINE_TILE, T)
    per_seq = T // TS
    blk0 = b0 * per_seq
    in_specs = [pl.BlockSpec((TOP_K, TS, D // 2), lambda i: (0, i, 0)),
                pl.BlockSpec((TS, D), lambda i: (i, 0)),
                pl.BlockSpec((TS, LANES), lambda i: (i, 0)),
                pl.BlockSpec((None, 6, D), lambda i: (b0 + i // per_seq, 0, 0))]
    args = [yg, x1, wtok, mod6]
    aliases = {}
    if out_prev is not None:
        in_specs.append(pl.BlockSpec(memory_space=pl.ANY))
        args.append(out_prev)
        aliases = {len(args) - 1: 0}
    return pl.pallas_call(
        _combine_kernel,
        grid=(Nc // TS,),
        in_specs=in_specs,
        out_specs=pl.BlockSpec((TS, D), lambda i: (blk0 + i, 0)),
        out_shape=jax.ShapeDtypeStruct((n_total, D), f32),
        input_output_aliases=aliases,
        compiler_params=pltpu.CompilerParams(dimension_semantics=("arbitrary",),
                                             vmem_limit_bytes=VMEM_LIMIT),
        name="combine",
    )(*args)


def _block_diag_mean(n, blk):
    m = np.kron(np.eye(n // blk, dtype=np.float32), np.full((blk, blk), 1.0 / blk, np.float32))
    return jnp.asarray(m, bf16)


def _layer(x, mod6, bias, l, w):
    B, T, D = x.shape
    N = B * T
    w_rg, w_re = w["w_router_group"][l], w["w_router_expert"][l]
    wr = jnp.zeros((D, ROUTER_COLS), f32)
    wr = wr.at[:, 0:N_GROUPS].set(w_rg).at[:, EXPERT_COL0:EXPERT_COL0 + N_EXPERTS].set(w_re)
    br = jnp.zeros((1, ROUTER_COLS), f32)
    br = br.at[0, 0:N_GROUPS].set(w["b_router_group"][l])
    br = br.at[0, EXPERT_COL0:EXPERT_COL0 + N_EXPERTS].set(w["b_router_expert"][l])
    wrh = wr.astype(bf16)
    wrl = (wr - wrh.astype(f32)).astype(bf16)
    p = dict(
        gmix=w["norm_mix_g"][l].reshape(1, D), gffn=w["norm_ffn_g"][l].reshape(1, D),
        w_in=w["w_in"][l].astype(bf16),
        dw=jnp.broadcast_to(w["dw_kernel"][l][:, None, :], (CONV_WIDTH, 8, D_CONV)), dwb=jnp.broadcast_to(w["dw_bias"][l][None, :], (8, D_CONV)),
        lng=w["conv_ln_g"][l].reshape(1, D_CONV), lnb=w["conv_ln_b"][l].reshape(1, D_CONV),
        wco=w["w_conv_out"][l].astype(bf16), wao=w["w_attn_out"][l].astype(bf16),
        wout=w["w_out"][l].astype(bf16),
        qg=(jnp.tile(w["q_norm_g"][l], N_Q_HEADS) * (HEAD_DIM ** -0.5 * LOG2E)).reshape(1, D_Q),
        kg=jnp.tile(w["k_norm_g"][l], N_KV_HEADS).reshape(1, D_KV),
        bq=_block_diag_mean(NORM_BLK, HEAD_DIM), bk=_block_diag_mean(D_KV, HEAD_DIM),
        bias=bias, sinks=w["sinks"][l], wrh=wrh, wr2=jnp.concatenate([wrh, wrl], axis=1), br=br,
    )
    w_gate, w_up, w_down = w["w_exp_gate"][l], w["w_exp_up"][l], w["w_exp_down"][l]

    n_chunks = MOE_CHUNKS if B % MOE_CHUNKS == 0 else 1
    Bc = B // n_chunks
    Nc = Bc * T
    n_blocks = -(-(Nc * TOP_K) // MOE_BLOCK) + N_EXPERTS
    blk0 = jnp.arange(n_blocks, dtype=i32) * MOE_BLOCK
    stage = []
    for ch in range(n_chunks):
        x1, h2, route, wtok, cnt = _mixer(x, mod6, p, ch * Bc, Bc)
        counts = cnt[:, 0].astype(i32)
        pcounts = (counts + MOE_BLOCK - 1) // MOE_BLOCK * MOE_BLOCK
        pend = jnp.cumsum(pcounts)
        pstart = pend - pcounts
        block_e = jnp.minimum(jnp.sum((pend[None, :] <= blk0[:, None]).astype(i32), axis=1), N_EXPERTS - 1)
        n_valid = jnp.clip((pstart + counts)[block_e] - blk0, 0, MOE_BLOCK).astype(i32)
        n_used = (pend[-1:] // MOE_BLOCK).astype(i32)
        dest = _dest_slots(pstart, route)
        xs = _sc_scatter(h2.reshape(Nc, D // 2), dest, n_blocks * MOE_BLOCK)
        stage.append((x1, wtok, dest, xs, block_e, n_valid, n_used, counts))
    ys = [_experts(be, nv, nu, cn, xs, w_gate, w_up, w_down) for (_, _, _, xs, be, nv, nu, cn) in stage]
    ygs = [_sc_gather(y, st[2], Nc) for y, st in zip(ys, stage)]
    out = None
    for ch in range(n_chunks):
        x1, wtok = stage[ch][0], stage[ch][1]
        out = _combine(ygs[ch], x1.reshape(Nc, D), wtok, mod6, T, ch * Bc, B * T, out)
    return out.reshape(B, T, D)


def kernel(x, c, w_ada, b_ada, norm_mix_g, w_in, dw_kernel, dw_bias, conv_ln_g, conv_ln_b,
           w_conv_out, q_norm_g, k_norm_g, sinks, w_attn_out, w_out, rel_bias_table, norm_ffn_g,
           w_router_group, b_router_group, w_router_expert, b_router_expert,
           w_exp_gate, w_exp_up, w_exp_down):
    w = dict(norm_mix_g=norm_mix_g, w_in=w_in, dw_kernel=dw_kernel, dw_bias=dw_bias,
             conv_ln_g=conv_ln_g, conv_ln_b=conv_ln_b, w_conv_out=w_conv_out, q_norm_g=q_norm_g,
             k_norm_g=k_norm_g, sinks=sinks, w_attn_out=w_attn_out, w_out=w_out,
             norm_ffn_g=norm_ffn_g, w_router_group=w_router_group, b_router_group=b_router_group,
             w_router_expert=w_router_expert, b_router_expert=b_router_expert,
             w_exp_gate=w_exp_gate, w_exp_up=w_exp_up, w_exp_down=w_exp_down)
    B = x.shape[0]
    bias = _bias_band(rel_bias_table).reshape(N_KV_HEADS, Q_PER_KV * WINDOW, 2 * WINDOW)
    for l in range(w_ada.shape[0]):
        mod6 = _modulation(c, w_ada[l], b_ada[l]).reshape(B, 6, D_MODEL)
        x = _layer(x, mod6, bias, l, w)
    return x
```
